```python
import jax, jax.numpy as jnp
from jax import lax
import numpy as np

D_MODEL = 1024
BATCH = 8
SEQ = 8192
DEPTH = 4

N_A = DEPTH // 2
N_B = DEPTH - N_A
POOL_WINDOWS = (2, 4, 8, 16)
POOL_GROUPS = len(POOL_WINDOWS)
GC = D_MODEL // POOL_GROUPS
HEAD_DIM = 64
N_HEADS = D_MODEL // HEAD_DIM
N_KV = max(1, N_HEADS // 8)
GROUP = N_HEADS // N_KV
WINDOW = 128
BLK = WINDOW
D_FF = 2816
CONV_W = 3
EPS = 1e-5

kernel_name = "yoco_pool_swa_sink_convffn"


def rmsnorm(x, g):
    xf = x.astype(jnp.float32)
    y = xf * lax.rsqrt(jnp.mean(xf * xf, axis=-1, keepdims=True) + EPS)
    return (y * g.astype(jnp.float32)).astype(x.dtype)


def pool_mixer(h, w, scale):
    B, S, D = h.shape
    c = jnp.cumsum(h.astype(jnp.float32), axis=1)
    t = jnp.arange(S)
    pooled = []
    for gi, win in enumerate(POOL_WINDOWS):
        cg = c[..., gi * GC:(gi + 1) * GC]
        lag = jnp.pad(cg, ((0, 0), (win, 0), (0, 0)))[:, :S]
        cnt = jnp.minimum(t + 1, win).astype(jnp.float32)[None, :, None]
        pooled.append((cg - lag) / cnt)
    pooled = jnp.stack(pooled, axis=2).astype(h.dtype) - h.reshape(B, S, POOL_GROUPS, GC)
    y = jnp.einsum('bsgc,gcd->bsgd', pooled, w).reshape(B, S, D)
    return y * scale


def conv_ffn(h, w_up, conv_w, conv_b, w_down):
    u = h @ w_up
    S = u.shape[1]
    up = jnp.pad(u, ((0, 0), (CONV_W - 1, 0), (0, 0)))
    u = sum(conv_w[k] * up[:, k:k + S] for k in range(CONV_W)) + conv_b
    gate, val = jnp.split(u, 2, axis=-1)
    return (jax.nn.silu(gate) * val) @ w_down


def swa_sink_attention(q, k, v, sinks):
    B, S = q.shape[:2]
    nb = S // BLK
    qb = q.reshape(B, nb, BLK, N_KV, GROUP, HEAD_DIM)
    kb = k.reshape(B, nb, BLK, N_KV, HEAD_DIM)
    vb = v.reshape(B, nb, BLK, N_KV, HEAD_DIM)
    pad = ((0, 0), (1, 0), (0, 0), (0, 0), (0, 0))
    kw = jnp.concatenate([jnp.pad(kb, pad)[:, :nb], kb], axis=2)
    vw = jnp.concatenate([jnp.pad(vb, pad)[:, :nb], vb], axis=2)
    s = jnp.einsum('bnqkgd,bnskd->bnkgqs', qb, kw).astype(jnp.float32) * (HEAD_DIM ** -0.5)
    qi = jnp.arange(BLK)[:, None]
    si = jnp.arange(2 * BLK)[None, :]
    band = (si > qi) & (si <= qi + BLK)
    valid = (jnp.arange(nb)[:, None, None] > 0) | (si >= BLK)[None]
    mask = (band[None] & valid)[None, :, None, None]
    sink = sinks.astype(jnp.float32).reshape(N_KV, GROUP)[None, None, :, :, None, None]
    s = jnp.where(mask, s, -jnp.inf)
    m = jnp.maximum(jnp.max(s, axis=-1, keepdims=True), sink)
    p = jnp.exp(s - m)
    denom = jnp.sum(p, axis=-1, keepdims=True) + jnp.exp(sink - m)
    pr = (p / denom).astype(v.dtype)
    o = jnp.einsum('bnkgqs,bnskd->bnqkgd', pr, vw)
    return o.reshape(B, S, N_HEADS * HEAD_DIM)


def _fwd_setup_inputs(seed: int = 0) -> dict:
    key = jax.random.key(seed)
    ks = jax.random.split(key, 20)
    f32 = jnp.float32
    nrm = lambda k, shp, s: jax.random.normal(k, shp, f32) * s
    KVW = 2 * N_KV * HEAD_DIM
    QW = N_HEADS * HEAD_DIM
    return {
        "x": nrm(ks[0], (BATCH, SEQ, D_MODEL), 1.0),
        "norm1_g": 1.0 + nrm(ks[1], (DEPTH, D_MODEL), 0.02),
        "norm2_g": 1.0 + nrm(ks[2], (DEPTH, D_MODEL), 0.02),
        "pool_w": nrm(ks[3], (N_A, POOL_GROUPS, GC, GC), GC ** -0.5),
        "pool_scale": 1.0 + nrm(ks[4], (N_A, D_MODEL), 0.02),
        "kv_norm_g": 1.0 + nrm(ks[5], (D_MODEL,), 0.02),
        "w_kv": nrm(ks[6], (D_MODEL, KVW), D_MODEL ** -0.5),
        "b_kv": nrm(ks[7], (KVW,), 0.02),
        "w_q": nrm(ks[8], (N_B, D_MODEL, QW), D_MODEL ** -0.5),
        "b_q": nrm(ks[9], (N_B, QW), 0.02),
        "sinks": nrm(ks[10], (N_B, N_HEADS), 1.0),
        "w_o": nrm(ks[11], (N_B, QW, D_MODEL), QW ** -0.5),
        "b_o": nrm(ks[12], (N_B, D_MODEL), 0.02),
        "ffn_up": nrm(ks[13], (DEPTH, D_MODEL, 2 * D_FF), D_MODEL ** -0.5),
        "ffn_conv_w": nrm(ks[14], (DEPTH, CONV_W, 2 * D_FF), CONV_W ** -0.5),
        "ffn_conv_b": nrm(ks[15], (DEPTH, 2 * D_FF), 0.02),
        "ffn_down": nrm(ks[16], (DEPTH, D_FF, D_MODEL), D_FF ** -0.5),
        "final_g": 1.0 + nrm(ks[17], (D_MODEL,), 0.02),
    }


def _fwd_reference(x, norm1_g, norm2_g, pool_w, pool_scale, kv_norm_g, w_kv, b_kv, w_q, b_q,
              sinks, w_o, b_o, ffn_up, ffn_conv_w, ffn_conv_b, ffn_down, final_g):
    B, S, D = x.shape
    k_sh = v_sh = None
    for l in range(DEPTH):
        h = rmsnorm(x, norm1_g[l])
        if l < N_A:
            x = x + pool_mixer(h, pool_w[l], pool_scale[l])
        else:
            j = l - N_A
            q = (h @ w_q[j] + b_q[j]).reshape(B, S, N_HEADS, HEAD_DIM)
            o = swa_sink_attention(q, k_sh, v_sh, sinks[j])
            x = x + (o @ w_o[j] + b_o[j])
        h = rmsnorm(x, norm2_g[l])
        x = x + conv_ffn(h, ffn_up[l], ffn_conv_w[l], ffn_conv_b[l], ffn_down[l])
        if l == N_A - 1:
            kv = rmsnorm(x, kv_norm_g) @ w_kv + b_kv
            k_sh, v_sh = jnp.split(kv.reshape(B, S, 2 * N_KV, HEAD_DIM), 2, axis=2)
    return rmsnorm(x, final_g)


import jax as _jax
import jax.numpy as _jnp

TWIN_FORMAT = 'train_step'
FWD_PARAMS = ['x', 'norm1_g', 'norm2_g', 'pool_w', 'pool_scale', 'kv_norm_g', 'w_kv', 'b_kv', 'w_q', 'b_q', 'sinks', 'w_o', 'b_o', 'ffn_up', 'ffn_conv_w', 'ffn_conv_b', 'ffn_down', 'final_g']
TWIN_WEIGHTS = ['norm1_g', 'norm2_g', 'pool_w', 'pool_scale', 'kv_norm_g', 'w_kv', 'b_kv', 'w_q', 'b_q', 'sinks', 'w_o', 'b_o', 'ffn_up', 'ffn_conv_w', 'ffn_conv_b', 'ffn_down', 'final_g']
TWIN_DIFF_INPUT = 'x'
TWIN_INPUTS = ['x', 'norm1_g', 'norm2_g', 'pool_w', 'pool_scale', 'kv_norm_g', 'w_kv', 'b_kv', 'w_q', 'b_q', 'sinks', 'w_o', 'b_o', 'ffn_up', 'ffn_conv_w', 'ffn_conv_b', 'ffn_down', 'final_g', 'loss_target', 'm_norm1_g', 'm_norm2_g', 'm_pool_w', 'm_pool_scale', 'm_kv_norm_g', 'm_w_kv', 'm_b_kv', 'm_w_q', 'm_b_q', 'm_sinks', 'm_w_o', 'm_b_o', 'm_ffn_up', 'm_ffn_conv_w', 'm_ffn_conv_b', 'm_ffn_down', 'm_final_g', 'v_norm1_g', 'v_norm2_g', 'v_pool_w', 'v_pool_scale', 'v_kv_norm_g', 'v_w_kv', 'v_b_kv', 'v_w_q', 'v_b_q', 'v_sinks', 'v_w_o', 'v_b_o', 'v_ffn_up', 'v_ffn_conv_w', 'v_ffn_conv_b', 'v_ffn_down', 'v_final_g']
TWIN_OUTPUTS = ['loss', 'grad_x', 'grad_norm1_g', 'grad_norm2_g', 'grad_pool_w', 'grad_pool_scale', 'grad_kv_norm_g', 'grad_w_kv', 'grad_b_kv', 'grad_w_q', 'grad_b_q', 'grad_sinks', 'grad_w_o', 'grad_b_o', 'grad_ffn_up', 'grad_ffn_conv_w', 'grad_ffn_conv_b', 'grad_ffn_down', 'grad_final_g', 'delta_norm1_g', 'delta_norm2_g', 'delta_pool_w', 'delta_pool_scale', 'delta_kv_norm_g', 'delta_w_kv', 'delta_b_kv', 'delta_w_q', 'delta_b_q', 'delta_sinks', 'delta_w_o', 'delta_b_o', 'delta_ffn_up', 'delta_ffn_conv_w', 'delta_ffn_conv_b', 'delta_ffn_down', 'delta_final_g', 'new_m_norm1_g', 'new_m_norm2_g', 'new_m_pool_w', 'new_m_pool_scale', 'new_m_kv_norm_g', 'new_m_w_kv', 'new_m_b_kv', 'new_m_w_q', 'new_m_b_q', 'new_m_sinks', 'new_m_w_o', 'new_m_b_o', 'new_m_ffn_up', 'new_m_ffn_conv_w', 'new_m_ffn_conv_b', 'new_m_ffn_down', 'new_m_final_g', 'new_v_norm1_g', 'new_v_norm2_g', 'new_v_pool_w', 'new_v_pool_scale', 'new_v_kv_norm_g', 'new_v_w_kv', 'new_v_b_kv', 'new_v_w_q', 'new_v_b_q', 'new_v_sinks', 'new_v_w_o', 'new_v_b_o', 'new_v_ffn_up', 'new_v_ffn_conv_w', 'new_v_ffn_conv_b', 'new_v_ffn_down', 'new_v_final_g']
TWIN_LEAF_KINDS = {'loss': 'loss', 'grad_x': 'grad_x', 'grad_norm1_g': 'grad_w', 'grad_norm2_g': 'grad_w', 'grad_pool_w': 'grad_w', 'grad_pool_scale': 'grad_w', 'grad_kv_norm_g': 'grad_w', 'grad_w_kv': 'grad_w', 'grad_b_kv': 'grad_w', 'grad_w_q': 'grad_w', 'grad_b_q': 'grad_w', 'grad_sinks': 'grad_w', 'grad_w_o': 'grad_w', 'grad_b_o': 'grad_w', 'grad_ffn_up': 'grad_w', 'grad_ffn_conv_w': 'grad_w', 'grad_ffn_conv_b': 'grad_w', 'grad_ffn_down': 'grad_w', 'grad_final_g': 'grad_w', 'delta_norm1_g': 'delta_w', 'delta_norm2_g': 'delta_w', 'delta_pool_w': 'delta_w', 'delta_pool_scale': 'delta_w', 'delta_kv_norm_g': 'delta_w', 'delta_w_kv': 'delta_w', 'delta_b_kv': 'delta_w', 'delta_w_q': 'delta_w', 'delta_b_q': 'delta_w', 'delta_sinks': 'delta_w', 'delta_w_o': 'delta_w', 'delta_b_o': 'delta_w', 'delta_ffn_up': 'delta_w', 'delta_ffn_conv_w': 'delta_w', 'delta_ffn_conv_b': 'delta_w', 'delta_ffn_down': 'delta_w', 'delta_final_g': 'delta_w', 'new_m_norm1_g': 'new_m', 'new_m_norm2_g': 'new_m', 'new_m_pool_w': 'new_m', 'new_m_pool_scale': 'new_m', 'new_m_kv_norm_g': 'new_m', 'new_m_w_kv': 'new_m', 'new_m_b_kv': 'new_m', 'new_m_w_q': 'new_m', 'new_m_b_q': 'new_m', 'new_m_sinks': 'new_m', 'new_m_w_o': 'new_m', 'new_m_b_o': 'new_m', 'new_m_ffn_up': 'new_m', 'new_m_ffn_conv_w': 'new_m', 'new_m_ffn_conv_b': 'new_m', 'new_m_ffn_down': 'new_m', 'new_m_final_g': 'new_m', 'new_v_norm1_g': 'new_v', 'new_v_norm2_g': 'new_v', 'new_v_pool_w': 'new_v', 'new_v_pool_scale': 'new_v', 'new_v_kv_norm_g': 'new_v', 'new_v_w_kv': 'new_v', 'new_v_b_kv': 'new_v', 'new_v_w_q': 'new_v', 'new_v_b_q': 'new_v', 'new_v_sinks': 'new_v', 'new_v_w_o': 'new_v', 'new_v_b_o': 'new_v', 'new_v_ffn_up': 'new_v', 'new_v_ffn_conv_w': 'new_v', 'new_v_ffn_conv_b': 'new_v', 'new_v_ffn_down': 'new_v', 'new_v_final_g': 'new_v'}


def _forward(args):
    return _fwd_reference(*[args[k] for k in FWD_PARAMS])


def _output_shape():
    def fwd():
        inp = _fwd_setup_inputs(0)
        return _fwd_reference(*[inp[k] for k in FWD_PARAMS])
    out = _jax.eval_shape(fwd)
    return out.shape, out.dtype

N_MICROBATCH = 1
ADAM_LR = 0.001
ADAM_B1 = 0.9
ADAM_B2 = 0.999
ADAM_EPS = 1e-08
ADAM_WD = 0.01
ADAM_STEP = 10
PER_EXAMPLE_BATCH_AXIS = {'x': 0, 'loss_target': 0}
SHARED_INPUTS = []
_WEIGHT_DTYPES = {'norm1_g': _jnp.float32, 'norm2_g': _jnp.float32, 'pool_w': _jnp.float32, 'pool_scale': _jnp.float32, 'kv_norm_g': _jnp.float32, 'w_kv': _jnp.float32, 'b_kv': _jnp.float32, 'w_q': _jnp.float32, 'b_q': _jnp.float32, 'sinks': _jnp.float32, 'w_o': _jnp.float32, 'b_o': _jnp.float32, 'ffn_up': _jnp.float32, 'ffn_conv_w': _jnp.float32, 'ffn_conv_b': _jnp.float32, 'ffn_down': _jnp.float32, 'final_g': _jnp.float32}
MOMENT_SCALE = {'norm1_g': 1.327437e-01, 'norm2_g': 1.383193e-01, 'pool_w': 1.893518e-01, 'pool_scale': 3.716467e-01, 'kv_norm_g': 5.540091e-02, 'w_kv': 1.088334e-01, 'b_kv': 4.580780e-01, 'w_q': 2.578315e-02, 'b_q': 2.546322e-02, 'sinks': 3.337795e-02, 'w_o': 2.856918e-02, 'b_o': 1.533856e-01, 'ffn_up': 5.816054e-02, 'ffn_conv_w': 5.888082e-02, 'ffn_conv_b': 5.791567e-02, 'ffn_down': 9.485320e-02, 'final_g': 6.436898e+01}


def _to_microbatches(a, axis):
    t = _jnp.moveaxis(a, axis, 0)
    t = t.reshape((N_MICROBATCH, t.shape[0] // N_MICROBATCH) + t.shape[1:])
    return _jnp.moveaxis(t, 1, axis + 1)


def setup_inputs(seed: int = 0) -> dict:
    inp = _fwd_setup_inputs(seed)
    key = _jax.random.fold_in(_jax.random.key(seed), 7919)
    shape, _ = _output_shape()
    out = dict(inp)
    out["loss_target"] = _jax.random.normal(_jax.random.fold_in(key, 0), shape, _jnp.float32)
    for i, name in enumerate(TWIN_WEIGHTS):
        w = inp[name].astype(_jnp.float32)
        if MOMENT_SCALE is None:
            s = _jnp.sqrt(_jnp.mean(_jnp.square(w)) + 1e-30)
        else:
            s = MOMENT_SCALE[name]
        km, kv = _jax.random.split(_jax.random.fold_in(key, i + 1))
        out[name] = w
        out["m_" + name] = s * _jax.random.normal(km, w.shape, _jnp.float32)
        out["v_" + name] = (s * s) * _jax.random.uniform(kv, w.shape, _jnp.float32, 0.5, 1.5)
    if N_MICROBATCH > 1:
        for name, axis in PER_EXAMPLE_BATCH_AXIS.items():
            out[name] = _to_microbatches(out[name], axis)
    return {'x': out['x'], 'norm1_g': out['norm1_g'], 'norm2_g': out['norm2_g'], 'pool_w': out['pool_w'], 'pool_scale': out['pool_scale'], 'kv_norm_g': out['kv_norm_g'], 'w_kv': out['w_kv'], 'b_kv': out['b_kv'], 'w_q': out['w_q'], 'b_q': out['b_q'], 'sinks': out['sinks'], 'w_o': out['w_o'], 'b_o': out['b_o'], 'ffn_up': out['ffn_up'], 'ffn_conv_w': out['ffn_conv_w'], 'ffn_conv_b': out['ffn_conv_b'], 'ffn_down': out['ffn_down'], 'final_g': out['final_g'], 'loss_target': out['loss_target'], 'm_norm1_g': out['m_norm1_g'], 'm_norm2_g': out['m_norm2_g'], 'm_pool_w': out['m_pool_w'], 'm_pool_scale': out['m_pool_scale'], 'm_kv_norm_g': out['m_kv_norm_g'], 'm_w_kv': out['m_w_kv'], 'm_b_kv': out['m_b_kv'], 'm_w_q': out['m_w_q'], 'm_b_q': out['m_b_q'], 'm_sinks': out['m_sinks'], 'm_w_o': out['m_w_o'], 'm_b_o': out['m_b_o'], 'm_ffn_up': out['m_ffn_up'], 'm_ffn_conv_w': out['m_ffn_conv_w'], 'm_ffn_conv_b': out['m_ffn_conv_b'], 'm_ffn_down': out['m_ffn_down'], 'm_final_g': out['m_final_g'], 'v_norm1_g': out['v_norm1_g'], 'v_norm2_g': out['v_norm2_g'], 'v_pool_w': out['v_pool_w'], 'v_pool_scale': out['v_pool_scale'], 'v_kv_norm_g': out['v_kv_norm_g'], 'v_w_kv': out['v_w_kv'], 'v_b_kv': out['v_b_kv'], 'v_w_q': out['v_w_q'], 'v_b_q': out['v_b_q'], 'v_sinks': out['v_sinks'], 'v_w_o': out['v_w_o'], 'v_b_o': out['v_b_o'], 'v_ffn_up': out['v_ffn_up'], 'v_ffn_conv_w': out['v_ffn_conv_w'], 'v_ffn_conv_b': out['v_ffn_conv_b'], 'v_ffn_down': out['v_ffn_down'], 'v_final_g': out['v_final_g']}


def _loss(weights, diff, rest, loss_target):
    with _jax.named_scope("forward"):
        args = {**rest, TWIN_DIFF_INPUT: diff, **{k: w.astype(_WEIGHT_DTYPES[k]) for k, w in weights.items()}}
        y = _forward(args)
    with _jax.named_scope("loss_head"):
        err = _jnp.square(y.astype(_jnp.float32) - loss_target)
        return 0.5 * _jnp.sum(_jnp.mean(err, axis=-1)) if err.ndim else 0.5 * err


def _adamw(w, g, m, v):
    m = ADAM_B1 * m + (1.0 - ADAM_B1) * g
    v = ADAM_B2 * v + (1.0 - ADAM_B2) * _jnp.square(g)
    m_hat = m / (1.0 - ADAM_B1 ** ADAM_STEP)
    v_hat = v / (1.0 - ADAM_B2 ** ADAM_STEP)
    delta = -ADAM_LR * (m_hat / (_jnp.sqrt(v_hat) + ADAM_EPS) + ADAM_WD * w)
    return delta, m, v


def reference(x, norm1_g, norm2_g, pool_w, pool_scale, kv_norm_g, w_kv, b_kv, w_q, b_q, sinks, w_o, b_o, ffn_up, ffn_conv_w, ffn_conv_b, ffn_down, final_g, loss_target, m_norm1_g, m_norm2_g, m_pool_w, m_pool_scale, m_kv_norm_g, m_w_kv, m_b_kv, m_w_q, m_b_q, m_sinks, m_w_o, m_b_o, m_ffn_up, m_ffn_conv_w, m_ffn_conv_b, m_ffn_down, m_final_g, v_norm1_g, v_norm2_g, v_pool_w, v_pool_scale, v_kv_norm_g, v_w_kv, v_b_kv, v_w_q, v_b_q, v_sinks, v_w_o, v_b_o, v_ffn_up, v_ffn_conv_w, v_ffn_conv_b, v_ffn_down, v_final_g):
    given = dict(x=x, norm1_g=norm1_g, norm2_g=norm2_g, pool_w=pool_w, pool_scale=pool_scale, kv_norm_g=kv_norm_g, w_kv=w_kv, b_kv=b_kv, w_q=w_q, b_q=b_q, sinks=sinks, w_o=w_o, b_o=b_o, ffn_up=ffn_up, ffn_conv_w=ffn_conv_w, ffn_conv_b=ffn_conv_b, ffn_down=ffn_down, final_g=final_g, loss_target=loss_target, m_norm1_g=m_norm1_g, m_norm2_g=m_norm2_g, m_pool_w=m_pool_w, m_pool_scale=m_pool_scale, m_kv_norm_g=m_kv_norm_g, m_w_kv=m_w_kv, m_b_kv=m_b_kv, m_w_q=m_w_q, m_b_q=m_b_q, m_sinks=m_sinks, m_w_o=m_w_o, m_b_o=m_b_o, m_ffn_up=m_ffn_up, m_ffn_conv_w=m_ffn_conv_w, m_ffn_conv_b=m_ffn_conv_b, m_ffn_down=m_ffn_down, m_final_g=m_final_g, v_norm1_g=v_norm1_g, v_norm2_g=v_norm2_g, v_pool_w=v_pool_w, v_pool_scale=v_pool_scale, v_kv_norm_g=v_kv_norm_g, v_w_kv=v_w_kv, v_b_kv=v_b_kv, v_w_q=v_w_q, v_b_q=v_b_q, v_sinks=v_sinks, v_w_o=v_w_o, v_b_o=v_b_o, v_ffn_up=v_ffn_up, v_ffn_conv_w=v_ffn_conv_w, v_ffn_conv_b=v_ffn_conv_b, v_ffn_down=v_ffn_down, v_final_g=v_final_g)
    weights = {n: given[n] for n in TWIN_WEIGHTS}
    shared = {n: given[n] for n in SHARED_INPUTS}
    per_example = {n: given[n] for n in ['x']}
    grad_fn = _jax.value_and_grad(_loss, argnums=(0, 1))

    def one_microbatch(ex, loss_target):
        ex = dict(ex)
        diff = ex.pop(TWIN_DIFF_INPUT)
        return grad_fn(weights, diff, {**shared, **ex}, loss_target)

    if N_MICROBATCH == 1:
        loss, (grad_w, grad_x) = one_microbatch(per_example, given["loss_target"])
    else:
        def body(carry, xs):
            loss_sum, grad_sum = carry
            l_k, (gw_k, gx_k) = one_microbatch(xs[0], xs[1])
            with _jax.named_scope("update"):
                return (loss_sum + l_k, _jax.tree.map(_jnp.add, grad_sum, gw_k)), gx_k

        init = (_jnp.zeros((), _jnp.float32), _jax.tree.map(_jnp.zeros_like, weights))
        (loss, grad_w), grad_x = _jax.lax.scan(body, init, (per_example, given["loss_target"]))
    with _jax.named_scope("update"):
        delta_w, new_m, new_v = {}, {}, {}
        for n in TWIN_WEIGHTS:
            delta_w[n], new_m[n], new_v[n] = _adamw(weights[n], grad_w[n], given["m_" + n], given["v_" + n])
    return (loss, grad_x, *[grad_w[n] for n in TWIN_WEIGHTS], *[delta_w[n] for n in TWIN_WEIGHTS],
            *[new_m[n] for n in TWIN_WEIGHTS], *[new_v[n] for n in TWIN_WEIGHTS])
```

```python
import functools

import jax
import jax.numpy as jnp
from jax import lax
from jax.experimental import pallas as pl
from jax.experimental.pallas import tpu as pltpu

D = 1024
DEPTH = 4
N_A = 2
N_B = 2
WINS = (2, 4, 8, 16)
GC = 256
HD = 64
NH = 16
BLK = 128
F = 2816
F2 = 2 * F
EPS = 1e-5
SCALE = HD ** -0.5
NEG = -1e30
HALO = 16
TN = 256
LANES = 128
SUBLANES = 8
VMEM_LIMIT = 56 * 1024 * 1024
ACC_BYTES = 6 * 1024 * 1024
EW_BYTES = 1024 * 1024

LR, B1, B2, AEPS, WD, STEP = 0.001, 0.9, 0.999, 1e-08, 0.01, 10

MX = jnp.bfloat16
F32 = jnp.float32
MESH = pl.DeviceIdType.MESH


def _cp(n_axes=1, vmem=VMEM_LIMIT):
    return pltpu.CompilerParams(dimension_semantics=("arbitrary",) * n_axes, vmem_limit_bytes=vmem)


def _dot(a, b):
    return jnp.dot(a, b, preferred_element_type=F32)


def _dot_nt(a, b):
    return lax.dot_general(a, b, (((1,), (1,)), ((), ())), preferred_element_type=F32)


def _dot_tn(a, b):
    return lax.dot_general(a, b, (((0,), (0,)), ((), ())), preferred_element_type=F32)


def _rms_fwd(x, g):
    r = lax.rsqrt(jnp.mean(x * x, axis=-1, keepdims=True) + EPS)
    xh = x * r
    return xh * g, xh, r


def _rms_bwd(dh, xh, r, g):
    dxh = dh * g
    return r * (dxh - xh * jnp.mean(dxh * xh, axis=-1, keepdims=True))


def _row_tile(s, want):
    return min(s, want)


def _pool_pm(e, h, row, tm):
    out = []
    for gi, win in enumerate(WINS):
        cols = slice(gi * GC, (gi + 1) * GC)
        s = e[:, cols]
        sh = 1
        while sh < win:
            s = s + pltpu.roll(s, sh, 0)
            sh *= 2
        inv = 1.0 / jnp.minimum(row + 1, win).astype(F32)
        out.append(s[HALO:] * inv - h[:, cols])
    return out


def _pool_fwd(x, g, pw, ps, layer):
    S = x.shape[0]
    tm = _row_tile(S, 512)
    hb = tm // HALO

    def body(x_ref, xh_ref, g_ref, pw_ref, ps_ref, o_ref):
        i = pl.program_id(0)
        x = x_ref[...]
        gg = g_ref[...]
        h, _, _ = _rms_fwd(x, gg)
        hh, _, _ = _rms_fwd(xh_ref[...], gg)
        hh = jnp.where(i > 0, hh, 0.0)
        e = jnp.concatenate([hh, h], axis=0)
        row = i * tm + lax.broadcasted_iota(jnp.int32, (tm, 1), 0)
        pm = _pool_pm(e, h, row, tm)
        for gi in range(len(WINS)):
            cols = slice(gi * GC, (gi + 1) * GC)
            z = _dot(pm[gi].astype(MX), pw_ref[gi])
            o_ref[:, cols] = x[:, cols] + z * ps_ref[:, cols]

    return pl.pallas_call(
        body, name="pool_fwd",
        grid=(S // tm,),
        in_specs=[
            pl.BlockSpec((tm, D), lambda i: (i, 0)),
            pl.BlockSpec((HALO, D), lambda i: (jnp.maximum(i * hb - 1, 0), 0)),
            pl.BlockSpec((None, 1, D), lambda i: (layer, 0, 0)),
            pl.BlockSpec((None, 4, GC, GC), lambda i: (layer, 0, 0, 0)),
            pl.BlockSpec((None, 1, D), lambda i: (layer, 0, 0)),
        ],
        out_specs=pl.BlockSpec((tm, D), lambda i: (i, 0)),
        out_shape=jax.ShapeDtypeStruct((S, D), F32),
        compiler_params=_cp(),
    )(x, x, g, pw, ps)


def _pool_bwd(x, dy, g, pw, ps, layer):
    S = x.shape[0]
    tm = _row_tile(S, 256)
    hb = tm // HALO
    n_i = S // tm
    n_h = S // HALO

    def body(x_ref, xh_ref, dy_ref, dyn_ref, g_ref, pw_ref, ps_ref, dx_ref, dpw_ref, dps_ref, dg_ref):
        i = pl.program_id(0)

        @pl.when(i == 0)
        def _():
            dpw_ref[...] = jnp.zeros_like(dpw_ref)
            dps_ref[...] = jnp.zeros_like(dps_ref)
            dg_ref[...] = jnp.zeros_like(dg_ref)

        x = x_ref[...]
        gg = g_ref[...]
        ps = ps_ref[...]
        h, xh, r = _rms_fwd(x, gg)
        hh, _, _ = _rms_fwd(xh_ref[...], gg)
        hh = jnp.where(i > 0, hh, 0.0)
        e = jnp.concatenate([hh, h], axis=0)
        row = i * tm + lax.broadcasted_iota(jnp.int32, (tm, 1), 0)
        rown = (i + 1) * tm + lax.broadcasted_iota(jnp.int32, (HALO, 1), 0)
        pm = _pool_pm(e, h, row, tm)
        dy = dy_ref[...]
        dz = dy * ps
        dzn = jnp.where(i < n_i - 1, dyn_ref[...] * ps, 0.0)
        parts = []
        for gi, win in enumerate(WINS):
            cols = slice(gi * GC, (gi + 1) * GC)
            w = pw_ref[gi]
            pmb = pm[gi].astype(MX)
            z = _dot(pmb, w)
            dps_ref[:, cols] += jnp.sum(dy[:, cols] * z, axis=0, keepdims=True)
            dzb = dz[:, cols].astype(MX)
            dpw_ref[gi] += _dot_tn(pmb, dzb)
            dpm = _dot_nt(dzb, w)
            dpmn = _dot_nt(dzn[:, cols].astype(MX), w)
            q = dpm * (1.0 / jnp.minimum(row + 1, win).astype(F32))
            qn = dpmn * (1.0 / jnp.minimum(rown + 1, win).astype(F32))
            s = jnp.concatenate([q, qn], axis=0)
            sh = 1
            while sh < win:
                s = s + pltpu.roll(s, tm + HALO - sh, 0)
                sh *= 2
            parts.append(s[:tm] - dpm)
        dh = jnp.concatenate(parts, axis=1)
        dg_ref[...] += jnp.sum(dh * xh, axis=0, keepdims=True)
        dx_ref[...] = dy + _rms_bwd(dh, xh, r, gg)

    return pl.pallas_call(
        body, name="pool_bwd",
        grid=(n_i,),
        in_specs=[
            pl.BlockSpec((tm, D), lambda i: (i, 0)),
            pl.BlockSpec((HALO, D), lambda i: (jnp.maximum(i * hb - 1, 0), 0)),
            pl.BlockSpec((tm, D), lambda i: (i, 0)),
            pl.BlockSpec((HALO, D), lambda i: (jnp.minimum((i + 1) * hb, n_h - 1), 0)),
            pl.BlockSpec((None, 1, D), lambda i: (layer, 0, 0)),
            pl.BlockSpec((None, 4, GC, GC), lambda i: (layer, 0, 0, 0)),
            pl.BlockSpec((None, 1, D), lambda i: (layer, 0, 0)),
        ],
        out_specs=[
            pl.BlockSpec((tm, D), lambda i: (i, 0)),
            pl.BlockSpec((4, GC, GC), lambda i: (0, 0, 0)),
            pl.BlockSpec((1, D), lambda i: (0, 0)),
            pl.BlockSpec((1, D), lambda i: (0, 0)),
        ],
        out_shape=[
            jax.ShapeDtypeStruct((S, D), F32),
            jax.ShapeDtypeStruct((4, GC, GC), F32),
            jax.ShapeDtypeStruct((1, D), F32),
            jax.ShapeDtypeStruct((1, D), F32),
        ],
        compiler_params=_cp(),
    )(x, x, dy, dy, g, pw, ps)


def _conv(u, prev, cw, rowi):
    um1 = jnp.where(rowi == 0, prev[7:8], pltpu.roll(u, 1, 0))
    um2 = jnp.where(rowi == 0, prev[6:7], jnp.where(rowi == 1, prev[7:8], pltpu.roll(u, 2, 0)))
    return cw[0:1] * um2 + cw[1:2] * um1 + cw[2:3] * u, um1, um2


def _ffn_fwd(x, g, wup, wdn, cw, cb, layer):
    S = x.shape[0]
    tm = _row_tile(S, 512)

    def body(x_ref, g_ref, wup_hbm, wdn_hbm, cw_ref, cb_ref, o_ref, u_ref, wup_v, wdn_v, carry, acc):
        i = pl.program_id(0)

        @pl.when(i == 0)
        def _():
            pltpu.sync_copy(wup_hbm.at[layer], wup_v)
            pltpu.sync_copy(wdn_hbm.at[layer], wdn_v)
            carry[...] = jnp.zeros_like(carry)

        x = x_ref[...]
        h, _, _ = _rms_fwd(x, g_ref[...])
        hb = h.astype(MX)
        rowi = lax.broadcasted_iota(jnp.int32, (tm, 1), 0)
        for j in range(F // TN):
            cg = slice(j * TN, (j + 1) * TN)
            cv = slice(F + j * TN, F + (j + 1) * TN)
            ug = _dot(hb, wup_v[:, cg])
            uv = _dot(hb, wup_v[:, cv])
            u_ref[:, cg] = ug.astype(u_ref.dtype)
            u_ref[:, cv] = uv.astype(u_ref.dtype)
            gt, _, _ = _conv(ug, carry[:, cg], cw_ref[:, cg], rowi)
            vl, _, _ = _conv(uv, carry[:, cv], cw_ref[:, cv], rowi)
            carry[:, cg] = ug[tm - SUBLANES:]
            carry[:, cv] = uv[tm - SUBLANES:]
            gt = gt + cb_ref[:, cg]
            vl = vl + cb_ref[:, cv]
            a = gt * jax.nn.sigmoid(gt) * vl
            contrib = _dot(a.astype(MX), wdn_v[cg, :])
            if j == 0:
                acc[...] = contrib
            else:
                acc[...] += contrib
        o_ref[...] = x + acc[...]

    return pl.pallas_call(
        body, name="ffn_fwd",
        grid=(S // tm,),
        in_specs=[
            pl.BlockSpec((tm, D), lambda i: (i, 0)),
            pl.BlockSpec((None, 1, D), lambda i: (layer, 0, 0)),
            pl.BlockSpec(memory_space=pl.ANY),
            pl.BlockSpec(memory_space=pl.ANY),
            pl.BlockSpec((None, 3, F2), lambda i: (layer, 0, 0)),
            pl.BlockSpec((None, 1, F2), lambda i: (layer, 0, 0)),
        ],
        out_specs=[
            pl.BlockSpec((tm, D), lambda i: (i, 0)),
            pl.BlockSpec((tm, F2), lambda i: (i, 0)),
        ],
        out_shape=[
            jax.ShapeDtypeStruct((S, D), F32),
            jax.ShapeDtypeStruct((S, F2), MX),
        ],
        scratch_shapes=[
            pltpu.VMEM((D, F2), MX),
            pltpu.VMEM((F, D), MX),
            pltpu.VMEM((SUBLANES, F2), F32),
            pltpu.VMEM((tm, D), F32),
        ],
        compiler_params=_cp(),
    )(x, g, wup, wdn, cw, cb)


def _ffn_bwd(x, dy, u, g, wup, wdn, cw, cb, layer):
    S = x.shape[0]
    tm = _row_tile(S, 256)
    n_i = S // tm
    hb_per = tm // HALO

    def body(x_ref, dy_ref, u_ref, uh_ref, g_ref, wup_hbm, wdn_hbm, cw_ref, cb_ref,
             dx_ref, du_ref, a_ref, h_ref, dg_ref, dcw_ref, dcb_ref, wup_v, wdn_v, carry, acc):
        i = pl.program_id(0)
        t = n_i - 1 - i

        @pl.when(i == 0)
        def _():
            pltpu.sync_copy(wup_hbm.at[layer], wup_v)
            pltpu.sync_copy(wdn_hbm.at[layer], wdn_v)
            carry[...] = jnp.zeros_like(carry)
            dg_ref[...] = jnp.zeros_like(dg_ref)
            dcw_ref[...] = jnp.zeros_like(dcw_ref)
            dcb_ref[...] = jnp.zeros_like(dcb_ref)

        x = x_ref[...]
        gg = g_ref[...]
        h, xh, r = _rms_fwd(x, gg)
        h_ref[...] = h.astype(h_ref.dtype)
        dy = dy_ref[...]
        dyb = dy.astype(MX)
        rowi = lax.broadcasted_iota(jnp.int32, (tm, 1), 0)
        for j in range(F // TN):
            cg = slice(j * TN, (j + 1) * TN)
            cv = slice(F + j * TN, F + (j + 1) * TN)
            res = []
            for cc in (cg, cv):
                uu = u_ref[:, cc].astype(F32)
                prev = jnp.where(t > 0, uh_ref[:, cc].astype(F32)[HALO - SUBLANES:], 0.0)
                res.append((uu,) + _conv(uu, prev, cw_ref[:, cc], rowi))
            (ug, gt, ug1, ug2), (uv, vl, uv1, uv2) = res
            gt = gt + cb_ref[:, cg]
            vl = vl + cb_ref[:, cv]
            sg = jax.nn.sigmoid(gt)
            sil = gt * sg
            a_ref[:, cg] = (sil * vl).astype(a_ref.dtype)
            da = _dot_nt(dyb, wdn_v[cg, :])
            dvl = da * sil
            dgt = da * vl * (sg * (1.0 + gt * (1.0 - sg)))
            for cc, dd, u0, u1, u2 in ((cg, dgt, ug, ug1, ug2), (cv, dvl, uv, uv1, uv2)):
                dcb_ref[:, cc] += jnp.sum(dd, axis=0, keepdims=True)
                dcw_ref[0:1, cc] += jnp.sum(dd * u2, axis=0, keepdims=True)
                dcw_ref[1:2, cc] += jnp.sum(dd * u1, axis=0, keepdims=True)
                dcw_ref[2:3, cc] += jnp.sum(dd * u0, axis=0, keepdims=True)
                nxt = carry[:, cc]
                dp1 = jnp.where(rowi == tm - 1, nxt[0:1], pltpu.roll(dd, tm - 1, 0))
                dp2 = jnp.where(rowi == tm - 1, nxt[1:2],
                                jnp.where(rowi == tm - 2, nxt[0:1], pltpu.roll(dd, tm - 2, 0)))
                cwc = cw_ref[:, cc]
                duu = (cwc[2:3] * dd + cwc[1:2] * dp1 + cwc[0:1] * dp2).astype(MX)
                carry[:, cc] = dd[0:SUBLANES]
                du_ref[:, cc] = duu.astype(du_ref.dtype)
                contrib = _dot_nt(duu, wup_v[:, cc])
                if j == 0 and cc is cg:
                    acc[...] = contrib
                else:
                    acc[...] += contrib
        dh = acc[...]
        dg_ref[...] += jnp.sum(dh * xh, axis=0, keepdims=True)
        dx_ref[...] = dy + _rms_bwd(dh, xh, r, gg)

    rev = lambda i: (n_i - 1 - i, 0)
    return pl.pallas_call(
        body, name="ffn_bwd",
        grid=(n_i,),
        in_specs=[
            pl.BlockSpec((tm, D), rev),
            pl.BlockSpec((tm, D), rev),
            pl.BlockSpec((tm, F2), rev),
            pl.BlockSpec((HALO, F2), lambda i: (jnp.maximum((n_i - 1 - i) * hb_per - 1, 0), 0)),
            pl.BlockSpec((None, 1, D), lambda i: (layer, 0, 0)),
            pl.BlockSpec(memory_space=pl.ANY),
            pl.BlockSpec(memory_space=pl.ANY),
            pl.BlockSpec((None, 3, F2), lambda i: (layer, 0, 0)),
            pl.BlockSpec((None, 1, F2), lambda i: (layer, 0, 0)),
        ],
        out_specs=[
            pl.BlockSpec((tm, D), rev),
            pl.BlockSpec((tm, F2), rev),
            pl.BlockSpec((tm, F), rev),
            pl.BlockSpec((tm, D), rev),
            pl.BlockSpec((1, D), lambda i: (0, 0)),
            pl.BlockSpec((3, F2), lambda i: (0, 0)),
            pl.BlockSpec((1, F2), lambda i: (0, 0)),
        ],
        out_shape=[
            jax.ShapeDtypeStruct((S, D), F32),
            jax.ShapeDtypeStruct((S, F2), MX),
            jax.ShapeDtypeStruct((S, F), MX),
            jax.ShapeDtypeStruct((S, D), MX),
            jax.ShapeDtypeStruct((1, D), F32),
            jax.ShapeDtypeStruct((3, F2), F32),
            jax.ShapeDtypeStruct((1, F2), F32),
        ],
        scratch_shapes=[
            pltpu.VMEM((D, F2), MX),
            pltpu.VMEM((F, D), MX),
            pltpu.VMEM((SUBLANES, F2), F32),
            pltpu.VMEM((tm, D), F32),
        ],
        compiler_params=_cp(),
    )(x, dy, u, u, g, wup, wdn, cw, cb)


def _tn_matmul(a, b, out_dtype, name):
    S, M = a.shape
    N = b.shape[1]
    bn = N
    while M * bn * 4 > ACC_BYTES and bn % (2 * LANES) == 0:
        bn //= 2
    bk = _row_tile(S, 512)
    nk = S // bk

    def body(a_ref, b_ref, o_ref, acc):
        k = pl.program_id(1)
        p = _dot_tn(a_ref[...].astype(MX), b_ref[...].astype(MX))

        @pl.when(k == 0)
        def _():
            acc[...] = p

        @pl.when(k > 0)
        def _():
            acc[...] += p

        @pl.when(k == nk - 1)
        def _():
            o_ref[...] = acc[...].astype(o_ref.dtype)

    return pl.pallas_call(
        body, name=name,
        grid=(N // bn, nk),
        in_specs=[
            pl.BlockSpec((bk, M), lambda j, k: (k, 0)),
            pl.BlockSpec((bk, bn), lambda j, k: (k, j)),
        ],
        out_specs=pl.BlockSpec((M, bn), lambda j, k: (0, j)),
        out_shape=jax.ShapeDtypeStruct((M, N), out_dtype),
        scratch_shapes=[pltpu.VMEM((M, bn), F32)],
        compiler_params=_cp(2),
    )(a, b)


def _rms_linear(x, g, w, b, g_layer, w_layer, name):
    S = x.shape[0]
    N = w.shape[-1]
    tm = _row_tile(S, 512)

    def body(x_ref, g_ref, w_ref, b_ref, o_ref):
        h, _, _ = _rms_fwd(x_ref[...], g_ref[...])
        o_ref[...] = (_dot(h.astype(MX), w_ref[...]) + b_ref[...]).astype(o_ref.dtype)

    return pl.pallas_call(
        body, name=name,
        grid=(S // tm,),
        in_specs=[
            pl.BlockSpec((tm, D), lambda i: (i, 0)),
            pl.BlockSpec((None, 1, D), lambda i: (g_layer, 0, 0)),
            pl.BlockSpec((None, D, N), lambda i: (w_layer, 0, 0)),
            pl.BlockSpec((None, 1, N), lambda i: (w_layer, 0, 0)),
        ],
        out_specs=pl.BlockSpec((tm, N), lambda i: (i, 0)),
        out_shape=jax.ShapeDtypeStruct((S, N), MX),
        compiler_params=_cp(),
    )(x, g, w, b)


def _linear_res(o, w, b, xres, layer):
    S = o.shape[0]
    tm = _row_tile(S, 512)

    def body(o_ref, w_ref, b_ref, x_ref, y_ref):
        y_ref[...] = x_ref[...] + _dot(o_ref[...], w_ref[...]) + b_ref[...]

    return pl.pallas_call(
        body, name="o_proj",
        grid=(S // tm,),
        in_specs=[
            pl.BlockSpec((tm, D), lambda i: (i, 0)),
            pl.BlockSpec((None, D, D), lambda i: (layer, 0, 0)),
            pl.BlockSpec((None, 1, D), lambda i: (layer, 0, 0)),
            pl.BlockSpec((tm, D), lambda i: (i, 0)),
        ],
        out_specs=pl.BlockSpec((tm, D), lambda i: (i, 0)),
        out_shape=jax.ShapeDtypeStruct((S, D), F32),
        compiler_params=_cp(),
    )(o, w, b, xres)


def _linear_nt(dy, w, layer):
    S = dy.shape[0]
    tm = _row_tile(S, 512)

    def body(dy_ref, w_ref, o_ref, db_ref):
        @pl.when(pl.program_id(0) == 0)
        def _():
            db_ref[...] = jnp.zeros_like(db_ref)

        dy = dy_ref[...]
        db_ref[...] += jnp.sum(dy, axis=0, keepdims=True)
        o_ref[...] = _dot_nt(dy.astype(MX), w_ref[...]).astype(o_ref.dtype)

    return pl.pallas_call(
        body, name="o_proj_bwd",
        grid=(S // tm,),
        in_specs=[
            pl.BlockSpec((tm, D), lambda i: (i, 0)),
            pl.BlockSpec((None, D, D), lambda i: (layer, 0, 0)),
        ],
        out_specs=[
            pl.BlockSpec((tm, D), lambda i: (i, 0)),
            pl.BlockSpec((1, D), lambda i: (0, 0)),
        ],
        out_shape=[
            jax.ShapeDtypeStruct((S, D), MX),
            jax.ShapeDtypeStruct((1, D), F32),
        ],
        compiler_params=_cp(),
    )(dy, w)


def _rms_linear_bwd(x, g, dzs, w, dy, g_layer, w_layer, name):
    S = x.shape[0]
    N = w.shape[-1]
    tm = _row_tile(S, 512)
    nz = len(dzs)

    def body(*refs):
        x_ref, g_ref = refs[0], refs[1]
        dz_refs = refs[2:2 + nz]
        w_ref, dy_ref, dx_ref, dg_ref, db_ref, h_ref, dzb_ref = refs[2 + nz:]

        @pl.when(pl.program_id(0) == 0)
        def _():
            dg_ref[...] = jnp.zeros_like(dg_ref)
            db_ref[...] = jnp.zeros_like(db_ref)

        gg = g_ref[...]
        h, xh, r = _rms_fwd(x_ref[...], gg)
        h_ref[...] = h.astype(h_ref.dtype)
        dz = dz_refs[0][...].astype(F32)
        for zr in dz_refs[1:]:
            dz = dz + zr[...].astype(F32)
        db_ref[...] += jnp.sum(dz, axis=0, keepdims=True)
        dzb = dz.astype(MX)
        dzb_ref[...] = dzb
        dh = _dot_nt(dzb, w_ref[...])
        dg_ref[...] += jnp.sum(dh * xh, axis=0, keepdims=True)
        dx_ref[...] = dy_ref[...] + _rms_bwd(dh, xh, r, gg)

    return pl.pallas_call(
        body, name=name,
        grid=(S // tm,),
        in_specs=[
            pl.BlockSpec((tm, D), lambda i: (i, 0)),
            pl.BlockSpec((None, 1, D), lambda i: (g_layer, 0, 0)),
        ] + [pl.BlockSpec((tm, N), lambda i: (i, 0))] * nz + [
            pl.BlockSpec((None, D, N), lambda i: (w_layer, 0, 0)),
            pl.BlockSpec((tm, D), lambda i: (i, 0)),
        ],
        out_specs=[
            pl.BlockSpec((tm, D), lambda i: (i, 0)),
            pl.BlockSpec((1, D), lambda i: (0, 0)),
            pl.BlockSpec((1, N), lambda i: (0, 0)),
            pl.BlockSpec((tm, D), lambda i: (i, 0)),
            pl.BlockSpec((tm, N), lambda i: (i, 0)),
        ],
        out_shape=[
            jax.ShapeDtypeStruct((S, D), F32),
            jax.ShapeDtypeStruct((1, D), F32),
            jax.ShapeDtypeStruct((1, N), F32),
            jax.ShapeDtypeStruct((S, D), MX),
            jax.ShapeDtypeStruct((S, N), MX),
        ],
        compiler_params=_cp(),
    )(x, g, *dzs, w, dy)


def _loss_head(x, g, tgt):
    S = x.shape[0]
    tm = _row_tile(S, 512)

    def body(x_ref, g_ref, t_ref, dx_ref, dg_ref, l_ref):
        @pl.when(pl.program_id(0) == 0)
        def _():
            dg_ref[...] = jnp.zeros_like(dg_ref)
            l_ref[...] = jnp.zeros_like(l_ref)

        gg = g_ref[...]
        y, xh, r = _rms_fwd(x_ref[...], gg)
        err = y - t_ref[...]
        tok = jnp.sum(err * err, axis=-1, keepdims=True) * (1.0 / D)
        l_ref[...] += 0.5 * jnp.sum(tok, axis=0, keepdims=True)
        dyv = err * (1.0 / D)
        dg_ref[...] += jnp.sum(dyv * xh, axis=0, keepdims=True)
        dx_ref[...] = _rms_bwd(dyv, xh, r, gg)

    return pl.pallas_call(
        body, name="loss_head",
        grid=(S // tm,),
        in_specs=[
            pl.BlockSpec((tm, D), lambda i: (i, 0)),
            pl.BlockSpec((1, D), lambda i: (0, 0)),
            pl.BlockSpec((tm, D), lambda i: (i, 0)),
        ],
        out_specs=[
            pl.BlockSpec((tm, D), lambda i: (i, 0)),
            pl.BlockSpec((1, D), lambda i: (0, 0)),
            pl.BlockSpec((1, LANES), lambda i: (0, 0)),
        ],
        out_shape=[
            jax.ShapeDtypeStruct((S, D), F32),
            jax.ShapeDtypeStruct((1, D), F32),
            jax.ShapeDtypeStruct((1, LANES), F32),
        ],
        compiler_params=_cp(),
    )(x, g, tgt)


def _attn_setup(kvp_ref, kvc_ref, n):
    kw = jnp.concatenate([kvp_ref[...], kvc_ref[...]], axis=0).astype(F32)
    kk, vv = kw[:, :LANES], kw[:, LANES:]
    lo = lax.broadcasted_iota(jnp.int32, (1, LANES), 1) < HD
    kr, vr = pltpu.roll(kk, HD, 1), pltpu.roll(vv, HD, 1)
    ks = [jnp.where(lo, kk, kr).astype(MX), jnp.where(lo, kr, kk).astype(MX)]
    vs = [jnp.where(lo, vv, vr).astype(MX), jnp.where(lo, vr, vv).astype(MX)]
    qi = lax.broadcasted_iota(jnp.int32, (BLK, 2 * BLK), 0)
    si = lax.broadcasted_iota(jnp.int32, (BLK, 2 * BLK), 1)
    mask = (si > qi) & (si <= qi + BLK) & jnp.logical_or(n > 0, si >= BLK)
    return ks, vs, lo, mask


def _attn_probs(qm, kg, mask, sink):
    s = jnp.where(mask, _dot_nt(qm, kg) * SCALE, NEG)
    m = jnp.maximum(jnp.max(s, axis=-1, keepdims=True), sink)
    p = jnp.exp(s - m)
    es = jnp.exp(sink - m)
    inv = 1.0 / (jnp.sum(p, axis=-1, keepdims=True) + es)
    return p * inv, es * inv


def _attn_specs(n_extra_q):
    q_spec = pl.BlockSpec((BLK, D), lambda n: (n, 0))
    return [q_spec] * n_extra_q + [
        pl.BlockSpec((BLK, 4 * HD), lambda n: (jnp.maximum(n - 1, 0), 0)),
        pl.BlockSpec((BLK, 4 * HD), lambda n: (n, 0)),
        pl.BlockSpec(memory_space=pltpu.SMEM),
    ]


def _attn_fwd(q, kv, sinks, layer):
    S = q.shape[0]

    def body(q_ref, kvp_ref, kvc_ref, sk_ref, o_ref):
        n = pl.program_id(0)
        ks, vs, lo, mask = _attn_setup(kvp_ref, kvc_ref, n)
        for j in range(NH // 2):
            cols = slice(j * LANES, (j + 1) * LANES)
            grp = (2 * j) // (NH // 2)
            qp = q_ref[:, cols]
            outs = []
            for half in range(2):
                sel = lo if half == 0 else jnp.logical_not(lo)
                qm = jnp.where(sel, qp, jnp.zeros_like(qp))
                pr, _ = _attn_probs(qm, ks[grp], mask, sk_ref[layer, 2 * j + half])
                outs.append(_dot(pr.astype(MX), vs[grp]))
            o_ref[:, cols] = jnp.where(lo, outs[0], outs[1]).astype(o_ref.dtype)

    return pl.pallas_call(
        body, name="attn_fwd",
        grid=(S // BLK,),
        in_specs=_attn_specs(1),
        out_specs=pl.BlockSpec((BLK, D), lambda n: (n, 0)),
        out_shape=jax.ShapeDtypeStruct((S, D), MX),
        compiler_params=_cp(),
    )(q, kv, kv, sinks)


def _attn_bwd(q, do, kv, sinks, layer):
    S = q.shape[0]

    def body(q_ref, do_ref, kvp_ref, kvc_ref, sk_ref, dq_ref, dkv_ref, dsk_ref):
        n = pl.program_id(0)

        @pl.when(n == 0)
        def _():
            dkv_ref[...] = jnp.zeros_like(dkv_ref)
            dsk_ref[...] = jnp.zeros_like(dsk_ref)

        ks, vs, lo, mask = _attn_setup(kvp_ref, kvc_ref, n)
        lane = lax.broadcasted_iota(jnp.int32, (1, LANES), 1)
        dk_acc = [jnp.zeros((2 * BLK, LANES), F32) for _ in range(2)]
        dv_acc = [jnp.zeros((2 * BLK, LANES), F32) for _ in range(2)]
        dsk = jnp.zeros((1, LANES), F32)
        for j in range(NH // 2):
            cols = slice(j * LANES, (j + 1) * LANES)
            grp = (2 * j) // (NH // 2)
            qp = q_ref[:, cols]
            dop = do_ref[:, cols]
            dqs = []
            for half in range(2):
                sel = lo if half == 0 else jnp.logical_not(lo)
                qm = jnp.where(sel, qp, jnp.zeros_like(qp))
                dom = jnp.where(sel, dop, jnp.zeros_like(dop))
                pr, psink = _attn_probs(qm, ks[grp], mask, sk_ref[layer, 2 * j + half])
                dpr = _dot_nt(dom, vs[grp])
                delta = jnp.sum(pr * dpr, axis=-1, keepdims=True)
                ds = (pr * (dpr - delta) * SCALE).astype(MX)
                dsk = dsk + jnp.where(lane == 2 * j + half,
                                      -jnp.sum(psink * delta, axis=0, keepdims=True), 0.0)
                dqs.append(_dot(ds, ks[grp]))
                dk_acc[grp] = dk_acc[grp] + _dot_tn(ds, qm)
                dv_acc[grp] = dv_acc[grp] + _dot_tn(pr.astype(MX), dom)
            dq_ref[:, cols] = jnp.where(lo, dqs[0], dqs[1]).astype(dq_ref.dtype)
        dsk_ref[...] += dsk
        tk = [a + pltpu.roll(a, HD, 1) for a in dk_acc]
        tv = [a + pltpu.roll(a, HD, 1) for a in dv_acc]
        contrib = jnp.concatenate([jnp.where(lo, tk[0], tk[1]), jnp.where(lo, tv[0], tv[1])], axis=1)

        @pl.when(n > 0)
        def _():
            rows = pl.ds(pl.multiple_of((n - 1) * BLK, BLK), 2 * BLK)
            dkv_ref[rows, :] += contrib

        @pl.when(n == 0)
        def _():
            dkv_ref[0:BLK, :] += contrib[BLK:]

    return pl.pallas_call(
        body, name="attn_bwd",
        grid=(S // BLK,),
        in_specs=_attn_specs(2),
        out_specs=[
            pl.BlockSpec((BLK, D), lambda n: (n, 0)),
            pl.BlockSpec((S, 4 * HD), lambda n: (0, 0)),
            pl.BlockSpec((1, LANES), lambda n: (0, 0)),
        ],
        out_shape=[
            jax.ShapeDtypeStruct((S, D), MX),
            jax.ShapeDtypeStruct((S, 4 * HD), F32),
            jax.ShapeDtypeStruct((1, LANES), F32),
        ],
        compiler_params=_cp(),
    )(q, do, kv, kv, sinks)


def _ew_rows(rows, cols, n_bufs=1):
    br = rows
    while br * cols * 4 * n_bufs > EW_BYTES and br % (2 * SUBLANES) == 0:
        br //= 2
    return br


def _sum_slots(r):
    _, R, C = r.shape
    br = _ew_rows(R, C, 2)

    def body(r_ref, o_ref):
        o_ref[...] = ((r_ref[0].astype(F32) + r_ref[1].astype(F32)) + r_ref[2].astype(F32)) + r_ref[3].astype(F32)

    return pl.pallas_call(
        body, name="sum_slots",
        grid=(R // br,),
        in_specs=[pl.BlockSpec((4, br, C), lambda i: (0, i, 0))],
        out_specs=pl.BlockSpec((br, C), lambda i: (i, 0)),
        out_shape=jax.ShapeDtypeStruct((R, C), F32),
        compiler_params=_cp(),
    )(r)


def _adamw(parts, w, m, v):
    R, C = w.shape
    br = _ew_rows(R, C)
    npart = len(parts)

    def body(*refs):
        p_refs = refs[:npart]
        w_ref, m_ref, v_ref, g_ref, d_ref, nm_ref, nv_ref = refs[npart:]
        g = p_refs[0][...]
        for pr in p_refs[1:]:
            g = g + pr[...]
        nm = B1 * m_ref[...] + (1.0 - B1) * g
        nv = B2 * v_ref[...] + (1.0 - B2) * (g * g)
        m_hat = nm / (1.0 - B1 ** STEP)
        v_hat = nv / (1.0 - B2 ** STEP)
        g_ref[...] = g
        d_ref[...] = -LR * (m_hat / (jnp.sqrt(v_hat) + AEPS) + WD * w_ref[...])
        nm_ref[...] = nm
        nv_ref[...] = nv

    spec = pl.BlockSpec((br, C), lambda i: (i, 0))
    return pl.pallas_call(
        body, name="adamw",
        grid=(R // br,),
        in_specs=[spec] * (npart + 3),
        out_specs=[spec] * 4,
        out_shape=[jax.ShapeDtypeStruct((R, C), F32)] * 4,
        compiler_params=_cp(),
    )(*parts, w, m, v)


def _coords():
    return lax.axis_index("x"), lax.axis_index("y"), lax.axis_index("c")


def _other_chips(x, y):
    return [(1 - x, y), (x, 1 - y), (1 - x, 1 - y)]


def _slot(ref, axis, chip, size):
    idx = [slice(None)] * 3
    idx[axis] = pl.ds(pl.multiple_of(chip * size, size), size)
    return ref.at[tuple(idx)]


def _all_gather_chips(shards, axes):
    nt = len(shards)

    def body(*refs):
        ins, outs = refs[:nt], refs[nt:2 * nt]
        send, recv, local = refs[2 * nt:]
        x, y, c = _coords()
        me = 2 * x + y
        chips = _other_chips(x, y)
        sizes = [ins[t].shape[axes[t]] for t in range(nt)]
        mine, pushes = [], []
        for t in range(nt):
            cp = pltpu.make_async_copy(ins[t], _slot(outs[t], axes[t], me, sizes[t]), local.at[t])
            cp.start()
            mine.append(cp)
            for k, (px, py) in enumerate(chips):
                rc = pltpu.make_async_remote_copy(
                    src_ref=ins[t], dst_ref=_slot(outs[t], axes[t], me, sizes[t]),
                    send_sem=send.at[t, k], recv_sem=recv.at[t, k],
                    device_id=(px, py, c), device_id_type=MESH)
                rc.start()
                pushes.append(rc)
        for t in range(nt):
            for k, (px, py) in enumerate(chips):
                pltpu.make_async_remote_copy(
                    src_ref=ins[t], dst_ref=_slot(outs[t], axes[t], 2 * px + py, sizes[t]),
                    send_sem=send.at[t, k], recv_sem=recv.at[t, k],
                    device_id=(px, py, c), device_id_type=MESH).wait_recv()
        for rc in pushes:
            rc.wait_send()
        for cp in mine:
            cp.wait()

    out_shape = []
    for s, ax in zip(shards, axes):
        shp = list(s.shape)
        shp[ax] *= 4
        out_shape.append(jax.ShapeDtypeStruct(tuple(shp), s.dtype))
    any_spec = pl.BlockSpec(memory_space=pl.ANY)
    return pl.pallas_call(
        body, name="gather_weights",
        in_specs=[any_spec] * nt,
        out_specs=[any_spec] * nt,
        out_shape=out_shape,
        scratch_shapes=[
            pltpu.SemaphoreType.DMA((nt, 3)),
            pltpu.SemaphoreType.DMA((nt, 3)),
            pltpu.SemaphoreType.DMA((nt,)),
        ],
    )(*shards)


def _scatter_grads(grads, axes, groups):
    ng = len(grads)
    nw = len(groups)

    def body(*refs):
        ins, outs = refs[:ng], refs[ng:ng + nw]
        send, recv, local = refs[ng + nw:]
        x, y, c = _coords()
        me = 2 * x + y
        chips = _other_chips(x, y)
        mine, pushes = [], []
        for w, members in enumerate(groups):
            for li, t in enumerate(members):
                size = ins[t].shape[axes[t]] // 4
                cp = pltpu.make_async_copy(_slot(ins[t], axes[t], me, size), outs[w].at[me, li], local.at[t])
                cp.start()
                mine.append(cp)
                for k, (px, py) in enumerate(chips):
                    rc = pltpu.make_async_remote_copy(
                        src_ref=_slot(ins[t], axes[t], 2 * px + py, size), dst_ref=outs[w].at[me, li],
                        send_sem=send.at[t, k], recv_sem=recv.at[t, k],
                        device_id=(px, py, c), device_id_type=MESH)
                    rc.start()
                    pushes.append(rc)
        for w, members in enumerate(groups):
            for li, t in enumerate(members):
                size = ins[t].shape[axes[t]] // 4
                for k, (px, py) in enumerate(chips):
                    pltpu.make_async_remote_copy(
                        src_ref=_slot(ins[t], axes[t], me, size), dst_ref=outs[w].at[2 * px + py, li],
                        send_sem=send.at[t, k], recv_sem=recv.at[t, k],
                        device_id=(px, py, c), device_id_type=MESH).wait_recv()
        for rc in pushes:
            rc.wait_send()
        for cp in mine:
            cp.wait()

    out_shape = []
    for members in groups:
        t = members[0]
        shp = list(grads[t].shape)
        shp[axes[t]] //= 4
        a, b, cdim = shp
        out_shape.append(jax.ShapeDtypeStruct((4, len(members), a, b, cdim), grads[t].dtype))
    any_spec = pl.BlockSpec(memory_space=pl.ANY)

    def call(*g):
        return pl.pallas_call(
            body, name="scatter_grads",
            in_specs=[any_spec] * ng,
            out_specs=[any_spec] * nw,
            out_shape=out_shape,
            scratch_shapes=[
                pltpu.SemaphoreType.DMA((ng, 3)),
                pltpu.SemaphoreType.DMA((ng, 3)),
                pltpu.SemaphoreType.DMA((ng,)),
            ],
        )(*g)

    return call(*grads)


def _swap_with_sibling(arrs):
    na = len(arrs)

    def body(*refs):
        ins, outs = refs[:na], refs[na:2 * na]
        send, recv = refs[2 * na:]
        x, y, c = _coords()
        cps = []
        for t in range(na):
            rc = pltpu.make_async_remote_copy(
                src_ref=ins[t], dst_ref=outs[t], send_sem=send.at[t], recv_sem=recv.at[t],
                device_id=(x, y, 1 - c), device_id_type=MESH)
            rc.start()
            cps.append(rc)
        for rc in cps:
            rc.wait()

    any_spec = pl.BlockSpec(memory_space=pl.ANY)
    return pl.pallas_call(
        body, name="swap_sibling",
        in_specs=[any_spec] * na,
        out_specs=[any_spec] * na,
        out_shape=[jax.ShapeDtypeStruct(a.shape, a.dtype) for a in arrs],
        scratch_shapes=[pltpu.SemaphoreType.DMA((na,)), pltpu.SemaphoreType.DMA((na,))],
    )(*arrs)


def _all_reduce_small(v):
    R = v.shape[0]

    def body(v_ref, o_ref, buf, send, recv, local):
        x, y, c = _coords()
        me = 4 * x + 2 * y + c
        cp = pltpu.make_async_copy(v_ref, buf.at[me], local)
        cp.start()
        pushes = []
        for k in range(1, 8):
            peer = (x ^ (k >> 2), y ^ ((k >> 1) & 1), c ^ (k & 1))
            rc = pltpu.make_async_remote_copy(
                src_ref=v_ref, dst_ref=buf.at[me], send_sem=send.at[k - 1], recv_sem=recv.at[k - 1],
                device_id=peer, device_id_type=MESH)
            rc.start()
            pushes.append(rc)
        for k in range(1, 8):
            px, py, pc = x ^ (k >> 2), y ^ ((k >> 1) & 1), c ^ (k & 1)
            pltpu.make_async_remote_copy(
                src_ref=v_ref, dst_ref=buf.at[4 * px + 2 * py + pc], send_sem=send.at[k - 1],
                recv_sem=recv.at[k - 1], device_id=(px, py, pc), device_id_type=MESH).wait_recv()
        for rc in pushes:
            rc.wait_send()
        cp.wait()
        tot = buf[0]
        for k in range(1, 8):
            tot = tot + buf[k]
        o_ref[...] = tot

    vm = pl.BlockSpec(memory_space=pltpu.VMEM)
    return pl.pallas_call(
        body, name="all_reduce_small",
        in_specs=[vm],
        out_specs=vm,
        out_shape=jax.ShapeDtypeStruct((R, LANES), F32),
        scratch_shapes=[
            pltpu.VMEM((8, R, LANES), F32),
            pltpu.SemaphoreType.DMA((7,)),
            pltpu.SemaphoreType.DMA((7,)),
            pltpu.SemaphoreType.DMA,
        ],
        compiler_params=pltpu.CompilerParams(vmem_limit_bytes=VMEM_LIMIT),
    )(v)


def _pack(arrs):
    flat = []
    for a in arrs:
        f = a.reshape(-1).astype(F32)
        flat.append(jnp.pad(f, (0, (-f.shape[0]) % LANES)))
    v = jnp.concatenate(flat)
    v = jnp.pad(v, (0, (-v.shape[0]) % (SUBLANES * LANES)))
    return v.reshape(-1, LANES)


def _unpack(v, shapes):
    flat = v.reshape(-1)
    out, off = [], 0
    for shp in shapes:
        n = 1
        for d in shp:
            n *= d
        out.append(flat[off:off + n].reshape(shp))
        off += n + (-n) % LANES
    return out


def _local_step(x, tgt, p):
    n1, n2 = p["norm1_g"], p["norm2_g"]
    saved = []
    xs = x
    kv = None
    for l in range(DEPTH):
        x_in = xs
        if l < N_A:
            xa = _pool_fwd(x_in, n1, p["pool_w"], p["pool_scale"], l)
            q = o = None
        else:
            j = l - N_A
            q = _rms_linear(x_in, n1, p["w_q"], p["b_q"], l, j, "q_proj")
            o = _attn_fwd(q, kv, p["sinks"], j)
            xa = _linear_res(o, p["w_o"], p["b_o"], x_in, j)
        xs, u = _ffn_fwd(xa, n2, p["ffn_up"], p["ffn_down"], p["ffn_conv_w"], p["ffn_conv_b"], l)
        saved.append((x_in, xa, u, q, o))
        if l == N_A - 1:
            kv = _rms_linear(xs, p["kv_norm_g"], p["w_kv"], p["b_kv"], 0, 0, "kv_proj")
            x_kv = xs

    dx, d_final_g, loss = _loss_head(xs, p["final_g"], tgt)

    g = {k: [None] * DEPTH for k in ("norm1_g", "norm2_g", "ffn_up", "ffn_conv_w", "ffn_conv_b", "ffn_down")}
    for k in ("pool_w", "pool_scale", "w_q", "b_q", "sinks", "w_o", "b_o"):
        g[k] = [None] * N_A
    g["final_g"] = d_final_g
    dkvs = []
    for l in reversed(range(DEPTH)):
        x_in, xa, u, q, o = saved[l]
        dxa, du, a, hb, g["norm2_g"][l], g["ffn_conv_w"][l], g["ffn_conv_b"][l] = _ffn_bwd(
            xa, dx, u, n2, p["ffn_up"], p["ffn_down"], p["ffn_conv_w"], p["ffn_conv_b"], l)
        g["ffn_up"][l] = _tn_matmul(hb, du, MX, "d_ffn_up")
        g["ffn_down"][l] = _tn_matmul(a, dx, MX, "d_ffn_down")
        if l < N_A:
            dx, g["pool_w"][l], g["pool_scale"][l], g["norm1_g"][l] = _pool_bwd(
                x_in, dxa, n1, p["pool_w"], p["pool_scale"], l)
        else:
            j = l - N_A
            d_o, g["b_o"][j] = _linear_nt(dxa, p["w_o"], j)
            g["w_o"][j] = _tn_matmul(o, dxa, MX, "d_w_o")
            dq, dkv, g["sinks"][j] = _attn_bwd(q, d_o, kv, p["sinks"], j)
            dkvs.append(dkv)
            dx, g["norm1_g"][l], g["b_q"][j], hq, dqb = _rms_linear_bwd(
                x_in, n1, [dq], p["w_q"], dxa, l, j, "q_proj_bwd")
            g["w_q"][j] = _tn_matmul(hq, dqb, MX, "d_w_q")
        if l == N_A:
            dx, g["kv_norm_g"], g["b_kv"], hk, dkvb = _rms_linear_bwd(
                x_kv, p["kv_norm_g"], dkvs, p["w_kv"], dx, 0, 0, "kv_proj_bwd")
            g["w_kv"] = [_tn_matmul(hk, dkvb, MX, "d_w_kv")]
    return loss, dx, g


SMALL = ("norm1_g", "norm2_g", "kv_norm_g", "b_kv", "b_q", "sinks", "b_o", "ffn_conv_b", "final_g")
SMALL_SHARDED = ("pool_scale", "ffn_conv_w")
BIG = ("pool_w", "w_kv", "w_q", "w_o", "ffn_up", "ffn_down")
ORDER = ("norm1_g", "norm2_g", "pool_w", "pool_scale", "kv_norm_g", "w_kv", "b_kv", "w_q", "b_q", "sinks",
         "w_o", "b_o", "ffn_up", "ffn_conv_w", "ffn_conv_b", "ffn_down", "final_g")


def _as3d(a):
    return a.reshape((-1,) + a.shape[-2:])


def kernel(x, norm1_g, norm2_g, pool_w, pool_scale, kv_norm_g, w_kv, b_kv, w_q, b_q, sinks, w_o, b_o, ffn_up, ffn_conv_w, ffn_conv_b, ffn_down, final_g, loss_target, m_norm1_g, m_norm2_g, m_pool_w, m_pool_scale, m_kv_norm_g, m_w_kv, m_b_kv, m_w_q, m_b_q, m_sinks, m_w_o, m_b_o, m_ffn_up, m_ffn_conv_w, m_ffn_conv_b, m_ffn_down, m_final_g, v_norm1_g, v_norm2_g, v_pool_w, v_pool_scale, v_kv_norm_g, v_w_kv, v_b_kv, v_w_q, v_b_q, v_sinks, v_w_o, v_b_o, v_ffn_up, v_ffn_conv_w, v_ffn_conv_b, v_ffn_down, v_final_g):
    W = dict(norm1_g=norm1_g, norm2_g=norm2_g, pool_w=pool_w, pool_scale=pool_scale, kv_norm_g=kv_norm_g,
             w_kv=w_kv, b_kv=b_kv, w_q=w_q, b_q=b_q, sinks=sinks, w_o=w_o, b_o=b_o, ffn_up=ffn_up,
             ffn_conv_w=ffn_conv_w, ffn_conv_b=ffn_conv_b, ffn_down=ffn_down, final_g=final_g)
    M = dict(norm1_g=m_norm1_g, norm2_g=m_norm2_g, pool_w=m_pool_w, pool_scale=m_pool_scale,
             kv_norm_g=m_kv_norm_g, w_kv=m_w_kv, b_kv=m_b_kv, w_q=m_w_q, b_q=m_b_q, sinks=m_sinks, w_o=m_w_o,
             b_o=m_b_o, ffn_up=m_ffn_up, ffn_conv_w=m_ffn_conv_w, ffn_conv_b=m_ffn_conv_b, ffn_down=m_ffn_down,
             final_g=m_final_g)
    V = dict(norm1_g=v_norm1_g, norm2_g=v_norm2_g, pool_w=v_pool_w, pool_scale=v_pool_scale,
             kv_norm_g=v_kv_norm_g, w_kv=v_w_kv, b_kv=v_b_kv, w_q=v_w_q, b_q=v_b_q, sinks=v_sinks, w_o=v_w_o,
             b_o=v_b_o, ffn_up=v_ffn_up, ffn_conv_w=v_ffn_conv_w, ffn_conv_b=v_ffn_conv_b, ffn_down=v_ffn_down,
             final_g=v_final_g)
    S = x.shape[1]
    chip = 2 * lax.axis_index("x") + lax.axis_index("y")

    gather_axis = dict(pool_w=1, w_kv=1, w_q=1, w_o=1, ffn_up=2, ffn_down=1, pool_scale=2, ffn_conv_w=2)
    names = BIG + SMALL_SHARDED
    shards = [_as3d(W[k].astype(MX)) for k in BIG] + [_as3d(W[k]) for k in SMALL_SHARDED]
    full = dict(zip(names, _all_gather_chips(shards, [gather_axis[k] for k in names])))

    p = dict(
        norm1_g=norm1_g.reshape(DEPTH, 1, D), norm2_g=norm2_g.reshape(DEPTH, 1, D),
        pool_w=full["pool_w"].reshape(N_A, 4, GC, GC), pool_scale=full["pool_scale"].reshape(N_A, 1, D),
        kv_norm_g=kv_norm_g.reshape(1, 1, D), w_kv=full["w_kv"], b_kv=b_kv.reshape(1, 1, 4 * HD),
        w_q=full["w_q"], b_q=b_q.reshape(N_B, 1, D), sinks=sinks, w_o=full["w_o"], b_o=b_o.reshape(N_B, 1, D),
        ffn_up=full["ffn_up"], ffn_conv_w=full["ffn_conv_w"], ffn_conv_b=ffn_conv_b.reshape(DEPTH, 1, F2),
        ffn_down=full["ffn_down"], final_g=final_g.reshape(1, D))

    loss, grad_x, g = _local_step(x.reshape(S, D), loss_target.reshape(S, D), p)

    small_full = dict(
        norm1_g=jnp.stack(g["norm1_g"]), norm2_g=jnp.stack(g["norm2_g"]), kv_norm_g=g["kv_norm_g"],
        b_kv=g["b_kv"], b_q=jnp.stack(g["b_q"]), sinks=jnp.stack([s[0, :NH] for s in g["sinks"]]),
        b_o=jnp.stack(g["b_o"]), ffn_conv_b=jnp.stack(g["ffn_conv_b"]), final_g=g["final_g"],
        pool_scale=jnp.stack(g["pool_scale"]), ffn_conv_w=jnp.stack(g["ffn_conv_w"]))
    small_names = SMALL + SMALL_SHARDED
    small_shapes = [tuple(W[k].shape) for k in SMALL] + [(N_A, D), (DEPTH, 3, F2)]
    packed = _pack([small_full[k] for k in small_names] + [loss])
    red = _unpack(_all_reduce_small(packed), small_shapes + [(1, LANES)])
    red_g = dict(zip(small_names, red[:-1]))
    loss_out = red[-1][0, 0]
    red_g["pool_scale"] = lax.dynamic_slice_in_dim(red_g["pool_scale"], chip * (D // 4), D // 4, axis=1)
    red_g["ffn_conv_w"] = lax.dynamic_slice_in_dim(red_g["ffn_conv_w"], chip * (F2 // 4), F2 // 4, axis=2)
    small_w_shapes = [tuple(W[k].shape) for k in small_names]
    sg, sd, sm, sv = _adamw([_pack([red_g[k] for k in small_names])], _pack([W[k] for k in small_names]),
                            _pack([M[k] for k in small_names]), _pack([V[k] for k in small_names]))
    out_g = dict(zip(small_names, _unpack(sg, small_w_shapes)))
    out_d = dict(zip(small_names, _unpack(sd, small_w_shapes)))
    out_m = dict(zip(small_names, _unpack(sm, small_w_shapes)))
    out_v = dict(zip(small_names, _unpack(sv, small_w_shapes)))

    grads, axes, groups = [], [], []
    for k in BIG:
        members = []
        for gl in g[k]:
            members.append(len(grads))
            grads.append(_as3d(gl.astype(MX)))
            axes.append(gather_axis[k])
        groups.append(members)
    landed = _scatter_grads(grads, axes, groups)
    partial = [_sum_slots(r.reshape(4, -1, r.shape[-1])) for r in landed]
    theirs = _swap_with_sibling(partial)
    for k, mine_p, their_p in zip(BIG, partial, theirs):
        shp = W[k].shape
        two_d = lambda a: a.reshape(mine_p.shape)
        res = _adamw([mine_p, their_p], two_d(W[k]), two_d(M[k]), two_d(V[k]))
        out_g[k], out_d[k], out_m[k], out_v[k] = [r.reshape(shp) for r in res]

    return (loss_out, grad_x.reshape(x.shape), *[out_g[k] for k in ORDER], *[out_d[k] for k in ORDER],
            *[out_m[k] for k in ORDER], *[out_v[k] for k in ORDER])
```

```python
import functools

import jax
import jax.numpy as jnp
from jax import lax
from jax.experimental import pallas as pl
from jax.experimental.pallas import tpu as pltpu

D = 1024
DEPTH = 4
N_A = 2
N_B = 2
WINS = (2, 4, 8, 16)
GC = 256
HD = 64
NH = 16
BLK = 128
F = 2816
F2 = 2 * F
EPS = 1e-5
SCALE = HD ** -0.5
NEG = -1e30
HALO = 16
TN = 256
LANES = 128
SUBLANES = 8
VMEM_LIMIT = 56 * 1024 * 1024
ACC_BYTES = 6 * 1024 * 1024
EW_BYTES = 1024 * 1024

LR, B1, B2, AEPS, WD, STEP = 0.001, 0.9, 0.999, 1e-08, 0.01, 10

MX = jnp.bfloat16
F32 = jnp.float32
MESH = pl.DeviceIdType.MESH


def _cp(n_axes=1, vmem=VMEM_LIMIT):
    return pltpu.CompilerParams(dimension_semantics=("arbitrary",) * n_axes, vmem_limit_bytes=vmem)


def _dot(a, b):
    return jnp.dot(a, b, preferred_element_type=F32)


def _dot_nt(a, b):
    return lax.dot_general(a, b, (((1,), (1,)), ((), ())), preferred_element_type=F32)


def _dot_tn(a, b):
    return lax.dot_general(a, b, (((0,), (0,)), ((), ())), preferred_element_type=F32)


def _rms_fwd(x, g):
    r = lax.rsqrt(jnp.mean(x * x, axis=-1, keepdims=True) + EPS)
    xh = x * r
    return xh * g, xh, r


def _rms_bwd(dh, xh, r, g):
    dxh = dh * g
    return r * (dxh - xh * jnp.mean(dxh * xh, axis=-1, keepdims=True))


def _row_tile(s, want):
    return min(s, want)


def _pool_pm(e, h, row, tm):
    out = []
    for gi, win in enumerate(WINS):
        cols = slice(gi * GC, (gi + 1) * GC)
        s = e[:, cols]
        sh = 1
        while sh < win:
            s = s + pltpu.roll(s, sh, 0)
            sh *= 2
        inv = 1.0 / jnp.minimum(row + 1, win).astype(F32)
        out.append(s[HALO:] * inv - h[:, cols])
    return out


def _pool_fwd(x, g, pw, ps, layer):
    S = x.shape[0]
    tm = _row_tile(S, 512)
    hb = tm // HALO

    def body(x_ref, xh_ref, g_ref, pw_ref, ps_ref, o_ref):
        i = pl.program_id(0)
        x = x_ref[...]
        gg = g_ref[...]
        h, _, _ = _rms_fwd(x, gg)
        hh, _, _ = _rms_fwd(xh_ref[...], gg)
        hh = jnp.where(i > 0, hh, 0.0)
        e = jnp.concatenate([hh, h], axis=0)
        row = i * tm + lax.broadcasted_iota(jnp.int32, (tm, 1), 0)
        pm = _pool_pm(e, h, row, tm)
        for gi in range(len(WINS)):
            cols = slice(gi * GC, (gi + 1) * GC)
            z = _dot(pm[gi].astype(MX), pw_ref[gi])
            o_ref[:, cols] = x[:, cols] + z * ps_ref[:, cols]

    return pl.pallas_call(
        body, name="pool_fwd",
        grid=(S // tm,),
        in_specs=[
            pl.BlockSpec((tm, D), lambda i: (i, 0)),
            pl.BlockSpec((HALO, D), lambda i: (jnp.maximum(i * hb - 1, 0), 0)),
            pl.BlockSpec((None, 1, D), lambda i: (layer, 0, 0)),
            pl.BlockSpec((None, 4, GC, GC), lambda i: (layer, 0, 0, 0)),
            pl.BlockSpec((None, 1, D), lambda i: (layer, 0, 0)),
        ],
        out_specs=pl.BlockSpec((tm, D), lambda i: (i, 0)),
        out_shape=jax.ShapeDtypeStruct((S, D), F32),
        compiler_params=_cp(),
    )(x, x, g, pw, ps)


def _pool_bwd(x, dy, g, pw, ps, layer):
    S = x.shape[0]
    tm = _row_tile(S, 256)
    hb = tm // HALO
    n_i = S // tm
    n_h = S // HALO

    def body(x_ref, xh_ref, dy_ref, dyn_ref, g_ref, pw_ref, ps_ref, dx_ref, dpw_ref, dps_ref, dg_ref):
        i = pl.program_id(0)

        @pl.when(i == 0)
        def _():
            dpw_ref[...] = jnp.zeros_like(dpw_ref)
            dps_ref[...] = jnp.zeros_like(dps_ref)
            dg_ref[...] = jnp.zeros_like(dg_ref)

        x = x_ref[...]
        gg = g_ref[...]
        ps = ps_ref[...]
        h, xh, r = _rms_fwd(x, gg)
        hh, _, _ = _rms_fwd(xh_ref[...], gg)
        hh = jnp.where(i > 0, hh, 0.0)
        e = jnp.concatenate([hh, h], axis=0)
        row = i * tm + lax.broadcasted_iota(jnp.int32, (tm, 1), 0)
        rown = (i + 1) * tm + lax.broadcasted_iota(jnp.int32, (HALO, 1), 0)
        pm = _pool_pm(e, h, row, tm)
        dy = dy_ref[...]
        dz = dy * ps
        dzn = jnp.where(i < n_i - 1, dyn_ref[...] * ps, 0.0)
        parts = []
        for gi, win in enumerate(WINS):
            cols = slice(gi * GC, (gi + 1) * GC)
            w = pw_ref[gi]
            pmb = pm[gi].astype(MX)
            z = _dot(pmb, w)
            dps_ref[:, cols] += jnp.sum(dy[:, cols] * z, axis=0, keepdims=True)
            dzb = dz[:, cols].astype(MX)
            dpw_ref[gi] += _dot_tn(pmb, dzb)
            dpm = _dot_nt(dzb, w)
            dpmn = _dot_nt(dzn[:, cols].astype(MX), w)
            q = dpm * (1.0 / jnp.minimum(row + 1, win).astype(F32))
            qn = dpmn * (1.0 / jnp.minimum(rown + 1, win).astype(F32))
            s = jnp.concatenate([q, qn], axis=0)
            sh = 1
            while sh < win:
                s = s + pltpu.roll(s, tm + HALO - sh, 0)
                sh *= 2
            parts.append(s[:tm] - dpm)
        dh = jnp.concatenate(parts, axis=1)
        dg_ref[...] += jnp.sum(dh * xh, axis=0, keepdims=True)
        dx_ref[...] = dy + _rms_bwd(dh, xh, r, gg)

    return pl.pallas_call(
        body, name="pool_bwd",
        grid=(n_i,),
        in_specs=[
            pl.BlockSpec((tm, D), lambda i: (i, 0)),
            pl.BlockSpec((HALO, D), lambda i: (jnp.maximum(i * hb - 1, 0), 0)),
            pl.BlockSpec((tm, D), lambda i: (i, 0)),
            pl.BlockSpec((HALO, D), lambda i: (jnp.minimum((i + 1) * hb, n_h - 1), 0)),
            pl.BlockSpec((None, 1, D), lambda i: (layer, 0, 0)),
            pl.BlockSpec((None, 4, GC, GC), lambda i: (layer, 0, 0, 0)),
            pl.BlockSpec((None, 1, D), lambda i: (layer, 0, 0)),
        ],
        out_specs=[
            pl.BlockSpec((tm, D), lambda i: (i, 0)),
            pl.BlockSpec((4, GC, GC), lambda i: (0, 0, 0)),
            pl.BlockSpec((1, D), lambda i: (0, 0)),
            pl.BlockSpec((1, D), lambda i: (0, 0)),
        ],
        out_shape=[
            jax.ShapeDtypeStruct((S, D), F32),
            jax.ShapeDtypeStruct((4, GC, GC), F32),
            jax.ShapeDtypeStruct((1, D), F32),
            jax.ShapeDtypeStruct((1, D), F32),
        ],
        compiler_params=_cp(),
    )(x, x, dy, dy, g, pw, ps)


def _conv(u, prev, cw, rowi):
    um1 = jnp.where(rowi == 0, prev[7:8], pltpu.roll(u, 1, 0))
    um2 = jnp.where(rowi == 0, prev[6:7], jnp.where(rowi == 1, prev[7:8], pltpu.roll(u, 2, 0)))
    return cw[0:1] * um2 + cw[1:2] * um1 + cw[2:3] * u, um1, um2


def _ffn_fwd(x, g, wup, wdn, cw, cb, layer):
    S = x.shape[0]
    tm = _row_tile(S, 512)

    def body(x_ref, g_ref, wup_hbm, wdn_hbm, cw_ref, cb_ref, o_ref, u_ref, wup_v, wdn_v, carry, acc):
        i = pl.program_id(0)

        @pl.when(i == 0)
        def _():
            pltpu.sync_copy(wup_hbm.at[0], wup_v)
            pltpu.sync_copy(wdn_hbm.at[0], wdn_v)
            carry[...] = jnp.zeros_like(carry)

        x = x_ref[...]
        h, _, _ = _rms_fwd(x, g_ref[...])
        hb = h.astype(MX)
        rowi = lax.broadcasted_iota(jnp.int32, (tm, 1), 0)
        for j in range(F // TN):
            cg = slice(j * TN, (j + 1) * TN)
            cv = slice(F + j * TN, F + (j + 1) * TN)
            ug = _dot(hb, wup_v[:, cg])
            uv = _dot(hb, wup_v[:, cv])
            u_ref[:, cg] = ug.astype(u_ref.dtype)
            u_ref[:, cv] = uv.astype(u_ref.dtype)
            gt, _, _ = _conv(ug, carry[:, cg], cw_ref[:, cg], rowi)
            vl, _, _ = _conv(uv, carry[:, cv], cw_ref[:, cv], rowi)
            carry[:, cg] = ug[tm - SUBLANES:]
            carry[:, cv] = uv[tm - SUBLANES:]
            gt = gt + cb_ref[:, cg]
            vl = vl + cb_ref[:, cv]
            a = gt * jax.nn.sigmoid(gt) * vl
            contrib = _dot(a.astype(MX), wdn_v[cg, :])
            if j == 0:
                acc[...] = contrib
            else:
                acc[...] += contrib
        o_ref[...] = x + acc[...]

    return pl.pallas_call(
        body, name="ffn_fwd",
        grid=(S // tm,),
        in_specs=[
            pl.BlockSpec((tm, D), lambda i: (i, 0)),
            pl.BlockSpec((None, 1, D), lambda i: (layer, 0, 0)),
            pl.BlockSpec(memory_space=pl.ANY),
            pl.BlockSpec(memory_space=pl.ANY),
            pl.BlockSpec((None, 3, F2), lambda i: (layer, 0, 0)),
            pl.BlockSpec((None, 1, F2), lambda i: (layer, 0, 0)),
        ],
        out_specs=[
            pl.BlockSpec((tm, D), lambda i: (i, 0)),
            pl.BlockSpec((tm, F2), lambda i: (i, 0)),
        ],
        out_shape=[
            jax.ShapeDtypeStruct((S, D), F32),
            jax.ShapeDtypeStruct((S, F2), MX),
        ],
        scratch_shapes=[
            pltpu.VMEM((D, F2), MX),
            pltpu.VMEM((F, D), MX),
            pltpu.VMEM((SUBLANES, F2), F32),
            pltpu.VMEM((tm, D), F32),
        ],
        compiler_params=_cp(),
    )(x, g, wup, wdn, cw, cb)


def _ffn_bwd(x, dy, u, g, wup, wdn, cw, cb, layer):
    S = x.shape[0]
    tm = _row_tile(S, 256)
    n_i = S // tm
    hb_per = tm // HALO

    def body(x_ref, dy_ref, u_ref, uh_ref, g_ref, wup_hbm, wdn_hbm, cw_ref, cb_ref,
             dx_ref, du_ref, a_ref, h_ref, dg_ref, dcw_ref, dcb_ref, wup_v, wdn_v, carry, acc):
        i = pl.program_id(0)
        t = n_i - 1 - i

        @pl.when(i == 0)
        def _():
            pltpu.sync_copy(wup_hbm.at[0], wup_v)
            pltpu.sync_copy(wdn_hbm.at[0], wdn_v)
            carry[...] = jnp.zeros_like(carry)
            dg_ref[...] = jnp.zeros_like(dg_ref)
            dcw_ref[...] = jnp.zeros_like(dcw_ref)
            dcb_ref[...] = jnp.zeros_like(dcb_ref)

        x = x_ref[...]
        gg = g_ref[...]
        h, xh, r = _rms_fwd(x, gg)
        h_ref[...] = h.astype(h_ref.dtype)
        dy = dy_ref[...]
        dyb = dy.astype(MX)
        rowi = lax.broadcasted_iota(jnp.int32, (tm, 1), 0)
        for j in range(F // TN):
            cg = slice(j * TN, (j + 1) * TN)
            cv = slice(F + j * TN, F + (j + 1) * TN)
            res = []
            for cc in (cg, cv):
                uu = u_ref[:, cc].astype(F32)
                prev = jnp.where(t > 0, uh_ref[:, cc].astype(F32)[HALO - SUBLANES:], 0.0)
                res.append((uu,) + _conv(uu, prev, cw_ref[:, cc], rowi))
            (ug, gt, ug1, ug2), (uv, vl, uv1, uv2) = res
            gt = gt + cb_ref[:, cg]
            vl = vl + cb_ref[:, cv]
            sg = jax.nn.sigmoid(gt)
            sil = gt * sg
            a_ref[:, cg] = (sil * vl).astype(a_ref.dtype)
            da = _dot_nt(dyb, wdn_v[cg, :])
            dvl = da * sil
            dgt = da * vl * (sg * (1.0 + gt * (1.0 - sg)))
            for cc, dd, u0, u1, u2 in ((cg, dgt, ug, ug1, ug2), (cv, dvl, uv, uv1, uv2)):
                dcb_ref[:, cc] += jnp.sum(dd, axis=0, keepdims=True)
                dcw_ref[0:1, cc] += jnp.sum(dd * u2, axis=0, keepdims=True)
                dcw_ref[1:2, cc] += jnp.sum(dd * u1, axis=0, keepdims=True)
                dcw_ref[2:3, cc] += jnp.sum(dd * u0, axis=0, keepdims=True)
                nxt = carry[:, cc]
                dp1 = jnp.where(rowi == tm - 1, nxt[0:1], pltpu.roll(dd, tm - 1, 0))
                dp2 = jnp.where(rowi == tm - 1, nxt[1:2],
                                jnp.where(rowi == tm - 2, nxt[0:1], pltpu.roll(dd, tm - 2, 0)))
                cwc = cw_ref[:, cc]
                duu = (cwc[2:3] * dd + cwc[1:2] * dp1 + cwc[0:1] * dp2).astype(MX)
                carry[:, cc] = dd[0:SUBLANES]
                du_ref[:, cc] = duu.astype(du_ref.dtype)
                contrib = _dot_nt(duu, wup_v[:, cc])
                if j == 0 and cc is cg:
                    acc[...] = contrib
                else:
                    acc[...] += contrib
        dh = acc[...]
        dg_ref[...] += jnp.sum(dh * xh, axis=0, keepdims=True)
        dx_ref[...] = dy + _rms_bwd(dh, xh, r, gg)

    rev = lambda i: (n_i - 1 - i, 0)
    return pl.pallas_call(
        body, name="ffn_bwd",
        grid=(n_i,),
        in_specs=[
            pl.BlockSpec((tm, D), rev),
            pl.BlockSpec((tm, D), rev),
            pl.BlockSpec((tm, F2), rev),
            pl.BlockSpec((HALO, F2), lambda i: (jnp.maximum((n_i - 1 - i) * hb_per - 1, 0), 0)),
            pl.BlockSpec((None, 1, D), lambda i: (layer, 0, 0)),
            pl.BlockSpec(memory_space=pl.ANY),
            pl.BlockSpec(memory_space=pl.ANY),
            pl.BlockSpec((None, 3, F2), lambda i: (layer, 0, 0)),
            pl.BlockSpec((None, 1, F2), lambda i: (layer, 0, 0)),
        ],
        out_specs=[
            pl.BlockSpec((tm, D), rev),
            pl.BlockSpec((tm, F2), rev),
            pl.BlockSpec((tm, F), rev),
            pl.BlockSpec((tm, D), rev),
            pl.BlockSpec((1, D), lambda i: (0, 0)),
            pl.BlockSpec((3, F2), lambda i: (0, 0)),
            pl.BlockSpec((1, F2), lambda i: (0, 0)),
        ],
        out_shape=[
            jax.ShapeDtypeStruct((S, D), F32),
            jax.ShapeDtypeStruct((S, F2), MX),
            jax.ShapeDtypeStruct((S, F), MX),
            jax.ShapeDtypeStruct((S, D), MX),
            jax.ShapeDtypeStruct((1, D), F32),
            jax.ShapeDtypeStruct((3, F2), F32),
            jax.ShapeDtypeStruct((1, F2), F32),
        ],
        scratch_shapes=[
            pltpu.VMEM((D, F2), MX),
            pltpu.VMEM((F, D), MX),
            pltpu.VMEM((SUBLANES, F2), F32),
            pltpu.VMEM((tm, D), F32),
        ],
        compiler_params=_cp(),
    )(x, dy, u, u, g, wup, wdn, cw, cb)


def _tn_matmul(a, b, out_dtype, name):
    S, M = a.shape
    N = b.shape[1]
    bn = N
    while M * bn * 4 > ACC_BYTES and bn % (2 * LANES) == 0:
        bn //= 2
    bk = _row_tile(S, 512)
    nk = S // bk

    def body(a_ref, b_ref, o_ref, acc):
        k = pl.program_id(1)
        p = _dot_tn(a_ref[...].astype(MX), b_ref[...].astype(MX))

        @pl.when(k == 0)
        def _():
            acc[...] = p

        @pl.when(k > 0)
        def _():
            acc[...] += p

        @pl.when(k == nk - 1)
        def _():
            o_ref[...] = acc[...].astype(o_ref.dtype)

    return pl.pallas_call(
        body, name=name,
        grid=(N // bn, nk),
        in_specs=[
            pl.BlockSpec((bk, M), lambda j, k: (k, 0)),
            pl.BlockSpec((bk, bn), lambda j, k: (k, j)),
        ],
        out_specs=pl.BlockSpec((M, bn), lambda j, k: (0, j)),
        out_shape=jax.ShapeDtypeStruct((M, N), out_dtype),
        scratch_shapes=[pltpu.VMEM((M, bn), F32)],
        compiler_params=_cp(2),
    )(a, b)


def _rms_linear(x, g, w, b, g_layer, b_layer, name):
    S = x.shape[0]
    N = w.shape[-1]
    tm = _row_tile(S, 512)

    def body(x_ref, g_ref, w_ref, b_ref, o_ref):
        h, _, _ = _rms_fwd(x_ref[...], g_ref[...])
        o_ref[...] = (_dot(h.astype(MX), w_ref[...]) + b_ref[...]).astype(o_ref.dtype)

    return pl.pallas_call(
        body, name=name,
        grid=(S // tm,),
        in_specs=[
            pl.BlockSpec((tm, D), lambda i: (i, 0)),
            pl.BlockSpec((None, 1, D), lambda i: (g_layer, 0, 0)),
            pl.BlockSpec((None, D, N), lambda i: (0, 0, 0)),
            pl.BlockSpec((None, 1, N), lambda i: (b_layer, 0, 0)),
        ],
        out_specs=pl.BlockSpec((tm, N), lambda i: (i, 0)),
        out_shape=jax.ShapeDtypeStruct((S, N), MX),
        compiler_params=_cp(),
    )(x, g, w, b)


def _linear_res(o, w, b, xres, layer):
    S = o.shape[0]
    tm = _row_tile(S, 512)

    def body(o_ref, w_ref, b_ref, x_ref, y_ref):
        y_ref[...] = x_ref[...] + _dot(o_ref[...], w_ref[...]) + b_ref[...]

    return pl.pallas_call(
        body, name="o_proj",
        grid=(S // tm,),
        in_specs=[
            pl.BlockSpec((tm, D), lambda i: (i, 0)),
            pl.BlockSpec((None, D, D), lambda i: (0, 0, 0)),
            pl.BlockSpec((None, 1, D), lambda i: (layer, 0, 0)),
            pl.BlockSpec((tm, D), lambda i: (i, 0)),
        ],
        out_specs=pl.BlockSpec((tm, D), lambda i: (i, 0)),
        out_shape=jax.ShapeDtypeStruct((S, D), F32),
        compiler_params=_cp(),
    )(o, w, b, xres)


def _linear_nt(dy, w):
    S = dy.shape[0]
    tm = _row_tile(S, 512)

    def body(dy_ref, w_ref, o_ref, db_ref):
        @pl.when(pl.program_id(0) == 0)
        def _():
            db_ref[...] = jnp.zeros_like(db_ref)

        dy = dy_ref[...]
        db_ref[...] += jnp.sum(dy, axis=0, keepdims=True)
        o_ref[...] = _dot_nt(dy.astype(MX), w_ref[...]).astype(o_ref.dtype)

    return pl.pallas_call(
        body, name="o_proj_bwd",
        grid=(S // tm,),
        in_specs=[
            pl.BlockSpec((tm, D), lambda i: (i, 0)),
            pl.BlockSpec((None, D, D), lambda i: (0, 0, 0)),
        ],
        out_specs=[
            pl.BlockSpec((tm, D), lambda i: (i, 0)),
            pl.BlockSpec((1, D), lambda i: (0, 0)),
        ],
        out_shape=[
            jax.ShapeDtypeStruct((S, D), MX),
            jax.ShapeDtypeStruct((1, D), F32),
        ],
        compiler_params=_cp(),
    )(dy, w)


def _rms_linear_bwd(x, g, dzs, w, dy, g_layer, name):
    S = x.shape[0]
    N = w.shape[-1]
    tm = _row_tile(S, 512)
    nz = len(dzs)

    def body(*refs):
        x_ref, g_ref = refs[0], refs[1]
        dz_refs = refs[2:2 + nz]
        w_ref, dy_ref, dx_ref, dg_ref, db_ref, h_ref, dzb_ref = refs[2 + nz:]

        @pl.when(pl.program_id(0) == 0)
        def _():
            dg_ref[...] = jnp.zeros_like(dg_ref)
            db_ref[...] = jnp.zeros_like(db_ref)

        gg = g_ref[...]
        h, xh, r = _rms_fwd(x_ref[...], gg)
        h_ref[...] = h.astype(h_ref.dtype)
        dz = dz_refs[0][...].astype(F32)
        for zr in dz_refs[1:]:
            dz = dz + zr[...].astype(F32)
        db_ref[...] += jnp.sum(dz, axis=0, keepdims=True)
        dzb = dz.astype(MX)
        dzb_ref[...] = dzb
        dh = _dot_nt(dzb, w_ref[...])
        dg_ref[...] += jnp.sum(dh * xh, axis=0, keepdims=True)
        dx_ref[...] = dy_ref[...] + _rms_bwd(dh, xh, r, gg)

    return pl.pallas_call(
        body, name=name,
        grid=(S // tm,),
        in_specs=[
            pl.BlockSpec((tm, D), lambda i: (i, 0)),
            pl.BlockSpec((None, 1, D), lambda i: (g_layer, 0, 0)),
        ] + [pl.BlockSpec((tm, N), lambda i: (i, 0))] * nz + [
            pl.BlockSpec((None, D, N), lambda i: (0, 0, 0)),
            pl.BlockSpec((tm, D), lambda i: (i, 0)),
        ],
        out_specs=[
            pl.BlockSpec((tm, D), lambda i: (i, 0)),
            pl.BlockSpec((1, D), lambda i: (0, 0)),
            pl.BlockSpec((1, N), lambda i: (0, 0)),
            pl.BlockSpec((tm, D), lambda i: (i, 0)),
            pl.BlockSpec((tm, N), lambda i: (i, 0)),
        ],
        out_shape=[
            jax.ShapeDtypeStruct((S, D), F32),
            jax.ShapeDtypeStruct((1, D), F32),
            jax.ShapeDtypeStruct((1, N), F32),
            jax.ShapeDtypeStruct((S, D), MX),
            jax.ShapeDtypeStruct((S, N), MX),
        ],
        compiler_params=_cp(),
    )(x, g, *dzs, w, dy)


def _loss_head(x, g, tgt):
    S = x.shape[0]
    tm = _row_tile(S, 512)

    def body(x_ref, g_ref, t_ref, dx_ref, dg_ref, l_ref):
        @pl.when(pl.program_id(0) == 0)
        def _():
            dg_ref[...] = jnp.zeros_like(dg_ref)
            l_ref[...] = jnp.zeros_like(l_ref)

        gg = g_ref[...]
        y, xh, r = _rms_fwd(x_ref[...], gg)
        err = y - t_ref[...]
        tok = jnp.sum(err * err, axis=-1, keepdims=True) * (1.0 / D)
        l_ref[...] += 0.5 * jnp.sum(tok, axis=0, keepdims=True)
        dyv = err * (1.0 / D)
        dg_ref[...] += jnp.sum(dyv * xh, axis=0, keepdims=True)
        dx_ref[...] = _rms_bwd(dyv, xh, r, gg)

    return pl.pallas_call(
        body, name="loss_head",
        grid=(S // tm,),
        in_specs=[
            pl.BlockSpec((tm, D), lambda i: (i, 0)),
            pl.BlockSpec((1, D), lambda i: (0, 0)),
            pl.BlockSpec((tm, D), lambda i: (i, 0)),
        ],
        out_specs=[
            pl.BlockSpec((tm, D), lambda i: (i, 0)),
            pl.BlockSpec((1, D), lambda i: (0, 0)),
            pl.BlockSpec((1, LANES), lambda i: (0, 0)),
        ],
        out_shape=[
            jax.ShapeDtypeStruct((S, D), F32),
            jax.ShapeDtypeStruct((1, D), F32),
            jax.ShapeDtypeStruct((1, LANES), F32),
        ],
        compiler_params=_cp(),
    )(x, g, tgt)


def _attn_setup(kvp_ref, kvc_ref, n):
    kw = jnp.concatenate([kvp_ref[...], kvc_ref[...]], axis=0).astype(F32)
    kk, vv = kw[:, :LANES], kw[:, LANES:]
    lo = lax.broadcasted_iota(jnp.int32, (1, LANES), 1) < HD
    kr, vr = pltpu.roll(kk, HD, 1), pltpu.roll(vv, HD, 1)
    ks = [jnp.where(lo, kk, kr).astype(MX), jnp.where(lo, kr, kk).astype(MX)]
    vs = [jnp.where(lo, vv, vr).astype(MX), jnp.where(lo, vr, vv).astype(MX)]
    qi = lax.broadcasted_iota(jnp.int32, (BLK, 2 * BLK), 0)
    si = lax.broadcasted_iota(jnp.int32, (BLK, 2 * BLK), 1)
    mask = (si > qi) & (si <= qi + BLK) & jnp.logical_or(n > 0, si >= BLK)
    return ks, vs, lo, mask


def _attn_probs(qm, kg, mask, sink):
    s = jnp.where(mask, _dot_nt(qm, kg) * SCALE, NEG)
    m = jnp.maximum(jnp.max(s, axis=-1, keepdims=True), sink)
    p = jnp.exp(s - m)
    es = jnp.exp(sink - m)
    inv = 1.0 / (jnp.sum(p, axis=-1, keepdims=True) + es)
    return p * inv, es * inv


def _attn_specs(n_extra_q):
    q_spec = pl.BlockSpec((BLK, D), lambda n: (n, 0))
    return [q_spec] * n_extra_q + [
        pl.BlockSpec((BLK, 4 * HD), lambda n: (jnp.maximum(n - 1, 0), 0)),
        pl.BlockSpec((BLK, 4 * HD), lambda n: (n, 0)),
        pl.BlockSpec(memory_space=pltpu.SMEM),
    ]


def _attn_fwd(q, kv, sinks, layer):
    S = q.shape[0]

    def body(q_ref, kvp_ref, kvc_ref, sk_ref, o_ref):
        n = pl.program_id(0)
        ks, vs, lo, mask = _attn_setup(kvp_ref, kvc_ref, n)
        for j in range(NH // 2):
            cols = slice(j * LANES, (j + 1) * LANES)
            grp = (2 * j) // (NH // 2)
            qp = q_ref[:, cols]
            outs = []
            for half in range(2):
                sel = lo if half == 0 else jnp.logical_not(lo)
                qm = jnp.where(sel, qp, jnp.zeros_like(qp))
                pr, _ = _attn_probs(qm, ks[grp], mask, sk_ref[layer, 2 * j + half])
                outs.append(_dot(pr.astype(MX), vs[grp]))
            o_ref[:, cols] = jnp.where(lo, outs[0], outs[1]).astype(o_ref.dtype)

    return pl.pallas_call(
        body, name="attn_fwd",
        grid=(S // BLK,),
        in_specs=_attn_specs(1),
        out_specs=pl.BlockSpec((BLK, D), lambda n: (n, 0)),
        out_shape=jax.ShapeDtypeStruct((S, D), MX),
        compiler_params=_cp(),
    )(q, kv, kv, sinks)


def _attn_bwd(q, do, kv, sinks, layer):
    S = q.shape[0]

    def body(q_ref, do_ref, kvp_ref, kvc_ref, sk_ref, dq_ref, dkv_ref, dsk_ref):
        n = pl.program_id(0)

        @pl.when(n == 0)
        def _():
            dkv_ref[...] = jnp.zeros_like(dkv_ref)
            dsk_ref[...] = jnp.zeros_like(dsk_ref)

        ks, vs, lo, mask = _attn_setup(kvp_ref, kvc_ref, n)
        lane = lax.broadcasted_iota(jnp.int32, (1, LANES), 1)
        dk_acc = [jnp.zeros((2 * BLK, LANES), F32) for _ in range(2)]
        dv_acc = [jnp.zeros((2 * BLK, LANES), F32) for _ in range(2)]
        dsk = jnp.zeros((1, LANES), F32)
        for j in range(NH // 2):
            cols = slice(j * LANES, (j + 1) * LANES)
            grp = (2 * j) // (NH // 2)
            qp = q_ref[:, cols]
            dop = do_ref[:, cols]
            dqs = []
            for half in range(2):
                sel = lo if half == 0 else jnp.logical_not(lo)
                qm = jnp.where(sel, qp, jnp.zeros_like(qp))
                dom = jnp.where(sel, dop, jnp.zeros_like(dop))
                pr, psink = _attn_probs(qm, ks[grp], mask, sk_ref[layer, 2 * j + half])
                dpr = _dot_nt(dom, vs[grp])
                delta = jnp.sum(pr * dpr, axis=-1, keepdims=True)
                ds = (pr * (dpr - delta) * SCALE).astype(MX)
                dsk = dsk + jnp.where(lane == 2 * j + half,
                                      -jnp.sum(psink * delta, axis=0, keepdims=True), 0.0)
                dqs.append(_dot(ds, ks[grp]))
                dk_acc[grp] = dk_acc[grp] + _dot_tn(ds, qm)
                dv_acc[grp] = dv_acc[grp] + _dot_tn(pr.astype(MX), dom)
            dq_ref[:, cols] = jnp.where(lo, dqs[0], dqs[1]).astype(dq_ref.dtype)
        dsk_ref[...] += dsk
        tk = [a + pltpu.roll(a, HD, 1) for a in dk_acc]
        tv = [a + pltpu.roll(a, HD, 1) for a in dv_acc]
        contrib = jnp.concatenate([jnp.where(lo, tk[0], tk[1]), jnp.where(lo, tv[0], tv[1])], axis=1)

        @pl.when(n > 0)
        def _():
            rows = pl.ds(pl.multiple_of((n - 1) * BLK, BLK), 2 * BLK)
            dkv_ref[rows, :] += contrib

        @pl.when(n == 0)
        def _():
            dkv_ref[0:BLK, :] += contrib[BLK:]

    return pl.pallas_call(
        body, name="attn_bwd",
        grid=(S // BLK,),
        in_specs=_attn_specs(2),
        out_specs=[
            pl.BlockSpec((BLK, D), lambda n: (n, 0)),
            pl.BlockSpec((S, 4 * HD), lambda n: (0, 0)),
            pl.BlockSpec((1, LANES), lambda n: (0, 0)),
        ],
        out_shape=[
            jax.ShapeDtypeStruct((S, D), MX),
            jax.ShapeDtypeStruct((S, 4 * HD), F32),
            jax.ShapeDtypeStruct((1, LANES), F32),
        ],
        compiler_params=_cp(),
    )(q, do, kv, kv, sinks)


def _ew_rows(rows, cols, n_bufs=1):
    br = rows
    while br * cols * 4 * n_bufs > EW_BYTES and br % (2 * SUBLANES) == 0:
        br //= 2
    return br


def _sum_slots(r):
    _, R, C = r.shape
    br = _ew_rows(R, C, 2)

    def body(r_ref, o_ref):
        o_ref[...] = ((r_ref[0].astype(F32) + r_ref[1].astype(F32)) + r_ref[2].astype(F32)) + r_ref[3].astype(F32)

    return pl.pallas_call(
        body, name="sum_slots",
        grid=(R // br,),
        in_specs=[pl.BlockSpec((4, br, C), lambda i: (0, i, 0))],
        out_specs=pl.BlockSpec((br, C), lambda i: (i, 0)),
        out_shape=jax.ShapeDtypeStruct((R, C), F32),
        compiler_params=_cp(),
    )(r)


def _adamw(parts, w, m, v):
    L, R, C = w.shape
    br = _ew_rows(R, C)
    npart = len(parts[0])

    def body(*refs):
        p_refs = refs[:L * npart]
        w_ref, m_ref, v_ref, g_ref, d_ref, nm_ref, nv_ref = refs[L * npart:]
        lyr = pl.program_id(0)
        for l in range(L):
            @pl.when(lyr == l)
            def _(l=l):
                g = p_refs[l * npart][...]
                for pr in p_refs[l * npart + 1:(l + 1) * npart]:
                    g = g + pr[...]
                nm = B1 * m_ref[...] + (1.0 - B1) * g
                nv = B2 * v_ref[...] + (1.0 - B2) * (g * g)
                m_hat = nm / (1.0 - B1 ** STEP)
                v_hat = nv / (1.0 - B2 ** STEP)
                g_ref[...] = g
                d_ref[...] = -LR * (m_hat / (jnp.sqrt(v_hat) + AEPS) + WD * w_ref[...])
                nm_ref[...] = nm
                nv_ref[...] = nv

    spec = pl.BlockSpec((None, br, C), lambda a, i: (a, i, 0))
    part_specs = [pl.BlockSpec((br, C), lambda a, i, l=l: (jnp.where(a == l, i, 0), 0))
                  for l in range(L) for _ in range(npart)]
    return pl.pallas_call(
        body, name="adamw",
        grid=(L, R // br),
        in_specs=part_specs + [spec] * 3,
        out_specs=[spec] * 4,
        out_shape=[jax.ShapeDtypeStruct((L, R, C), F32)] * 4,
        compiler_params=_cp(2),
    )(*[a for lp in parts for a in lp], w, m, v)


def _coords():
    return lax.axis_index("x"), lax.axis_index("y"), lax.axis_index("c")


def _other_chips(x, y):
    return [(1 - x, y), (x, 1 - y), (1 - x, 1 - y)]


def _slot(ref, axis, chip, size):
    idx = [slice(None)] * 3
    idx[axis] = pl.ds(pl.multiple_of(chip * size, size), size)
    return ref.at[tuple(idx)]


HBM_SPEC = pl.BlockSpec(memory_space=pltpu.HBM)
SEM_SPEC = pl.BlockSpec(memory_space=pltpu.SEMAPHORE)
ANY_SPEC = pl.BlockSpec(memory_space=pl.ANY)
EFFECT = pltpu.SideEffectType.DATAFLOW_SIDE_EFFECTING


def _place_local(name, srcs, out_shapes, view_src, view_dst):
    nt = len(srcs)

    def body(*refs):
        ins, outs, sems = refs[:nt], refs[nt:2 * nt], refs[2 * nt]
        x, y, _ = _coords()
        me = 2 * x + y
        cps = [pltpu.make_async_copy(view_src(t, ins[t], me), view_dst(t, outs[t], me), sems.at[t]) for t in range(nt)]
        for cp in cps:
            cp.start()
        for cp in cps:
            cp.wait()

    return pl.pallas_call(
        body, name=name,
        in_specs=[ANY_SPEC] * nt,
        out_specs=[ANY_SPEC] * nt,
        out_shape=out_shapes,
        scratch_shapes=[pltpu.SemaphoreType.DMA((nt,))],
    )(*srcs)


def _copies(srcs, lands, views, x, y, c, send, recv):
    me = 2 * x + y
    out = []
    for t in range(len(srcs)):
        for k, (px, py) in enumerate(_other_chips(x, y)):
            sv, dv, rv = views(t, srcs[t], lands[t], me, 2 * px + py)
            n = 3 * t + k
            push = pltpu.make_async_remote_copy(src_ref=sv, dst_ref=dv, send_sem=send.at[n], recv_sem=recv.at[n],
                                                device_id=(px, py, c), device_id_type=MESH)
            land = pltpu.make_async_remote_copy(src_ref=sv, dst_ref=rv, send_sem=send.at[n], recv_sem=recv.at[n],
                                                device_id=(px, py, c), device_id_type=MESH)
            out.append((push, land))
    return out


def _push_start(name, srcs, lands, views):
    ns = len(srcs)

    def body(*refs):
        src_refs, land_refs = refs[:ns], refs[ns:2 * ns]
        send, recv = refs[2 * ns], refs[2 * ns + 1]
        token = refs[-1]
        x, y, c = _coords()
        for push, _ in _copies(src_refs, land_refs, views, x, y, c, send, recv):
            push.start()
        token[...] = jnp.zeros_like(token)

    hbm = lambda a: pltpu.HBM(a.shape, a.dtype)
    res = pl.pallas_call(
        body, name=name,
        in_specs=[HBM_SPEC] * (2 * ns),
        out_specs=[SEM_SPEC, SEM_SPEC] + [HBM_SPEC] * (2 * ns) + [pl.BlockSpec(memory_space=pltpu.VMEM)],
        out_shape=[pltpu.SemaphoreType.DMA((3 * ns,)), pltpu.SemaphoreType.DMA((3 * ns,))]
        + [hbm(a) for a in srcs] + [hbm(a) for a in lands] + [jax.ShapeDtypeStruct((SUBLANES, LANES), F32)],
        input_output_aliases={i: 2 + i for i in range(2 * ns)},
        compiler_params=pltpu.CompilerParams(has_side_effects=EFFECT),
    )(*[pltpu.with_memory_space_constraint(a, pltpu.HBM) for a in list(srcs) + list(lands)])
    return res[0], res[1], res[2:2 + ns], res[2 + ns:2 + 2 * ns], res[-1]


def _push_wait(name, send, recv, srcs, lands, views, after):
    ns = len(srcs)

    def body(*refs):
        src_refs, land_refs = refs[:ns], refs[ns:2 * ns]
        send_ref, recv_ref = refs[2 * ns], refs[2 * ns + 1]
        x, y, c = _coords()
        for _, land in _copies(src_refs, land_refs, views, x, y, c, send_ref, recv_ref):
            land.wait_send()
            land.wait_recv()

    hbm = lambda a: pltpu.HBM(a.shape, a.dtype)
    res = pl.pallas_call(
        body, name=name,
        in_specs=[HBM_SPEC] * (2 * ns) + [SEM_SPEC, SEM_SPEC, ANY_SPEC],
        out_specs=[HBM_SPEC] * (2 * ns),
        out_shape=[hbm(a) for a in srcs] + [hbm(a) for a in lands],
        input_output_aliases={i: i for i in range(2 * ns)},
        compiler_params=pltpu.CompilerParams(has_side_effects=EFFECT),
    )(*srcs, *lands, send, recv, after)
    return res[ns:]


def _gather_views(axes):
    def views(t, src, land, me, peer):
        size = src.shape[axes[t]]
        return src, _slot(land, axes[t], me, size), _slot(land, axes[t], peer, size)
    return views


def _scatter_views(axes):
    def views(t, src, land, me, peer):
        size = src.shape[axes[t]] // 4
        return _slot(src, axes[t], peer, size), land.at[me], land.at[peer]
    return views


def _gather_place(shards, axes):
    shapes = []
    for s, ax in zip(shards, axes):
        shp = list(s.shape)
        shp[ax] *= 4
        shapes.append(jax.ShapeDtypeStruct(tuple(shp), s.dtype))
    return _place_local("gather_place", shards, shapes,
                        lambda t, src, me: src,
                        lambda t, dst, me: _slot(dst, axes[t], me, shards[t].shape[axes[t]]))


def _scatter_place(name, grads, axes):
    shapes = []
    for g, ax in zip(grads, axes):
        shp = list(g.shape)
        shp[ax] //= 4
        shapes.append(jax.ShapeDtypeStruct((4,) + tuple(shp), g.dtype))
    return _place_local(name, grads, shapes,
                        lambda t, src, me: _slot(src, axes[t], me, grads[t].shape[axes[t]] // 4),
                        lambda t, dst, me: dst.at[me])


def _swap_with_sibling(arrs):
    na = len(arrs)

    def body(*refs):
        ins, outs = refs[:na], refs[na:2 * na]
        send, recv = refs[2 * na:]
        x, y, c = _coords()
        cps = []
        for t in range(na):
            rc = pltpu.make_async_remote_copy(
                src_ref=ins[t], dst_ref=outs[t], send_sem=send.at[t], recv_sem=recv.at[t],
                device_id=(x, y, 1 - c), device_id_type=MESH)
            rc.start()
            cps.append(rc)
        for rc in cps:
            rc.wait()

    any_spec = pl.BlockSpec(memory_space=pl.ANY)
    return pl.pallas_call(
        body, name="swap_sibling",
        in_specs=[any_spec] * na,
        out_specs=[any_spec] * na,
        out_shape=[jax.ShapeDtypeStruct(a.shape, a.dtype) for a in arrs],
        scratch_shapes=[pltpu.SemaphoreType.DMA((na,)), pltpu.SemaphoreType.DMA((na,))],
    )(*arrs)


def _all_reduce_small(v):
    R = v.shape[0]

    def body(v_ref, o_ref, buf, send, recv, local):
        x, y, c = _coords()
        me = 4 * x + 2 * y + c
        cp = pltpu.make_async_copy(v_ref, buf.at[me], local)
        cp.start()
        pushes = []
        for k in range(1, 8):
            peer = (x ^ (k >> 2), y ^ ((k >> 1) & 1), c ^ (k & 1))
            rc = pltpu.make_async_remote_copy(
                src_ref=v_ref, dst_ref=buf.at[me], send_sem=send.at[k - 1], recv_sem=recv.at[k - 1],
                device_id=peer, device_id_type=MESH)
            rc.start()
            pushes.append(rc)
        for k in range(1, 8):
            px, py, pc = x ^ (k >> 2), y ^ ((k >> 1) & 1), c ^ (k & 1)
            pltpu.make_async_remote_copy(
                src_ref=v_ref, dst_ref=buf.at[4 * px + 2 * py + pc], send_sem=send.at[k - 1],
                recv_sem=recv.at[k - 1], device_id=(px, py, pc), device_id_type=MESH).wait_recv()
        for rc in pushes:
            rc.wait_send()
        cp.wait()
        tot = buf[0]
        for k in range(1, 8):
            tot = tot + buf[k]
        o_ref[...] = tot

    vm = pl.BlockSpec(memory_space=pltpu.VMEM)
    return pl.pallas_call(
        body, name="all_reduce_small",
        in_specs=[vm],
        out_specs=vm,
        out_shape=jax.ShapeDtypeStruct((R, LANES), F32),
        scratch_shapes=[
            pltpu.VMEM((8, R, LANES), F32),
            pltpu.SemaphoreType.DMA((7,)),
            pltpu.SemaphoreType.DMA((7,)),
            pltpu.SemaphoreType.DMA,
        ],
        compiler_params=pltpu.CompilerParams(vmem_limit_bytes=VMEM_LIMIT),
    )(v)


def _pack(arrs):
    flat = []
    for a in arrs:
        f = a.reshape(-1).astype(F32)
        flat.append(jnp.pad(f, (0, (-f.shape[0]) % LANES)))
    v = jnp.concatenate(flat)
    v = jnp.pad(v, (0, (-v.shape[0]) % (SUBLANES * LANES)))
    return v.reshape(-1, LANES)


def _unpack(v, shapes):
    flat = v.reshape(-1)
    out, off = [], 0
    for shp in shapes:
        n = 1
        for d in shp:
            n *= d
        out.append(flat[off:off + n].reshape(shp))
        off += n + (-n) % LANES
    return out


def _local_step(x, tgt, sp, weights_for, on_grads):
    n1, n2 = sp["norm1_g"], sp["norm2_g"]
    w = dict(weights_for(0, x))
    saved = []
    xs = x
    kv = None
    for l in range(DEPTH):
        x_in = xs
        if l == N_A:
            w.update(weights_for(2, x_in))
            kv = _rms_linear(x_in, sp["kv_norm_g"], w["w_kv"], sp["b_kv"], 0, 0, "kv_proj")
        if l == N_A + 1:
            w.update(weights_for(3, x_in))
        if l < N_A:
            xa = _pool_fwd(x_in, n1, w["pool_w"], w["pool_scale"], l)
            q = o = None
        else:
            j = l - N_A
            q = _rms_linear(x_in, n1, w["w_q", j], sp["b_q"], l, j, "q_proj")
            o = _attn_fwd(q, kv, sp["sinks"], j)
            xa = _linear_res(o, w["w_o", j], sp["b_o"], x_in, j)
        if l == 1:
            w.update(weights_for(1, xa))
        xs, u = _ffn_fwd(xa, n2, w["ffn_up", l], w["ffn_down", l], w["ffn_conv_w"], sp["ffn_conv_b"], l)
        saved.append((x_in, xa, u, q, o))

    dx, d_final_g, loss = _loss_head(xs, sp["final_g"], tgt)

    g = {k: [None] * DEPTH for k in ("norm1_g", "norm2_g", "ffn_conv_w", "ffn_conv_b")}
    for k in ("pool_scale", "b_q", "sinks", "b_o"):
        g[k] = [None] * N_A
    g["final_g"] = d_final_g
    dkvs = []
    pending = {}
    for l in reversed(range(DEPTH)):
        x_in, xa, u, q, o = saved[l]
        dxa, du, a, hb, g["norm2_g"][l], g["ffn_conv_w"][l], g["ffn_conv_b"][l] = _ffn_bwd(
            xa, dx, u, n2, w["ffn_up", l], w["ffn_down", l], w["ffn_conv_w"], sp["ffn_conv_b"], l)
        pending["ffn_up", l] = _tn_matmul(hb, du, MX, "d_ffn_up")
        pending["ffn_down", l] = _tn_matmul(a, dx, MX, "d_ffn_down")
        zero = on_grads(DEPTH - 1 - l, pending)
        pending = {}
        n1, n2 = n1 + zero, n2 + zero
        if l < N_A:
            dx, d_pw, g["pool_scale"][l], g["norm1_g"][l] = _pool_bwd(
                x_in, dxa, n1, w["pool_w"], w["pool_scale"], l)
            pending["pool_w", l] = d_pw.astype(MX)
        else:
            j = l - N_A
            d_o, g["b_o"][j] = _linear_nt(dxa, w["w_o", j])
            pending["w_o", j] = _tn_matmul(o, dxa, MX, "d_w_o")
            dq, dkv, g["sinks"][j] = _attn_bwd(q, d_o, kv, sp["sinks"], j)
            dkvs.append(dkv)
            dx, g["norm1_g"][l], g["b_q"][j], hq, dqb = _rms_linear_bwd(
                x_in, n1, [dq], w["w_q", j], dxa, l, "q_proj_bwd")
            pending["w_q", j] = _tn_matmul(hq, dqb, MX, "d_w_q")
        if l == N_A:
            dx, g["kv_norm_g"], g["b_kv"], hk, dkvb = _rms_linear_bwd(
                x_in, sp["kv_norm_g"], dkvs, w["w_kv"], dx, 0, "kv_proj_bwd")
            pending["w_kv", 0] = _tn_matmul(hk, dkvb, MX, "d_w_kv")
    on_grads(DEPTH, pending)
    return loss, dx, g


SMALL = ("norm1_g", "norm2_g", "kv_norm_g", "b_kv", "b_q", "sinks", "b_o", "ffn_conv_b", "final_g")
SMALL_SHARDED = ("pool_scale", "ffn_conv_w")
BIG = ("pool_w", "w_kv", "w_q", "w_o", "ffn_up", "ffn_down")
ORDER = ("norm1_g", "norm2_g", "pool_w", "pool_scale", "kv_norm_g", "w_kv", "b_kv", "w_q", "b_q", "sinks",
         "w_o", "b_o", "ffn_up", "ffn_conv_w", "ffn_conv_b", "ffn_down", "final_g")


def _as3d(a):
    return a.reshape((-1,) + a.shape[-2:])


def kernel(x, norm1_g, norm2_g, pool_w, pool_scale, kv_norm_g, w_kv, b_kv, w_q, b_q, sinks, w_o, b_o, ffn_up, ffn_conv_w, ffn_conv_b, ffn_down, final_g, loss_target, m_norm1_g, m_norm2_g, m_pool_w, m_pool_scale, m_kv_norm_g, m_w_kv, m_b_kv, m_w_q, m_b_q, m_sinks, m_w_o, m_b_o, m_ffn_up, m_ffn_conv_w, m_ffn_conv_b, m_ffn_down, m_final_g, v_norm1_g, v_norm2_g, v_pool_w, v_pool_scale, v_kv_norm_g, v_w_kv, v_b_kv, v_w_q, v_b_q, v_sinks, v_w_o, v_b_o, v_ffn_up, v_ffn_conv_w, v_ffn_conv_b, v_ffn_down, v_final_g):
    W = dict(norm1_g=norm1_g, norm2_g=norm2_g, pool_w=pool_w, pool_scale=pool_scale, kv_norm_g=kv_norm_g,
             w_kv=w_kv, b_kv=b_kv, w_q=w_q, b_q=b_q, sinks=sinks, w_o=w_o, b_o=b_o, ffn_up=ffn_up,
             ffn_conv_w=ffn_conv_w, ffn_conv_b=ffn_conv_b, ffn_down=ffn_down, final_g=final_g)
    M = dict(norm1_g=m_norm1_g, norm2_g=m_norm2_g, pool_w=m_pool_w, pool_scale=m_pool_scale,
             kv_norm_g=m_kv_norm_g, w_kv=m_w_kv, b_kv=m_b_kv, w_q=m_w_q, b_q=m_b_q, sinks=m_sinks, w_o=m_w_o,
             b_o=m_b_o, ffn_up=m_ffn_up, ffn_conv_w=m_ffn_conv_w, ffn_conv_b=m_ffn_conv_b, ffn_down=m_ffn_down,
             final_g=m_final_g)
    V = dict(norm1_g=v_norm1_g, norm2_g=v_norm2_g, pool_w=v_pool_w, pool_scale=v_pool_scale,
             kv_norm_g=v_kv_norm_g, w_kv=v_w_kv, b_kv=v_b_kv, w_q=v_w_q, b_q=v_b_q, sinks=v_sinks, w_o=v_w_o,
             b_o=v_b_o, ffn_up=v_ffn_up, ffn_conv_w=v_ffn_conv_w, ffn_conv_b=v_ffn_conv_b, ffn_down=v_ffn_down,
             final_g=v_final_g)
    S = x.shape[1]
    chip = 2 * lax.axis_index("x") + lax.axis_index("y")

    gather_axis = dict(pool_w=1, w_kv=1, w_q=1, w_o=1, ffn_up=2, ffn_down=1, pool_scale=2, ffn_conv_w=2)
    per_layer = lambda k, l: _as3d(W[k][l].astype(MX))
    stages = [
        [("pool_w", _as3d(pool_w.astype(MX))), ("pool_scale", _as3d(pool_scale)), ("ffn_conv_w", ffn_conv_w),
         (("ffn_up", 0), per_layer("ffn_up", 0)), (("ffn_down", 0), per_layer("ffn_down", 0))],
        [(("ffn_up", 1), per_layer("ffn_up", 1)), (("ffn_down", 1), per_layer("ffn_down", 1))],
        [("w_kv", _as3d(w_kv.astype(MX))), (("w_q", 0), per_layer("w_q", 0)), (("w_o", 0), per_layer("w_o", 0)),
         (("ffn_up", 2), per_layer("ffn_up", 2)), (("ffn_down", 2), per_layer("ffn_down", 2))],
        [(("w_q", 1), per_layer("w_q", 1)), (("w_o", 1), per_layer("w_o", 1)),
         (("ffn_up", 3), per_layer("ffn_up", 3)), (("ffn_down", 3), per_layer("ffn_down", 3))],
    ]
    axis_of = lambda key: gather_axis[key if isinstance(key, str) else key[0]]
    all_keys = [k for st in stages for k, _ in st]
    placed = dict(zip(all_keys, _gather_place([a for st in stages for _, a in st], [axis_of(k) for k in all_keys])))
    gathers, zero = [], 0.0
    for si, st in enumerate(stages):
        keys = [k for k, _ in st]
        axes = [axis_of(k) for k in keys]
        send, recv, srcs, lands, token = _push_start(
            f"gather_start_{si}", [a for _, a in st], [placed[k] for k in keys], _gather_views(axes))
        gathers.append((keys, axes, send, recv, srcs, lands))
        zero = zero + token[0, 0]

    def weights_for(stage, after):
        keys, axes, send, recv, srcs, lands = gathers[stage]
        out = dict(zip(keys, _push_wait(f"gather_wait_{stage}", send, recv, srcs, lands, _gather_views(axes), after)))
        if stage == 0:
            out["pool_w"] = out["pool_w"].reshape(N_A, 4, GC, GC)
            out["pool_scale"] = out["pool_scale"].reshape(N_A, 1, D)
        return out

    scatters = []

    def on_grads(stage, grads):
        keys = list(grads)
        axes = [axis_of(k) for k in keys]
        arrs = [_as3d(grads[k]) for k in keys]
        lands = _scatter_place("scatter_place", arrs, axes)
        send, recv, srcs, lands, token = _push_start(f"scatter_start_{stage}", arrs, lands, _scatter_views(axes))
        scatters.append((stage, keys, axes, send, recv, srcs, lands))
        return token[0, 0]

    sp = dict(
        norm1_g=norm1_g.reshape(DEPTH, 1, D) + zero, norm2_g=norm2_g.reshape(DEPTH, 1, D),
        kv_norm_g=kv_norm_g.reshape(1, 1, D), b_kv=b_kv.reshape(1, 1, 4 * HD), b_q=b_q.reshape(N_B, 1, D),
        sinks=sinks, b_o=b_o.reshape(N_B, 1, D), ffn_conv_b=ffn_conv_b.reshape(DEPTH, 1, F2),
        final_g=final_g.reshape(1, D))

    x2d = x.reshape(S, D)
    loss, grad_x, g = _local_step(x2d, loss_target.reshape(S, D), sp, weights_for, on_grads)

    small_full = dict(
        norm1_g=jnp.stack(g["norm1_g"]), norm2_g=jnp.stack(g["norm2_g"]), kv_norm_g=g["kv_norm_g"],
        b_kv=g["b_kv"], b_q=jnp.stack(g["b_q"]), sinks=jnp.stack([s[0, :NH] for s in g["sinks"]]),
        b_o=jnp.stack(g["b_o"]), ffn_conv_b=jnp.stack(g["ffn_conv_b"]), final_g=g["final_g"],
        pool_scale=jnp.stack(g["pool_scale"]), ffn_conv_w=jnp.stack(g["ffn_conv_w"]))
    small_names = SMALL + SMALL_SHARDED
    small_shapes = [tuple(W[k].shape) for k in SMALL] + [(N_A, D), (DEPTH, 3, F2)]
    packed = _pack([small_full[k] for k in small_names] + [loss])
    red = _unpack(_all_reduce_small(packed), small_shapes + [(1, LANES)])
    red_g = dict(zip(small_names, red[:-1]))
    loss_out = red[-1][0, 0]
    red_g["pool_scale"] = lax.dynamic_slice_in_dim(red_g["pool_scale"], chip * (D // 4), D // 4, axis=1)
    red_g["ffn_conv_w"] = lax.dynamic_slice_in_dim(red_g["ffn_conv_w"], chip * (F2 // 4), F2 // 4, axis=2)
    small_w_shapes = [tuple(W[k].shape) for k in small_names]
    pk = lambda d: _pack([d[k] for k in small_names])[None]
    res = _adamw([[_pack([red_g[k] for k in small_names])]], pk(W), pk(M), pk(V))
    out_g, out_d, out_m, out_v = [dict(zip(small_names, _unpack(r, small_w_shapes))) for r in res]

    pkeys, partial = [], []
    for stage, keys, axes, send, recv, srcs, lands in scatters:
        landed = _push_wait(f"scatter_wait_{stage}", send, recv, srcs, lands, _scatter_views(axes), grad_x)
        for k, r in zip(keys, landed):
            pkeys.append(k)
            partial.append(_sum_slots(r.reshape(4, -1, r.shape[-1])))
    mine = dict(zip(pkeys, partial))
    theirs = dict(zip(pkeys, _swap_with_sibling(partial)))
    for k in BIG:
        n_l = len([pk_ for pk_ in pkeys if pk_[0] == k])
        shp = W[k].shape
        rows, cols = mine[k, 0].shape
        three_d = lambda a: a.reshape(n_l, rows, cols)
        res = _adamw([[mine[k, l], theirs[k, l]] for l in range(n_l)], three_d(W[k]), three_d(M[k]), three_d(V[k]))
        out_g[k], out_d[k], out_m[k], out_v[k] = [r.reshape(shp) for r in res]

    return (loss_out, grad_x.reshape(x.shape), *[out_g[k] for k in ORDER], *[out_d[k] for k in ORDER],
            *[out_m[k] for k in ORDER], *[out_v[k] for k in ORDER])
```

```python
import functools

import jax
import jax.numpy as jnp
from jax import lax
from jax.experimental import pallas as pl
from jax.experimental.pallas import tpu as pltpu

D = 1024
DEPTH = 4
N_A = 2
N_B = 2
WINS = (2, 4, 8, 16)
GC = 256
HD = 64
NH = 16
BLK = 128
F = 2816
F2 = 2 * F
EPS = 1e-5
SCALE = HD ** -0.5
NEG = -1e30
HALO = 16
TN = 256
LANES = 128
SUBLANES = 8
VMEM_LIMIT = 56 * 1024 * 1024
FFN_VMEM_LIMIT = 60 * 1024 * 1024
ACC_BYTES = 6 * 1024 * 1024
EW_BYTES = 1024 * 1024

LR, B1, B2, AEPS, WD, STEP = 0.001, 0.9, 0.999, 1e-08, 0.01, 10

MX = jnp.bfloat16
F32 = jnp.float32
MESH = pl.DeviceIdType.MESH


def _cp(n_axes=1, vmem=VMEM_LIMIT):
    return pltpu.CompilerParams(dimension_semantics=("arbitrary",) * n_axes, vmem_limit_bytes=vmem)


def _dot(a, b):
    return jnp.dot(a, b, preferred_element_type=F32)


def _dot_nt(a, b):
    return lax.dot_general(a, b, (((1,), (1,)), ((), ())), preferred_element_type=F32)


def _dot_tn(a, b):
    return lax.dot_general(a, b, (((0,), (0,)), ((), ())), preferred_element_type=F32)


def _rms_fwd(x, g):
    r = lax.rsqrt(jnp.mean(x * x, axis=-1, keepdims=True) + EPS)
    xh = x * r
    return xh * g, xh, r


def _rms_bwd(dh, xh, r, g):
    dxh = dh * g
    return r * (dxh - xh * jnp.mean(dxh * xh, axis=-1, keepdims=True))


def _row_tile(s, want):
    return min(s, want)


def _pool_pm(e, h, row, tm):
    out = []
    for gi, win in enumerate(WINS):
        cols = slice(gi * GC, (gi + 1) * GC)
        s = e[:, cols]
        sh = 1
        while sh < win:
            s = s + pltpu.roll(s, sh, 0)
            sh *= 2
        inv = 1.0 / jnp.minimum(row + 1, win).astype(F32)
        out.append(s[HALO:] * inv - h[:, cols])
    return out


def _pool_fwd(x, g, pw, ps, layer):
    S = x.shape[0]
    tm = _row_tile(S, 512)
    hb = tm // HALO

    def body(x_ref, xh_ref, g_ref, pw_ref, ps_ref, o_ref):
        i = pl.program_id(0)
        x = x_ref[...]
        gg = g_ref[...]
        h, _, _ = _rms_fwd(x, gg)
        hh, _, _ = _rms_fwd(xh_ref[...], gg)
        hh = jnp.where(i > 0, hh, 0.0)
        e = jnp.concatenate([hh, h], axis=0)
        row = i * tm + lax.broadcasted_iota(jnp.int32, (tm, 1), 0)
        pm = _pool_pm(e, h, row, tm)
        for gi in range(len(WINS)):
            cols = slice(gi * GC, (gi + 1) * GC)
            z = _dot(pm[gi].astype(MX), pw_ref[gi])
            o_ref[:, cols] = x[:, cols] + z * ps_ref[:, cols]

    return pl.pallas_call(
        body, name="pool_fwd",
        grid=(S // tm,),
        in_specs=[
            pl.BlockSpec((tm, D), lambda i: (i, 0)),
            pl.BlockSpec((HALO, D), lambda i: (jnp.maximum(i * hb - 1, 0), 0)),
            pl.BlockSpec((None, 1, D), lambda i: (layer, 0, 0)),
            pl.BlockSpec((None, 4, GC, GC), lambda i: (layer, 0, 0, 0)),
            pl.BlockSpec((None, 1, D), lambda i: (layer, 0, 0)),
        ],
        out_specs=pl.BlockSpec((tm, D), lambda i: (i, 0)),
        out_shape=jax.ShapeDtypeStruct((S, D), F32),
        compiler_params=_cp(),
    )(x, x, g, pw, ps)


def _pool_bwd(x, dy, g, pw, ps, layer):
    S = x.shape[0]
    tm = _row_tile(S, 256)
    hb = tm // HALO
    n_i = S // tm
    n_h = S // HALO

    def body(x_ref, xh_ref, dy_ref, dyn_ref, g_ref, pw_ref, ps_ref, dx_ref, dpw_ref, dps_ref, dg_ref):
        i = pl.program_id(0)

        @pl.when(i == 0)
        def _():
            dpw_ref[...] = jnp.zeros_like(dpw_ref)
            dps_ref[...] = jnp.zeros_like(dps_ref)
            dg_ref[...] = jnp.zeros_like(dg_ref)

        x = x_ref[...]
        gg = g_ref[...]
        ps = ps_ref[...]
        h, xh, r = _rms_fwd(x, gg)
        hh, _, _ = _rms_fwd(xh_ref[...], gg)
        hh = jnp.where(i > 0, hh, 0.0)
        e = jnp.concatenate([hh, h], axis=0)
        row = i * tm + lax.broadcasted_iota(jnp.int32, (tm, 1), 0)
        rown = (i + 1) * tm + lax.broadcasted_iota(jnp.int32, (HALO, 1), 0)
        pm = _pool_pm(e, h, row, tm)
        dy = dy_ref[...]
        dz = dy * ps
        dzn = jnp.where(i < n_i - 1, dyn_ref[...] * ps, 0.0)
        parts = []
        for gi, win in enumerate(WINS):
            cols = slice(gi * GC, (gi + 1) * GC)
            w = pw_ref[gi]
            pmb = pm[gi].astype(MX)
            z = _dot(pmb, w)
            dps_ref[:, cols] += jnp.sum(dy[:, cols] * z, axis=0, keepdims=True)
            dzb = dz[:, cols].astype(MX)
            dpw_ref[gi] += _dot_tn(pmb, dzb)
            dpm = _dot_nt(dzb, w)
            dpmn = _dot_nt(dzn[:, cols].astype(MX), w)
            q = dpm * (1.0 / jnp.minimum(row + 1, win).astype(F32))
            qn = dpmn * (1.0 / jnp.minimum(rown + 1, win).astype(F32))
            s = jnp.concatenate([q, qn], axis=0)
            sh = 1
            while sh < win:
                s = s + pltpu.roll(s, tm + HALO - sh, 0)
                sh *= 2
            parts.append(s[:tm] - dpm)
        dh = jnp.concatenate(parts, axis=1)
        dg_ref[...] += jnp.sum(dh * xh, axis=0, keepdims=True)
        dx_ref[...] = dy + _rms_bwd(dh, xh, r, gg)

    return pl.pallas_call(
        body, name="pool_bwd",
        grid=(n_i,),
        in_specs=[
            pl.BlockSpec((tm, D), lambda i: (i, 0)),
            pl.BlockSpec((HALO, D), lambda i: (jnp.maximum(i * hb - 1, 0), 0)),
            pl.BlockSpec((tm, D), lambda i: (i, 0)),
            pl.BlockSpec((HALO, D), lambda i: (jnp.minimum((i + 1) * hb, n_h - 1), 0)),
            pl.BlockSpec((None, 1, D), lambda i: (layer, 0, 0)),
            pl.BlockSpec((None, 4, GC, GC), lambda i: (layer, 0, 0, 0)),
            pl.BlockSpec((None, 1, D), lambda i: (layer, 0, 0)),
        ],
        out_specs=[
            pl.BlockSpec((tm, D), lambda i: (i, 0)),
            pl.BlockSpec((4, GC, GC), lambda i: (0, 0, 0)),
            pl.BlockSpec((1, D), lambda i: (0, 0)),
            pl.BlockSpec((1, D), lambda i: (0, 0)),
        ],
        out_shape=[
            jax.ShapeDtypeStruct((S, D), F32),
            jax.ShapeDtypeStruct((4, GC, GC), F32),
            jax.ShapeDtypeStruct((1, D), F32),
            jax.ShapeDtypeStruct((1, D), F32),
        ],
        compiler_params=_cp(),
    )(x, x, dy, dy, g, pw, ps)


def _conv(u, prev, cw, rowi):
    um1 = jnp.where(rowi == 0, prev[7:8], pltpu.roll(u, 1, 0))
    um2 = jnp.where(rowi == 0, prev[6:7], jnp.where(rowi == 1, prev[7:8], pltpu.roll(u, 2, 0)))
    return cw[0:1] * um2 + cw[1:2] * um1 + cw[2:3] * u, um1, um2


def _ffn_fwd(x, g, wup, wdn, cw, cb, layer):
    S = x.shape[0]
    tm = _row_tile(S, 512)

    def body(x_ref, g_ref, wup_hbm, wdn_hbm, cw_ref, cb_ref, o_ref, u_ref, uc_ref, wup_v, wdn_v, carry, act):
        i = pl.program_id(0)

        @pl.when(i == 0)
        def _():
            pltpu.sync_copy(wup_hbm.at[0], wup_v)
            pltpu.sync_copy(wdn_hbm.at[0], wdn_v)
            carry[...] = jnp.zeros_like(carry)

        x = x_ref[...]
        h, _, _ = _rms_fwd(x, g_ref[...])
        hb = h.astype(MX)
        rowi = lax.broadcasted_iota(jnp.int32, (tm, 1), 0)
        for j in range(F // TN):
            cg = slice(j * TN, (j + 1) * TN)
            cv = slice(F + j * TN, F + (j + 1) * TN)
            ug = _dot(hb, wup_v[:, cg])
            uv = _dot(hb, wup_v[:, cv])
            u_ref[:, cg] = ug.astype(u_ref.dtype)
            u_ref[:, cv] = uv.astype(u_ref.dtype)
            gt, _, _ = _conv(ug, carry[:, cg], cw_ref[:, cg], rowi)
            vl, _, _ = _conv(uv, carry[:, cv], cw_ref[:, cv], rowi)
            carry[:, cg] = ug[tm - SUBLANES:]
            carry[:, cv] = uv[tm - SUBLANES:]
            gt = gt + cb_ref[:, cg]
            vl = vl + cb_ref[:, cv]
            uc_ref[:, cg] = gt.astype(uc_ref.dtype)
            uc_ref[:, cv] = vl.astype(uc_ref.dtype)
            act[:, cg] = (gt * jax.nn.sigmoid(gt) * vl).astype(act.dtype)
        o_ref[...] = x + _dot(act[...], wdn_v[...])

    return pl.pallas_call(
        body, name="ffn_fwd",
        grid=(S // tm,),
        in_specs=[
            pl.BlockSpec((tm, D), lambda i: (i, 0)),
            pl.BlockSpec((None, 1, D), lambda i: (layer, 0, 0)),
            pl.BlockSpec(memory_space=pl.ANY),
            pl.BlockSpec(memory_space=pl.ANY),
            pl.BlockSpec((None, 3, F2), lambda i: (layer, 0, 0)),
            pl.BlockSpec((None, 1, F2), lambda i: (layer, 0, 0)),
        ],
        out_specs=[
            pl.BlockSpec((tm, D), lambda i: (i, 0)),
            pl.BlockSpec((tm, F2), lambda i: (i, 0)),
            pl.BlockSpec((tm, F2), lambda i: (i, 0)),
        ],
        out_shape=[
            jax.ShapeDtypeStruct((S, D), F32),
            jax.ShapeDtypeStruct((S, F2), MX),
            jax.ShapeDtypeStruct((S, F2), MX),
        ],
        scratch_shapes=[
            pltpu.VMEM((D, F2), MX),
            pltpu.VMEM((F, D), MX),
            pltpu.VMEM((SUBLANES, F2), F32),
            pltpu.VMEM((tm, F), MX),
        ],
        compiler_params=_cp(vmem=FFN_VMEM_LIMIT),
    )(x, g, wup, wdn, cw, cb)


def _ffn_bwd(x, dy, u, uc, g, wup, wdn, cw, layer):
    S = x.shape[0]
    tm = _row_tile(S, 256)
    n_i = S // tm

    def body(x_ref, dy_ref, u_ref, uc_ref, g_ref, wup_hbm, wdn_hbm, cw_ref,
             dx_ref, du_ref, a_ref, h_ref, dg_ref, dcw_ref, dcb_ref, wup_v, wdn_v, carry):
        i = pl.program_id(0)

        @pl.when(i == 0)
        def _():
            pltpu.sync_copy(wup_hbm.at[0], wup_v)
            pltpu.sync_copy(wdn_hbm.at[0], wdn_v)
            carry[...] = jnp.zeros_like(carry)
            dg_ref[...] = jnp.zeros_like(dg_ref)
            dcw_ref[...] = jnp.zeros_like(dcw_ref)
            dcb_ref[...] = jnp.zeros_like(dcb_ref)

        x = x_ref[...]
        gg = g_ref[...]
        h, xh, r = _rms_fwd(x, gg)
        h_ref[...] = h.astype(h_ref.dtype)
        dy = dy_ref[...]
        dyb = dy.astype(MX)
        rowi = lax.broadcasted_iota(jnp.int32, (tm, 1), 0)
        for j in range(F // TN):
            cg = slice(j * TN, (j + 1) * TN)
            cv = slice(F + j * TN, F + (j + 1) * TN)
            gt = uc_ref[:, cg].astype(F32)
            vl = uc_ref[:, cv].astype(F32)
            sg = jax.nn.sigmoid(gt)
            sil = gt * sg
            a_ref[:, cg] = (sil * vl).astype(a_ref.dtype)
            da = _dot_nt(dyb, wdn_v[cg, :])
            dvl = da * sil
            dgt = da * vl * (sg * (1.0 + gt * (1.0 - sg)))
            for cc, dd in ((cg, dgt), (cv, dvl)):
                nxt = carry[:, cc]
                dp1 = jnp.where(rowi == tm - 1, nxt[0:1], pltpu.roll(dd, tm - 1, 0))
                dp2 = jnp.where(rowi == tm - 1, nxt[1:2],
                                jnp.where(rowi == tm - 2, nxt[0:1], pltpu.roll(dd, tm - 2, 0)))
                carry[:, cc] = dd[0:SUBLANES]
                uu = u_ref[:, cc].astype(F32)
                dcb_ref[:, cc] += jnp.sum(dd, axis=0, keepdims=True)
                dcw_ref[0:1, cc] += jnp.sum(dp2 * uu, axis=0, keepdims=True)
                dcw_ref[1:2, cc] += jnp.sum(dp1 * uu, axis=0, keepdims=True)
                dcw_ref[2:3, cc] += jnp.sum(dd * uu, axis=0, keepdims=True)
                cwc = cw_ref[:, cc]
                du_ref[:, cc] = (cwc[2:3] * dd + cwc[1:2] * dp1 + cwc[0:1] * dp2).astype(du_ref.dtype)
        dh = _dot_nt(du_ref[...], wup_v[...])
        dg_ref[...] += jnp.sum(dh * xh, axis=0, keepdims=True)
        dx_ref[...] = dy + _rms_bwd(dh, xh, r, gg)

    rev = lambda i: (n_i - 1 - i, 0)
    return pl.pallas_call(
        body, name="ffn_bwd",
        grid=(n_i,),
        in_specs=[
            pl.BlockSpec((tm, D), rev),
            pl.BlockSpec((tm, D), rev),
            pl.BlockSpec((tm, F2), rev),
            pl.BlockSpec((tm, F2), rev),
            pl.BlockSpec((None, 1, D), lambda i: (layer, 0, 0)),
            pl.BlockSpec(memory_space=pl.ANY),
            pl.BlockSpec(memory_space=pl.ANY),
            pl.BlockSpec((None, 3, F2), lambda i: (layer, 0, 0)),
        ],
        out_specs=[
            pl.BlockSpec((tm, D), rev),
            pl.BlockSpec((tm, F2), rev),
            pl.BlockSpec((tm, F), rev),
            pl.BlockSpec((tm, D), rev),
            pl.BlockSpec((1, D), lambda i: (0, 0)),
            pl.BlockSpec((3, F2), lambda i: (0, 0)),
            pl.BlockSpec((1, F2), lambda i: (0, 0)),
        ],
        out_shape=[
            jax.ShapeDtypeStruct((S, D), F32),
            jax.ShapeDtypeStruct((S, F2), MX),
            jax.ShapeDtypeStruct((S, F), MX),
            jax.ShapeDtypeStruct((S, D), MX),
            jax.ShapeDtypeStruct((1, D), F32),
            jax.ShapeDtypeStruct((3, F2), F32),
            jax.ShapeDtypeStruct((1, F2), F32),
        ],
        scratch_shapes=[
            pltpu.VMEM((D, F2), MX),
            pltpu.VMEM((F, D), MX),
            pltpu.VMEM((SUBLANES, F2), F32),
        ],
        compiler_params=_cp(vmem=FFN_VMEM_LIMIT),
    )(x, dy, u, uc, g, wup, wdn, cw)


def _tn_matmul(a, b, out_dtype, name):
    S, M = a.shape
    N = b.shape[1]
    bn = N
    while M * bn * 4 > ACC_BYTES and bn % (2 * LANES) == 0:
        bn //= 2
    bk = _row_tile(S, 1024)
    nk = S // bk

    def body(a_ref, b_ref, o_ref, acc):
        k = pl.program_id(1)
        p = _dot_tn(a_ref[...].astype(MX), b_ref[...].astype(MX))

        @pl.when(k == 0)
        def _():
            acc[...] = p

        @pl.when(k > 0)
        def _():
            acc[...] += p

        @pl.when(k == nk - 1)
        def _():
            o_ref[...] = acc[...].astype(o_ref.dtype)

    return pl.pallas_call(
        body, name=name,
        grid=(N // bn, nk),
        in_specs=[
            pl.BlockSpec((bk, M), lambda j, k: (k, 0)),
            pl.BlockSpec((bk, bn), lambda j, k: (k, j)),
        ],
        out_specs=pl.BlockSpec((M, bn), lambda j, k: (0, j)),
        out_shape=jax.ShapeDtypeStruct((M, N), out_dtype),
        scratch_shapes=[pltpu.VMEM((M, bn), F32)],
        compiler_params=_cp(2),
    )(a, b)


def _rms_linear(x, g, w, b, g_layer, b_layer, name):
    S = x.shape[0]
    N = w.shape[-1]
    tm = _row_tile(S, 512)

    def body(x_ref, g_ref, w_ref, b_ref, o_ref):
        h, _, _ = _rms_fwd(x_ref[...], g_ref[...])
        o_ref[...] = (_dot(h.astype(MX), w_ref[...]) + b_ref[...]).astype(o_ref.dtype)

    return pl.pallas_call(
        body, name=name,
        grid=(S // tm,),
        in_specs=[
            pl.BlockSpec((tm, D), lambda i: (i, 0)),
            pl.BlockSpec((None, 1, D), lambda i: (g_layer, 0, 0)),
            pl.BlockSpec((None, D, N), lambda i: (0, 0, 0)),
            pl.BlockSpec((None, 1, N), lambda i: (b_layer, 0, 0)),
        ],
        out_specs=pl.BlockSpec((tm, N), lambda i: (i, 0)),
        out_shape=jax.ShapeDtypeStruct((S, N), MX),
        compiler_params=_cp(),
    )(x, g, w, b)


def _linear_res(o, w, b, xres, layer):
    S = o.shape[0]
    tm = _row_tile(S, 512)

    def body(o_ref, w_ref, b_ref, x_ref, y_ref):
        y_ref[...] = x_ref[...] + _dot(o_ref[...], w_ref[...]) + b_ref[...]

    return pl.pallas_call(
        body, name="o_proj",
        grid=(S // tm,),
        in_specs=[
            pl.BlockSpec((tm, D), lambda i: (i, 0)),
            pl.BlockSpec((None, D, D), lambda i: (0, 0, 0)),
            pl.BlockSpec((None, 1, D), lambda i: (layer, 0, 0)),
            pl.BlockSpec((tm, D), lambda i: (i, 0)),
        ],
        out_specs=pl.BlockSpec((tm, D), lambda i: (i, 0)),
        out_shape=jax.ShapeDtypeStruct((S, D), F32),
        compiler_params=_cp(),
    )(o, w, b, xres)


def _linear_nt(dy, w):
    S = dy.shape[0]
    tm = _row_tile(S, 512)

    def body(dy_ref, w_ref, o_ref, db_ref):
        @pl.when(pl.program_id(0) == 0)
        def _():
            db_ref[...] = jnp.zeros_like(db_ref)

        dy = dy_ref[...]
        db_ref[...] += jnp.sum(dy, axis=0, keepdims=True)
        o_ref[...] = _dot_nt(dy.astype(MX), w_ref[...]).astype(o_ref.dtype)

    return pl.pallas_call(
        body, name="o_proj_bwd",
        grid=(S // tm,),
        in_specs=[
            pl.BlockSpec((tm, D), lambda i: (i, 0)),
            pl.BlockSpec((None, D, D), lambda i: (0, 0, 0)),
        ],
        out_specs=[
            pl.BlockSpec((tm, D), lambda i: (i, 0)),
            pl.BlockSpec((1, D), lambda i: (0, 0)),
        ],
        out_shape=[
            jax.ShapeDtypeStruct((S, D), MX),
            jax.ShapeDtypeStruct((1, D), F32),
        ],
        compiler_params=_cp(),
    )(dy, w)


def _rms_linear_bwd(x, g, dzs, w, dy, g_layer, name):
    S = x.shape[0]
    N = w.shape[-1]
    tm = _row_tile(S, 512)
    nz = len(dzs)

    def body(*refs):
        x_ref, g_ref = refs[0], refs[1]
        dz_refs = refs[2:2 + nz]
        w_ref, dy_ref, dx_ref, dg_ref, db_ref, h_ref, dzb_ref = refs[2 + nz:]

        @pl.when(pl.program_id(0) == 0)
        def _():
            dg_ref[...] = jnp.zeros_like(dg_ref)
            db_ref[...] = jnp.zeros_like(db_ref)

        gg = g_ref[...]
        h, xh, r = _rms_fwd(x_ref[...], gg)
        h_ref[...] = h.astype(h_ref.dtype)
        dz = dz_refs[0][...].astype(F32)
        for zr in dz_refs[1:]:
            dz = dz + zr[...].astype(F32)
        db_ref[...] += jnp.sum(dz, axis=0, keepdims=True)
        dzb = dz.astype(MX)
        dzb_ref[...] = dzb
        dh = _dot_nt(dzb, w_ref[...])
        dg_ref[...] += jnp.sum(dh * xh, axis=0, keepdims=True)
        dx_ref[...] = dy_ref[...] + _rms_bwd(dh, xh, r, gg)

    return pl.pallas_call(
        body, name=name,
        grid=(S // tm,),
        in_specs=[
            pl.BlockSpec((tm, D), lambda i: (i, 0)),
            pl.BlockSpec((None, 1, D), lambda i: (g_layer, 0, 0)),
        ] + [pl.BlockSpec((tm, N), lambda i: (i, 0))] * nz + [
            pl.BlockSpec((None, D, N), lambda i: (0, 0, 0)),
            pl.BlockSpec((tm, D), lambda i: (i, 0)),
        ],
        out_specs=[
            pl.BlockSpec((tm, D), lambda i: (i, 0)),
            pl.BlockSpec((1, D), lambda i: (0, 0)),
            pl.BlockSpec((1, N), lambda i: (0, 0)),
            pl.BlockSpec((tm, D), lambda i: (i, 0)),
            pl.BlockSpec((tm, N), lambda i: (i, 0)),
        ],
        out_shape=[
            jax.ShapeDtypeStruct((S, D), F32),
            jax.ShapeDtypeStruct((1, D), F32),
            jax.ShapeDtypeStruct((1, N), F32),
            jax.ShapeDtypeStruct((S, D), MX),
            jax.ShapeDtypeStruct((S, N), MX),
        ],
        compiler_params=_cp(),
    )(x, g, *dzs, w, dy)


def _loss_head(x, g, tgt):
    S = x.shape[0]
    tm = _row_tile(S, 512)

    def body(x_ref, g_ref, t_ref, dx_ref, dg_ref, l_ref):
        @pl.when(pl.program_id(0) == 0)
        def _():
            dg_ref[...] = jnp.zeros_like(dg_ref)
            l_ref[...] = jnp.zeros_like(l_ref)

        gg = g_ref[...]
        y, xh, r = _rms_fwd(x_ref[...], gg)
        err = y - t_ref[...]
        tok = jnp.sum(err * err, axis=-1, keepdims=True) * (1.0 / D)
        l_ref[...] += 0.5 * jnp.sum(tok, axis=0, keepdims=True)
        dyv = err * (1.0 / D)
        dg_ref[...] += jnp.sum(dyv * xh, axis=0, keepdims=True)
        dx_ref[...] = _rms_bwd(dyv, xh, r, gg)

    return pl.pallas_call(
        body, name="loss_head",
        grid=(S // tm,),
        in_specs=[
            pl.BlockSpec((tm, D), lambda i: (i, 0)),
            pl.BlockSpec((1, D), lambda i: (0, 0)),
            pl.BlockSpec((tm, D), lambda i: (i, 0)),
        ],
        out_specs=[
            pl.BlockSpec((tm, D), lambda i: (i, 0)),
            pl.BlockSpec((1, D), lambda i: (0, 0)),
            pl.BlockSpec((1, LANES), lambda i: (0, 0)),
        ],
        out_shape=[
            jax.ShapeDtypeStruct((S, D), F32),
            jax.ShapeDtypeStruct((1, D), F32),
            jax.ShapeDtypeStruct((1, LANES), F32),
        ],
        compiler_params=_cp(),
    )(x, g, tgt)


def _attn_setup(kvp_ref, kvc_ref, n):
    kw = jnp.concatenate([kvp_ref[...], kvc_ref[...]], axis=0).astype(F32)
    kk, vv = kw[:, :LANES], kw[:, LANES:]
    lo = lax.broadcasted_iota(jnp.int32, (1, LANES), 1) < HD
    kr, vr = pltpu.roll(kk, HD, 1), pltpu.roll(vv, HD, 1)
    ks = [jnp.where(lo, kk, kr).astype(MX), jnp.where(lo, kr, kk).astype(MX)]
    vs = [jnp.where(lo, vv, vr).astype(MX), jnp.where(lo, vr, vv).astype(MX)]
    qi = lax.broadcasted_iota(jnp.int32, (BLK, 2 * BLK), 0)
    si = lax.broadcasted_iota(jnp.int32, (BLK, 2 * BLK), 1)
    mask = (si > qi) & (si <= qi + BLK) & jnp.logical_or(n > 0, si >= BLK)
    return ks, vs, lo, mask


def _attn_probs(qm, kg, mask, sink):
    s = jnp.where(mask, _dot_nt(qm, kg) * SCALE, NEG)
    m = jnp.maximum(jnp.max(s, axis=-1, keepdims=True), sink)
    p = jnp.exp(s - m)
    es = jnp.exp(sink - m)
    inv = 1.0 / (jnp.sum(p, axis=-1, keepdims=True) + es)
    return p * inv, es * inv


def _attn_specs(n_extra_q):
    q_spec = pl.BlockSpec((BLK, D), lambda n: (n, 0))
    return [q_spec] * n_extra_q + [
        pl.BlockSpec((BLK, 4 * HD), lambda n: (jnp.maximum(n - 1, 0), 0)),
        pl.BlockSpec((BLK, 4 * HD), lambda n: (n, 0)),
        pl.BlockSpec(memory_space=pltpu.SMEM),
    ]


def _attn_fwd(q, kv, sinks, layer):
    S = q.shape[0]

    def body(q_ref, kvp_ref, kvc_ref, sk_ref, o_ref):
        n = pl.program_id(0)
        ks, vs, lo, mask = _attn_setup(kvp_ref, kvc_ref, n)
        for j in range(NH // 2):
            cols = slice(j * LANES, (j + 1) * LANES)
            grp = (2 * j) // (NH // 2)
            qp = q_ref[:, cols]
            outs = []
            for half in range(2):
                sel = lo if half == 0 else jnp.logical_not(lo)
                qm = jnp.where(sel, qp, jnp.zeros_like(qp))
                pr, _ = _attn_probs(qm, ks[grp], mask, sk_ref[layer, 2 * j + half])
                outs.append(_dot(pr.astype(MX), vs[grp]))
            o_ref[:, cols] = jnp.where(lo, outs[0], outs[1]).astype(o_ref.dtype)

    return pl.pallas_call(
        body, name="attn_fwd",
        grid=(S // BLK,),
        in_specs=_attn_specs(1),
        out_specs=pl.BlockSpec((BLK, D), lambda n: (n, 0)),
        out_shape=jax.ShapeDtypeStruct((S, D), MX),
        compiler_params=_cp(),
    )(q, kv, kv, sinks)


def _attn_bwd(q, do, kv, sinks, layer):
    S = q.shape[0]

    def body(q_ref, do_ref, kvp_ref, kvc_ref, sk_ref, dq_ref, dkv_ref, dsk_ref):
        n = pl.program_id(0)

        @pl.when(n == 0)
        def _():
            dkv_ref[...] = jnp.zeros_like(dkv_ref)
            dsk_ref[...] = jnp.zeros_like(dsk_ref)

        ks, vs, lo, mask = _attn_setup(kvp_ref, kvc_ref, n)
        lane = lax.broadcasted_iota(jnp.int32, (1, LANES), 1)
        dk_acc = [jnp.zeros((2 * BLK, LANES), F32) for _ in range(2)]
        dv_acc = [jnp.zeros((2 * BLK, LANES), F32) for _ in range(2)]
        dsk = jnp.zeros((1, LANES), F32)
        for j in range(NH // 2):
            cols = slice(j * LANES, (j + 1) * LANES)
            grp = (2 * j) // (NH // 2)
            qp = q_ref[:, cols]
            dop = do_ref[:, cols]
            dqs = []
            for half in range(2):
                sel = lo if half == 0 else jnp.logical_not(lo)
                qm = jnp.where(sel, qp, jnp.zeros_like(qp))
                dom = jnp.where(sel, dop, jnp.zeros_like(dop))
                pr, psink = _attn_probs(qm, ks[grp], mask, sk_ref[layer, 2 * j + half])
                dpr = _dot_nt(dom, vs[grp])
                delta = jnp.sum(pr * dpr, axis=-1, keepdims=True)
                ds = (pr * (dpr - delta) * SCALE).astype(MX)
                dsk = dsk + jnp.where(lane == 2 * j + half,
                                      -jnp.sum(psink * delta, axis=0, keepdims=True), 0.0)
                dqs.append(_dot(ds, ks[grp]))
                dk_acc[grp] = dk_acc[grp] + _dot_tn(ds, qm)
                dv_acc[grp] = dv_acc[grp] + _dot_tn(pr.astype(MX), dom)
            dq_ref[:, cols] = jnp.where(lo, dqs[0], dqs[1]).astype(dq_ref.dtype)
        dsk_ref[...] += dsk
        tk = [a + pltpu.roll(a, HD, 1) for a in dk_acc]
        tv = [a + pltpu.roll(a, HD, 1) for a in dv_acc]
        contrib = jnp.concatenate([jnp.where(lo, tk[0], tk[1]), jnp.where(lo, tv[0], tv[1])], axis=1)

        @pl.when(n > 0)
        def _():
            rows = pl.ds(pl.multiple_of((n - 1) * BLK, BLK), 2 * BLK)
            dkv_ref[rows, :] += contrib

        @pl.when(n == 0)
        def _():
            dkv_ref[0:BLK, :] += contrib[BLK:]

    return pl.pallas_call(
        body, name="attn_bwd",
        grid=(S // BLK,),
        in_specs=_attn_specs(2),
        out_specs=[
            pl.BlockSpec((BLK, D), lambda n: (n, 0)),
            pl.BlockSpec((S, 4 * HD), lambda n: (0, 0)),
            pl.BlockSpec((1, LANES), lambda n: (0, 0)),
        ],
        out_shape=[
            jax.ShapeDtypeStruct((S, D), MX),
            jax.ShapeDtypeStruct((S, 4 * HD), F32),
            jax.ShapeDtypeStruct((1, LANES), F32),
        ],
        compiler_params=_cp(),
    )(q, do, kv, kv, sinks)


def _ew_rows(rows, cols, n_bufs=1):
    br = rows
    while br * cols * 4 * n_bufs > EW_BYTES and br % (2 * SUBLANES) == 0:
        br //= 2
    return br


def _adamw(parts, w, m, v):
    L, R, C = w.shape
    br = _ew_rows(R, C)
    npart = len(parts[0])

    def body(*refs):
        p_refs = refs[:L * npart]
        w_ref, m_ref, v_ref, g_ref, d_ref, nm_ref, nv_ref = refs[L * npart:]
        lyr = pl.program_id(0)
        for l in range(L):
            @pl.when(lyr == l)
            def _(l=l):
                g = p_refs[l * npart][...]
                for pr in p_refs[l * npart + 1:(l + 1) * npart]:
                    g = g + pr[...]
                nm = B1 * m_ref[...] + (1.0 - B1) * g
                nv = B2 * v_ref[...] + (1.0 - B2) * (g * g)
                m_hat = nm / (1.0 - B1 ** STEP)
                v_hat = nv / (1.0 - B2 ** STEP)
                g_ref[...] = g
                d_ref[...] = -LR * (m_hat / (jnp.sqrt(v_hat) + AEPS) + WD * w_ref[...])
                nm_ref[...] = nm
                nv_ref[...] = nv

    spec = pl.BlockSpec((None, br, C), lambda a, i: (a, i, 0))
    part_specs = [pl.BlockSpec((br, C), lambda a, i, l=l: (jnp.where(a == l, i, 0), 0))
                  for l in range(L) for _ in range(npart)]
    return pl.pallas_call(
        body, name="adamw",
        grid=(L, R // br),
        in_specs=part_specs + [spec] * 3,
        out_specs=[spec] * 4,
        out_shape=[jax.ShapeDtypeStruct((L, R, C), F32)] * 4,
        compiler_params=_cp(2),
    )(*[a for lp in parts for a in lp], w, m, v)


def _coords():
    return lax.axis_index("x"), lax.axis_index("y"), lax.axis_index("c")


def _other_chips(x, y):
    return [(1 - x, y), (x, 1 - y), (1 - x, 1 - y)]


def _slot(ref, axis, chip, size):
    idx = [slice(None)] * 3
    idx[axis] = pl.ds(pl.multiple_of(chip * size, size), size)
    return ref.at[tuple(idx)]


HBM_SPEC = pl.BlockSpec(memory_space=pltpu.HBM)
SEM_SPEC = pl.BlockSpec(memory_space=pltpu.SEMAPHORE)
ANY_SPEC = pl.BlockSpec(memory_space=pl.ANY)
EFFECT = pltpu.SideEffectType.DATAFLOW_SIDE_EFFECTING


def _slot_specs(shape, axis, br, lead):
    _, b, c = shape
    nrb = b // br
    first = (lambda a: a) if lead is None else (lambda a: lead)
    shard = pl.BlockSpec((None, br, c), lambda a, i, me: (first(a), i, 0))
    if axis == 1:
        slot = pl.BlockSpec((None, br, c), lambda a, i, me: (a, me[0] * nrb + i, 0))
    else:
        slot = pl.BlockSpec((None, br, c), lambda a, i, me: (a, i, me[0]))
    return shard, slot


def _shard_rows(b, c):
    br = b
    while br * c * 4 > 2 * EW_BYTES and br % (4 * SUBLANES) == 0:
        br //= 2
    return br


def _gather_place(shard, axis, me, dtype, lead=None):
    a_dim, b, c = shard.shape
    if lead is not None:
        a_dim = 1
    br = _shard_rows(b, c)
    shp = [a_dim, b, c]
    shp[axis] *= 4
    shard_spec, slot_spec = _slot_specs((a_dim, b, c), axis, br, lead)

    def body(me_ref, s_ref, o_ref):
        o_ref[...] = s_ref[...].astype(o_ref.dtype)

    return pl.pallas_call(
        body, name="gather_place",
        grid_spec=pltpu.PrefetchScalarGridSpec(
            num_scalar_prefetch=1, grid=(a_dim, b // br), in_specs=[shard_spec], out_specs=slot_spec),
        out_shape=jax.ShapeDtypeStruct(tuple(shp), dtype),
        compiler_params=_cp(2),
    )(me, shard)


def _sum_landed(grad, landed, axis, me):
    a_dim, b, c = landed.shape[1:]
    br = _shard_rows(b, c)
    shard_spec, slot_spec = _slot_specs((a_dim, b, c), axis, br, None)

    def body(me_ref, own_ref, r_ref, o_ref):
        o_ref[...] = ((own_ref[...].astype(F32) + r_ref[0].astype(F32)) + r_ref[1].astype(F32)) + r_ref[2].astype(F32)

    return pl.pallas_call(
        body, name="sum_landed",
        grid_spec=pltpu.PrefetchScalarGridSpec(
            num_scalar_prefetch=1, grid=(a_dim, b // br),
            in_specs=[slot_spec, pl.BlockSpec((3, None, br, c), lambda a, i, me: (0, a, i, 0))],
            out_specs=shard_spec),
        out_shape=jax.ShapeDtypeStruct((a_dim, b, c), F32),
        compiler_params=_cp(2),
    )(me, grad, landed)


def _copies(refs, plan, send, recv):
    x, y, c = _coords()
    me = 2 * x + y
    out, t = [], 0
    while plan(refs, me, t, 0, me) is not None:
        for k, (px, py) in enumerate(_other_chips(x, y)):
            sv, dv = plan(refs, me, t, k, 2 * px + py)
            out.append(pltpu.make_async_remote_copy(
                src_ref=sv, dst_ref=dv, send_sem=send.at[3 * t + k], recv_sem=recv.at[3 * t + k],
                device_id=(px, py, c), device_id_type=MESH))
        t += 1
    return out


def _push_start(name, bufs, n_ex, plan):
    nb = len(bufs)

    def body(*refs):
        send, recv, token = refs[nb], refs[nb + 1], refs[-1]
        for cp in _copies(refs[:nb], plan, send, recv):
            cp.start()
        token[...] = jnp.zeros_like(token)

    res = pl.pallas_call(
        body, name=name,
        in_specs=[HBM_SPEC] * nb,
        out_specs=[SEM_SPEC, SEM_SPEC] + [HBM_SPEC] * nb + [pl.BlockSpec(memory_space=pltpu.VMEM)],
        out_shape=[pltpu.SemaphoreType.DMA((3 * n_ex,)), pltpu.SemaphoreType.DMA((3 * n_ex,))]
        + [pltpu.HBM(a.shape, a.dtype) for a in bufs] + [jax.ShapeDtypeStruct((SUBLANES, LANES), F32)],
        input_output_aliases={i: 2 + i for i in range(nb)},
        compiler_params=pltpu.CompilerParams(has_side_effects=EFFECT),
    )(*[pltpu.with_memory_space_constraint(a, pltpu.HBM) for a in bufs])
    return res[0], res[1], res[2:2 + nb], res[-1]


def _push_wait(name, send, recv, bufs, plan, after):
    nb = len(bufs)

    def body(*refs):
        for cp in _copies(refs[:nb], plan, refs[nb], refs[nb + 1]):
            cp.wait_send()
            cp.wait_recv()

    return pl.pallas_call(
        body, name=name,
        in_specs=[HBM_SPEC] * nb + [SEM_SPEC, SEM_SPEC, ANY_SPEC],
        out_specs=[HBM_SPEC] * nb,
        out_shape=[pltpu.HBM(a.shape, a.dtype) for a in bufs],
        input_output_aliases={i: i for i in range(nb)},
        compiler_params=pltpu.CompilerParams(has_side_effects=EFFECT),
    )(*bufs, send, recv, after)


def _gather_plan(axes):
    def plan(refs, me, t, k, peer):
        if t >= len(axes):
            return None
        size = refs[t].shape[axes[t]] // 4
        mine = _slot(refs[t], axes[t], me, size)
        return mine, mine
    return plan


def _scatter_plan(axes):
    n = len(axes)

    def plan(refs, me, t, k, peer):
        if t >= n:
            return None
        size = refs[t].shape[axes[t]] // 4
        return _slot(refs[t], axes[t], peer, size), refs[n + t].at[k]
    return plan


def _swap_with_sibling(arrs):
    na = len(arrs)

    def body(*refs):
        ins, outs = refs[:na], refs[na:2 * na]
        send, recv = refs[2 * na:]
        x, y, c = _coords()
        cps = []
        for t in range(na):
            rc = pltpu.make_async_remote_copy(
                src_ref=ins[t], dst_ref=outs[t], send_sem=send.at[t], recv_sem=recv.at[t],
                device_id=(x, y, 1 - c), device_id_type=MESH)
            rc.start()
            cps.append(rc)
        for rc in cps:
            rc.wait()

    any_spec = pl.BlockSpec(memory_space=pl.ANY)
    return pl.pallas_call(
        body, name="swap_sibling",
        in_specs=[any_spec] * na,
        out_specs=[any_spec] * na,
        out_shape=[jax.ShapeDtypeStruct(a.shape, a.dtype) for a in arrs],
        scratch_shapes=[pltpu.SemaphoreType.DMA((na,)), pltpu.SemaphoreType.DMA((na,))],
    )(*arrs)


def _all_reduce_small(v):
    R = v.shape[0]

    def body(v_ref, o_ref, buf, send, recv, local):
        x, y, c = _coords()
        me = 4 * x + 2 * y + c
        cp = pltpu.make_async_copy(v_ref, buf.at[me], local)
        cp.start()
        pushes = []
        for k in range(1, 8):
            peer = (x ^ (k >> 2), y ^ ((k >> 1) & 1), c ^ (k & 1))
            rc = pltpu.make_async_remote_copy(
                src_ref=v_ref, dst_ref=buf.at[me], send_sem=send.at[k - 1], recv_sem=recv.at[k - 1],
                device_id=peer, device_id_type=MESH)
            rc.start()
            pushes.append(rc)
        for k in range(1, 8):
            px, py, pc = x ^ (k >> 2), y ^ ((k >> 1) & 1), c ^ (k & 1)
            pltpu.make_async_remote_copy(
                src_ref=v_ref, dst_ref=buf.at[4 * px + 2 * py + pc], send_sem=send.at[k - 1],
                recv_sem=recv.at[k - 1], device_id=(px, py, pc), device_id_type=MESH).wait_recv()
        for rc in pushes:
            rc.wait_send()
        cp.wait()
        tot = buf[0]
        for k in range(1, 8):
            tot = tot + buf[k]
        o_ref[...] = tot

    vm = pl.BlockSpec(memory_space=pltpu.VMEM)
    return pl.pallas_call(
        body, name="all_reduce_small",
        in_specs=[vm],
        out_specs=vm,
        out_shape=jax.ShapeDtypeStruct((R, LANES), F32),
        scratch_shapes=[
            pltpu.VMEM((8, R, LANES), F32),
            pltpu.SemaphoreType.DMA((7,)),
            pltpu.SemaphoreType.DMA((7,)),
            pltpu.SemaphoreType.DMA,
        ],
        compiler_params=pltpu.CompilerParams(vmem_limit_bytes=VMEM_LIMIT),
    )(v)


def _pack(arrs):
    flat = []
    for a in arrs:
        f = a.reshape(-1).astype(F32)
        flat.append(jnp.pad(f, (0, (-f.shape[0]) % LANES)))
    v = jnp.concatenate(flat)
    v = jnp.pad(v, (0, (-v.shape[0]) % (SUBLANES * LANES)))
    return v.reshape(-1, LANES)


def _unpack(v, shapes):
    flat = v.reshape(-1)
    out, off = [], 0
    for shp in shapes:
        n = 1
        for d in shp:
            n *= d
        out.append(flat[off:off + n].reshape(shp))
        off += n + (-n) % LANES
    return out


def _local_step(x, tgt, sp, weights_for, on_grads):
    n1, n2 = sp["norm1_g"], sp["norm2_g"]
    w = dict(weights_for(0, x))
    saved = []
    xs = x
    kv = None
    for l in range(DEPTH):
        x_in = xs
        if l == N_A:
            w.update(weights_for(2, x_in))
            kv = _rms_linear(x_in, sp["kv_norm_g"], w["w_kv"], sp["b_kv"], 0, 0, "kv_proj")
        if l == N_A + 1:
            w.update(weights_for(3, x_in))
        if l < N_A:
            xa = _pool_fwd(x_in, n1, w["pool_w"], w["pool_scale"], l)
            q = o = None
        else:
            j = l - N_A
            q = _rms_linear(x_in, n1, w["w_q", j], sp["b_q"], l, j, "q_proj")
            o = _attn_fwd(q, kv, sp["sinks"], j)
            xa = _linear_res(o, w["w_o", j], sp["b_o"], x_in, j)
        if l == 1:
            w.update(weights_for(1, xa))
        xs, u, uc = _ffn_fwd(xa, n2, w["ffn_up", l], w["ffn_down", l], w["ffn_conv_w"], sp["ffn_conv_b"], l)
        saved.append((x_in, xa, u, uc, q, o))

    dx, d_final_g, loss = _loss_head(xs, sp["final_g"], tgt)

    g = {k: [None] * DEPTH for k in ("norm1_g", "norm2_g", "ffn_conv_w", "ffn_conv_b")}
    for k in ("pool_scale", "b_q", "sinks", "b_o"):
        g[k] = [None] * N_A
    g["final_g"] = d_final_g
    dkvs = []
    pending = {}
    for l in reversed(range(DEPTH)):
        x_in, xa, u, uc, q, o = saved[l]
        dxa, du, a, hb, g["norm2_g"][l], g["ffn_conv_w"][l], g["ffn_conv_b"][l] = _ffn_bwd(
            xa, dx, u, uc, n2, w["ffn_up", l], w["ffn_down", l], w["ffn_conv_w"], l)
        pending["ffn_up", l] = _tn_matmul(hb, du, MX, "d_ffn_up")
        pending["ffn_down", l] = _tn_matmul(a, dx, MX, "d_ffn_down")
        zero = on_grads(DEPTH - 1 - l, pending)
        pending = {}
        n1, n2 = n1 + zero, n2 + zero
        if l < N_A:
            dx, d_pw, g["pool_scale"][l], g["norm1_g"][l] = _pool_bwd(
                x_in, dxa, n1, w["pool_w"], w["pool_scale"], l)
            pending["pool_w", l] = d_pw.astype(MX)
        else:
            j = l - N_A
            d_o, g["b_o"][j] = _linear_nt(dxa, w["w_o", j])
            pending["w_o", j] = _tn_matmul(o, dxa, MX, "d_w_o")
            dq, dkv, g["sinks"][j] = _attn_bwd(q, d_o, kv, sp["sinks"], j)
            dkvs.append(dkv)
            dx, g["norm1_g"][l], g["b_q"][j], hq, dqb = _rms_linear_bwd(
                x_in, n1, [dq], w["w_q", j], dxa, l, "q_proj_bwd")
            pending["w_q", j] = _tn_matmul(hq, dqb, MX, "d_w_q")
        if l == N_A:
            dx, g["kv_norm_g"], g["b_kv"], hk, dkvb = _rms_linear_bwd(
                x_in, sp["kv_norm_g"], dkvs, w["w_kv"], dx, 0, "kv_proj_bwd")
            pending["w_kv", 0] = _tn_matmul(hk, dkvb, MX, "d_w_kv")
    on_grads(DEPTH, pending)
    return loss, dx, g


SMALL = ("norm1_g", "norm2_g", "kv_norm_g", "b_kv", "b_q", "sinks", "b_o", "ffn_conv_b", "final_g")
SMALL_SHARDED = ("pool_scale", "ffn_conv_w")
BIG = ("pool_w", "w_kv", "w_q", "w_o", "ffn_up", "ffn_down")
ORDER = ("norm1_g", "norm2_g", "pool_w", "pool_scale", "kv_norm_g", "w_kv", "b_kv", "w_q", "b_q", "sinks",
         "w_o", "b_o", "ffn_up", "ffn_conv_w", "ffn_conv_b", "ffn_down", "final_g")


def _as3d(a):
    return a.reshape((-1,) + a.shape[-2:])


def kernel(x, norm1_g, norm2_g, pool_w, pool_scale, kv_norm_g, w_kv, b_kv, w_q, b_q, sinks, w_o, b_o, ffn_up, ffn_conv_w, ffn_conv_b, ffn_down, final_g, loss_target, m_norm1_g, m_norm2_g, m_pool_w, m_pool_scale, m_kv_norm_g, m_w_kv, m_b_kv, m_w_q, m_b_q, m_sinks, m_w_o, m_b_o, m_ffn_up, m_ffn_conv_w, m_ffn_conv_b, m_ffn_down, m_final_g, v_norm1_g, v_norm2_g, v_pool_w, v_pool_scale, v_kv_norm_g, v_w_kv, v_b_kv, v_w_q, v_b_q, v_sinks, v_w_o, v_b_o, v_ffn_up, v_ffn_conv_w, v_ffn_conv_b, v_ffn_down, v_final_g):
    W = dict(norm1_g=norm1_g, norm2_g=norm2_g, pool_w=pool_w, pool_scale=pool_scale, kv_norm_g=kv_norm_g,
             w_kv=w_kv, b_kv=b_kv, w_q=w_q, b_q=b_q, sinks=sinks, w_o=w_o, b_o=b_o, ffn_up=ffn_up,
             ffn_conv_w=ffn_conv_w, ffn_conv_b=ffn_conv_b, ffn_down=ffn_down, final_g=final_g)
    M = dict(norm1_g=m_norm1_g, norm2_g=m_norm2_g, pool_w=m_pool_w, pool_scale=m_pool_scale,
             kv_norm_g=m_kv_norm_g, w_kv=m_w_kv, b_kv=m_b_kv, w_q=m_w_q, b_q=m_b_q, sinks=m_sinks, w_o=m_w_o,
             b_o=m_b_o, ffn_up=m_ffn_up, ffn_conv_w=m_ffn_conv_w, ffn_conv_b=m_ffn_conv_b, ffn_down=m_ffn_down,
             final_g=m_final_g)
    V = dict(norm1_g=v_norm1_g, norm2_g=v_norm2_g, pool_w=v_pool_w, pool_scale=v_pool_scale,
             kv_norm_g=v_kv_norm_g, w_kv=v_w_kv, b_kv=v_b_kv, w_q=v_w_q, b_q=v_b_q, sinks=v_sinks, w_o=v_w_o,
             b_o=v_b_o, ffn_up=v_ffn_up, ffn_conv_w=v_ffn_conv_w, ffn_conv_b=v_ffn_conv_b, ffn_down=v_ffn_down,
             final_g=v_final_g)
    S = x.shape[1]
    chip = 2 * lax.axis_index("x") + lax.axis_index("y")

    gather_axis = dict(pool_w=1, w_kv=1, w_q=1, w_o=1, ffn_up=2, ffn_down=1, pool_scale=2, ffn_conv_w=2)
    me = chip.reshape(1).astype(jnp.int32)
    axis_of = lambda key: gather_axis[key if isinstance(key, str) else key[0]]

    def placed(key, dtype):
        if isinstance(key, str):
            return _gather_place(_as3d(W[key]), axis_of(key), me, dtype)
        return _gather_place(W[key[0]], axis_of(key), me, dtype, lead=key[1])

    stages = [
        ["pool_w", "pool_scale", "ffn_conv_w", ("ffn_up", 0), ("ffn_down", 0)],
        [("ffn_up", 1), ("ffn_down", 1)],
        ["w_kv", ("w_q", 0), ("w_o", 0), ("ffn_up", 2), ("ffn_down", 2)],
        [("w_q", 1), ("w_o", 1), ("ffn_up", 3), ("ffn_down", 3)],
    ]
    gathers, zero = [], 0.0
    for si, keys in enumerate(stages):
        axes = [axis_of(k) for k in keys]
        bufs = [placed(k, F32 if k in SMALL_SHARDED else MX) for k in keys]
        send, recv, bufs, token = _push_start(f"gather_start_{si}", bufs, len(keys), _gather_plan(axes))
        gathers.append((keys, axes, send, recv, bufs))
        zero = zero + token[0, 0]

    def weights_for(stage, after):
        keys, axes, send, recv, bufs = gathers[stage]
        out = dict(zip(keys, _push_wait(f"gather_wait_{stage}", send, recv, bufs, _gather_plan(axes), after)))
        if stage == 0:
            out["pool_w"] = out["pool_w"].reshape(N_A, 4, GC, GC)
            out["pool_scale"] = out["pool_scale"].reshape(N_A, 1, D)
        return out

    scatters = []

    def on_grads(stage, grads):
        keys = list(grads)
        axes = [axis_of(k) for k in keys]
        arrs = [_as3d(grads[k]) for k in keys]
        lands = []
        for a, ax in zip(arrs, axes):
            shp = list(a.shape)
            shp[ax] //= 4
            lands.append(lax.empty((3,) + tuple(shp), a.dtype))
        send, recv, bufs, token = _push_start(f"scatter_start_{stage}", arrs + lands, len(keys), _scatter_plan(axes))
        scatters.append((stage, keys, axes, send, recv, bufs))
        return token[0, 0]

    sp = dict(
        norm1_g=norm1_g.reshape(DEPTH, 1, D) + zero, norm2_g=norm2_g.reshape(DEPTH, 1, D),
        kv_norm_g=kv_norm_g.reshape(1, 1, D), b_kv=b_kv.reshape(1, 1, 4 * HD), b_q=b_q.reshape(N_B, 1, D),
        sinks=sinks, b_o=b_o.reshape(N_B, 1, D), ffn_conv_b=ffn_conv_b.reshape(DEPTH, 1, F2),
        final_g=final_g.reshape(1, D))

    x2d = x.reshape(S, D)
    loss, grad_x, g = _local_step(x2d, loss_target.reshape(S, D), sp, weights_for, on_grads)

    small_full = dict(
        norm1_g=jnp.stack(g["norm1_g"]), norm2_g=jnp.stack(g["norm2_g"]), kv_norm_g=g["kv_norm_g"],
        b_kv=g["b_kv"], b_q=jnp.stack(g["b_q"]), sinks=jnp.stack([s[0, :NH] for s in g["sinks"]]),
        b_o=jnp.stack(g["b_o"]), ffn_conv_b=jnp.stack(g["ffn_conv_b"]), final_g=g["final_g"],
        pool_scale=jnp.stack(g["pool_scale"]), ffn_conv_w=jnp.stack(g["ffn_conv_w"]))
    small_names = SMALL + SMALL_SHARDED
    small_shapes = [tuple(W[k].shape) for k in SMALL] + [(N_A, D), (DEPTH, 3, F2)]
    packed = _pack([small_full[k] for k in small_names] + [loss])
    red = _unpack(_all_reduce_small(packed), small_shapes + [(1, LANES)])
    red_g = dict(zip(small_names, red[:-1]))
    loss_out = red[-1][0, 0]
    red_g["pool_scale"] = lax.dynamic_slice_in_dim(red_g["pool_scale"], chip * (D // 4), D // 4, axis=1)
    red_g["ffn_conv_w"] = lax.dynamic_slice_in_dim(red_g["ffn_conv_w"], chip * (F2 // 4), F2 // 4, axis=2)
    small_w_shapes = [tuple(W[k].shape) for k in small_names]
    pk = lambda d: _pack([d[k] for k in small_names])[None]
    res = _adamw([[_pack([red_g[k] for k in small_names])]], pk(W), pk(M), pk(V))
    out_g, out_d, out_m, out_v = [dict(zip(small_names, _unpack(r, small_w_shapes))) for r in res]

    pkeys, partial = [], []
    for stage, keys, axes, send, recv, bufs in scatters:
        bufs = _push_wait(f"scatter_wait_{stage}", send, recv, bufs, _scatter_plan(axes), grad_x)
        n = len(keys)
        for k, ax, grad, landed in zip(keys, axes, bufs[:n], bufs[n:]):
            pkeys.append(k)
            p_sum = _sum_landed(grad, landed, ax, me)
            partial.append(p_sum.reshape(-1, p_sum.shape[-1]))
    mine = dict(zip(pkeys, partial))
    theirs = dict(zip(pkeys, _swap_with_sibling(partial)))
    for k in BIG:
        n_l = len([pk_ for pk_ in pkeys if pk_[0] == k])
        shp = W[k].shape
        rows, cols = mine[k, 0].shape
        three_d = lambda a: a.reshape(n_l, rows, cols)
        res = _adamw([[mine[k, l], theirs[k, l]] for l in range(n_l)], three_d(W[k]), three_d(M[k]), three_d(V[k]))
        out_g[k], out_d[k], out_m[k], out_v[k] = [r.reshape(shp) for r in res]

    return (loss_out, grad_x.reshape(x.shape), *[out_g[k] for k in ORDER], *[out_d[k] for k in ORDER],
            *[out_m[k] for k in ORDER], *[out_v[k] for k in ORDER])
```

```python
import functools

import jax
import jax.numpy as jnp
from jax import lax
from jax.experimental import pallas as pl
from jax.experimental.pallas import tpu as pltpu

D = 1024
DEPTH = 4
N_A = 2
N_B = 2
WINS = (2, 4, 8, 16)
GC = 256
HD = 64
NH = 16
BLK = 128
F = 2816
F2 = 2 * F
EPS = 1e-5
SCALE = HD ** -0.5
NEG = -1e30
HALO = 16
TN = 256
UP_GROUP = 3
LANES = 128
SUBLANES = 8
VMEM_LIMIT = 56 * 1024 * 1024
FFN_VMEM_LIMIT = 60 * 1024 * 1024
ACC_BYTES = 6 * 1024 * 1024
EW_BYTES = 1024 * 1024

LR, B1, B2, AEPS, WD, STEP = 0.001, 0.9, 0.999, 1e-08, 0.01, 10

MX = jnp.bfloat16
F32 = jnp.float32
MESH = pl.DeviceIdType.MESH


def _cp(n_axes=1, vmem=VMEM_LIMIT):
    return pltpu.CompilerParams(dimension_semantics=("arbitrary",) * n_axes, vmem_limit_bytes=vmem)


def _dot(a, b):
    return jnp.dot(a, b, preferred_element_type=F32)


def _dot_nt(a, b):
    return lax.dot_general(a, b, (((1,), (1,)), ((), ())), preferred_element_type=F32)


def _dot_tn(a, b):
    return lax.dot_general(a, b, (((0,), (0,)), ((), ())), preferred_element_type=F32)


def _rms_fwd(x, g):
    r = lax.rsqrt(jnp.mean(x * x, axis=-1, keepdims=True) + EPS)
    xh = x * r
    return xh * g, xh, r


def _rms_bwd(dh, xh, r, g):
    dxh = dh * g
    return r * (dxh - xh * jnp.mean(dxh * xh, axis=-1, keepdims=True))


def _row_tile(s, want):
    return min(s, want)


def _pool_pm(e, h, row, tm):
    out = []
    for gi, win in enumerate(WINS):
        cols = slice(gi * GC, (gi + 1) * GC)
        s = e[:, cols]
        sh = 1
        while sh < win:
            s = s + pltpu.roll(s, sh, 0)
            sh *= 2
        inv = 1.0 / jnp.minimum(row + 1, win).astype(F32)
        out.append(s[HALO:] * inv - h[:, cols])
    return out


def _pool_fwd(x, g, pw, ps, layer):
    S = x.shape[0]
    tm = _row_tile(S, 512)
    hb = tm // HALO

    def body(x_ref, xh_ref, g_ref, pw_ref, ps_ref, o_ref):
        i = pl.program_id(0)
        x = x_ref[...]
        gg = g_ref[...]
        h, _, _ = _rms_fwd(x, gg)
        hh, _, _ = _rms_fwd(xh_ref[...], gg)
        hh = jnp.where(i > 0, hh, 0.0)
        e = jnp.concatenate([hh, h], axis=0)
        row = i * tm + lax.broadcasted_iota(jnp.int32, (tm, 1), 0)
        pm = _pool_pm(e, h, row, tm)
        for gi in range(len(WINS)):
            cols = slice(gi * GC, (gi + 1) * GC)
            z = _dot(pm[gi].astype(MX), pw_ref[gi])
            o_ref[:, cols] = x[:, cols] + z * ps_ref[:, cols]

    return pl.pallas_call(
        body, name="pool_fwd",
        grid=(S // tm,),
        in_specs=[
            pl.BlockSpec((tm, D), lambda i: (i, 0)),
            pl.BlockSpec((HALO, D), lambda i: (jnp.maximum(i * hb - 1, 0), 0)),
            pl.BlockSpec((None, 1, D), lambda i: (layer, 0, 0)),
            pl.BlockSpec((None, 4, GC, GC), lambda i: (layer, 0, 0, 0)),
            pl.BlockSpec((None, 1, D), lambda i: (layer, 0, 0)),
        ],
        out_specs=pl.BlockSpec((tm, D), lambda i: (i, 0)),
        out_shape=jax.ShapeDtypeStruct((S, D), F32),
        compiler_params=_cp(),
    )(x, x, g, pw, ps)


def _pool_bwd(x, dy, g, pw, ps, layer):
    S = x.shape[0]
    tm = _row_tile(S, 256)
    hb = tm // HALO
    n_i = S // tm
    n_h = S // HALO

    def body(x_ref, xh_ref, dy_ref, dyn_ref, g_ref, pw_ref, ps_ref, dx_ref, dpw_ref, dps_ref, dg_ref):
        i = pl.program_id(0)

        @pl.when(i == 0)
        def _():
            dpw_ref[...] = jnp.zeros_like(dpw_ref)
            dps_ref[...] = jnp.zeros_like(dps_ref)
            dg_ref[...] = jnp.zeros_like(dg_ref)

        x = x_ref[...]
        gg = g_ref[...]
        ps = ps_ref[...]
        h, xh, r = _rms_fwd(x, gg)
        hh, _, _ = _rms_fwd(xh_ref[...], gg)
        hh = jnp.where(i > 0, hh, 0.0)
        e = jnp.concatenate([hh, h], axis=0)
        row = i * tm + lax.broadcasted_iota(jnp.int32, (tm, 1), 0)
        rown = (i + 1) * tm + lax.broadcasted_iota(jnp.int32, (HALO, 1), 0)
        pm = _pool_pm(e, h, row, tm)
        dy = dy_ref[...]
        dz = dy * ps
        dzn = jnp.where(i < n_i - 1, dyn_ref[...] * ps, 0.0)
        parts = []
        for gi, win in enumerate(WINS):
            cols = slice(gi * GC, (gi + 1) * GC)
            w = pw_ref[gi]
            pmb = pm[gi].astype(MX)
            z = _dot(pmb, w)
            dps_ref[:, cols] += jnp.sum(dy[:, cols] * z, axis=0, keepdims=True)
            dzb = dz[:, cols].astype(MX)
            dpw_ref[gi] += _dot_tn(pmb, dzb)
            dpm = _dot_nt(dzb, w)
            dpmn = _dot_nt(dzn[:, cols].astype(MX), w)
            q = dpm * (1.0 / jnp.minimum(row + 1, win).astype(F32))
            qn = dpmn * (1.0 / jnp.minimum(rown + 1, win).astype(F32))
            s = jnp.concatenate([q, qn], axis=0)
            sh = 1
            while sh < win:
                s = s + pltpu.roll(s, tm + HALO - sh, 0)
                sh *= 2
            parts.append(s[:tm] - dpm)
        dh = jnp.concatenate(parts, axis=1)
        dg_ref[...] += jnp.sum(dh * xh, axis=0, keepdims=True)
        dx_ref[...] = dy + _rms_bwd(dh, xh, r, gg)

    return pl.pallas_call(
        body, name="pool_bwd",
        grid=(n_i,),
        in_specs=[
            pl.BlockSpec((tm, D), lambda i: (i, 0)),
            pl.BlockSpec((HALO, D), lambda i: (jnp.maximum(i * hb - 1, 0), 0)),
            pl.BlockSpec((tm, D), lambda i: (i, 0)),
            pl.BlockSpec((HALO, D), lambda i: (jnp.minimum((i + 1) * hb, n_h - 1), 0)),
            pl.BlockSpec((None, 1, D), lambda i: (layer, 0, 0)),
            pl.BlockSpec((None, 4, GC, GC), lambda i: (layer, 0, 0, 0)),
            pl.BlockSpec((None, 1, D), lambda i: (layer, 0, 0)),
        ],
        out_specs=[
            pl.BlockSpec((tm, D), lambda i: (i, 0)),
            pl.BlockSpec((4, GC, GC), lambda i: (0, 0, 0)),
            pl.BlockSpec((1, D), lambda i: (0, 0)),
            pl.BlockSpec((1, D), lambda i: (0, 0)),
        ],
        out_shape=[
            jax.ShapeDtypeStruct((S, D), F32),
            jax.ShapeDtypeStruct((4, GC, GC), F32),
            jax.ShapeDtypeStruct((1, D), F32),
            jax.ShapeDtypeStruct((1, D), F32),
        ],
        compiler_params=_cp(),
    )(x, x, dy, dy, g, pw, ps)


def _conv(u, prev, cw, rowi):
    um1 = jnp.where(rowi == 0, prev[7:8], pltpu.roll(u, 1, 0))
    um2 = jnp.where(rowi == 0, prev[6:7], jnp.where(rowi == 1, prev[7:8], pltpu.roll(u, 2, 0)))
    return cw[0:1] * um2 + cw[1:2] * um1 + cw[2:3] * u, um1, um2


def _ffn_fwd(x, g, wup, wdn, cw, cb, layer):
    S = x.shape[0]
    tm = _row_tile(S, 512)

    def body(x_ref, g_ref, wup_hbm, wdn_hbm, cw_ref, cb_ref, o_ref, u_ref, uc_ref, wup_v, wdn_v, carry, act):
        i = pl.program_id(0)

        @pl.when(i == 0)
        def _():
            pltpu.sync_copy(wup_hbm.at[0], wup_v)
            pltpu.sync_copy(wdn_hbm.at[0], wdn_v)
            carry[...] = jnp.zeros_like(carry)

        x = x_ref[...]
        h, _, _ = _rms_fwd(x, g_ref[...])
        hb = h.astype(MX)
        rowi = lax.broadcasted_iota(jnp.int32, (tm, 1), 0)
        for j in range(F // TN):
            cg = slice(j * TN, (j + 1) * TN)
            cv = slice(F + j * TN, F + (j + 1) * TN)
            ug = _dot(hb, wup_v[:, cg])
            uv = _dot(hb, wup_v[:, cv])
            u_ref[:, cg] = ug.astype(u_ref.dtype)
            u_ref[:, cv] = uv.astype(u_ref.dtype)
            gt, _, _ = _conv(ug, carry[:, cg], cw_ref[:, cg], rowi)
            vl, _, _ = _conv(uv, carry[:, cv], cw_ref[:, cv], rowi)
            carry[:, cg] = ug[tm - SUBLANES:]
            carry[:, cv] = uv[tm - SUBLANES:]
            gt = gt + cb_ref[:, cg]
            vl = vl + cb_ref[:, cv]
            uc_ref[:, cg] = gt.astype(uc_ref.dtype)
            uc_ref[:, cv] = vl.astype(uc_ref.dtype)
            act[:, cg] = (gt * jax.nn.sigmoid(gt) * vl).astype(act.dtype)
        o_ref[...] = x + _dot(act[...], wdn_v[...])

    return pl.pallas_call(
        body, name="ffn_fwd",
        grid=(S // tm,),
        in_specs=[
            pl.BlockSpec((tm, D), lambda i: (i, 0)),
            pl.BlockSpec((None, 1, D), lambda i: (layer, 0, 0)),
            pl.BlockSpec(memory_space=pl.ANY),
            pl.BlockSpec(memory_space=pl.ANY),
            pl.BlockSpec((None, 3, F2), lambda i: (layer, 0, 0)),
            pl.BlockSpec((None, 1, F2), lambda i: (layer, 0, 0)),
        ],
        out_specs=[
            pl.BlockSpec((tm, D), lambda i: (i, 0)),
            pl.BlockSpec((tm, F2), lambda i: (i, 0)),
            pl.BlockSpec((tm, F2), lambda i: (i, 0)),
        ],
        out_shape=[
            jax.ShapeDtypeStruct((S, D), F32),
            jax.ShapeDtypeStruct((S, F2), MX),
            jax.ShapeDtypeStruct((S, F2), MX),
        ],
        scratch_shapes=[
            pltpu.VMEM((D, F2), MX),
            pltpu.VMEM((F, D), MX),
            pltpu.VMEM((SUBLANES, F2), F32),
            pltpu.VMEM((tm, F), MX),
        ],
        compiler_params=_cp(vmem=FFN_VMEM_LIMIT),
    )(x, g, wup, wdn, cw, cb)


def _ffn_bwd(x, dy, u, uc, g, wup, wdn, cw, layer):
    S = x.shape[0]
    tm = _row_tile(S, 256)
    n_i = S // tm

    def body(x_ref, dy_ref, u_ref, uc_ref, g_ref, wup_hbm, wdn_hbm, cw_ref,
             dx_ref, du_ref, a_ref, h_ref, dg_ref, dcw_ref, dcb_ref, wup_v, wdn_v, carry):
        i = pl.program_id(0)

        @pl.when(i == 0)
        def _():
            pltpu.sync_copy(wup_hbm.at[0], wup_v)
            pltpu.sync_copy(wdn_hbm.at[0], wdn_v)
            carry[...] = jnp.zeros_like(carry)
            dg_ref[...] = jnp.zeros_like(dg_ref)
            dcw_ref[...] = jnp.zeros_like(dcw_ref)
            dcb_ref[...] = jnp.zeros_like(dcb_ref)

        x = x_ref[...]
        gg = g_ref[...]
        h, xh, r = _rms_fwd(x, gg)
        h_ref[...] = h.astype(h_ref.dtype)
        dy = dy_ref[...]
        dyb = dy.astype(MX)
        rowi = lax.broadcasted_iota(jnp.int32, (tm, 1), 0)
        dh, dus = None, ([], [])
        for j in range(F // TN):
            cg = slice(j * TN, (j + 1) * TN)
            cv = slice(F + j * TN, F + (j + 1) * TN)
            gt = uc_ref[:, cg].astype(F32)
            vl = uc_ref[:, cv].astype(F32)
            sg = jax.nn.sigmoid(gt)
            sil = gt * sg
            a_ref[:, cg] = (sil * vl).astype(a_ref.dtype)
            da = _dot_nt(dyb, wdn_v[cg, :])
            dvl = da * sil
            dgt = (da * vl) * (sg + sil * (1.0 - sg))
            for cc, dd in ((cg, dgt), (cv, dvl)):
                nxt = carry[:, cc]
                dp1 = jnp.where(rowi == tm - 1, nxt[0:1], pltpu.roll(dd, tm - 1, 0))
                dp2 = jnp.where(rowi == tm - 1, nxt[1:2],
                                jnp.where(rowi == tm - 2, nxt[0:1], pltpu.roll(dd, tm - 2, 0)))
                carry[:, cc] = dd[0:SUBLANES]
                uu = u_ref[:, cc].astype(F32)
                dcb_ref[:, cc] += jnp.sum(dd, axis=0, keepdims=True)
                dcw_ref[0:1, cc] += jnp.sum(dp2 * uu, axis=0, keepdims=True)
                dcw_ref[1:2, cc] += jnp.sum(dp1 * uu, axis=0, keepdims=True)
                dcw_ref[2:3, cc] += jnp.sum(dd * uu, axis=0, keepdims=True)
                cwc = cw_ref[:, cc]
                duu = (cwc[2:3] * dd + cwc[1:2] * dp1 + cwc[0:1] * dp2).astype(MX)
                du_ref[:, cc] = duu
                dus[cc is cv].append(duu)
            if len(dus[0]) == UP_GROUP or j == F // TN - 1:
                first = j + 1 - len(dus[0])
                for side, base in ((0, 0), (1, F)):
                    cols = slice(base + first * TN, base + (j + 1) * TN)
                    part = _dot_nt(jnp.concatenate(dus[side], axis=1), wup_v[:, cols])
                    dh = part if dh is None else dh + part
                dus = ([], [])
        dg_ref[...] += jnp.sum(dh * xh, axis=0, keepdims=True)
        dx_ref[...] = dy + _rms_bwd(dh, xh, r, gg)

    rev = lambda i: (n_i - 1 - i, 0)
    return pl.pallas_call(
        body, name="ffn_bwd",
        grid=(n_i,),
        in_specs=[
            pl.BlockSpec((tm, D), rev),
            pl.BlockSpec((tm, D), rev),
            pl.BlockSpec((tm, F2), rev),
            pl.BlockSpec((tm, F2), rev),
            pl.BlockSpec((None, 1, D), lambda i: (layer, 0, 0)),
            pl.BlockSpec(memory_space=pl.ANY),
            pl.BlockSpec(memory_space=pl.ANY),
            pl.BlockSpec((None, 3, F2), lambda i: (layer, 0, 0)),
        ],
        out_specs=[
            pl.BlockSpec((tm, D), rev),
            pl.BlockSpec((tm, F2), rev),
            pl.BlockSpec((tm, F), rev),
            pl.BlockSpec((tm, D), rev),
            pl.BlockSpec((1, D), lambda i: (0, 0)),
            pl.BlockSpec((3, F2), lambda i: (0, 0)),
            pl.BlockSpec((1, F2), lambda i: (0, 0)),
        ],
        out_shape=[
            jax.ShapeDtypeStruct((S, D), F32),
            jax.ShapeDtypeStruct((S, F2), MX),
            jax.ShapeDtypeStruct((S, F), MX),
            jax.ShapeDtypeStruct((S, D), MX),
            jax.ShapeDtypeStruct((1, D), F32),
            jax.ShapeDtypeStruct((3, F2), F32),
            jax.ShapeDtypeStruct((1, F2), F32),
        ],
        scratch_shapes=[
            pltpu.VMEM((D, F2), MX),
            pltpu.VMEM((F, D), MX),
            pltpu.VMEM((SUBLANES, F2), F32),
        ],
        compiler_params=_cp(vmem=FFN_VMEM_LIMIT),
    )(x, dy, u, uc, g, wup, wdn, cw)


def _tn_matmul(a, b, out_dtype, name):
    S, M = a.shape
    N = b.shape[1]
    bn = N
    while M * bn * 4 > ACC_BYTES and bn % (2 * LANES) == 0:
        bn //= 2
    bk = _row_tile(S, 1024)
    nk = S // bk

    def body(a_ref, b_ref, o_ref, acc):
        k = pl.program_id(1)
        p = _dot_tn(a_ref[...].astype(MX), b_ref[...].astype(MX))

        @pl.when(k == 0)
        def _():
            acc[...] = p

        @pl.when(k > 0)
        def _():
            acc[...] += p

        @pl.when(k == nk - 1)
        def _():
            o_ref[...] = acc[...].astype(o_ref.dtype)

    return pl.pallas_call(
        body, name=name,
        grid=(N // bn, nk),
        in_specs=[
            pl.BlockSpec((bk, M), lambda j, k: (k, 0)),
            pl.BlockSpec((bk, bn), lambda j, k: (k, j)),
        ],
        out_specs=pl.BlockSpec((M, bn), lambda j, k: (0, j)),
        out_shape=jax.ShapeDtypeStruct((M, N), out_dtype),
        scratch_shapes=[pltpu.VMEM((M, bn), F32)],
        compiler_params=_cp(2),
    )(a, b)


def _rms_linear(x, g, w, b, g_layer, b_layer, name):
    S = x.shape[0]
    N = w.shape[-1]
    tm = _row_tile(S, 512)

    def body(x_ref, g_ref, w_ref, b_ref, o_ref):
        h, _, _ = _rms_fwd(x_ref[...], g_ref[...])
        o_ref[...] = (_dot(h.astype(MX), w_ref[...]) + b_ref[...]).astype(o_ref.dtype)

    return pl.pallas_call(
        body, name=name,
        grid=(S // tm,),
        in_specs=[
            pl.BlockSpec((tm, D), lambda i: (i, 0)),
            pl.BlockSpec((None, 1, D), lambda i: (g_layer, 0, 0)),
            pl.BlockSpec((None, D, N), lambda i: (0, 0, 0)),
            pl.BlockSpec((None, 1, N), lambda i: (b_layer, 0, 0)),
        ],
        out_specs=pl.BlockSpec((tm, N), lambda i: (i, 0)),
        out_shape=jax.ShapeDtypeStruct((S, N), MX),
        compiler_params=_cp(),
    )(x, g, w, b)


def _linear_res(o, w, b, xres, layer):
    S = o.shape[0]
    tm = _row_tile(S, 512)

    def body(o_ref, w_ref, b_ref, x_ref, y_ref):
        y_ref[...] = x_ref[...] + _dot(o_ref[...], w_ref[...]) + b_ref[...]

    return pl.pallas_call(
        body, name="o_proj",
        grid=(S // tm,),
        in_specs=[
            pl.BlockSpec((tm, D), lambda i: (i, 0)),
            pl.BlockSpec((None, D, D), lambda i: (0, 0, 0)),
            pl.BlockSpec((None, 1, D), lambda i: (layer, 0, 0)),
            pl.BlockSpec((tm, D), lambda i: (i, 0)),
        ],
        out_specs=pl.BlockSpec((tm, D), lambda i: (i, 0)),
        out_shape=jax.ShapeDtypeStruct((S, D), F32),
        compiler_params=_cp(),
    )(o, w, b, xres)


def _linear_nt(dy, w):
    S = dy.shape[0]
    tm = _row_tile(S, 512)

    def body(dy_ref, w_ref, o_ref, db_ref):
        @pl.when(pl.program_id(0) == 0)
        def _():
            db_ref[...] = jnp.zeros_like(db_ref)

        dy = dy_ref[...]
        db_ref[...] += jnp.sum(dy, axis=0, keepdims=True)
        o_ref[...] = _dot_nt(dy.astype(MX), w_ref[...]).astype(o_ref.dtype)

    return pl.pallas_call(
        body, name="o_proj_bwd",
        grid=(S // tm,),
        in_specs=[
            pl.BlockSpec((tm, D), lambda i: (i, 0)),
            pl.BlockSpec((None, D, D), lambda i: (0, 0, 0)),
        ],
        out_specs=[
            pl.BlockSpec((tm, D), lambda i: (i, 0)),
            pl.BlockSpec((1, D), lambda i: (0, 0)),
        ],
        out_shape=[
            jax.ShapeDtypeStruct((S, D), MX),
            jax.ShapeDtypeStruct((1, D), F32),
        ],
        compiler_params=_cp(),
    )(dy, w)


def _rms_linear_bwd(x, g, dzs, w, dy, g_layer, name):
    S = x.shape[0]
    N = w.shape[-1]
    tm = _row_tile(S, 512)
    nz = len(dzs)

    def body(*refs):
        x_ref, g_ref = refs[0], refs[1]
        dz_refs = refs[2:2 + nz]
        w_ref, dy_ref, dx_ref, dg_ref, db_ref, h_ref, dzb_ref = refs[2 + nz:]

        @pl.when(pl.program_id(0) == 0)
        def _():
            dg_ref[...] = jnp.zeros_like(dg_ref)
            db_ref[...] = jnp.zeros_like(db_ref)

        gg = g_ref[...]
        h, xh, r = _rms_fwd(x_ref[...], gg)
        h_ref[...] = h.astype(h_ref.dtype)
        dz = dz_refs[0][...].astype(F32)
        for zr in dz_refs[1:]:
            dz = dz + zr[...].astype(F32)
        db_ref[...] += jnp.sum(dz, axis=0, keepdims=True)
        dzb = dz.astype(MX)
        dzb_ref[...] = dzb
        dh = _dot_nt(dzb, w_ref[...])
        dg_ref[...] += jnp.sum(dh * xh, axis=0, keepdims=True)
        dx_ref[...] = dy_ref[...] + _rms_bwd(dh, xh, r, gg)

    return pl.pallas_call(
        body, name=name,
        grid=(S // tm,),
        in_specs=[
            pl.BlockSpec((tm, D), lambda i: (i, 0)),
            pl.BlockSpec((None, 1, D), lambda i: (g_layer, 0, 0)),
        ] + [pl.BlockSpec((tm, N), lambda i: (i, 0))] * nz + [
            pl.BlockSpec((None, D, N), lambda i: (0, 0, 0)),
            pl.BlockSpec((tm, D), lambda i: (i, 0)),
        ],
        out_specs=[
            pl.BlockSpec((tm, D), lambda i: (i, 0)),
            pl.BlockSpec((1, D), lambda i: (0, 0)),
            pl.BlockSpec((1, N), lambda i: (0, 0)),
            pl.BlockSpec((tm, D), lambda i: (i, 0)),
            pl.BlockSpec((tm, N), lambda i: (i, 0)),
        ],
        out_shape=[
            jax.ShapeDtypeStruct((S, D), F32),
            jax.ShapeDtypeStruct((1, D), F32),
            jax.ShapeDtypeStruct((1, N), F32),
            jax.ShapeDtypeStruct((S, D), MX),
            jax.ShapeDtypeStruct((S, N), MX),
        ],
        compiler_params=_cp(),
    )(x, g, *dzs, w, dy)


def _loss_head(x, g, tgt):
    S = x.shape[0]
    tm = _row_tile(S, 512)

    def body(x_ref, g_ref, t_ref, dx_ref, dg_ref, l_ref):
        @pl.when(pl.program_id(0) == 0)
        def _():
            dg_ref[...] = jnp.zeros_like(dg_ref)
            l_ref[...] = jnp.zeros_like(l_ref)

        gg = g_ref[...]
        y, xh, r = _rms_fwd(x_ref[...], gg)
        err = y - t_ref[...]
        tok = jnp.sum(err * err, axis=-1, keepdims=True) * (1.0 / D)
        l_ref[...] += 0.5 * jnp.sum(tok, axis=0, keepdims=True)
        dyv = err * (1.0 / D)
        dg_ref[...] += jnp.sum(dyv * xh, axis=0, keepdims=True)
        dx_ref[...] = _rms_bwd(dyv, xh, r, gg)

    return pl.pallas_call(
        body, name="loss_head",
        grid=(S // tm,),
        in_specs=[
            pl.BlockSpec((tm, D), lambda i: (i, 0)),
            pl.BlockSpec((1, D), lambda i: (0, 0)),
            pl.BlockSpec((tm, D), lambda i: (i, 0)),
        ],
        out_specs=[
            pl.BlockSpec((tm, D), lambda i: (i, 0)),
            pl.BlockSpec((1, D), lambda i: (0, 0)),
            pl.BlockSpec((1, LANES), lambda i: (0, 0)),
        ],
        out_shape=[
            jax.ShapeDtypeStruct((S, D), F32),
            jax.ShapeDtypeStruct((1, D), F32),
            jax.ShapeDtypeStruct((1, LANES), F32),
        ],
        compiler_params=_cp(),
    )(x, g, tgt)


HPG = NH // 2
ROWS = HPG * BLK


def _attn_setup(kvp_ref, kvc_ref, n):
    kw = jnp.concatenate([kvp_ref[...], kvc_ref[...]], axis=0).astype(F32)
    kk, vv = kw[:, :LANES], kw[:, LANES:]
    lo = lax.broadcasted_iota(jnp.int32, (1, LANES), 1) < HD
    kr, vr = pltpu.roll(kk, HD, 1), pltpu.roll(vv, HD, 1)
    ks = [jnp.where(lo, kk, kr).astype(MX), jnp.where(lo, kr, kk).astype(MX)]
    vs = [jnp.where(lo, vv, vr).astype(MX), jnp.where(lo, vr, vv).astype(MX)]
    qi = lax.broadcasted_iota(jnp.int32, (ROWS, 2 * BLK), 0) & (BLK - 1)
    si = lax.broadcasted_iota(jnp.int32, (ROWS, 2 * BLK), 1)
    mask = (si > qi) & (si <= qi + BLK) & jnp.logical_or(n > 0, si >= BLK)
    return ks, vs, lo, mask


def _stack_heads(ref, grp, lo):
    parts = []
    for j in range(4 * grp, 4 * grp + 4):
        slab = ref[:, j * LANES:(j + 1) * LANES]
        zero = jnp.zeros_like(slab)
        parts += [jnp.where(lo, slab, zero), jnp.where(lo, zero, slab)]
    return jnp.concatenate(parts, axis=0)


def _unstack_heads(st, lo):
    return [jnp.where(lo, st[2 * i * BLK:(2 * i + 1) * BLK], st[(2 * i + 1) * BLK:(2 * i + 2) * BLK])
            for i in range(4)]


def _sink_column(sk_ref, layer, grp):
    head = lax.broadcasted_iota(jnp.int32, (ROWS, 1), 0) // BLK
    col = jnp.zeros((ROWS, 1), F32)
    for h in range(HPG):
        col = jnp.where(head == h, sk_ref[layer, HPG * grp + h], col)
    return col


def _attn_probs(qs, kg, mask, sink):
    s = jnp.where(mask, _dot_nt(qs, kg) * SCALE, NEG)
    m = jnp.maximum(jnp.max(s, axis=-1, keepdims=True), sink)
    p = jnp.exp(s - m)
    es = jnp.exp(sink - m)
    inv = 1.0 / (jnp.sum(p, axis=-1, keepdims=True) + es)
    return p * inv, es * inv


def _attn_specs(n_extra_q):
    q_spec = pl.BlockSpec((BLK, D), lambda n: (n, 0))
    return [q_spec] * n_extra_q + [
        pl.BlockSpec((BLK, 4 * HD), lambda n: (jnp.maximum(n - 1, 0), 0)),
        pl.BlockSpec((BLK, 4 * HD), lambda n: (n, 0)),
        pl.BlockSpec(memory_space=pltpu.SMEM),
    ]


def _attn_fwd(q, kv, sinks, layer):
    S = q.shape[0]

    def body(q_ref, kvp_ref, kvc_ref, sk_ref, o_ref):
        n = pl.program_id(0)
        ks, vs, lo, mask = _attn_setup(kvp_ref, kvc_ref, n)
        for grp in range(2):
            qs = _stack_heads(q_ref, grp, lo)
            pr, _ = _attn_probs(qs, ks[grp], mask, _sink_column(sk_ref, layer, grp))
            outs = _unstack_heads(_dot(pr.astype(MX), vs[grp]), lo)
            for i in range(4):
                j = 4 * grp + i
                o_ref[:, j * LANES:(j + 1) * LANES] = outs[i].astype(o_ref.dtype)

    return pl.pallas_call(
        body, name="attn_fwd",
        grid=(S // BLK,),
        in_specs=_attn_specs(1),
        out_specs=pl.BlockSpec((BLK, D), lambda n: (n, 0)),
        out_shape=jax.ShapeDtypeStruct((S, D), MX),
        compiler_params=_cp(),
    )(q, kv, kv, sinks)


def _attn_bwd(q, do, kv, sinks, layer):
    S = q.shape[0]

    def body(q_ref, do_ref, kvp_ref, kvc_ref, sk_ref, dq_ref, dkv_ref, dsk_ref):
        n = pl.program_id(0)

        @pl.when(n == 0)
        def _():
            dkv_ref[...] = jnp.zeros_like(dkv_ref)
            dsk_ref[...] = jnp.zeros_like(dsk_ref)

        ks, vs, lo, mask = _attn_setup(kvp_ref, kvc_ref, n)
        lane = lax.broadcasted_iota(jnp.int32, (1, LANES), 1)
        dsk = jnp.zeros((1, LANES), F32)
        tk, tv = [], []
        for grp in range(2):
            qs = _stack_heads(q_ref, grp, lo)
            dos = _stack_heads(do_ref, grp, lo)
            pr, psink = _attn_probs(qs, ks[grp], mask, _sink_column(sk_ref, layer, grp))
            dpr = _dot_nt(dos, vs[grp])
            delta = jnp.sum(pr * dpr, axis=-1, keepdims=True)
            ds = (pr * (dpr - delta) * SCALE).astype(MX)
            sd = psink * delta
            for h in range(HPG):
                dsk = dsk + jnp.where(lane == HPG * grp + h,
                                      -jnp.sum(sd[h * BLK:(h + 1) * BLK], axis=0, keepdims=True), 0.0)
            dqs = _unstack_heads(_dot(ds, ks[grp]), lo)
            for i in range(4):
                j = 4 * grp + i
                dq_ref[:, j * LANES:(j + 1) * LANES] = dqs[i].astype(dq_ref.dtype)
            dk = _dot_tn(qs, ds).T
            dv = _dot_tn(dos, pr.astype(MX)).T
            tk.append(dk + pltpu.roll(dk, HD, 1))
            tv.append(dv + pltpu.roll(dv, HD, 1))
        dsk_ref[...] += dsk
        contrib = jnp.concatenate([jnp.where(lo, tk[0], tk[1]), jnp.where(lo, tv[0], tv[1])], axis=1)

        @pl.when(n > 0)
        def _():
            rows = pl.ds(pl.multiple_of((n - 1) * BLK, BLK), 2 * BLK)
            dkv_ref[rows, :] += contrib

        @pl.when(n == 0)
        def _():
            dkv_ref[0:BLK, :] += contrib[BLK:]

    return pl.pallas_call(
        body, name="attn_bwd",
        grid=(S // BLK,),
        in_specs=_attn_specs(2),
        out_specs=[
            pl.BlockSpec((BLK, D), lambda n: (n, 0)),
            pl.BlockSpec((S, 4 * HD), lambda n: (0, 0)),
            pl.BlockSpec((1, LANES), lambda n: (0, 0)),
        ],
        out_shape=[
            jax.ShapeDtypeStruct((S, D), MX),
            jax.ShapeDtypeStruct((S, 4 * HD), F32),
            jax.ShapeDtypeStruct((1, LANES), F32),
        ],
        compiler_params=_cp(),
    )(q, do, kv, kv, sinks)


def _ew_rows(rows, cols, n_bufs=1):
    br = rows
    while br * cols * 4 * n_bufs > EW_BYTES and br % (2 * SUBLANES) == 0:
        br //= 2
    return br


def _adamw(parts, w, m, v):
    L, R, C = w.shape
    br = _ew_rows(R, C)
    npart = len(parts[0])

    def body(*refs):
        p_refs = refs[:L * npart]
        w_ref, m_ref, v_ref, g_ref, d_ref, nm_ref, nv_ref = refs[L * npart:]
        lyr = pl.program_id(0)
        for l in range(L):
            @pl.when(lyr == l)
            def _(l=l):
                g = p_refs[l * npart][...]
                for pr in p_refs[l * npart + 1:(l + 1) * npart]:
                    g = g + pr[...]
                nm = B1 * m_ref[...] + (1.0 - B1) * g
                nv = B2 * v_ref[...] + (1.0 - B2) * (g * g)
                m_hat = nm / (1.0 - B1 ** STEP)
                v_hat = nv / (1.0 - B2 ** STEP)
                g_ref[...] = g
                d_ref[...] = -LR * (m_hat / (jnp.sqrt(v_hat) + AEPS) + WD * w_ref[...])
                nm_ref[...] = nm
                nv_ref[...] = nv

    spec = pl.BlockSpec((None, br, C), lambda a, i: (a, i, 0))
    part_specs = [pl.BlockSpec((br, C), lambda a, i, l=l: (jnp.where(a == l, i, 0), 0))
                  for l in range(L) for _ in range(npart)]
    return pl.pallas_call(
        body, name="adamw",
        grid=(L, R // br),
        in_specs=part_specs + [spec] * 3,
        out_specs=[spec] * 4,
        out_shape=[jax.ShapeDtypeStruct((L, R, C), F32)] * 4,
        compiler_params=_cp(2),
    )(*[a for lp in parts for a in lp], w, m, v)


def _coords():
    return lax.axis_index("x"), lax.axis_index("y"), lax.axis_index("c")


def _other_chips(x, y):
    return [(1 - x, y), (x, 1 - y), (1 - x, 1 - y)]


def _slot(ref, axis, chip, size):
    idx = [slice(None)] * 3
    idx[axis] = pl.ds(pl.multiple_of(chip * size, size), size)
    return ref.at[tuple(idx)]


HBM_SPEC = pl.BlockSpec(memory_space=pltpu.HBM)
SEM_SPEC = pl.BlockSpec(memory_space=pltpu.SEMAPHORE)
ANY_SPEC = pl.BlockSpec(memory_space=pl.ANY)
EFFECT = pltpu.SideEffectType.DATAFLOW_SIDE_EFFECTING


def _slot_specs(shape, axis, br, lead):
    _, b, c = shape
    nrb = b // br
    first = (lambda a: a) if lead is None else (lambda a: lead)
    shard = pl.BlockSpec((None, br, c), lambda a, i, me: (first(a), i, 0))
    if axis == 1:
        slot = pl.BlockSpec((None, br, c), lambda a, i, me: (a, me[0] * nrb + i, 0))
    else:
        slot = pl.BlockSpec((None, br, c), lambda a, i, me: (a, i, me[0]))
    return shard, slot


def _shard_rows(b, c):
    br = b
    while br * c * 4 > 2 * EW_BYTES and br % (4 * SUBLANES) == 0:
        br //= 2
    return br


def _gather_place(shard, axis, me, dtype, lead=None):
    a_dim, b, c = shard.shape
    if lead is not None:
        a_dim = 1
    br = _shard_rows(b, c)
    shp = [a_dim, b, c]
    shp[axis] *= 4
    shard_spec, slot_spec = _slot_specs((a_dim, b, c), axis, br, lead)

    def body(me_ref, s_ref, o_ref):
        o_ref[...] = s_ref[...].astype(o_ref.dtype)

    return pl.pallas_call(
        body, name="gather_place",
        grid_spec=pltpu.PrefetchScalarGridSpec(
            num_scalar_prefetch=1, grid=(a_dim, b // br), in_specs=[shard_spec], out_specs=slot_spec),
        out_shape=jax.ShapeDtypeStruct(tuple(shp), dtype),
        compiler_params=_cp(2),
    )(me, shard)


def _sum_landed(grad, landed, axis, me):
    a_dim, b, c = landed.shape[1:]
    br = _shard_rows(b, c)
    shard_spec, slot_spec = _slot_specs((a_dim, b, c), axis, br, None)

    def body(me_ref, own_ref, r_ref, o_ref):
        o_ref[...] = ((own_ref[...].astype(F32) + r_ref[0].astype(F32)) + r_ref[1].astype(F32)) + r_ref[2].astype(F32)

    return pl.pallas_call(
        body, name="sum_landed",
        grid_spec=pltpu.PrefetchScalarGridSpec(
            num_scalar_prefetch=1, grid=(a_dim, b // br),
            in_specs=[slot_spec, pl.BlockSpec((3, None, br, c), lambda a, i, me: (0, a, i, 0))],
            out_specs=shard_spec),
        out_shape=jax.ShapeDtypeStruct((a_dim, b, c), F32),
        compiler_params=_cp(2),
    )(me, grad, landed)


def _copies(refs, plan, send, recv):
    x, y, c = _coords()
    me = 2 * x + y
    out, t = [], 0
    while plan(refs, me, t, 0, me) is not None:
        for k, (px, py) in enumerate(_other_chips(x, y)):
            sv, dv = plan(refs, me, t, k, 2 * px + py)
            out.append(pltpu.make_async_remote_copy(
                src_ref=sv, dst_ref=dv, send_sem=send.at[3 * t + k], recv_sem=recv.at[3 * t + k],
                device_id=(px, py, c), device_id_type=MESH))
        t += 1
    return out


def _push_start(name, bufs, n_ex, plan):
    nb = len(bufs)

    def body(*refs):
        send, recv, token = refs[nb], refs[nb + 1], refs[-1]
        for cp in _copies(refs[:nb], plan, send, recv):
            cp.start()
        token[...] = jnp.zeros_like(token)

    res = pl.pallas_call(
        body, name=name,
        in_specs=[HBM_SPEC] * nb,
        out_specs=[SEM_SPEC, SEM_SPEC] + [HBM_SPEC] * nb + [pl.BlockSpec(memory_space=pltpu.VMEM)],
        out_shape=[pltpu.SemaphoreType.DMA((3 * n_ex,)), pltpu.SemaphoreType.DMA((3 * n_ex,))]
        + [pltpu.HBM(a.shape, a.dtype) for a in bufs] + [jax.ShapeDtypeStruct((SUBLANES, LANES), F32)],
        input_output_aliases={i: 2 + i for i in range(nb)},
        compiler_params=pltpu.CompilerParams(has_side_effects=EFFECT),
    )(*[pltpu.with_memory_space_constraint(a, pltpu.HBM) for a in bufs])
    return res[0], res[1], res[2:2 + nb], res[-1]


def _push_wait(name, send, recv, bufs, plan, after):
    nb = len(bufs)

    def body(*refs):
        for cp in _copies(refs[:nb], plan, refs[nb], refs[nb + 1]):
            cp.wait_send()
            cp.wait_recv()

    return pl.pallas_call(
        body, name=name,
        in_specs=[HBM_SPEC] * nb + [SEM_SPEC, SEM_SPEC, ANY_SPEC],
        out_specs=[HBM_SPEC] * nb,
        out_shape=[pltpu.HBM(a.shape, a.dtype) for a in bufs],
        input_output_aliases={i: i for i in range(nb)},
        compiler_params=pltpu.CompilerParams(has_side_effects=EFFECT),
    )(*bufs, send, recv, after)


def _gather_plan(axes):
    def plan(refs, me, t, k, peer):
        if t >= len(axes):
            return None
        size = refs[t].shape[axes[t]] // 4
        mine = _slot(refs[t], axes[t], me, size)
        return mine, mine
    return plan


def _scatter_plan(axes):
    n = len(axes)

    def plan(refs, me, t, k, peer):
        if t >= n:
            return None
        size = refs[t].shape[axes[t]] // 4
        return _slot(refs[t], axes[t], peer, size), refs[n + t].at[k]
    return plan


def _swap_with_sibling(arrs):
    na = len(arrs)

    def body(*refs):
        ins, outs = refs[:na], refs[na:2 * na]
        send, recv = refs[2 * na:]
        x, y, c = _coords()
        cps = []
        for t in range(na):
            rc = pltpu.make_async_remote_copy(
                src_ref=ins[t], dst_ref=outs[t], send_sem=send.at[t], recv_sem=recv.at[t],
                device_id=(x, y, 1 - c), device_id_type=MESH)
            rc.start()
            cps.append(rc)
        for rc in cps:
            rc.wait()

    any_spec = pl.BlockSpec(memory_space=pl.ANY)
    return pl.pallas_call(
        body, name="swap_sibling",
        in_specs=[any_spec] * na,
        out_specs=[any_spec] * na,
        out_shape=[jax.ShapeDtypeStruct(a.shape, a.dtype) for a in arrs],
        scratch_shapes=[pltpu.SemaphoreType.DMA((na,)), pltpu.SemaphoreType.DMA((na,))],
    )(*arrs)


def _all_reduce_small(v):
    R = v.shape[0]

    def body(v_ref, o_ref, buf, send, recv, local):
        x, y, c = _coords()
        me = 4 * x + 2 * y + c
        cp = pltpu.make_async_copy(v_ref, buf.at[me], local)
        cp.start()
        pushes = []
        for k in range(1, 8):
            peer = (x ^ (k >> 2), y ^ ((k >> 1) & 1), c ^ (k & 1))
            rc = pltpu.make_async_remote_copy(
                src_ref=v_ref, dst_ref=buf.at[me], send_sem=send.at[k - 1], recv_sem=recv.at[k - 1],
                device_id=peer, device_id_type=MESH)
            rc.start()
            pushes.append(rc)
        for k in range(1, 8):
            px, py, pc = x ^ (k >> 2), y ^ ((k >> 1) & 1), c ^ (k & 1)
            pltpu.make_async_remote_copy(
                src_ref=v_ref, dst_ref=buf.at[4 * px + 2 * py + pc], send_sem=send.at[k - 1],
                recv_sem=recv.at[k - 1], device_id=(px, py, pc), device_id_type=MESH).wait_recv()
        for rc in pushes:
            rc.wait_send()
        cp.wait()
        tot = buf[0]
        for k in range(1, 8):
            tot = tot + buf[k]
        o_ref[...] = tot

    vm = pl.BlockSpec(memory_space=pltpu.VMEM)
    return pl.pallas_call(
        body, name="all_reduce_small",
        in_specs=[vm],
        out_specs=vm,
        out_shape=jax.ShapeDtypeStruct((R, LANES), F32),
        scratch_shapes=[
            pltpu.VMEM((8, R, LANES), F32),
            pltpu.SemaphoreType.DMA((7,)),
            pltpu.SemaphoreType.DMA((7,)),
            pltpu.SemaphoreType.DMA,
        ],
        compiler_params=pltpu.CompilerParams(vmem_limit_bytes=VMEM_LIMIT),
    )(v)


def _pack(arrs):
    flat = []
    for a in arrs:
        f = a.reshape(-1).astype(F32)
        flat.append(jnp.pad(f, (0, (-f.shape[0]) % LANES)))
    v = jnp.concatenate(flat)
    v = jnp.pad(v, (0, (-v.shape[0]) % (SUBLANES * LANES)))
    return v.reshape(-1, LANES)


def _unpack(v, shapes):
    flat = v.reshape(-1)
    out, off = [], 0
    for shp in shapes:
        n = 1
        for d in shp:
            n *= d
        out.append(flat[off:off + n].reshape(shp))
        off += n + (-n) % LANES
    return out


def _local_step(x, tgt, sp, weights_for, on_grads):
    n1, n2 = sp["norm1_g"], sp["norm2_g"]
    w = dict(weights_for(0, x))
    saved = []
    xs = x
    kv = None
    for l in range(DEPTH):
        x_in = xs
        if l == N_A:
            w.update(weights_for(2, x_in))
            kv = _rms_linear(x_in, sp["kv_norm_g"], w["w_kv"], sp["b_kv"], 0, 0, "kv_proj")
        if l == N_A + 1:
            w.update(weights_for(3, x_in))
        if l < N_A:
            xa = _pool_fwd(x_in, n1, w["pool_w"], w["pool_scale"], l)
            q = o = None
        else:
            j = l - N_A
            q = _rms_linear(x_in, n1, w["w_q", j], sp["b_q"], l, j, "q_proj")
            o = _attn_fwd(q, kv, sp["sinks"], j)
            xa = _linear_res(o, w["w_o", j], sp["b_o"], x_in, j)
        if l == 1:
            w.update(weights_for(1, xa))
        xs, u, uc = _ffn_fwd(xa, n2, w["ffn_up", l], w["ffn_down", l], w["ffn_conv_w"], sp["ffn_conv_b"], l)
        saved.append((x_in, xa, u, uc, q, o))

    dx, d_final_g, loss = _loss_head(xs, sp["final_g"], tgt)

    g = {k: [None] * DEPTH for k in ("norm1_g", "norm2_g", "ffn_conv_w", "ffn_conv_b")}
    for k in ("pool_scale", "b_q", "sinks", "b_o"):
        g[k] = [None] * N_A
    g["final_g"] = d_final_g
    dkvs = []
    pending = {}
    for l in reversed(range(DEPTH)):
        x_in, xa, u, uc, q, o = saved[l]
        dxa, du, a, hb, g["norm2_g"][l], g["ffn_conv_w"][l], g["ffn_conv_b"][l] = _ffn_bwd(
            xa, dx, u, uc, n2, w["ffn_up", l], w["ffn_down", l], w["ffn_conv_w"], l)
        pending["ffn_up", l] = _tn_matmul(hb, du, MX, "d_ffn_up")
        if l == 0:
            n1 = n1 + on_grads(DEPTH + 1, pending)
            pending = {}
        pending["ffn_down", l] = _tn_matmul(a, dx, MX, "d_ffn_down")
        zero = on_grads(DEPTH - 1 - l, pending)
        pending = {}
        n1, n2 = n1 + zero, n2 + zero
        if l < N_A:
            dx, d_pw, g["pool_scale"][l], g["norm1_g"][l] = _pool_bwd(
                x_in, dxa, n1, w["pool_w"], w["pool_scale"], l)
            pending["pool_w", l] = d_pw.astype(MX)
        else:
            j = l - N_A
            d_o, g["b_o"][j] = _linear_nt(dxa, w["w_o", j])
            pending["w_o", j] = _tn_matmul(o, dxa, MX, "d_w_o")
            dq, dkv, g["sinks"][j] = _attn_bwd(q, d_o, kv, sp["sinks"], j)
            dkvs.append(dkv)
            dx, g["norm1_g"][l], g["b_q"][j], hq, dqb = _rms_linear_bwd(
                x_in, n1, [dq], w["w_q", j], dxa, l, "q_proj_bwd")
            pending["w_q", j] = _tn_matmul(hq, dqb, MX, "d_w_q")
        if l == N_A:
            dx, g["kv_norm_g"], g["b_kv"], hk, dkvb = _rms_linear_bwd(
                x_in, sp["kv_norm_g"], dkvs, w["w_kv"], dx, 0, "kv_proj_bwd")
            pending["w_kv", 0] = _tn_matmul(hk, dkvb, MX, "d_w_kv")
    on_grads(DEPTH, pending)
    return loss, dx, g


SMALL = ("norm1_g", "norm2_g", "kv_norm_g", "b_kv", "b_q", "sinks", "b_o", "ffn_conv_b", "final_g")
SMALL_SHARDED = ("pool_scale", "ffn_conv_w")
BIG = ("pool_w", "w_kv", "w_q", "w_o", "ffn_up", "ffn_down")
ORDER = ("norm1_g", "norm2_g", "pool_w", "pool_scale", "kv_norm_g", "w_kv", "b_kv", "w_q", "b_q", "sinks",
         "w_o", "b_o", "ffn_up", "ffn_conv_w", "ffn_conv_b", "ffn_down", "final_g")


def _as3d(a):
    return a.reshape((-1,) + a.shape[-2:])


def kernel(x, norm1_g, norm2_g, pool_w, pool_scale, kv_norm_g, w_kv, b_kv, w_q, b_q, sinks, w_o, b_o, ffn_up, ffn_conv_w, ffn_conv_b, ffn_down, final_g, loss_target, m_norm1_g, m_norm2_g, m_pool_w, m_pool_scale, m_kv_norm_g, m_w_kv, m_b_kv, m_w_q, m_b_q, m_sinks, m_w_o, m_b_o, m_ffn_up, m_ffn_conv_w, m_ffn_conv_b, m_ffn_down, m_final_g, v_norm1_g, v_norm2_g, v_pool_w, v_pool_scale, v_kv_norm_g, v_w_kv, v_b_kv, v_w_q, v_b_q, v_sinks, v_w_o, v_b_o, v_ffn_up, v_ffn_conv_w, v_ffn_conv_b, v_ffn_down, v_final_g):
    W = dict(norm1_g=norm1_g, norm2_g=norm2_g, pool_w=pool_w, pool_scale=pool_scale, kv_norm_g=kv_norm_g,
             w_kv=w_kv, b_kv=b_kv, w_q=w_q, b_q=b_q, sinks=sinks, w_o=w_o, b_o=b_o, ffn_up=ffn_up,
             ffn_conv_w=ffn_conv_w, ffn_conv_b=ffn_conv_b, ffn_down=ffn_down, final_g=final_g)
    M = dict(norm1_g=m_norm1_g, norm2_g=m_norm2_g, pool_w=m_pool_w, pool_scale=m_pool_scale,
             kv_norm_g=m_kv_norm_g, w_kv=m_w_kv, b_kv=m_b_kv, w_q=m_w_q, b_q=m_b_q, sinks=m_sinks, w_o=m_w_o,
             b_o=m_b_o, ffn_up=m_ffn_up, ffn_conv_w=m_ffn_conv_w, ffn_conv_b=m_ffn_conv_b, ffn_down=m_ffn_down,
             final_g=m_final_g)
    V = dict(norm1_g=v_norm1_g, norm2_g=v_norm2_g, pool_w=v_pool_w, pool_scale=v_pool_scale,
             kv_norm_g=v_kv_norm_g, w_kv=v_w_kv, b_kv=v_b_kv, w_q=v_w_q, b_q=v_b_q, sinks=v_sinks, w_o=v_w_o,
             b_o=v_b_o, ffn_up=v_ffn_up, ffn_conv_w=v_ffn_conv_w, ffn_conv_b=v_ffn_conv_b, ffn_down=v_ffn_down,
             final_g=v_final_g)
    S = x.shape[1]
    chip = 2 * lax.axis_index("x") + lax.axis_index("y")

    gather_axis = dict(pool_w=1, w_kv=1, w_q=1, w_o=1, ffn_up=2, ffn_down=1, pool_scale=2, ffn_conv_w=2)
    me = chip.reshape(1).astype(jnp.int32)
    axis_of = lambda key: gather_axis[key if isinstance(key, str) else key[0]]

    def placed(key, dtype):
        if isinstance(key, str):
            return _gather_place(_as3d(W[key]), axis_of(key), me, dtype)
        return _gather_place(W[key[0]], axis_of(key), me, dtype, lead=key[1])

    stages = [
        ["pool_w", "pool_scale", "ffn_conv_w", ("ffn_up", 0), ("ffn_down", 0)],
        [("ffn_up", 1), ("ffn_down", 1)],
        ["w_kv", ("w_q", 0), ("w_o", 0), ("ffn_up", 2), ("ffn_down", 2)],
        [("w_q", 1), ("w_o", 1), ("ffn_up", 3), ("ffn_down", 3)],
    ]
    gathers, zero = [], 0.0
    for si, keys in enumerate(stages):
        axes = [axis_of(k) for k in keys]
        bufs = [placed(k, F32 if k in SMALL_SHARDED else MX) for k in keys]
        send, recv, bufs, token = _push_start(f"gather_start_{si}", bufs, len(keys), _gather_plan(axes))
        gathers.append((keys, axes, send, recv, bufs))
        zero = zero + token[0, 0]

    def weights_for(stage, after):
        keys, axes, send, recv, bufs = gathers[stage]
        out = dict(zip(keys, _push_wait(f"gather_wait_{stage}", send, recv, bufs, _gather_plan(axes), after)))
        if stage == 0:
            out["pool_w"] = out["pool_w"].reshape(N_A, 4, GC, GC)
            out["pool_scale"] = out["pool_scale"].reshape(N_A, 1, D)
        return out

    scatters = []

    def on_grads(stage, grads):
        keys = list(grads)
        axes = [axis_of(k) for k in keys]
        arrs = [_as3d(grads[k]) for k in keys]
        lands = []
        for a, ax in zip(arrs, axes):
            shp = list(a.shape)
            shp[ax] //= 4
            lands.append(lax.empty((3,) + tuple(shp), a.dtype))
        send, recv, bufs, token = _push_start(f"scatter_start_{stage}", arrs + lands, len(keys), _scatter_plan(axes))
        scatters.append((stage, keys, axes, send, recv, bufs))
        return token[0, 0]

    sp = dict(
        norm1_g=norm1_g.reshape(DEPTH, 1, D) + zero, norm2_g=norm2_g.reshape(DEPTH, 1, D),
        kv_norm_g=kv_norm_g.reshape(1, 1, D), b_kv=b_kv.reshape(1, 1, 4 * HD), b_q=b_q.reshape(N_B, 1, D),
        sinks=sinks, b_o=b_o.reshape(N_B, 1, D), ffn_conv_b=ffn_conv_b.reshape(DEPTH, 1, F2),
        final_g=final_g.reshape(1, D))

    x2d = x.reshape(S, D)
    loss, grad_x, g = _local_step(x2d, loss_target.reshape(S, D), sp, weights_for, on_grads)

    small_full = dict(
        norm1_g=jnp.stack(g["norm1_g"]), norm2_g=jnp.stack(g["norm2_g"]), kv_norm_g=g["kv_norm_g"],
        b_kv=g["b_kv"], b_q=jnp.stack(g["b_q"]), sinks=jnp.stack([s[0, :NH] for s in g["sinks"]]),
        b_o=jnp.stack(g["b_o"]), ffn_conv_b=jnp.stack(g["ffn_conv_b"]), final_g=g["final_g"],
        pool_scale=jnp.stack(g["pool_scale"]), ffn_conv_w=jnp.stack(g["ffn_conv_w"]))
    small_names = SMALL + SMALL_SHARDED
    small_shapes = [tuple(W[k].shape) for k in SMALL] + [(N_A, D), (DEPTH, 3, F2)]
    packed = _pack([small_full[k] for k in small_names] + [loss])
    red = _unpack(_all_reduce_small(packed), small_shapes + [(1, LANES)])
    red_g = dict(zip(small_names, red[:-1]))
    loss_out = red[-1][0, 0]
    red_g["pool_scale"] = lax.dynamic_slice_in_dim(red_g["pool_scale"], chip * (D // 4), D // 4, axis=1)
    red_g["ffn_conv_w"] = lax.dynamic_slice_in_dim(red_g["ffn_conv_w"], chip * (F2 // 4), F2 // 4, axis=2)
    small_w_shapes = [tuple(W[k].shape) for k in small_names]
    pk = lambda d: _pack([d[k] for k in small_names])[None]
    res = _adamw([[_pack([red_g[k] for k in small_names])]], pk(W), pk(M), pk(V))
    out_g, out_d, out_m, out_v = [dict(zip(small_names, _unpack(r, small_w_shapes))) for r in res]

    pkeys, partial = [], []
    for stage, keys, axes, send, recv, bufs in scatters:
        bufs = _push_wait(f"scatter_wait_{stage}", send, recv, bufs, _scatter_plan(axes), grad_x)
        n = len(keys)
        for k, ax, grad, landed in zip(keys, axes, bufs[:n], bufs[n:]):
            pkeys.append(k)
            p_sum = _sum_landed(grad, landed, ax, me)
            partial.append(p_sum.reshape(-1, p_sum.shape[-1]))
    mine = dict(zip(pkeys, partial))
    theirs = dict(zip(pkeys, _swap_with_sibling(partial)))
    for k in BIG:
        n_l = len([pk_ for pk_ in pkeys if pk_[0] == k])
        shp = W[k].shape
        rows, cols = mine[k, 0].shape
        three_d = lambda a: a.reshape(n_l, rows, cols)
        res = _adamw([[mine[k, l], theirs[k, l]] for l in range(n_l)], three_d(W[k]), three_d(M[k]), three_d(V[k]))
        out_g[k], out_d[k], out_m[k], out_v[k] = [r.reshape(shp) for r in res]

    return (loss_out, grad_x.reshape(x.shape), *[out_g[k] for k in ORDER], *[out_d[k] for k in ORDER],
            *[out_m[k] for k in ORDER], *[out_v[k] for k in ORDER])
```

```python
import functools

import jax
import jax.numpy as jnp
from jax import lax
from jax.experimental import pallas as pl
from jax.experimental.pallas import tpu as pltpu

D = 1024
DEPTH = 4
N_A = 2
N_B = 2
WINS = (2, 4, 8, 16)
GC = 256
HD = 64
NH = 16
BLK = 128
F = 2816
F2 = 2 * F
EPS = 1e-5
SCALE = HD ** -0.5
NEG = -1e30
HALO = 16
TN = 256
UP_GROUP = 3
LANES = 128
SUBLANES = 8
VMEM_LIMIT = 56 * 1024 * 1024
FFN_VMEM_LIMIT = 60 * 1024 * 1024
ACC_BYTES = 6 * 1024 * 1024
EW_BYTES = 1024 * 1024

LR, B1, B2, AEPS, WD, STEP = 0.001, 0.9, 0.999, 1e-08, 0.01, 10

MX = jnp.bfloat16
F32 = jnp.float32
MESH = pl.DeviceIdType.MESH


def _cp(n_axes=1, vmem=VMEM_LIMIT):
    return pltpu.CompilerParams(dimension_semantics=("arbitrary",) * n_axes, vmem_limit_bytes=vmem)


def _dot(a, b):
    return jnp.dot(a, b, preferred_element_type=F32)


def _dot_nt(a, b):
    return lax.dot_general(a, b, (((1,), (1,)), ((), ())), preferred_element_type=F32)


def _dot_tn(a, b):
    return lax.dot_general(a, b, (((0,), (0,)), ((), ())), preferred_element_type=F32)


def _rms_fwd(x, g):
    r = lax.rsqrt(jnp.mean(x * x, axis=-1, keepdims=True) + EPS)
    xh = x * r
    return xh * g, xh, r


def _rms_bwd(dh, xh, r, g):
    dxh = dh * g
    return r * (dxh - xh * jnp.mean(dxh * xh, axis=-1, keepdims=True))


def _row_tile(s, want):
    return min(s, want)


def _pool_pm(e, h, row, tm):
    out = []
    for gi, win in enumerate(WINS):
        cols = slice(gi * GC, (gi + 1) * GC)
        s = e[:, cols]
        sh = 1
        while sh < win:
            s = s + pltpu.roll(s, sh, 0)
            sh *= 2
        inv = 1.0 / jnp.minimum(row + 1, win).astype(F32)
        out.append(s[HALO:] * inv - h[:, cols])
    return out


def _pool_fwd(x, g, pw, ps, layer):
    S = x.shape[0]
    tm = _row_tile(S, 512)
    hb = tm // HALO

    def body(x_ref, xh_ref, g_ref, pw_ref, ps_ref, o_ref):
        i = pl.program_id(0)
        x = x_ref[...]
        gg = g_ref[...]
        h, _, _ = _rms_fwd(x, gg)
        hh, _, _ = _rms_fwd(xh_ref[...], gg)
        hh = jnp.where(i > 0, hh, 0.0)
        e = jnp.concatenate([hh, h], axis=0)
        row = i * tm + lax.broadcasted_iota(jnp.int32, (tm, 1), 0)
        pm = _pool_pm(e, h, row, tm)
        for gi in range(len(WINS)):
            cols = slice(gi * GC, (gi + 1) * GC)
            z = _dot(pm[gi].astype(MX), pw_ref[gi])
            o_ref[:, cols] = x[:, cols] + z * ps_ref[:, cols]

    return pl.pallas_call(
        body, name="pool_fwd",
        grid=(S // tm,),
        in_specs=[
            pl.BlockSpec((tm, D), lambda i: (i, 0)),
            pl.BlockSpec((HALO, D), lambda i: (jnp.maximum(i * hb - 1, 0), 0)),
            pl.BlockSpec((None, 1, D), lambda i: (layer, 0, 0)),
            pl.BlockSpec((None, 4, GC, GC), lambda i: (layer, 0, 0, 0)),
            pl.BlockSpec((None, 1, D), lambda i: (layer, 0, 0)),
        ],
        out_specs=pl.BlockSpec((tm, D), lambda i: (i, 0)),
        out_shape=jax.ShapeDtypeStruct((S, D), F32),
        compiler_params=_cp(),
    )(x, x, g, pw, ps)


def _pool_bwd(x, dy, g, pw, ps, layer):
    S = x.shape[0]
    tm = _row_tile(S, 256)
    hb = tm // HALO
    n_i = S // tm
    n_h = S // HALO

    def body(x_ref, xh_ref, dy_ref, dyn_ref, g_ref, pw_ref, ps_ref, dx_ref, dpw_ref, dps_ref, dg_ref):
        i = pl.program_id(0)

        @pl.when(i == 0)
        def _():
            dpw_ref[...] = jnp.zeros_like(dpw_ref)
            dps_ref[...] = jnp.zeros_like(dps_ref)
            dg_ref[...] = jnp.zeros_like(dg_ref)

        x = x_ref[...]
        gg = g_ref[...]
        ps = ps_ref[...]
        h, xh, r = _rms_fwd(x, gg)
        hh, _, _ = _rms_fwd(xh_ref[...], gg)
        hh = jnp.where(i > 0, hh, 0.0)
        e = jnp.concatenate([hh, h], axis=0)
        row = i * tm + lax.broadcasted_iota(jnp.int32, (tm, 1), 0)
        rown = (i + 1) * tm + lax.broadcasted_iota(jnp.int32, (HALO, 1), 0)
        pm = _pool_pm(e, h, row, tm)
        dy = dy_ref[...]
        dz = dy * ps
        dzn = jnp.where(i < n_i - 1, dyn_ref[...] * ps, 0.0)
        parts = []
        for gi, win in enumerate(WINS):
            cols = slice(gi * GC, (gi + 1) * GC)
            w = pw_ref[gi]
            pmb = pm[gi].astype(MX)
            z = _dot(pmb, w)
            dps_ref[:, cols] += jnp.sum(dy[:, cols] * z, axis=0, keepdims=True)
            dzb = dz[:, cols].astype(MX)
            dpw_ref[gi] += _dot_tn(pmb, dzb)
            dpm = _dot_nt(dzb, w)
            dpmn = _dot_nt(dzn[:, cols].astype(MX), w)
            q = dpm * (1.0 / jnp.minimum(row + 1, win).astype(F32))
            qn = dpmn * (1.0 / jnp.minimum(rown + 1, win).astype(F32))
            s = jnp.concatenate([q, qn], axis=0)
            sh = 1
            while sh < win:
                s = s + pltpu.roll(s, tm + HALO - sh, 0)
                sh *= 2
            parts.append(s[:tm] - dpm)
        dh = jnp.concatenate(parts, axis=1)
        dg_ref[...] += jnp.sum(dh * xh, axis=0, keepdims=True)
        dx_ref[...] = dy + _rms_bwd(dh, xh, r, gg)

    return pl.pallas_call(
        body, name="pool_bwd",
        grid=(n_i,),
        in_specs=[
            pl.BlockSpec((tm, D), lambda i: (i, 0)),
            pl.BlockSpec((HALO, D), lambda i: (jnp.maximum(i * hb - 1, 0), 0)),
            pl.BlockSpec((tm, D), lambda i: (i, 0)),
            pl.BlockSpec((HALO, D), lambda i: (jnp.minimum((i + 1) * hb, n_h - 1), 0)),
            pl.BlockSpec((None, 1, D), lambda i: (layer, 0, 0)),
            pl.BlockSpec((None, 4, GC, GC), lambda i: (layer, 0, 0, 0)),
            pl.BlockSpec((None, 1, D), lambda i: (layer, 0, 0)),
        ],
        out_specs=[
            pl.BlockSpec((tm, D), lambda i: (i, 0)),
            pl.BlockSpec((4, GC, GC), lambda i: (0, 0, 0)),
            pl.BlockSpec((1, D), lambda i: (0, 0)),
            pl.BlockSpec((1, D), lambda i: (0, 0)),
        ],
        out_shape=[
            jax.ShapeDtypeStruct((S, D), F32),
            jax.ShapeDtypeStruct((4, GC, GC), F32),
            jax.ShapeDtypeStruct((1, D), F32),
            jax.ShapeDtypeStruct((1, D), F32),
        ],
        compiler_params=_cp(),
    )(x, x, dy, dy, g, pw, ps)


N_STAGE = 4


def _rows_before(slot, u, prev8):
    tm = u.shape[0]
    m1, m2 = [], []
    for c in range(TN // LANES):
        lanes = slice(c * LANES, (c + 1) * LANES)
        slot[c, 0:SUBLANES, :] = prev8[:, lanes]
        slot[c, SUBLANES:SUBLANES + tm, :] = u[:, lanes]
        m1.append(slot[c, pl.ds(SUBLANES - 1, tm), :])
        m2.append(slot[c, pl.ds(SUBLANES - 2, tm), :])
    return jnp.concatenate(m1, axis=1), jnp.concatenate(m2, axis=1)


def _rows_after(slot, d, next8):
    tm = d.shape[0]
    p1, p2 = [], []
    for c in range(TN // LANES):
        lanes = slice(c * LANES, (c + 1) * LANES)
        slot[c, 0:tm, :] = d[:, lanes]
        slot[c, tm:tm + SUBLANES, :] = next8[:, lanes]
        p1.append(slot[c, pl.ds(1, tm), :])
        p2.append(slot[c, pl.ds(2, tm), :])
    return jnp.concatenate(p1, axis=1), jnp.concatenate(p2, axis=1)


def _conv(slot, u, prev8, cw):
    um1, um2 = _rows_before(slot, u, prev8)
    return cw[0:1] * um2 + cw[1:2] * um1 + cw[2:3] * u


def _ffn_fwd(x, g, wup, wdn, cw, cb, layer):
    S = x.shape[0]
    tm = _row_tile(S, 512)

    def body(x_ref, g_ref, wup_hbm, wdn_hbm, cw_ref, cb_ref, o_ref, u_ref, uc_ref, wup_v, wdn_v, carry, act, stage):
        i = pl.program_id(0)

        @pl.when(i == 0)
        def _():
            pltpu.sync_copy(wup_hbm.at[0], wup_v)
            pltpu.sync_copy(wdn_hbm.at[0], wdn_v)
            carry[...] = jnp.zeros_like(carry)

        x = x_ref[...]
        h, _, _ = _rms_fwd(x, g_ref[...])
        hb = h.astype(MX)
        for j in range(F // TN):
            cg = slice(j * TN, (j + 1) * TN)
            cv = slice(F + j * TN, F + (j + 1) * TN)
            ug = _dot(hb, wup_v[:, cg])
            uv = _dot(hb, wup_v[:, cv])
            u_ref[:, cg] = ug.astype(u_ref.dtype)
            u_ref[:, cv] = uv.astype(u_ref.dtype)
            gt = _conv(stage.at[2 * (j % 2)], ug, carry[:, cg], cw_ref[:, cg])
            vl = _conv(stage.at[2 * (j % 2) + 1], uv, carry[:, cv], cw_ref[:, cv])
            carry[:, cg] = ug[tm - SUBLANES:]
            carry[:, cv] = uv[tm - SUBLANES:]
            gt = gt + cb_ref[:, cg]
            vl = vl + cb_ref[:, cv]
            uc_ref[:, cg] = gt.astype(uc_ref.dtype)
            uc_ref[:, cv] = vl.astype(uc_ref.dtype)
            act[:, cg] = (gt * jax.nn.sigmoid(gt) * vl).astype(act.dtype)
        o_ref[...] = x + _dot(act[...], wdn_v[...])

    return pl.pallas_call(
        body, name="ffn_fwd",
        grid=(S // tm,),
        in_specs=[
            pl.BlockSpec((tm, D), lambda i: (i, 0)),
            pl.BlockSpec((None, 1, D), lambda i: (layer, 0, 0)),
            pl.BlockSpec(memory_space=pl.ANY),
            pl.BlockSpec(memory_space=pl.ANY),
            pl.BlockSpec((None, 3, F2), lambda i: (layer, 0, 0)),
            pl.BlockSpec((None, 1, F2), lambda i: (layer, 0, 0)),
        ],
        out_specs=[
            pl.BlockSpec((tm, D), lambda i: (i, 0)),
            pl.BlockSpec((tm, F2), lambda i: (i, 0)),
            pl.BlockSpec((tm, F2), lambda i: (i, 0)),
        ],
        out_shape=[
            jax.ShapeDtypeStruct((S, D), F32),
            jax.ShapeDtypeStruct((S, F2), MX),
            jax.ShapeDtypeStruct((S, F2), MX),
        ],
        scratch_shapes=[
            pltpu.VMEM((D, F2), MX),
            pltpu.VMEM((F, D), MX),
            pltpu.VMEM((SUBLANES, F2), F32),
            pltpu.VMEM((tm, F), MX),
            pltpu.VMEM((N_STAGE, TN // LANES, tm + SUBLANES, LANES), F32),
        ],
        compiler_params=_cp(vmem=FFN_VMEM_LIMIT),
    )(x, g, wup, wdn, cw, cb)


def _ffn_bwd(x, dy, u, uc, g, wup, wdn, cw, layer):
    S = x.shape[0]
    tm = _row_tile(S, 256)
    n_i = S // tm

    def body(x_ref, dy_ref, u_ref, uc_ref, g_ref, wup_hbm, wdn_hbm, cw_ref,
             dx_ref, du_ref, a_ref, h_ref, dg_ref, dcw_ref, dcb_ref, wup_v, wdn_v, carry, stage):
        i = pl.program_id(0)

        @pl.when(i == 0)
        def _():
            pltpu.sync_copy(wup_hbm.at[0], wup_v)
            pltpu.sync_copy(wdn_hbm.at[0], wdn_v)
            carry[...] = jnp.zeros_like(carry)
            dg_ref[...] = jnp.zeros_like(dg_ref)
            dcw_ref[...] = jnp.zeros_like(dcw_ref)
            dcb_ref[...] = jnp.zeros_like(dcb_ref)

        x = x_ref[...]
        gg = g_ref[...]
        h, xh, r = _rms_fwd(x, gg)
        h_ref[...] = h.T.astype(h_ref.dtype)
        dy = dy_ref[...]
        dyb = dy.astype(MX)
        dh, dus = None, ([], [])
        for j in range(F // TN):
            cg = slice(j * TN, (j + 1) * TN)
            cv = slice(F + j * TN, F + (j + 1) * TN)
            gt = uc_ref[:, cg].astype(F32)
            vl = uc_ref[:, cv].astype(F32)
            sg = jax.nn.sigmoid(gt)
            sil = gt * sg
            a_ref[cg, :] = (sil * vl).T.astype(a_ref.dtype)
            da = _dot_nt(dyb, wdn_v[cg, :])
            dvl = da * sil
            dgt = (da * vl) * (sg + sil * (1.0 - sg))
            for cc, dd in ((cg, dgt), (cv, dvl)):
                dp1, dp2 = _rows_after(stage.at[2 * (j % 2) + (cc is cv)], dd, carry[:, cc])
                carry[:, cc] = dd[0:SUBLANES]
                uu = u_ref[:, cc].astype(F32)
                dcb_ref[:, cc] += jnp.sum(dd, axis=0, keepdims=True)
                dcw_ref[0:1, cc] += jnp.sum(dp2 * uu, axis=0, keepdims=True)
                dcw_ref[1:2, cc] += jnp.sum(dp1 * uu, axis=0, keepdims=True)
                dcw_ref[2:3, cc] += jnp.sum(dd * uu, axis=0, keepdims=True)
                cwc = cw_ref[:, cc]
                duu = (cwc[2:3] * dd + cwc[1:2] * dp1 + cwc[0:1] * dp2).astype(MX)
                du_ref[:, cc] = duu
                dus[cc is cv].append(duu)
            if len(dus[0]) == UP_GROUP or j == F // TN - 1:
                first = j + 1 - len(dus[0])
                for side, base in ((0, 0), (1, F)):
                    cols = slice(base + first * TN, base + (j + 1) * TN)
                    part = _dot_nt(jnp.concatenate(dus[side], axis=1), wup_v[:, cols])
                    dh = part if dh is None else dh + part
                dus = ([], [])
        dg_ref[...] += jnp.sum(dh * xh, axis=0, keepdims=True)
        dx_ref[...] = dy + _rms_bwd(dh, xh, r, gg)

    rev = lambda i: (n_i - 1 - i, 0)
    return pl.pallas_call(
        body, name="ffn_bwd",
        grid=(n_i,),
        in_specs=[
            pl.BlockSpec((tm, D), rev),
            pl.BlockSpec((tm, D), rev),
            pl.BlockSpec((tm, F2), rev),
            pl.BlockSpec((tm, F2), rev),
            pl.BlockSpec((None, 1, D), lambda i: (layer, 0, 0)),
            pl.BlockSpec(memory_space=pl.ANY),
            pl.BlockSpec(memory_space=pl.ANY),
            pl.BlockSpec((None, 3, F2), lambda i: (layer, 0, 0)),
        ],
        out_specs=[
            pl.BlockSpec((tm, D), rev),
            pl.BlockSpec((tm, F2), rev),
            pl.BlockSpec((F, tm), lambda i: (0, n_i - 1 - i)),
            pl.BlockSpec((D, tm), lambda i: (0, n_i - 1 - i)),
            pl.BlockSpec((1, D), lambda i: (0, 0)),
            pl.BlockSpec((3, F2), lambda i: (0, 0)),
            pl.BlockSpec((1, F2), lambda i: (0, 0)),
        ],
        out_shape=[
            jax.ShapeDtypeStruct((S, D), F32),
            jax.ShapeDtypeStruct((S, F2), MX),
            jax.ShapeDtypeStruct((F, S), MX),
            jax.ShapeDtypeStruct((D, S), MX),
            jax.ShapeDtypeStruct((1, D), F32),
            jax.ShapeDtypeStruct((3, F2), F32),
            jax.ShapeDtypeStruct((1, F2), F32),
        ],
        scratch_shapes=[
            pltpu.VMEM((D, F2), MX),
            pltpu.VMEM((F, D), MX),
            pltpu.VMEM((SUBLANES, F2), F32),
            pltpu.VMEM((N_STAGE, TN // LANES, tm + SUBLANES, LANES), F32),
        ],
        compiler_params=_cp(vmem=FFN_VMEM_LIMIT),
    )(x, dy, u, uc, g, wup, wdn, cw)


def _tn_matmul(a, b, out_dtype, name, a_is_transposed=False):
    S, N = b.shape
    M = a.shape[0] if a_is_transposed else a.shape[1]
    bn = N
    while M * bn * 4 > ACC_BYTES and bn % (2 * LANES) == 0:
        bn //= 2
    bk = _row_tile(S, 1024)
    nk = S // bk
    a_spec = pl.BlockSpec((M, bk), lambda j, k: (0, k)) if a_is_transposed else pl.BlockSpec((bk, M), lambda j, k: (k, 0))

    def body(a_ref, b_ref, o_ref, acc):
        k = pl.program_id(1)
        if a_is_transposed:
            p = _dot(a_ref[...].astype(MX), b_ref[...].astype(MX))
        else:
            p = _dot_tn(a_ref[...].astype(MX), b_ref[...].astype(MX))

        @pl.when(k == 0)
        def _():
            acc[...] = p

        @pl.when(k > 0)
        def _():
            acc[...] += p

        @pl.when(k == nk - 1)
        def _():
            o_ref[...] = acc[...].astype(o_ref.dtype)

    return pl.pallas_call(
        body, name=name,
        grid=(N // bn, nk),
        in_specs=[
            a_spec,
            pl.BlockSpec((bk, bn), lambda j, k: (k, j)),
        ],
        out_specs=pl.BlockSpec((M, bn), lambda j, k: (0, j)),
        out_shape=jax.ShapeDtypeStruct((M, N), out_dtype),
        scratch_shapes=[pltpu.VMEM((M, bn), F32)],
        compiler_params=_cp(2),
    )(a, b)


def _rms_linear(x, g, w, b, g_layer, b_layer, name):
    S = x.shape[0]
    N = w.shape[-1]
    tm = _row_tile(S, 512)

    def body(x_ref, g_ref, w_ref, b_ref, o_ref):
        h, _, _ = _rms_fwd(x_ref[...], g_ref[...])
        o_ref[...] = (_dot(h.astype(MX), w_ref[...]) + b_ref[...]).astype(o_ref.dtype)

    return pl.pallas_call(
        body, name=name,
        grid=(S // tm,),
        in_specs=[
            pl.BlockSpec((tm, D), lambda i: (i, 0)),
            pl.BlockSpec((None, 1, D), lambda i: (g_layer, 0, 0)),
            pl.BlockSpec((None, D, N), lambda i: (0, 0, 0)),
            pl.BlockSpec((None, 1, N), lambda i: (b_layer, 0, 0)),
        ],
        out_specs=pl.BlockSpec((tm, N), lambda i: (i, 0)),
        out_shape=jax.ShapeDtypeStruct((S, N), MX),
        compiler_params=_cp(),
    )(x, g, w, b)


def _linear_res(o, w, b, xres, layer):
    S = o.shape[0]
    tm = _row_tile(S, 512)

    def body(o_ref, w_ref, b_ref, x_ref, y_ref):
        y_ref[...] = x_ref[...] + _dot(o_ref[...], w_ref[...]) + b_ref[...]

    return pl.pallas_call(
        body, name="o_proj",
        grid=(S // tm,),
        in_specs=[
            pl.BlockSpec((tm, D), lambda i: (i, 0)),
            pl.BlockSpec((None, D, D), lambda i: (0, 0, 0)),
            pl.BlockSpec((None, 1, D), lambda i: (layer, 0, 0)),
            pl.BlockSpec((tm, D), lambda i: (i, 0)),
        ],
        out_specs=pl.BlockSpec((tm, D), lambda i: (i, 0)),
        out_shape=jax.ShapeDtypeStruct((S, D), F32),
        compiler_params=_cp(),
    )(o, w, b, xres)


def _linear_nt(dy, w):
    S = dy.shape[0]
    tm = _row_tile(S, 512)

    def body(dy_ref, w_ref, o_ref, db_ref):
        @pl.when(pl.program_id(0) == 0)
        def _():
            db_ref[...] = jnp.zeros_like(db_ref)

        dy = dy_ref[...]
        db_ref[...] += jnp.sum(dy, axis=0, keepdims=True)
        o_ref[...] = _dot_nt(dy.astype(MX), w_ref[...]).astype(o_ref.dtype)

    return pl.pallas_call(
        body, name="o_proj_bwd",
        grid=(S // tm,),
        in_specs=[
            pl.BlockSpec((tm, D), lambda i: (i, 0)),
            pl.BlockSpec((None, D, D), lambda i: (0, 0, 0)),
        ],
        out_specs=[
            pl.BlockSpec((tm, D), lambda i: (i, 0)),
            pl.BlockSpec((1, D), lambda i: (0, 0)),
        ],
        out_shape=[
            jax.ShapeDtypeStruct((S, D), MX),
            jax.ShapeDtypeStruct((1, D), F32),
        ],
        compiler_params=_cp(),
    )(dy, w)


def _rms_linear_bwd(x, g, dzs, w, dy, g_layer, name):
    S = x.shape[0]
    N = w.shape[-1]
    tm = _row_tile(S, 512)
    nz = len(dzs)

    def body(*refs):
        x_ref, g_ref = refs[0], refs[1]
        dz_refs = refs[2:2 + nz]
        w_ref, dy_ref, dx_ref, dg_ref, db_ref, h_ref, dzb_ref = refs[2 + nz:]

        @pl.when(pl.program_id(0) == 0)
        def _():
            dg_ref[...] = jnp.zeros_like(dg_ref)
            db_ref[...] = jnp.zeros_like(db_ref)

        gg = g_ref[...]
        h, xh, r = _rms_fwd(x_ref[...], gg)
        h_ref[...] = h.astype(h_ref.dtype)
        dz = dz_refs[0][...].astype(F32)
        for zr in dz_refs[1:]:
            dz = dz + zr[...].astype(F32)
        db_ref[...] += jnp.sum(dz, axis=0, keepdims=True)
        dzb = dz.astype(MX)
        dzb_ref[...] = dzb
        dh = _dot_nt(dzb, w_ref[...])
        dg_ref[...] += jnp.sum(dh * xh, axis=0, keepdims=True)
        dx_ref[...] = dy_ref[...] + _rms_bwd(dh, xh, r, gg)

    return pl.pallas_call(
        body, name=name,
        grid=(S // tm,),
        in_specs=[
            pl.BlockSpec((tm, D), lambda i: (i, 0)),
            pl.BlockSpec((None, 1, D), lambda i: (g_layer, 0, 0)),
        ] + [pl.BlockSpec((tm, N), lambda i: (i, 0))] * nz + [
            pl.BlockSpec((None, D, N), lambda i: (0, 0, 0)),
            pl.BlockSpec((tm, D), lambda i: (i, 0)),
        ],
        out_specs=[
            pl.BlockSpec((tm, D), lambda i: (i, 0)),
            pl.BlockSpec((1, D), lambda i: (0, 0)),
            pl.BlockSpec((1, N), lambda i: (0, 0)),
            pl.BlockSpec((tm, D), lambda i: (i, 0)),
            pl.BlockSpec((tm, N), lambda i: (i, 0)),
        ],
        out_shape=[
            jax.ShapeDtypeStruct((S, D), F32),
            jax.ShapeDtypeStruct((1, D), F32),
            jax.ShapeDtypeStruct((1, N), F32),
            jax.ShapeDtypeStruct((S, D), MX),
            jax.ShapeDtypeStruct((S, N), MX),
        ],
        compiler_params=_cp(),
    )(x, g, *dzs, w, dy)


def _loss_head(x, g, tgt):
    S = x.shape[0]
    tm = _row_tile(S, 512)

    def body(x_ref, g_ref, t_ref, dx_ref, dg_ref, l_ref):
        @pl.when(pl.program_id(0) == 0)
        def _():
            dg_ref[...] = jnp.zeros_like(dg_ref)
            l_ref[...] = jnp.zeros_like(l_ref)

        gg = g_ref[...]
        y, xh, r = _rms_fwd(x_ref[...], gg)
        err = y - t_ref[...]
        tok = jnp.sum(err * err, axis=-1, keepdims=True) * (1.0 / D)
        l_ref[...] += 0.5 * jnp.sum(tok, axis=0, keepdims=True)
        dyv = err * (1.0 / D)
        dg_ref[...] += jnp.sum(dyv * xh, axis=0, keepdims=True)
        dx_ref[...] = _rms_bwd(dyv, xh, r, gg)

    return pl.pallas_call(
        body, name="loss_head",
        grid=(S // tm,),
        in_specs=[
            pl.BlockSpec((tm, D), lambda i: (i, 0)),
            pl.BlockSpec((1, D), lambda i: (0, 0)),
            pl.BlockSpec((tm, D), lambda i: (i, 0)),
        ],
        out_specs=[
            pl.BlockSpec((tm, D), lambda i: (i, 0)),
            pl.BlockSpec((1, D), lambda i: (0, 0)),
            pl.BlockSpec((1, LANES), lambda i: (0, 0)),
        ],
        out_shape=[
            jax.ShapeDtypeStruct((S, D), F32),
            jax.ShapeDtypeStruct((1, D), F32),
            jax.ShapeDtypeStruct((1, LANES), F32),
        ],
        compiler_params=_cp(),
    )(x, g, tgt)


HPG = NH // 2
ROWS = HPG * BLK


def _attn_setup(kvp_ref, kvc_ref, n):
    kw = jnp.concatenate([kvp_ref[...], kvc_ref[...]], axis=0).astype(F32)
    kk, vv = kw[:, :LANES], kw[:, LANES:]
    lo = lax.broadcasted_iota(jnp.int32, (1, LANES), 1) < HD
    kr, vr = pltpu.roll(kk, HD, 1), pltpu.roll(vv, HD, 1)
    ks = [jnp.where(lo, kk, kr).astype(MX), jnp.where(lo, kr, kk).astype(MX)]
    vs = [jnp.where(lo, vv, vr).astype(MX), jnp.where(lo, vr, vv).astype(MX)]
    qi = lax.broadcasted_iota(jnp.int32, (ROWS, 2 * BLK), 0) & (BLK - 1)
    si = lax.broadcasted_iota(jnp.int32, (ROWS, 2 * BLK), 1)
    mask = (si > qi) & (si <= qi + BLK) & jnp.logical_or(n > 0, si >= BLK)
    return ks, vs, lo, mask


def _stack_heads(ref, grp, lo):
    parts = []
    for j in range(4 * grp, 4 * grp + 4):
        slab = ref[:, j * LANES:(j + 1) * LANES]
        zero = jnp.zeros_like(slab)
        parts += [jnp.where(lo, slab, zero), jnp.where(lo, zero, slab)]
    return jnp.concatenate(parts, axis=0)


def _unstack_heads(st, lo):
    return [jnp.where(lo, st[2 * i * BLK:(2 * i + 1) * BLK], st[(2 * i + 1) * BLK:(2 * i + 2) * BLK])
            for i in range(4)]


def _sink_column(sk_ref, layer, grp):
    head = lax.broadcasted_iota(jnp.int32, (ROWS, 1), 0) // BLK
    col = jnp.zeros((ROWS, 1), F32)
    for h in range(HPG):
        col = jnp.where(head == h, sk_ref[layer, HPG * grp + h], col)
    return col


def _attn_probs(qs, kg, mask, sink):
    s = jnp.where(mask, _dot_nt(qs, kg) * SCALE, NEG)
    m = jnp.maximum(jnp.max(s, axis=-1, keepdims=True), sink)
    p = jnp.exp(s - m)
    es = jnp.exp(sink - m)
    inv = 1.0 / (jnp.sum(p, axis=-1, keepdims=True) + es)
    return p * inv, es * inv


def _attn_specs(n_extra_q):
    q_spec = pl.BlockSpec((BLK, D), lambda n: (n, 0))
    return [q_spec] * n_extra_q + [
        pl.BlockSpec((BLK, 4 * HD), lambda n: (jnp.maximum(n - 1, 0), 0)),
        pl.BlockSpec((BLK, 4 * HD), lambda n: (n, 0)),
        pl.BlockSpec(memory_space=pltpu.SMEM),
    ]


def _attn_fwd(q, kv, sinks, layer):
    S = q.shape[0]

    def body(q_ref, kvp_ref, kvc_ref, sk_ref, o_ref):
        n = pl.program_id(0)
        ks, vs, lo, mask = _attn_setup(kvp_ref, kvc_ref, n)
        for grp in range(2):
            qs = _stack_heads(q_ref, grp, lo)
            pr, _ = _attn_probs(qs, ks[grp], mask, _sink_column(sk_ref, layer, grp))
            outs = _unstack_heads(_dot(pr.astype(MX), vs[grp]), lo)
            for i in range(4):
                j = 4 * grp + i
                o_ref[:, j * LANES:(j + 1) * LANES] = outs[i].astype(o_ref.dtype)

    return pl.pallas_call(
        body, name="attn_fwd",
        grid=(S // BLK,),
        in_specs=_attn_specs(1),
        out_specs=pl.BlockSpec((BLK, D), lambda n: (n, 0)),
        out_shape=jax.ShapeDtypeStruct((S, D), MX),
        compiler_params=_cp(),
    )(q, kv, kv, sinks)


def _attn_bwd(q, do, kv, sinks, layer):
    S = q.shape[0]

    def body(q_ref, do_ref, kvp_ref, kvc_ref, sk_ref, dq_ref, dkv_ref, dsk_ref):
        n = pl.program_id(0)

        @pl.when(n == 0)
        def _():
            dkv_ref[...] = jnp.zeros_like(dkv_ref)
            dsk_ref[...] = jnp.zeros_like(dsk_ref)

        ks, vs, lo, mask = _attn_setup(kvp_ref, kvc_ref, n)
        lane = lax.broadcasted_iota(jnp.int32, (1, LANES), 1)
        dsk = jnp.zeros((1, LANES), F32)
        tk, tv = [], []
        for grp in range(2):
            qs = _stack_heads(q_ref, grp, lo)
            dos = _stack_heads(do_ref, grp, lo)
            pr, psink = _attn_probs(qs, ks[grp], mask, _sink_column(sk_ref, layer, grp))
            dpr = _dot_nt(dos, vs[grp])
            delta = jnp.sum(pr * dpr, axis=-1, keepdims=True)
            ds = (pr * (dpr - delta) * SCALE).astype(MX)
            sd = psink * delta
            for h in range(HPG):
                dsk = dsk + jnp.where(lane == HPG * grp + h,
                                      -jnp.sum(sd[h * BLK:(h + 1) * BLK], axis=0, keepdims=True), 0.0)
            dqs = _unstack_heads(_dot(ds, ks[grp]), lo)
            for i in range(4):
                j = 4 * grp + i
                dq_ref[:, j * LANES:(j + 1) * LANES] = dqs[i].astype(dq_ref.dtype)
            dk = _dot_tn(qs, ds).T
            dv = _dot_tn(dos, pr.astype(MX)).T
            tk.append(dk + pltpu.roll(dk, HD, 1))
            tv.append(dv + pltpu.roll(dv, HD, 1))
        dsk_ref[...] += dsk
        contrib = jnp.concatenate([jnp.where(lo, tk[0], tk[1]), jnp.where(lo, tv[0], tv[1])], axis=1)

        @pl.when(n > 0)
        def _():
            rows = pl.ds(pl.multiple_of((n - 1) * BLK, BLK), 2 * BLK)
            dkv_ref[rows, :] += contrib

        @pl.when(n == 0)
        def _():
            dkv_ref[0:BLK, :] += contrib[BLK:]

    return pl.pallas_call(
        body, name="attn_bwd",
        grid=(S // BLK,),
        in_specs=_attn_specs(2),
        out_specs=[
            pl.BlockSpec((BLK, D), lambda n: (n, 0)),
            pl.BlockSpec((S, 4 * HD), lambda n: (0, 0)),
            pl.BlockSpec((1, LANES), lambda n: (0, 0)),
        ],
        out_shape=[
            jax.ShapeDtypeStruct((S, D), MX),
            jax.ShapeDtypeStruct((S, 4 * HD), F32),
            jax.ShapeDtypeStruct((1, LANES), F32),
        ],
        compiler_params=_cp(),
    )(q, do, kv, kv, sinks)


def _ew_rows(rows, cols, n_bufs=1):
    br = rows
    while br * cols * 4 * n_bufs > EW_BYTES and br % (2 * SUBLANES) == 0:
        br //= 2
    return br


def _adamw(parts, w, m, v):
    L, R, C = w.shape
    br = _ew_rows(R, C)
    npart = len(parts[0])

    def body(*refs):
        p_refs = refs[:L * npart]
        w_ref, m_ref, v_ref, g_ref, d_ref, nm_ref, nv_ref = refs[L * npart:]
        lyr = pl.program_id(0)
        for l in range(L):
            @pl.when(lyr == l)
            def _(l=l):
                g = p_refs[l * npart][...]
                for pr in p_refs[l * npart + 1:(l + 1) * npart]:
                    g = g + pr[...]
                nm = B1 * m_ref[...] + (1.0 - B1) * g
                nv = B2 * v_ref[...] + (1.0 - B2) * (g * g)
                m_hat = nm / (1.0 - B1 ** STEP)
                v_hat = nv / (1.0 - B2 ** STEP)
                g_ref[...] = g
                d_ref[...] = -LR * (m_hat / (jnp.sqrt(v_hat) + AEPS) + WD * w_ref[...])
                nm_ref[...] = nm
                nv_ref[...] = nv

    spec = pl.BlockSpec((None, br, C), lambda a, i: (a, i, 0))
    part_specs = [pl.BlockSpec((br, C), lambda a, i, l=l: (jnp.where(a == l, i, 0), 0))
                  for l in range(L) for _ in range(npart)]
    return pl.pallas_call(
        body, name="adamw",
        grid=(L, R // br),
        in_specs=part_specs + [spec] * 3,
        out_specs=[spec] * 4,
        out_shape=[jax.ShapeDtypeStruct((L, R, C), F32)] * 4,
        compiler_params=_cp(2),
    )(*[a for lp in parts for a in lp], w, m, v)


def _coords():
    return lax.axis_index("x"), lax.axis_index("y"), lax.axis_index("c")


def _other_chips(x, y):
    return [(1 - x, y), (x, 1 - y), (1 - x, 1 - y)]


def _slot(ref, axis, chip, size):
    idx = [slice(None)] * 3
    idx[axis] = pl.ds(pl.multiple_of(chip * size, size), size)
    return ref.at[tuple(idx)]


HBM_SPEC = pl.BlockSpec(memory_space=pltpu.HBM)
SEM_SPEC = pl.BlockSpec(memory_space=pltpu.SEMAPHORE)
ANY_SPEC = pl.BlockSpec(memory_space=pl.ANY)
EFFECT = pltpu.SideEffectType.DATAFLOW_SIDE_EFFECTING


def _slot_specs(shape, axis, br, lead):
    _, b, c = shape
    nrb = b // br
    first = (lambda a: a) if lead is None else (lambda a: lead)
    shard = pl.BlockSpec((None, br, c), lambda a, i, me: (first(a), i, 0))
    if axis == 1:
        slot = pl.BlockSpec((None, br, c), lambda a, i, me: (a, me[0] * nrb + i, 0))
    else:
        slot = pl.BlockSpec((None, br, c), lambda a, i, me: (a, i, me[0]))
    return shard, slot


def _shard_rows(b, c):
    br = b
    while br * c * 4 > 2 * EW_BYTES and br % (4 * SUBLANES) == 0:
        br //= 2
    return br


def _gather_place(shard, axis, me, dtype, lead=None):
    a_dim, b, c = shard.shape
    if lead is not None:
        a_dim = 1
    br = _shard_rows(b, c)
    shp = [a_dim, b, c]
    shp[axis] *= 4
    shard_spec, slot_spec = _slot_specs((a_dim, b, c), axis, br, lead)

    def body(me_ref, s_ref, o_ref):
        o_ref[...] = s_ref[...].astype(o_ref.dtype)

    return pl.pallas_call(
        body, name="gather_place",
        grid_spec=pltpu.PrefetchScalarGridSpec(
            num_scalar_prefetch=1, grid=(a_dim, b // br), in_specs=[shard_spec], out_specs=slot_spec),
        out_shape=jax.ShapeDtypeStruct(tuple(shp), dtype),
        compiler_params=_cp(2),
    )(me, shard)


def _sum_landed(grad, landed, axis, me):
    a_dim, b, c = landed.shape[1:]
    br = _shard_rows(b, c)
    shard_spec, slot_spec = _slot_specs((a_dim, b, c), axis, br, None)

    def body(me_ref, own_ref, r_ref, o_ref):
        o_ref[...] = ((own_ref[...].astype(F32) + r_ref[0].astype(F32)) + r_ref[1].astype(F32)) + r_ref[2].astype(F32)

    return pl.pallas_call(
        body, name="sum_landed",
        grid_spec=pltpu.PrefetchScalarGridSpec(
            num_scalar_prefetch=1, grid=(a_dim, b // br),
            in_specs=[slot_spec, pl.BlockSpec((3, None, br, c), lambda a, i, me: (0, a, i, 0))],
            out_specs=shard_spec),
        out_shape=jax.ShapeDtypeStruct((a_dim, b, c), F32),
        compiler_params=_cp(2),
    )(me, grad, landed)


def _copies(refs, plan, send, recv):
    x, y, c = _coords()
    me = 2 * x + y
    out, t = [], 0
    while plan(refs, me, t, 0, me) is not None:
        for k, (px, py) in enumerate(_other_chips(x, y)):
            sv, dv = plan(refs, me, t, k, 2 * px + py)
            out.append(pltpu.make_async_remote_copy(
                src_ref=sv, dst_ref=dv, send_sem=send.at[3 * t + k], recv_sem=recv.at[3 * t + k],
                device_id=(px, py, c), device_id_type=MESH))
        t += 1
    return out


def _push_start(name, bufs, n_ex, plan):
    nb = len(bufs)

    def body(*refs):
        send, recv, token = refs[nb], refs[nb + 1], refs[-1]
        for cp in _copies(refs[:nb], plan, send, recv):
            cp.start()
        token[...] = jnp.zeros_like(token)

    res = pl.pallas_call(
        body, name=name,
        in_specs=[HBM_SPEC] * nb,
        out_specs=[SEM_SPEC, SEM_SPEC] + [HBM_SPEC] * nb + [pl.BlockSpec(memory_space=pltpu.VMEM)],
        out_shape=[pltpu.SemaphoreType.DMA((3 * n_ex,)), pltpu.SemaphoreType.DMA((3 * n_ex,))]
        + [pltpu.HBM(a.shape, a.dtype) for a in bufs] + [jax.ShapeDtypeStruct((SUBLANES, LANES), F32)],
        input_output_aliases={i: 2 + i for i in range(nb)},
        compiler_params=pltpu.CompilerParams(has_side_effects=EFFECT),
    )(*[pltpu.with_memory_space_constraint(a, pltpu.HBM) for a in bufs])
    return res[0], res[1], res[2:2 + nb], res[-1]


def _push_wait(name, send, recv, bufs, plan, after):
    nb = len(bufs)

    def body(*refs):
        for cp in _copies(refs[:nb], plan, refs[nb], refs[nb + 1]):
            cp.wait_send()
            cp.wait_recv()

    return pl.pallas_call(
        body, name=name,
        in_specs=[HBM_SPEC] * nb + [SEM_SPEC, SEM_SPEC, ANY_SPEC],
        out_specs=[HBM_SPEC] * nb,
        out_shape=[pltpu.HBM(a.shape, a.dtype) for a in bufs],
        input_output_aliases={i: i for i in range(nb)},
        compiler_params=pltpu.CompilerParams(has_side_effects=EFFECT),
    )(*bufs, send, recv, after)


def _gather_plan(axes):
    def plan(refs, me, t, k, peer):
        if t >= len(axes):
            return None
        size = refs[t].shape[axes[t]] // 4
        mine = _slot(refs[t], axes[t], me, size)
        return mine, mine
    return plan


def _scatter_plan(axes):
    n = len(axes)

    def plan(refs, me, t, k, peer):
        if t >= n:
            return None
        size = refs[t].shape[axes[t]] // 4
        return _slot(refs[t], axes[t], peer, size), refs[n + t].at[k]
    return plan


def _swap_with_sibling(arrs):
    na = len(arrs)

    def body(*refs):
        ins, outs = refs[:na], refs[na:2 * na]
        send, recv = refs[2 * na:]
        x, y, c = _coords()
        cps = []
        for t in range(na):
            rc = pltpu.make_async_remote_copy(
                src_ref=ins[t], dst_ref=outs[t], send_sem=send.at[t], recv_sem=recv.at[t],
                device_id=(x, y, 1 - c), device_id_type=MESH)
            rc.start()
            cps.append(rc)
        for rc in cps:
            rc.wait()

    any_spec = pl.BlockSpec(memory_space=pl.ANY)
    return pl.pallas_call(
        body, name="swap_sibling",
        in_specs=[any_spec] * na,
        out_specs=[any_spec] * na,
        out_shape=[jax.ShapeDtypeStruct(a.shape, a.dtype) for a in arrs],
        scratch_shapes=[pltpu.SemaphoreType.DMA((na,)), pltpu.SemaphoreType.DMA((na,))],
    )(*arrs)


def _all_reduce_small(v):
    R = v.shape[0]

    def body(v_ref, o_ref, buf, send, recv, local):
        x, y, c = _coords()
        me = 4 * x + 2 * y + c
        cp = pltpu.make_async_copy(v_ref, buf.at[me], local)
        cp.start()
        pushes = []
        for k in range(1, 8):
            peer = (x ^ (k >> 2), y ^ ((k >> 1) & 1), c ^ (k & 1))
            rc = pltpu.make_async_remote_copy(
                src_ref=v_ref, dst_ref=buf.at[me], send_sem=send.at[k - 1], recv_sem=recv.at[k - 1],
                device_id=peer, device_id_type=MESH)
            rc.start()
            pushes.append(rc)
        for k in range(1, 8):
            px, py, pc = x ^ (k >> 2), y ^ ((k >> 1) & 1), c ^ (k & 1)
            pltpu.make_async_remote_copy(
                src_ref=v_ref, dst_ref=buf.at[4 * px + 2 * py + pc], send_sem=send.at[k - 1],
                recv_sem=recv.at[k - 1], device_id=(px, py, pc), device_id_type=MESH).wait_recv()
        for rc in pushes:
            rc.wait_send()
        cp.wait()
        tot = buf[0]
        for k in range(1, 8):
            tot = tot + buf[k]
        o_ref[...] = tot

    vm = pl.BlockSpec(memory_space=pltpu.VMEM)
    return pl.pallas_call(
        body, name="all_reduce_small",
        in_specs=[vm],
        out_specs=vm,
        out_shape=jax.ShapeDtypeStruct((R, LANES), F32),
        scratch_shapes=[
            pltpu.VMEM((8, R, LANES), F32),
            pltpu.SemaphoreType.DMA((7,)),
            pltpu.SemaphoreType.DMA((7,)),
            pltpu.SemaphoreType.DMA,
        ],
        compiler_params=pltpu.CompilerParams(vmem_limit_bytes=VMEM_LIMIT),
    )(v)


def _pack(arrs):
    flat = []
    for a in arrs:
        f = a.reshape(-1).astype(F32)
        flat.append(jnp.pad(f, (0, (-f.shape[0]) % LANES)))
    v = jnp.concatenate(flat)
    v = jnp.pad(v, (0, (-v.shape[0]) % (SUBLANES * LANES)))
    return v.reshape(-1, LANES)


def _unpack(v, shapes):
    flat = v.reshape(-1)
    out, off = [], 0
    for shp in shapes:
        n = 1
        for d in shp:
            n *= d
        out.append(flat[off:off + n].reshape(shp))
        off += n + (-n) % LANES
    return out


def _local_step(x, tgt, sp, weights_for, on_grads):
    n1, n2 = sp["norm1_g"], sp["norm2_g"]
    w = dict(weights_for(0, x))
    saved = []
    xs = x
    kv = None
    for l in range(DEPTH):
        x_in = xs
        if l == N_A:
            w.update(weights_for(2, x_in))
            kv = _rms_linear(x_in, sp["kv_norm_g"], w["w_kv"], sp["b_kv"], 0, 0, "kv_proj")
        if l == N_A + 1:
            w.update(weights_for(3, x_in))
        if l < N_A:
            xa = _pool_fwd(x_in, n1, w["pool_w"], w["pool_scale"], l)
            q = o = None
        else:
            j = l - N_A
            q = _rms_linear(x_in, n1, w["w_q", j], sp["b_q"], l, j, "q_proj")
            o = _attn_fwd(q, kv, sp["sinks"], j)
            xa = _linear_res(o, w["w_o", j], sp["b_o"], x_in, j)
        if l == 1:
            w.update(weights_for(1, xa))
        xs, u, uc = _ffn_fwd(xa, n2, w["ffn_up", l], w["ffn_down", l], w["ffn_conv_w"], sp["ffn_conv_b"], l)
        saved.append((x_in, xa, u, uc, q, o))

    dx, d_final_g, loss = _loss_head(xs, sp["final_g"], tgt)

    g = {k: [None] * DEPTH for k in ("norm1_g", "norm2_g", "ffn_conv_w", "ffn_conv_b")}
    for k in ("pool_scale", "b_q", "sinks", "b_o"):
        g[k] = [None] * N_A
    g["final_g"] = d_final_g
    dkvs = []
    pending = {}
    for l in reversed(range(DEPTH)):
        x_in, xa, u, uc, q, o = saved[l]
        dxa, du, a, hb, g["norm2_g"][l], g["ffn_conv_w"][l], g["ffn_conv_b"][l] = _ffn_bwd(
            xa, dx, u, uc, n2, w["ffn_up", l], w["ffn_down", l], w["ffn_conv_w"], l)
        pending["ffn_up", l] = _tn_matmul(hb, du, MX, "d_ffn_up", a_is_transposed=True)
        if l == 0:
            n1 = n1 + on_grads(DEPTH + 1, pending)
            pending = {}
        pending["ffn_down", l] = _tn_matmul(a, dx, MX, "d_ffn_down", a_is_transposed=True)
        zero = on_grads(DEPTH - 1 - l, pending)
        pending = {}
        n1, n2 = n1 + zero, n2 + zero
        if l < N_A:
            dx, d_pw, g["pool_scale"][l], g["norm1_g"][l] = _pool_bwd(
                x_in, dxa, n1, w["pool_w"], w["pool_scale"], l)
            pending["pool_w", l] = d_pw.astype(MX)
        else:
            j = l - N_A
            d_o, g["b_o"][j] = _linear_nt(dxa, w["w_o", j])
            pending["w_o", j] = _tn_matmul(o, dxa, MX, "d_w_o")
            dq, dkv, g["sinks"][j] = _attn_bwd(q, d_o, kv, sp["sinks"], j)
            dkvs.append(dkv)
            dx, g["norm1_g"][l], g["b_q"][j], hq, dqb = _rms_linear_bwd(
                x_in, n1, [dq], w["w_q", j], dxa, l, "q_proj_bwd")
            pending["w_q", j] = _tn_matmul(hq, dqb, MX, "d_w_q")
        if l == N_A:
            dx, g["kv_norm_g"], g["b_kv"], hk, dkvb = _rms_linear_bwd(
                x_in, sp["kv_norm_g"], dkvs, w["w_kv"], dx, 0, "kv_proj_bwd")
            pending["w_kv", 0] = _tn_matmul(hk, dkvb, MX, "d_w_kv")
    on_grads(DEPTH, pending)
    return loss, dx, g


SMALL = ("norm1_g", "norm2_g", "kv_norm_g", "b_kv", "b_q", "sinks", "b_o", "ffn_conv_b", "final_g")
SMALL_SHARDED = ("pool_scale", "ffn_conv_w")
BIG = ("pool_w", "w_kv", "w_q", "w_o", "ffn_up", "ffn_down")
ORDER = ("norm1_g", "norm2_g", "pool_w", "pool_scale", "kv_norm_g", "w_kv", "b_kv", "w_q", "b_q", "sinks",
         "w_o", "b_o", "ffn_up", "ffn_conv_w", "ffn_conv_b", "ffn_down", "final_g")


def _as3d(a):
    return a.reshape((-1,) + a.shape[-2:])


def kernel(x, norm1_g, norm2_g, pool_w, pool_scale, kv_norm_g, w_kv, b_kv, w_q, b_q, sinks, w_o, b_o, ffn_up, ffn_conv_w, ffn_conv_b, ffn_down, final_g, loss_target, m_norm1_g, m_norm2_g, m_pool_w, m_pool_scale, m_kv_norm_g, m_w_kv, m_b_kv, m_w_q, m_b_q, m_sinks, m_w_o, m_b_o, m_ffn_up, m_ffn_conv_w, m_ffn_conv_b, m_ffn_down, m_final_g, v_norm1_g, v_norm2_g, v_pool_w, v_pool_scale, v_kv_norm_g, v_w_kv, v_b_kv, v_w_q, v_b_q, v_sinks, v_w_o, v_b_o, v_ffn_up, v_ffn_conv_w, v_ffn_conv_b, v_ffn_down, v_final_g):
    W = dict(norm1_g=norm1_g, norm2_g=norm2_g, pool_w=pool_w, pool_scale=pool_scale, kv_norm_g=kv_norm_g,
             w_kv=w_kv, b_kv=b_kv, w_q=w_q, b_q=b_q, sinks=sinks, w_o=w_o, b_o=b_o, ffn_up=ffn_up,
             ffn_conv_w=ffn_conv_w, ffn_conv_b=ffn_conv_b, ffn_down=ffn_down, final_g=final_g)
    M = dict(norm1_g=m_norm1_g, norm2_g=m_norm2_g, pool_w=m_pool_w, pool_scale=m_pool_scale,
             kv_norm_g=m_kv_norm_g, w_kv=m_w_kv, b_kv=m_b_kv, w_q=m_w_q, b_q=m_b_q, sinks=m_sinks, w_o=m_w_o,
             b_o=m_b_o, ffn_up=m_ffn_up, ffn_conv_w=m_ffn_conv_w, ffn_conv_b=m_ffn_conv_b, ffn_down=m_ffn_down,
             final_g=m_final_g)
    V = dict(norm1_g=v_norm1_g, norm2_g=v_norm2_g, pool_w=v_pool_w, pool_scale=v_pool_scale,
             kv_norm_g=v_kv_norm_g, w_kv=v_w_kv, b_kv=v_b_kv, w_q=v_w_q, b_q=v_b_q, sinks=v_sinks, w_o=v_w_o,
             b_o=v_b_o, ffn_up=v_ffn_up, ffn_conv_w=v_ffn_conv_w, ffn_conv_b=v_ffn_conv_b, ffn_down=v_ffn_down,
             final_g=v_final_g)
    S = x.shape[1]
    chip = 2 * lax.axis_index("x") + lax.axis_index("y")

    gather_axis = dict(pool_w=1, w_kv=1, w_q=1, w_o=1, ffn_up=2, ffn_down=1, pool_scale=2, ffn_conv_w=2)
    me = chip.reshape(1).astype(jnp.int32)
    axis_of = lambda key: gather_axis[key if isinstance(key, str) else key[0]]

    def placed(key, dtype):
        if isinstance(key, str):
            return _gather_place(_as3d(W[key]), axis_of(key), me, dtype)
        return _gather_place(W[key[0]], axis_of(key), me, dtype, lead=key[1])

    stages = [
        ["pool_w", "pool_scale", "ffn_conv_w", ("ffn_up", 0), ("ffn_down", 0)],
        [("ffn_up", 1), ("ffn_down", 1)],
        ["w_kv", ("w_q", 0), ("w_o", 0), ("ffn_up", 2), ("ffn_down", 2)],
        [("w_q", 1), ("w_o", 1), ("ffn_up", 3), ("ffn_down", 3)],
    ]
    gathers, zero = [], 0.0
    for si, keys in enumerate(stages):
        axes = [axis_of(k) for k in keys]
        bufs = [placed(k, F32 if k in SMALL_SHARDED else MX) for k in keys]
        send, recv, bufs, token = _push_start(f"gather_start_{si}", bufs, len(keys), _gather_plan(axes))
        gathers.append((keys, axes, send, recv, bufs))
        zero = zero + token[0, 0]

    def weights_for(stage, after):
        keys, axes, send, recv, bufs = gathers[stage]
        out = dict(zip(keys, _push_wait(f"gather_wait_{stage}", send, recv, bufs, _gather_plan(axes), after)))
        if stage == 0:
            out["pool_w"] = out["pool_w"].reshape(N_A, 4, GC, GC)
            out["pool_scale"] = out["pool_scale"].reshape(N_A, 1, D)
        return out

    scatters = []

    def on_grads(stage, grads):
        keys = list(grads)
        axes = [axis_of(k) for k in keys]
        arrs = [_as3d(grads[k]) for k in keys]
        lands = []
        for a, ax in zip(arrs, axes):
            shp = list(a.shape)
            shp[ax] //= 4
            lands.append(lax.empty((3,) + tuple(shp), a.dtype))
        send, recv, bufs, token = _push_start(f"scatter_start_{stage}", arrs + lands, len(keys), _scatter_plan(axes))
        scatters.append((stage, keys, axes, send, recv, bufs))
        return token[0, 0]

    sp = dict(
        norm1_g=norm1_g.reshape(DEPTH, 1, D) + zero, norm2_g=norm2_g.reshape(DEPTH, 1, D),
        kv_norm_g=kv_norm_g.reshape(1, 1, D), b_kv=b_kv.reshape(1, 1, 4 * HD), b_q=b_q.reshape(N_B, 1, D),
        sinks=sinks, b_o=b_o.reshape(N_B, 1, D), ffn_conv_b=ffn_conv_b.reshape(DEPTH, 1, F2),
        final_g=final_g.reshape(1, D))

    x2d = x.reshape(S, D)
    loss, grad_x, g = _local_step(x2d, loss_target.reshape(S, D), sp, weights_for, on_grads)

    small_full = dict(
        norm1_g=jnp.stack(g["norm1_g"]), norm2_g=jnp.stack(g["norm2_g"]), kv_norm_g=g["kv_norm_g"],
        b_kv=g["b_kv"], b_q=jnp.stack(g["b_q"]), sinks=jnp.stack([s[0, :NH] for s in g["sinks"]]),
        b_o=jnp.stack(g["b_o"]), ffn_conv_b=jnp.stack(g["ffn_conv_b"]), final_g=g["final_g"],
        pool_scale=jnp.stack(g["pool_scale"]), ffn_conv_w=jnp.stack(g["ffn_conv_w"]))
    small_names = SMALL + SMALL_SHARDED
    small_shapes = [tuple(W[k].shape) for k in SMALL] + [(N_A, D), (DEPTH, 3, F2)]
    packed = _pack([small_full[k] for k in small_names] + [loss])
    red = _unpack(_all_reduce_small(packed), small_shapes + [(1, LANES)])
    red_g = dict(zip(small_names, red[:-1]))
    loss_out = red[-1][0, 0]
    red_g["pool_scale"] = lax.dynamic_slice_in_dim(red_g["pool_scale"], chip * (D // 4), D // 4, axis=1)
    red_g["ffn_conv_w"] = lax.dynamic_slice_in_dim(red_g["ffn_conv_w"], chip * (F2 // 4), F2 // 4, axis=2)
    small_w_shapes = [tuple(W[k].shape) for k in small_names]
    pk = lambda d: _pack([d[k] for k in small_names])[None]
    res = _adamw([[_pack([red_g[k] for k in small_names])]], pk(W), pk(M), pk(V))
    out_g, out_d, out_m, out_v = [dict(zip(small_names, _unpack(r, small_w_shapes))) for r in res]

    pkeys, partial = [], []
    for stage, keys, axes, send, recv, bufs in scatters:
        bufs = _push_wait(f"scatter_wait_{stage}", send, recv, bufs, _scatter_plan(axes), grad_x)
        n = len(keys)
        for k, ax, grad, landed in zip(keys, axes, bufs[:n], bufs[n:]):
            pkeys.append(k)
            p_sum = _sum_landed(grad, landed, ax, me)
            partial.append(p_sum.reshape(-1, p_sum.shape[-1]))
    mine = dict(zip(pkeys, partial))
    theirs = dict(zip(pkeys, _swap_with_sibling(partial)))
    for k in BIG:
        n_l = len([pk_ for pk_ in pkeys if pk_[0] == k])
        shp = W[k].shape
        rows, cols = mine[k, 0].shape
        three_d = lambda a: a.reshape(n_l, rows, cols)
        res = _adamw([[mine[k, l], theirs[k, l]] for l in range(n_l)], three_d(W[k]), three_d(M[k]), three_d(V[k]))
        out_g[k], out_d[k], out_m[k], out_v[k] = [r.reshape(shp) for r in res]

    return (loss_out, grad_x.reshape(x.shape), *[out_g[k] for k in ORDER], *[out_d[k] for k in ORDER],
            *[out_m[k] for k in ORDER], *[out_v[k] for k in ORDER])
```

```python
import functools

import jax
import jax.numpy as jnp
from jax import lax
from jax.experimental import pallas as pl
from jax.experimental.pallas import tpu as pltpu

D = 1024
DEPTH = 4
N_A = 2
N_B = 2
WINS = (2, 4, 8, 16)
GC = 256
HD = 64
NH = 16
BLK = 128
F = 2816
F2 = 2 * F
EPS = 1e-5
SCALE = HD ** -0.5
NEG = -1e30
HALO = 16
TN = 256
UP_GROUP = 3
LANES = 128
SUBLANES = 8
VMEM_LIMIT = 56 * 1024 * 1024
FFN_VMEM_LIMIT = 60 * 1024 * 1024
ACC_BYTES = 6 * 1024 * 1024
EW_BYTES = 1024 * 1024

LR, B1, B2, AEPS, WD, STEP = 0.001, 0.9, 0.999, 1e-08, 0.01, 10

MX = jnp.bfloat16
F32 = jnp.float32
MESH = pl.DeviceIdType.MESH


def _cp(n_axes=1, vmem=VMEM_LIMIT):
    return pltpu.CompilerParams(dimension_semantics=("arbitrary",) * n_axes, vmem_limit_bytes=vmem)


def _dot(a, b):
    return jnp.dot(a, b, preferred_element_type=F32)


def _dot_nt(a, b):
    return lax.dot_general(a, b, (((1,), (1,)), ((), ())), preferred_element_type=F32)


def _dot_tn(a, b):
    return lax.dot_general(a, b, (((0,), (0,)), ((), ())), preferred_element_type=F32)


def _rms_fwd(x, g):
    r = lax.rsqrt(jnp.mean(x * x, axis=-1, keepdims=True) + EPS)
    xh = x * r
    return xh * g, xh, r


def _rms_bwd(dh, xh, r, g):
    dxh = dh * g
    return r * (dxh - xh * jnp.mean(dxh * xh, axis=-1, keepdims=True))


def _row_tile(s, want):
    return min(s, want)


def _pool_pm(e, h, row, tm):
    out = []
    for gi, win in enumerate(WINS):
        cols = slice(gi * GC, (gi + 1) * GC)
        s = e[:, cols]
        sh = 1
        while sh < win:
            s = s + pltpu.roll(s, sh, 0)
            sh *= 2
        inv = 1.0 / jnp.minimum(row + 1, win).astype(F32)
        out.append(s[HALO:] * inv - h[:, cols])
    return out


def _pool_fwd(x, g, pw, ps, layer):
    S = x.shape[0]
    tm = _row_tile(S, 512)
    hb = tm // HALO

    def body(x_ref, xh_ref, g_ref, pw_ref, ps_ref, o_ref):
        i = pl.program_id(0)
        x = x_ref[...]
        gg = g_ref[...]
        h, _, _ = _rms_fwd(x, gg)
        hh, _, _ = _rms_fwd(xh_ref[...], gg)
        hh = jnp.where(i > 0, hh, 0.0)
        e = jnp.concatenate([hh, h], axis=0)
        row = i * tm + lax.broadcasted_iota(jnp.int32, (tm, 1), 0)
        pm = _pool_pm(e, h, row, tm)
        for gi in range(len(WINS)):
            cols = slice(gi * GC, (gi + 1) * GC)
            z = _dot(pm[gi].astype(MX), pw_ref[gi])
            o_ref[:, cols] = x[:, cols] + z * ps_ref[:, cols]

    return pl.pallas_call(
        body, name="pool_fwd",
        grid=(S // tm,),
        in_specs=[
            pl.BlockSpec((tm, D), lambda i: (i, 0)),
            pl.BlockSpec((HALO, D), lambda i: (jnp.maximum(i * hb - 1, 0), 0)),
            pl.BlockSpec((None, 1, D), lambda i: (layer, 0, 0)),
            pl.BlockSpec((None, 4, GC, GC), lambda i: (layer, 0, 0, 0)),
            pl.BlockSpec((None, 1, D), lambda i: (layer, 0, 0)),
        ],
        out_specs=pl.BlockSpec((tm, D), lambda i: (i, 0)),
        out_shape=jax.ShapeDtypeStruct((S, D), F32),
        compiler_params=_cp(),
    )(x, x, g, pw, ps)


def _pool_bwd(x, dy, g, pw, ps, layer):
    S = x.shape[0]
    tm = _row_tile(S, 256)
    hb = tm // HALO
    n_i = S // tm
    n_h = S // HALO

    def body(x_ref, xh_ref, dy_ref, dyn_ref, g_ref, pw_ref, ps_ref, dx_ref, dpw_ref, dps_ref, dg_ref):
        i = pl.program_id(0)

        @pl.when(i == 0)
        def _():
            dpw_ref[...] = jnp.zeros_like(dpw_ref)
            dps_ref[...] = jnp.zeros_like(dps_ref)
            dg_ref[...] = jnp.zeros_like(dg_ref)

        x = x_ref[...]
        gg = g_ref[...]
        ps = ps_ref[...]
        h, xh, r = _rms_fwd(x, gg)
        hh, _, _ = _rms_fwd(xh_ref[...], gg)
        hh = jnp.where(i > 0, hh, 0.0)
        e = jnp.concatenate([hh, h], axis=0)
        row = i * tm + lax.broadcasted_iota(jnp.int32, (tm, 1), 0)
        rown = (i + 1) * tm + lax.broadcasted_iota(jnp.int32, (HALO, 1), 0)
        pm = _pool_pm(e, h, row, tm)
        dy = dy_ref[...]
        dz = dy * ps
        dzn = jnp.where(i < n_i - 1, dyn_ref[...] * ps, 0.0)
        parts = []
        for gi, win in enumerate(WINS):
            cols = slice(gi * GC, (gi + 1) * GC)
            w = pw_ref[gi]
            pmb = pm[gi].astype(MX)
            z = _dot(pmb, w)
            dps_ref[:, cols] += jnp.sum(dy[:, cols] * z, axis=0, keepdims=True)
            dzb = dz[:, cols].astype(MX)
            dpw_ref[gi] += _dot_tn(pmb, dzb)
            dpm = _dot_nt(dzb, w)
            dpmn = _dot_nt(dzn[:, cols].astype(MX), w)
            q = dpm * (1.0 / jnp.minimum(row + 1, win).astype(F32))
            qn = dpmn * (1.0 / jnp.minimum(rown + 1, win).astype(F32))
            s = jnp.concatenate([q, qn], axis=0)
            sh = 1
            while sh < win:
                s = s + pltpu.roll(s, tm + HALO - sh, 0)
                sh *= 2
            parts.append(s[:tm] - dpm)
        dh = jnp.concatenate(parts, axis=1)
        dg_ref[...] += jnp.sum(dh * xh, axis=0, keepdims=True)
        dx_ref[...] = dy + _rms_bwd(dh, xh, r, gg)

    return pl.pallas_call(
        body, name="pool_bwd",
        grid=(n_i,),
        in_specs=[
            pl.BlockSpec((tm, D), lambda i: (i, 0)),
            pl.BlockSpec((HALO, D), lambda i: (jnp.maximum(i * hb - 1, 0), 0)),
            pl.BlockSpec((tm, D), lambda i: (i, 0)),
            pl.BlockSpec((HALO, D), lambda i: (jnp.minimum((i + 1) * hb, n_h - 1), 0)),
            pl.BlockSpec((None, 1, D), lambda i: (layer, 0, 0)),
            pl.BlockSpec((None, 4, GC, GC), lambda i: (layer, 0, 0, 0)),
            pl.BlockSpec((None, 1, D), lambda i: (layer, 0, 0)),
        ],
        out_specs=[
            pl.BlockSpec((tm, D), lambda i: (i, 0)),
            pl.BlockSpec((4, GC, GC), lambda i: (0, 0, 0)),
            pl.BlockSpec((1, D), lambda i: (0, 0)),
            pl.BlockSpec((1, D), lambda i: (0, 0)),
        ],
        out_shape=[
            jax.ShapeDtypeStruct((S, D), F32),
            jax.ShapeDtypeStruct((4, GC, GC), F32),
            jax.ShapeDtypeStruct((1, D), F32),
            jax.ShapeDtypeStruct((1, D), F32),
        ],
        compiler_params=_cp(),
    )(x, x, dy, dy, g, pw, ps)


N_STAGE = 4


def _rows_before(slot, u, prev8):
    tm = u.shape[0]
    m1, m2 = [], []
    for c in range(TN // LANES):
        lanes = slice(c * LANES, (c + 1) * LANES)
        slot[c, 0:SUBLANES, :] = prev8[:, lanes]
        slot[c, SUBLANES:SUBLANES + tm, :] = u[:, lanes]
        m1.append(slot[c, pl.ds(SUBLANES - 1, tm), :])
        m2.append(slot[c, pl.ds(SUBLANES - 2, tm), :])
    return jnp.concatenate(m1, axis=1), jnp.concatenate(m2, axis=1)


def _rows_after(slot, d, next8):
    tm = d.shape[0]
    p1, p2 = [], []
    for c in range(TN // LANES):
        lanes = slice(c * LANES, (c + 1) * LANES)
        slot[c, 0:tm, :] = d[:, lanes]
        slot[c, tm:tm + SUBLANES, :] = next8[:, lanes]
        p1.append(slot[c, pl.ds(1, tm), :])
        p2.append(slot[c, pl.ds(2, tm), :])
    return jnp.concatenate(p1, axis=1), jnp.concatenate(p2, axis=1)


def _conv(slot, u, prev8, cw):
    um1, um2 = _rows_before(slot, u, prev8)
    return cw[0:1] * um2 + cw[1:2] * um1 + cw[2:3] * u


def _ffn_fwd(x, g, wup, wdn, cw, cb, layer):
    S = x.shape[0]
    tm = _row_tile(S, 512)

    def body(x_ref, g_ref, wup_hbm, wdn_hbm, cw_ref, cb_ref, o_ref, u_ref, uc_ref, wup_v, wdn_v, carry, act, stage):
        i = pl.program_id(0)

        @pl.when(i == 0)
        def _():
            pltpu.sync_copy(wup_hbm.at[0], wup_v)
            pltpu.sync_copy(wdn_hbm.at[0], wdn_v)
            carry[...] = jnp.zeros_like(carry)

        x = x_ref[...]
        h, _, _ = _rms_fwd(x, g_ref[...])
        hb = h.astype(MX)
        for j in range(F // TN):
            cg = slice(j * TN, (j + 1) * TN)
            cv = slice(F + j * TN, F + (j + 1) * TN)
            ug = _dot(hb, wup_v[:, cg])
            uv = _dot(hb, wup_v[:, cv])
            u_ref[:, cg] = ug.astype(u_ref.dtype)
            u_ref[:, cv] = uv.astype(u_ref.dtype)
            gt = _conv(stage.at[2 * (j % 2)], ug, carry[:, cg], cw_ref[:, cg])
            vl = _conv(stage.at[2 * (j % 2) + 1], uv, carry[:, cv], cw_ref[:, cv])
            carry[:, cg] = ug[tm - SUBLANES:]
            carry[:, cv] = uv[tm - SUBLANES:]
            gt = gt + cb_ref[:, cg]
            vl = vl + cb_ref[:, cv]
            uc_ref[:, cg] = gt.astype(uc_ref.dtype)
            uc_ref[:, cv] = vl.astype(uc_ref.dtype)
            act[:, cg] = (gt * jax.nn.sigmoid(gt) * vl).astype(act.dtype)
        o_ref[...] = x + _dot(act[...], wdn_v[...])

    return pl.pallas_call(
        body, name="ffn_fwd",
        grid=(S // tm,),
        in_specs=[
            pl.BlockSpec((tm, D), lambda i: (i, 0)),
            pl.BlockSpec((None, 1, D), lambda i: (layer, 0, 0)),
            pl.BlockSpec(memory_space=pl.ANY),
            pl.BlockSpec(memory_space=pl.ANY),
            pl.BlockSpec((None, 3, F2), lambda i: (layer, 0, 0)),
            pl.BlockSpec((None, 1, F2), lambda i: (layer, 0, 0)),
        ],
        out_specs=[
            pl.BlockSpec((tm, D), lambda i: (i, 0)),
            pl.BlockSpec((tm, F2), lambda i: (i, 0)),
            pl.BlockSpec((tm, F2), lambda i: (i, 0)),
        ],
        out_shape=[
            jax.ShapeDtypeStruct((S, D), F32),
            jax.ShapeDtypeStruct((S, F2), MX),
            jax.ShapeDtypeStruct((S, F2), MX),
        ],
        scratch_shapes=[
            pltpu.VMEM((D, F2), MX),
            pltpu.VMEM((F, D), MX),
            pltpu.VMEM((SUBLANES, F2), F32),
            pltpu.VMEM((tm, F), MX),
            pltpu.VMEM((N_STAGE, TN // LANES, tm + SUBLANES, LANES), F32),
        ],
        compiler_params=_cp(vmem=FFN_VMEM_LIMIT),
    )(x, g, wup, wdn, cw, cb)


def _ffn_bwd(x, dy, u, uc, g, wup, wdn, cw, layer):
    S = x.shape[0]
    tm = _row_tile(S, 256)
    n_i = S // tm

    def body(x_ref, dy_ref, u_ref, uc_ref, g_ref, wup_hbm, wdn_hbm, cw_ref,
             dx_ref, du_ref, a_ref, h_ref, dg_ref, dcw_ref, dcb_ref, wup_v, wdn_v, carry, stage):
        i = pl.program_id(0)

        @pl.when(i == 0)
        def _():
            pltpu.sync_copy(wup_hbm.at[0], wup_v)
            pltpu.sync_copy(wdn_hbm.at[0], wdn_v)
            carry[...] = jnp.zeros_like(carry)
            dg_ref[...] = jnp.zeros_like(dg_ref)
            dcw_ref[...] = jnp.zeros_like(dcw_ref)
            dcb_ref[...] = jnp.zeros_like(dcb_ref)

        x = x_ref[...]
        gg = g_ref[...]
        h, xh, r = _rms_fwd(x, gg)
        h_ref[...] = h.T.astype(h_ref.dtype)
        dy = dy_ref[...]
        dyb = dy.astype(MX)
        dh, dus = None, ([], [])
        for j in range(F // TN):
            cg = slice(j * TN, (j + 1) * TN)
            cv = slice(F + j * TN, F + (j + 1) * TN)
            gt = uc_ref[:, cg].astype(F32)
            vl = uc_ref[:, cv].astype(F32)
            sg = jax.nn.sigmoid(gt)
            sil = gt * sg
            a_ref[cg, :] = (sil * vl).T.astype(a_ref.dtype)
            da = _dot_nt(dyb, wdn_v[cg, :])
            dvl = da * sil
            dgt = (da * vl) * (sg + sil * (1.0 - sg))
            for cc, dd in ((cg, dgt), (cv, dvl)):
                dp1, dp2 = _rows_after(stage.at[2 * (j % 2) + (cc is cv)], dd, carry[:, cc])
                carry[:, cc] = dd[0:SUBLANES]
                uu = u_ref[:, cc].astype(F32)
                dcb_ref[:, cc] += jnp.sum(dd, axis=0, keepdims=True)
                dcw_ref[0:1, cc] += jnp.sum(dp2 * uu, axis=0, keepdims=True)
                dcw_ref[1:2, cc] += jnp.sum(dp1 * uu, axis=0, keepdims=True)
                dcw_ref[2:3, cc] += jnp.sum(dd * uu, axis=0, keepdims=True)
                cwc = cw_ref[:, cc]
                duu = (cwc[2:3] * dd + cwc[1:2] * dp1 + cwc[0:1] * dp2).astype(MX)
                du_ref[:, cc] = duu
                dus[cc is cv].append(duu)
            if len(dus[0]) == UP_GROUP or j == F // TN - 1:
                first = j + 1 - len(dus[0])
                for side, base in ((0, 0), (1, F)):
                    cols = slice(base + first * TN, base + (j + 1) * TN)
                    part = _dot_nt(jnp.concatenate(dus[side], axis=1), wup_v[:, cols])
                    dh = part if dh is None else dh + part
                dus = ([], [])
        dg_ref[...] += jnp.sum(dh * xh, axis=0, keepdims=True)
        dx_ref[...] = dy + _rms_bwd(dh, xh, r, gg)

    rev = lambda i: (n_i - 1 - i, 0)
    return pl.pallas_call(
        body, name="ffn_bwd",
        grid=(n_i,),
        in_specs=[
            pl.BlockSpec((tm, D), rev),
            pl.BlockSpec((tm, D), rev),
            pl.BlockSpec((tm, F2), rev),
            pl.BlockSpec((tm, F2), rev),
            pl.BlockSpec((None, 1, D), lambda i: (layer, 0, 0)),
            pl.BlockSpec(memory_space=pl.ANY),
            pl.BlockSpec(memory_space=pl.ANY),
            pl.BlockSpec((None, 3, F2), lambda i: (layer, 0, 0)),
        ],
        out_specs=[
            pl.BlockSpec((tm, D), rev),
            pl.BlockSpec((tm, F2), rev),
            pl.BlockSpec((F, tm), lambda i: (0, n_i - 1 - i)),
            pl.BlockSpec((D, tm), lambda i: (0, n_i - 1 - i)),
            pl.BlockSpec((1, D), lambda i: (0, 0)),
            pl.BlockSpec((3, F2), lambda i: (0, 0)),
            pl.BlockSpec((1, F2), lambda i: (0, 0)),
        ],
        out_shape=[
            jax.ShapeDtypeStruct((S, D), F32),
            jax.ShapeDtypeStruct((S, F2), MX),
            jax.ShapeDtypeStruct((F, S), MX),
            jax.ShapeDtypeStruct((D, S), MX),
            jax.ShapeDtypeStruct((1, D), F32),
            jax.ShapeDtypeStruct((3, F2), F32),
            jax.ShapeDtypeStruct((1, F2), F32),
        ],
        scratch_shapes=[
            pltpu.VMEM((D, F2), MX),
            pltpu.VMEM((F, D), MX),
            pltpu.VMEM((SUBLANES, F2), F32),
            pltpu.VMEM((N_STAGE, TN // LANES, tm + SUBLANES, LANES), F32),
        ],
        compiler_params=_cp(vmem=FFN_VMEM_LIMIT),
    )(x, dy, u, uc, g, wup, wdn, cw)


def _tn_matmul(a, b, out_dtype, name, a_is_transposed=False):
    S, N = b.shape
    M = a.shape[0] if a_is_transposed else a.shape[1]
    bn = N
    while M * bn * 4 > ACC_BYTES and bn % (2 * LANES) == 0:
        bn //= 2
    bk = _row_tile(S, 1024)
    nk = S // bk
    a_spec = pl.BlockSpec((M, bk), lambda j, k: (0, k)) if a_is_transposed else pl.BlockSpec((bk, M), lambda j, k: (k, 0))

    def body(a_ref, b_ref, o_ref, acc):
        k = pl.program_id(1)
        if a_is_transposed:
            p = _dot(a_ref[...].astype(MX), b_ref[...].astype(MX))
        else:
            p = _dot_tn(a_ref[...].astype(MX), b_ref[...].astype(MX))

        @pl.when(k == 0)
        def _():
            acc[...] = p

        @pl.when(k > 0)
        def _():
            acc[...] += p

        @pl.when(k == nk - 1)
        def _():
            o_ref[...] = acc[...].astype(o_ref.dtype)

    return pl.pallas_call(
        body, name=name,
        grid=(N // bn, nk),
        in_specs=[
            a_spec,
            pl.BlockSpec((bk, bn), lambda j, k: (k, j)),
        ],
        out_specs=pl.BlockSpec((M, bn), lambda j, k: (0, j)),
        out_shape=jax.ShapeDtypeStruct((M, N), out_dtype),
        scratch_shapes=[pltpu.VMEM((M, bn), F32)],
        compiler_params=_cp(2),
    )(a, b)


def _rms_linear(x, g, w, b, g_layer, b_layer, name):
    S = x.shape[0]
    N = w.shape[-1]
    tm = _row_tile(S, 512)

    def body(x_ref, g_ref, w_ref, b_ref, o_ref):
        h, _, _ = _rms_fwd(x_ref[...], g_ref[...])
        o_ref[...] = (_dot(h.astype(MX), w_ref[...]) + b_ref[...]).astype(o_ref.dtype)

    return pl.pallas_call(
        body, name=name,
        grid=(S // tm,),
        in_specs=[
            pl.BlockSpec((tm, D), lambda i: (i, 0)),
            pl.BlockSpec((None, 1, D), lambda i: (g_layer, 0, 0)),
            pl.BlockSpec((None, D, N), lambda i: (0, 0, 0)),
            pl.BlockSpec((None, 1, N), lambda i: (b_layer, 0, 0)),
        ],
        out_specs=pl.BlockSpec((tm, N), lambda i: (i, 0)),
        out_shape=jax.ShapeDtypeStruct((S, N), MX),
        compiler_params=_cp(),
    )(x, g, w, b)


def _linear_res(o, w, b, xres, layer):
    S = o.shape[0]
    tm = _row_tile(S, 512)

    def body(o_ref, w_ref, b_ref, x_ref, y_ref):
        y_ref[...] = x_ref[...] + _dot(o_ref[...], w_ref[...]) + b_ref[...]

    return pl.pallas_call(
        body, name="o_proj",
        grid=(S // tm,),
        in_specs=[
            pl.BlockSpec((tm, D), lambda i: (i, 0)),
            pl.BlockSpec((None, D, D), lambda i: (0, 0, 0)),
            pl.BlockSpec((None, 1, D), lambda i: (layer, 0, 0)),
            pl.BlockSpec((tm, D), lambda i: (i, 0)),
        ],
        out_specs=pl.BlockSpec((tm, D), lambda i: (i, 0)),
        out_shape=jax.ShapeDtypeStruct((S, D), F32),
        compiler_params=_cp(),
    )(o, w, b, xres)


def _linear_nt(dy, w):
    S = dy.shape[0]
    tm = _row_tile(S, 512)

    def body(dy_ref, w_ref, o_ref, db_ref):
        @pl.when(pl.program_id(0) == 0)
        def _():
            db_ref[...] = jnp.zeros_like(db_ref)

        dy = dy_ref[...]
        db_ref[...] += jnp.sum(dy, axis=0, keepdims=True)
        o_ref[...] = _dot_nt(dy.astype(MX), w_ref[...]).astype(o_ref.dtype)

    return pl.pallas_call(
        body, name="o_proj_bwd",
        grid=(S // tm,),
        in_specs=[
            pl.BlockSpec((tm, D), lambda i: (i, 0)),
            pl.BlockSpec((None, D, D), lambda i: (0, 0, 0)),
        ],
        out_specs=[
            pl.BlockSpec((tm, D), lambda i: (i, 0)),
            pl.BlockSpec((1, D), lambda i: (0, 0)),
        ],
        out_shape=[
            jax.ShapeDtypeStruct((S, D), MX),
            jax.ShapeDtypeStruct((1, D), F32),
        ],
        compiler_params=_cp(),
    )(dy, w)


def _rms_linear_bwd(x, g, dzs, w, dy, g_layer, name):
    S = x.shape[0]
    N = w.shape[-1]
    tm = _row_tile(S, 512)
    nz = len(dzs)

    def body(*refs):
        x_ref, g_ref = refs[0], refs[1]
        dz_refs = refs[2:2 + nz]
        w_ref, dy_ref, dx_ref, dg_ref, db_ref, h_ref, dzb_ref = refs[2 + nz:]

        @pl.when(pl.program_id(0) == 0)
        def _():
            dg_ref[...] = jnp.zeros_like(dg_ref)
            db_ref[...] = jnp.zeros_like(db_ref)

        gg = g_ref[...]
        h, xh, r = _rms_fwd(x_ref[...], gg)
        h_ref[...] = h.astype(h_ref.dtype)
        dz = dz_refs[0][...].astype(F32)
        for zr in dz_refs[1:]:
            dz = dz + zr[...].astype(F32)
        db_ref[...] += jnp.sum(dz, axis=0, keepdims=True)
        dzb = dz.astype(MX)
        dzb_ref[...] = dzb
        dh = _dot_nt(dzb, w_ref[...])
        dg_ref[...] += jnp.sum(dh * xh, axis=0, keepdims=True)
        dx_ref[...] = dy_ref[...] + _rms_bwd(dh, xh, r, gg)

    return pl.pallas_call(
        body, name=name,
        grid=(S // tm,),
        in_specs=[
            pl.BlockSpec((tm, D), lambda i: (i, 0)),
            pl.BlockSpec((None, 1, D), lambda i: (g_layer, 0, 0)),
        ] + [pl.BlockSpec((tm, N), lambda i: (i, 0))] * nz + [
            pl.BlockSpec((None, D, N), lambda i: (0, 0, 0)),
            pl.BlockSpec((tm, D), lambda i: (i, 0)),
        ],
        out_specs=[
            pl.BlockSpec((tm, D), lambda i: (i, 0)),
            pl.BlockSpec((1, D), lambda i: (0, 0)),
            pl.BlockSpec((1, N), lambda i: (0, 0)),
            pl.BlockSpec((tm, D), lambda i: (i, 0)),
            pl.BlockSpec((tm, N), lambda i: (i, 0)),
        ],
        out_shape=[
            jax.ShapeDtypeStruct((S, D), F32),
            jax.ShapeDtypeStruct((1, D), F32),
            jax.ShapeDtypeStruct((1, N), F32),
            jax.ShapeDtypeStruct((S, D), MX),
            jax.ShapeDtypeStruct((S, N), MX),
        ],
        compiler_params=_cp(),
    )(x, g, *dzs, w, dy)


def _loss_head(x, g, tgt):
    S = x.shape[0]
    tm = _row_tile(S, 512)

    def body(x_ref, g_ref, t_ref, dx_ref, dg_ref, l_ref):
        @pl.when(pl.program_id(0) == 0)
        def _():
            dg_ref[...] = jnp.zeros_like(dg_ref)
            l_ref[...] = jnp.zeros_like(l_ref)

        gg = g_ref[...]
        y, xh, r = _rms_fwd(x_ref[...], gg)
        err = y - t_ref[...]
        tok = jnp.sum(err * err, axis=-1, keepdims=True) * (1.0 / D)
        l_ref[...] += 0.5 * jnp.sum(tok, axis=0, keepdims=True)
        dyv = err * (1.0 / D)
        dg_ref[...] += jnp.sum(dyv * xh, axis=0, keepdims=True)
        dx_ref[...] = _rms_bwd(dyv, xh, r, gg)

    return pl.pallas_call(
        body, name="loss_head",
        grid=(S // tm,),
        in_specs=[
            pl.BlockSpec((tm, D), lambda i: (i, 0)),
            pl.BlockSpec((1, D), lambda i: (0, 0)),
            pl.BlockSpec((tm, D), lambda i: (i, 0)),
        ],
        out_specs=[
            pl.BlockSpec((tm, D), lambda i: (i, 0)),
            pl.BlockSpec((1, D), lambda i: (0, 0)),
            pl.BlockSpec((1, LANES), lambda i: (0, 0)),
        ],
        out_shape=[
            jax.ShapeDtypeStruct((S, D), F32),
            jax.ShapeDtypeStruct((1, D), F32),
            jax.ShapeDtypeStruct((1, LANES), F32),
        ],
        compiler_params=_cp(),
    )(x, g, tgt)


HPG = NH // 2
ROWS = HPG * BLK


def _attn_setup(kvp_ref, kvc_ref, n):
    kw = jnp.concatenate([kvp_ref[...], kvc_ref[...]], axis=0).astype(F32)
    kk, vv = kw[:, :LANES], kw[:, LANES:]
    lo = lax.broadcasted_iota(jnp.int32, (1, LANES), 1) < HD
    kr, vr = pltpu.roll(kk, HD, 1), pltpu.roll(vv, HD, 1)
    ks = [jnp.where(lo, kk, kr).astype(MX), jnp.where(lo, kr, kk).astype(MX)]
    vs = [jnp.where(lo, vv, vr).astype(MX), jnp.where(lo, vr, vv).astype(MX)]
    qi = lax.broadcasted_iota(jnp.int32, (ROWS, 2 * BLK), 0) & (BLK - 1)
    si = lax.broadcasted_iota(jnp.int32, (ROWS, 2 * BLK), 1)
    mask = (si > qi) & (si <= qi + BLK) & jnp.logical_or(n > 0, si >= BLK)
    return ks, vs, lo, mask


def _stack_heads(ref, grp, lo):
    parts = []
    for j in range(4 * grp, 4 * grp + 4):
        slab = ref[:, j * LANES:(j + 1) * LANES]
        zero = jnp.zeros_like(slab)
        parts += [jnp.where(lo, slab, zero), jnp.where(lo, zero, slab)]
    return jnp.concatenate(parts, axis=0)


def _unstack_heads(st, lo):
    return [jnp.where(lo, st[2 * i * BLK:(2 * i + 1) * BLK], st[(2 * i + 1) * BLK:(2 * i + 2) * BLK])
            for i in range(4)]


def _sink_column(sk_ref, layer, grp):
    head = lax.broadcasted_iota(jnp.int32, (ROWS, 1), 0) // BLK
    col = jnp.zeros((ROWS, 1), F32)
    for h in range(HPG):
        col = jnp.where(head == h, sk_ref[layer, HPG * grp + h], col)
    return col


def _attn_probs(qs, kg, mask, sink):
    s = jnp.where(mask, _dot_nt(qs, kg) * SCALE, NEG)
    m = jnp.maximum(jnp.max(s, axis=-1, keepdims=True), sink)
    p = jnp.exp(s - m)
    es = jnp.exp(sink - m)
    inv = 1.0 / (jnp.sum(p, axis=-1, keepdims=True) + es)
    return p * inv, es * inv


def _attn_specs(n_extra_q):
    q_spec = pl.BlockSpec((BLK, D), lambda n: (n, 0))
    return [q_spec] * n_extra_q + [
        pl.BlockSpec((BLK, 4 * HD), lambda n: (jnp.maximum(n - 1, 0), 0)),
        pl.BlockSpec((BLK, 4 * HD), lambda n: (n, 0)),
        pl.BlockSpec(memory_space=pltpu.SMEM),
    ]


def _attn_fwd(q, kv, sinks, layer):
    S = q.shape[0]

    def body(q_ref, kvp_ref, kvc_ref, sk_ref, o_ref):
        n = pl.program_id(0)
        ks, vs, lo, mask = _attn_setup(kvp_ref, kvc_ref, n)
        for grp in range(2):
            qs = _stack_heads(q_ref, grp, lo)
            pr, _ = _attn_probs(qs, ks[grp], mask, _sink_column(sk_ref, layer, grp))
            outs = _unstack_heads(_dot(pr.astype(MX), vs[grp]), lo)
            for i in range(4):
                j = 4 * grp + i
                o_ref[:, j * LANES:(j + 1) * LANES] = outs[i].astype(o_ref.dtype)

    return pl.pallas_call(
        body, name="attn_fwd",
        grid=(S // BLK,),
        in_specs=_attn_specs(1),
        out_specs=pl.BlockSpec((BLK, D), lambda n: (n, 0)),
        out_shape=jax.ShapeDtypeStruct((S, D), MX),
        compiler_params=_cp(),
    )(q, kv, kv, sinks)


def _attn_bwd(q, do, kv, sinks, layer):
    S = q.shape[0]

    def body(q_ref, do_ref, kvp_ref, kvc_ref, sk_ref, dq_ref, dkv_ref, dsk_ref):
        n = pl.program_id(0)

        @pl.when(n == 0)
        def _():
            dkv_ref[...] = jnp.zeros_like(dkv_ref)
            dsk_ref[...] = jnp.zeros_like(dsk_ref)

        ks, vs, lo, mask = _attn_setup(kvp_ref, kvc_ref, n)
        lane = lax.broadcasted_iota(jnp.int32, (1, LANES), 1)
        dsk = jnp.zeros((1, LANES), F32)
        tk, tv = [], []
        for grp in range(2):
            qs = _stack_heads(q_ref, grp, lo)
            dos = _stack_heads(do_ref, grp, lo)
            pr, psink = _attn_probs(qs, ks[grp], mask, _sink_column(sk_ref, layer, grp))
            dpr = _dot_nt(dos, vs[grp])
            delta = jnp.sum(pr * dpr, axis=-1, keepdims=True)
            ds = (pr * (dpr - delta) * SCALE).astype(MX)
            sd = psink * delta
            for h in range(HPG):
                dsk = dsk + jnp.where(lane == HPG * grp + h,
                                      -jnp.sum(sd[h * BLK:(h + 1) * BLK], axis=0, keepdims=True), 0.0)
            dqs = _unstack_heads(_dot(ds, ks[grp]), lo)
            for i in range(4):
                j = 4 * grp + i
                dq_ref[:, j * LANES:(j + 1) * LANES] = dqs[i].astype(dq_ref.dtype)
            dk = _dot_tn(qs, ds).T
            dv = _dot_tn(dos, pr.astype(MX)).T
            tk.append(dk + pltpu.roll(dk, HD, 1))
            tv.append(dv + pltpu.roll(dv, HD, 1))
        dsk_ref[...] += dsk
        contrib = jnp.concatenate([jnp.where(lo, tk[0], tk[1]), jnp.where(lo, tv[0], tv[1])], axis=1)

        @pl.when(n > 0)
        def _():
            rows = pl.ds(pl.multiple_of((n - 1) * BLK, BLK), 2 * BLK)
            dkv_ref[rows, :] += contrib

        @pl.when(n == 0)
        def _():
            dkv_ref[0:BLK, :] += contrib[BLK:]

    return pl.pallas_call(
        body, name="attn_bwd",
        grid=(S // BLK,),
        in_specs=_attn_specs(2),
        out_specs=[
            pl.BlockSpec((BLK, D), lambda n: (n, 0)),
            pl.BlockSpec((S, 4 * HD), lambda n: (0, 0)),
            pl.BlockSpec((1, LANES), lambda n: (0, 0)),
        ],
        out_shape=[
            jax.ShapeDtypeStruct((S, D), MX),
            jax.ShapeDtypeStruct((S, 4 * HD), F32),
            jax.ShapeDtypeStruct((1, LANES), F32),
        ],
        compiler_params=_cp(),
    )(q, do, kv, kv, sinks)


def _ew_rows(rows, cols, n_bufs=1):
    br = rows
    while br * cols * 4 * n_bufs > EW_BYTES and br % (2 * SUBLANES) == 0:
        br //= 2
    return br


def _adamw(parts, w, m, v):
    L, R, C = w.shape
    br = _ew_rows(R, C)
    npart = len(parts[0])

    def body(*refs):
        p_refs = refs[:L * npart]
        w_ref, m_ref, v_ref, g_ref, d_ref, nm_ref, nv_ref = refs[L * npart:]
        lyr = pl.program_id(0)
        for l in range(L):
            @pl.when(lyr == l)
            def _(l=l):
                g = p_refs[l * npart][...]
                for pr in p_refs[l * npart + 1:(l + 1) * npart]:
                    g = g + pr[...]
                nm = B1 * m_ref[...] + (1.0 - B1) * g
                nv = B2 * v_ref[...] + (1.0 - B2) * (g * g)
                m_hat = nm / (1.0 - B1 ** STEP)
                v_hat = nv / (1.0 - B2 ** STEP)
                g_ref[...] = g
                d_ref[...] = -LR * (m_hat / (jnp.sqrt(v_hat) + AEPS) + WD * w_ref[...])
                nm_ref[...] = nm
                nv_ref[...] = nv

    spec = pl.BlockSpec((None, br, C), lambda a, i: (a, i, 0))
    part_specs = [pl.BlockSpec((br, C), lambda a, i, l=l: (jnp.where(a == l, i, 0), 0))
                  for l in range(L) for _ in range(npart)]
    return pl.pallas_call(
        body, name="adamw",
        grid=(L, R // br),
        in_specs=part_specs + [spec] * 3,
        out_specs=[spec] * 4,
        out_shape=[jax.ShapeDtypeStruct((L, R, C), F32)] * 4,
        compiler_params=_cp(2),
    )(*[a for lp in parts for a in lp], w, m, v)


def _coords():
    return lax.axis_index("x"), lax.axis_index("y"), lax.axis_index("c")


def _other_chips(x, y):
    return [(1 - x, y), (x, 1 - y), (1 - x, 1 - y)]


def _slot(ref, axis, chip, size):
    idx = [slice(None)] * 3
    idx[axis] = pl.ds(pl.multiple_of(chip * size, size), size)
    return ref.at[tuple(idx)]


HBM_SPEC = pl.BlockSpec(memory_space=pltpu.HBM)
SEM_SPEC = pl.BlockSpec(memory_space=pltpu.SEMAPHORE)
ANY_SPEC = pl.BlockSpec(memory_space=pl.ANY)
EFFECT = pltpu.SideEffectType.DATAFLOW_SIDE_EFFECTING


def _slot_specs(shape, axis, br, lead):
    _, b, c = shape
    nrb = b // br
    first = (lambda a: a) if lead is None else (lambda a: lead)
    shard = pl.BlockSpec((None, br, c), lambda a, i, me: (first(a), i, 0))
    if axis == 1:
        slot = pl.BlockSpec((None, br, c), lambda a, i, me: (a, me[0] * nrb + i, 0))
    else:
        slot = pl.BlockSpec((None, br, c), lambda a, i, me: (a, i, me[0]))
    return shard, slot


def _shard_rows(b, c):
    br = b
    while br * c * 4 > 2 * EW_BYTES and br % (4 * SUBLANES) == 0:
        br //= 2
    return br


def _gather_place(shard, axis, me, dtype, lead=None):
    a_dim, b, c = shard.shape
    if lead is not None:
        a_dim = 1
    br = _shard_rows(b, c)
    shp = [a_dim, b, c]
    shp[axis] *= 4
    shard_spec, slot_spec = _slot_specs((a_dim, b, c), axis, br, lead)

    def body(me_ref, s_ref, o_ref):
        o_ref[...] = s_ref[...].astype(o_ref.dtype)

    return pl.pallas_call(
        body, name="gather_place",
        grid_spec=pltpu.PrefetchScalarGridSpec(
            num_scalar_prefetch=1, grid=(a_dim, b // br), in_specs=[shard_spec], out_specs=slot_spec),
        out_shape=jax.ShapeDtypeStruct(tuple(shp), dtype),
        compiler_params=_cp(2),
    )(me, shard)


def _sum_landed(grad, landed, axis, me):
    a_dim, b, c = landed.shape[1:]
    br = _shard_rows(b, c)
    shard_spec, slot_spec = _slot_specs((a_dim, b, c), axis, br, None)

    def body(me_ref, own_ref, r_ref, o_ref):
        o_ref[...] = ((own_ref[...].astype(F32) + r_ref[0].astype(F32)) + r_ref[1].astype(F32)) + r_ref[2].astype(F32)

    return pl.pallas_call(
        body, name="sum_landed",
        grid_spec=pltpu.PrefetchScalarGridSpec(
            num_scalar_prefetch=1, grid=(a_dim, b // br),
            in_specs=[slot_spec, pl.BlockSpec((3, None, br, c), lambda a, i, me: (0, a, i, 0))],
            out_specs=shard_spec),
        out_shape=jax.ShapeDtypeStruct((a_dim, b, c), F32),
        compiler_params=_cp(2),
    )(me, grad, landed)


def _copies(refs, plan, send, recv, to_sibling):
    x, y, c = _coords()
    me = 2 * x + y
    if to_sibling:
        targets = [((x, y, 1 - c), me)]
    else:
        targets = [((px, py, c), 2 * px + py) for px, py in _other_chips(x, y)]
    out, t = [], 0
    while plan(refs, me, t, 0, me) is not None:
        for k, (device, peer) in enumerate(targets):
            sv, dv = plan(refs, me, t, k, peer)
            n = len(targets) * t + k
            out.append(pltpu.make_async_remote_copy(
                src_ref=sv, dst_ref=dv, send_sem=send.at[n], recv_sem=recv.at[n],
                device_id=device, device_id_type=MESH))
        t += 1
    return out


def _push_start(name, bufs, n_copies, plan, to_sibling=False):
    nb = len(bufs)

    def body(*refs):
        send, recv, token = refs[nb], refs[nb + 1], refs[-1]
        for cp in _copies(refs[:nb], plan, send, recv, to_sibling):
            cp.start()
        token[...] = jnp.zeros_like(token)

    res = pl.pallas_call(
        body, name=name,
        in_specs=[HBM_SPEC] * nb,
        out_specs=[SEM_SPEC, SEM_SPEC] + [HBM_SPEC] * nb + [pl.BlockSpec(memory_space=pltpu.VMEM)],
        out_shape=[pltpu.SemaphoreType.DMA((n_copies,)), pltpu.SemaphoreType.DMA((n_copies,))]
        + [pltpu.HBM(a.shape, a.dtype) for a in bufs] + [jax.ShapeDtypeStruct((SUBLANES, LANES), F32)],
        input_output_aliases={i: 2 + i for i in range(nb)},
        compiler_params=pltpu.CompilerParams(has_side_effects=EFFECT),
    )(*[pltpu.with_memory_space_constraint(a, pltpu.HBM) for a in bufs])
    return res[0], res[1], res[2:2 + nb], res[-1]


def _push_wait(name, send, recv, bufs, plan, after, to_sibling=False):
    nb = len(bufs)

    def body(*refs):
        for cp in _copies(refs[:nb], plan, refs[nb], refs[nb + 1], to_sibling):
            cp.wait_send()
            cp.wait_recv()

    return pl.pallas_call(
        body, name=name,
        in_specs=[HBM_SPEC] * nb + [SEM_SPEC, SEM_SPEC, ANY_SPEC],
        out_specs=[HBM_SPEC] * nb,
        out_shape=[pltpu.HBM(a.shape, a.dtype) for a in bufs],
        input_output_aliases={i: i for i in range(nb)},
        compiler_params=pltpu.CompilerParams(has_side_effects=EFFECT),
    )(*bufs, send, recv, after)


def _gather_plan(axes):
    def plan(refs, me, t, k, peer):
        if t >= len(axes):
            return None
        size = refs[t].shape[axes[t]] // 4
        mine = _slot(refs[t], axes[t], me, size)
        return mine, mine
    return plan


def _half_slot(ref, axis, chip):
    c = lax.axis_index("c")
    if axis == 1:
        half = ref.shape[1] // 8
        return ref.at[:, pl.ds(pl.multiple_of(chip * 2 * half + c * half, 2 * SUBLANES), half), :]
    half = ref.shape[1] // 2
    size = ref.shape[2] // 4
    return ref.at[:, pl.ds(pl.multiple_of(c * half, 2 * SUBLANES), half), pl.ds(pl.multiple_of(chip * size, LANES), size)]


def _gather_half_plan(axes):
    def plan(refs, me, t, k, peer):
        if t >= len(axes):
            return None
        mine = _half_slot(refs[t], axes[t], me)
        return mine, mine
    return plan


def _gather_pass_plan(axes):
    def plan(refs, me, t, k, peer):
        if t >= 3 * len(axes):
            return None
        x, y, _ = _coords()
        px, py = _other_chips(x, y)[t % 3]
        landed = _half_slot(refs[t // 3], axes[t // 3], 2 * px + py)
        return landed, landed
    return plan


def _scatter_plan(axes):
    n = len(axes)

    def plan(refs, me, t, k, peer):
        if t >= n:
            return None
        size = refs[t].shape[axes[t]] // 4
        return _slot(refs[t], axes[t], peer, size), refs[n + t].at[k]
    return plan


def _swap_with_sibling(arrs):
    na = len(arrs)

    def body(*refs):
        ins, outs = refs[:na], refs[na:2 * na]
        send, recv = refs[2 * na:]
        x, y, c = _coords()
        cps = []
        for t in range(na):
            rc = pltpu.make_async_remote_copy(
                src_ref=ins[t], dst_ref=outs[t], send_sem=send.at[t], recv_sem=recv.at[t],
                device_id=(x, y, 1 - c), device_id_type=MESH)
            rc.start()
            cps.append(rc)
        for rc in cps:
            rc.wait()

    any_spec = pl.BlockSpec(memory_space=pl.ANY)
    return pl.pallas_call(
        body, name="swap_sibling",
        in_specs=[any_spec] * na,
        out_specs=[any_spec] * na,
        out_shape=[jax.ShapeDtypeStruct(a.shape, a.dtype) for a in arrs],
        scratch_shapes=[pltpu.SemaphoreType.DMA((na,)), pltpu.SemaphoreType.DMA((na,))],
    )(*arrs)


def _all_reduce_small(v):
    R = v.shape[0]

    def body(v_ref, o_ref, buf, send, recv, local):
        x, y, c = _coords()
        me = 4 * x + 2 * y + c
        cp = pltpu.make_async_copy(v_ref, buf.at[me], local)
        cp.start()
        pushes = []
        for k in range(1, 8):
            peer = (x ^ (k >> 2), y ^ ((k >> 1) & 1), c ^ (k & 1))
            rc = pltpu.make_async_remote_copy(
                src_ref=v_ref, dst_ref=buf.at[me], send_sem=send.at[k - 1], recv_sem=recv.at[k - 1],
                device_id=peer, device_id_type=MESH)
            rc.start()
            pushes.append(rc)
        for k in range(1, 8):
            px, py, pc = x ^ (k >> 2), y ^ ((k >> 1) & 1), c ^ (k & 1)
            pltpu.make_async_remote_copy(
                src_ref=v_ref, dst_ref=buf.at[4 * px + 2 * py + pc], send_sem=send.at[k - 1],
                recv_sem=recv.at[k - 1], device_id=(px, py, pc), device_id_type=MESH).wait_recv()
        for rc in pushes:
            rc.wait_send()
        cp.wait()
        tot = buf[0]
        for k in range(1, 8):
            tot = tot + buf[k]
        o_ref[...] = tot

    vm = pl.BlockSpec(memory_space=pltpu.VMEM)
    return pl.pallas_call(
        body, name="all_reduce_small",
        in_specs=[vm],
        out_specs=vm,
        out_shape=jax.ShapeDtypeStruct((R, LANES), F32),
        scratch_shapes=[
            pltpu.VMEM((8, R, LANES), F32),
            pltpu.SemaphoreType.DMA((7,)),
            pltpu.SemaphoreType.DMA((7,)),
            pltpu.SemaphoreType.DMA,
        ],
        compiler_params=pltpu.CompilerParams(vmem_limit_bytes=VMEM_LIMIT),
    )(v)


def _pack(arrs):
    flat = []
    for a in arrs:
        f = a.reshape(-1).astype(F32)
        flat.append(jnp.pad(f, (0, (-f.shape[0]) % LANES)))
    v = jnp.concatenate(flat)
    v = jnp.pad(v, (0, (-v.shape[0]) % (SUBLANES * LANES)))
    return v.reshape(-1, LANES)


def _unpack(v, shapes):
    flat = v.reshape(-1)
    out, off = [], 0
    for shp in shapes:
        n = 1
        for d in shp:
            n *= d
        out.append(flat[off:off + n].reshape(shp))
        off += n + (-n) % LANES
    return out


def _local_step(x, tgt, sp, weights_for, on_grads):
    n1, n2 = sp["norm1_g"], sp["norm2_g"]
    w = dict(weights_for(0, x))
    saved = []
    xs = x
    kv = None
    for l in range(DEPTH):
        x_in = xs
        if l >= N_A:
            w.update(weights_for(1 + 2 * l - N_A, x_in))
        if l == N_A:
            kv = _rms_linear(x_in, sp["kv_norm_g"], w["w_kv"], sp["b_kv"], 0, 0, "kv_proj")
        if l < N_A:
            xa = _pool_fwd(x_in, n1, w["pool_w"], w["pool_scale"], l)
            q = o = None
        else:
            j = l - N_A
            q = _rms_linear(x_in, n1, w["w_q", j], sp["b_q"], l, j, "q_proj")
            o = _attn_fwd(q, kv, sp["sinks"], j)
            xa = _linear_res(o, w["w_o", j], sp["b_o"], x_in, j)
        w.update(weights_for(1 + l if l < N_A else 2 + 2 * l - N_A, xa))
        xs, u, uc = _ffn_fwd(xa, n2, w["ffn_up", l], w["ffn_down", l], w["ffn_conv_w"], sp["ffn_conv_b"], l)
        saved.append((x_in, xa, u, uc, q, o))

    dx, d_final_g, loss = _loss_head(xs, sp["final_g"], tgt)

    g = {k: [None] * DEPTH for k in ("norm1_g", "norm2_g", "ffn_conv_w", "ffn_conv_b")}
    for k in ("pool_scale", "b_q", "sinks", "b_o"):
        g[k] = [None] * N_A
    g["final_g"] = d_final_g
    dkvs = []
    pending = {}
    for l in reversed(range(DEPTH)):
        x_in, xa, u, uc, q, o = saved[l]
        dxa, du, a, hb, g["norm2_g"][l], g["ffn_conv_w"][l], g["ffn_conv_b"][l] = _ffn_bwd(
            xa, dx, u, uc, n2, w["ffn_up", l], w["ffn_down", l], w["ffn_conv_w"], l)
        pending["ffn_up", l] = _tn_matmul(hb, du, MX, "d_ffn_up", a_is_transposed=True)
        if l == 0:
            n1 = n1 + on_grads(DEPTH + 1, pending)
            pending = {}
        pending["ffn_down", l] = _tn_matmul(a, dx, MX, "d_ffn_down", a_is_transposed=True)
        zero = on_grads(DEPTH - 1 - l, pending)
        pending = {}
        n1, n2 = n1 + zero, n2 + zero
        if l < N_A:
            dx, d_pw, g["pool_scale"][l], g["norm1_g"][l] = _pool_bwd(
                x_in, dxa, n1, w["pool_w"], w["pool_scale"], l)
            pending["pool_w", l] = d_pw.astype(MX)
        else:
            j = l - N_A
            d_o, g["b_o"][j] = _linear_nt(dxa, w["w_o", j])
            pending["w_o", j] = _tn_matmul(o, dxa, MX, "d_w_o")
            dq, dkv, g["sinks"][j] = _attn_bwd(q, d_o, kv, sp["sinks"], j)
            dkvs.append(dkv)
            dx, g["norm1_g"][l], g["b_q"][j], hq, dqb = _rms_linear_bwd(
                x_in, n1, [dq], w["w_q", j], dxa, l, "q_proj_bwd")
            pending["w_q", j] = _tn_matmul(hq, dqb, MX, "d_w_q")
        if l == N_A:
            dx, g["kv_norm_g"], g["b_kv"], hk, dkvb = _rms_linear_bwd(
                x_in, sp["kv_norm_g"], dkvs, w["w_kv"], dx, 0, "kv_proj_bwd")
            pending["w_kv", 0] = _tn_matmul(hk, dkvb, MX, "d_w_kv")
    on_grads(DEPTH, pending)
    return loss, dx, g


SMALL = ("norm1_g", "norm2_g", "kv_norm_g", "b_kv", "b_q", "sinks", "b_o", "ffn_conv_b", "final_g")
SMALL_SHARDED = ("pool_scale", "ffn_conv_w")
BIG = ("pool_w", "w_kv", "w_q", "w_o", "ffn_up", "ffn_down")
ORDER = ("norm1_g", "norm2_g", "pool_w", "pool_scale", "kv_norm_g", "w_kv", "b_kv", "w_q", "b_q", "sinks",
         "w_o", "b_o", "ffn_up", "ffn_conv_w", "ffn_conv_b", "ffn_down", "final_g")


def _as3d(a):
    return a.reshape((-1,) + a.shape[-2:])


def kernel(x, norm1_g, norm2_g, pool_w, pool_scale, kv_norm_g, w_kv, b_kv, w_q, b_q, sinks, w_o, b_o, ffn_up, ffn_conv_w, ffn_conv_b, ffn_down, final_g, loss_target, m_norm1_g, m_norm2_g, m_pool_w, m_pool_scale, m_kv_norm_g, m_w_kv, m_b_kv, m_w_q, m_b_q, m_sinks, m_w_o, m_b_o, m_ffn_up, m_ffn_conv_w, m_ffn_conv_b, m_ffn_down, m_final_g, v_norm1_g, v_norm2_g, v_pool_w, v_pool_scale, v_kv_norm_g, v_w_kv, v_b_kv, v_w_q, v_b_q, v_sinks, v_w_o, v_b_o, v_ffn_up, v_ffn_conv_w, v_ffn_conv_b, v_ffn_down, v_final_g):
    W = dict(norm1_g=norm1_g, norm2_g=norm2_g, pool_w=pool_w, pool_scale=pool_scale, kv_norm_g=kv_norm_g,
             w_kv=w_kv, b_kv=b_kv, w_q=w_q, b_q=b_q, sinks=sinks, w_o=w_o, b_o=b_o, ffn_up=ffn_up,
             ffn_conv_w=ffn_conv_w, ffn_conv_b=ffn_conv_b, ffn_down=ffn_down, final_g=final_g)
    M = dict(norm1_g=m_norm1_g, norm2_g=m_norm2_g, pool_w=m_pool_w, pool_scale=m_pool_scale,
             kv_norm_g=m_kv_norm_g, w_kv=m_w_kv, b_kv=m_b_kv, w_q=m_w_q, b_q=m_b_q, sinks=m_sinks, w_o=m_w_o,
             b_o=m_b_o, ffn_up=m_ffn_up, ffn_conv_w=m_ffn_conv_w, ffn_conv_b=m_ffn_conv_b, ffn_down=m_ffn_down,
             final_g=m_final_g)
    V = dict(norm1_g=v_norm1_g, norm2_g=v_norm2_g, pool_w=v_pool_w, pool_scale=v_pool_scale,
             kv_norm_g=v_kv_norm_g, w_kv=v_w_kv, b_kv=v_b_kv, w_q=v_w_q, b_q=v_b_q, sinks=v_sinks, w_o=v_w_o,
             b_o=v_b_o, ffn_up=v_ffn_up, ffn_conv_w=v_ffn_conv_w, ffn_conv_b=v_ffn_conv_b, ffn_down=v_ffn_down,
             final_g=v_final_g)
    S = x.shape[1]
    chip = 2 * lax.axis_index("x") + lax.axis_index("y")

    gather_axis = dict(pool_w=1, w_kv=1, w_q=1, w_o=1, ffn_up=2, ffn_down=1, pool_scale=2, ffn_conv_w=2)
    me = chip.reshape(1).astype(jnp.int32)
    axis_of = lambda key: gather_axis[key if isinstance(key, str) else key[0]]

    def placed(key, dtype):
        if isinstance(key, str):
            return _gather_place(_as3d(W[key]), axis_of(key), me, dtype)
        return _gather_place(W[key[0]], axis_of(key), me, dtype, lead=key[1])

    stages = [
        ["pool_w", "pool_scale", "ffn_conv_w"],
        [("ffn_up", 0), ("ffn_down", 0)],
        [("ffn_up", 1), ("ffn_down", 1)],
        ["w_kv", ("w_q", 0), ("w_o", 0)],
        [("ffn_up", 2), ("ffn_down", 2)],
        [("w_q", 1), ("w_o", 1)],
        [("ffn_up", 3), ("ffn_down", 3)],
    ]
    TWO_LEVEL = 1
    gathers, zero = [], 0.0
    for si, keys in enumerate(stages):
        axes = [axis_of(k) for k in keys]
        bufs = [placed(k, F32 if k in SMALL_SHARDED else MX) for k in keys]
        plan = _gather_half_plan(axes) if si == TWO_LEVEL else _gather_plan(axes)
        send, recv, bufs, token = _push_start(f"gather_start_{si}", bufs, 3 * len(keys), plan)
        gathers.append((keys, axes, send, recv, bufs))
        zero = zero + token[0, 0]

    def weights_for(stage, after):
        keys, axes, send, recv, bufs = gathers[stage]
        if stage == TWO_LEVEL:
            bufs = _push_wait(f"gather_wait_{stage}", send, recv, bufs, _gather_half_plan(axes), after)
            send, recv, bufs, _ = _push_start("gather_pass_start", bufs, 3 * len(keys), _gather_pass_plan(axes), True)
            bufs = _push_wait("gather_pass_wait", send, recv, bufs, _gather_pass_plan(axes), after, True)
        else:
            bufs = _push_wait(f"gather_wait_{stage}", send, recv, bufs, _gather_plan(axes), after)
        out = dict(zip(keys, bufs))
        if stage == 0:
            out["pool_w"] = out["pool_w"].reshape(N_A, 4, GC, GC)
            out["pool_scale"] = out["pool_scale"].reshape(N_A, 1, D)
        return out

    scatters = []

    def on_grads(stage, grads):
        keys = list(grads)
        axes = [axis_of(k) for k in keys]
        arrs = [_as3d(grads[k]) for k in keys]
        lands = []
        for a, ax in zip(arrs, axes):
            shp = list(a.shape)
            shp[ax] //= 4
            lands.append(lax.empty((3,) + tuple(shp), a.dtype))
        send, recv, bufs, token = _push_start(f"scatter_start_{stage}", arrs + lands, 3 * len(keys), _scatter_plan(axes))
        scatters.append((stage, keys, axes, send, recv, bufs))
        return token[0, 0]

    sp = dict(
        norm1_g=norm1_g.reshape(DEPTH, 1, D) + zero, norm2_g=norm2_g.reshape(DEPTH, 1, D),
        kv_norm_g=kv_norm_g.reshape(1, 1, D), b_kv=b_kv.reshape(1, 1, 4 * HD), b_q=b_q.reshape(N_B, 1, D),
        sinks=sinks, b_o=b_o.reshape(N_B, 1, D), ffn_conv_b=ffn_conv_b.reshape(DEPTH, 1, F2),
        final_g=final_g.reshape(1, D))

    x2d = x.reshape(S, D)
    loss, grad_x, g = _local_step(x2d, loss_target.reshape(S, D), sp, weights_for, on_grads)

    small_full = dict(
        norm1_g=jnp.stack(g["norm1_g"]), norm2_g=jnp.stack(g["norm2_g"]), kv_norm_g=g["kv_norm_g"],
        b_kv=g["b_kv"], b_q=jnp.stack(g["b_q"]), sinks=jnp.stack([s[0, :NH] for s in g["sinks"]]),
        b_o=jnp.stack(g["b_o"]), ffn_conv_b=jnp.stack(g["ffn_conv_b"]), final_g=g["final_g"],
        pool_scale=jnp.stack(g["pool_scale"]), ffn_conv_w=jnp.stack(g["ffn_conv_w"]))
    small_names = SMALL + SMALL_SHARDED
    small_shapes = [tuple(W[k].shape) for k in SMALL] + [(N_A, D), (DEPTH, 3, F2)]
    packed = _pack([small_full[k] for k in small_names] + [loss])
    red = _unpack(_all_reduce_small(packed), small_shapes + [(1, LANES)])
    red_g = dict(zip(small_names, red[:-1]))
    loss_out = red[-1][0, 0]
    red_g["pool_scale"] = lax.dynamic_slice_in_dim(red_g["pool_scale"], chip * (D // 4), D // 4, axis=1)
    red_g["ffn_conv_w"] = lax.dynamic_slice_in_dim(red_g["ffn_conv_w"], chip * (F2 // 4), F2 // 4, axis=2)
    small_w_shapes = [tuple(W[k].shape) for k in small_names]
    pk = lambda d: _pack([d[k] for k in small_names])[None]
    res = _adamw([[_pack([red_g[k] for k in small_names])]], pk(W), pk(M), pk(V))
    out_g, out_d, out_m, out_v = [dict(zip(small_names, _unpack(r, small_w_shapes))) for r in res]

    pkeys, partial = [], []
    for stage, keys, axes, send, recv, bufs in scatters:
        bufs = _push_wait(f"scatter_wait_{stage}", send, recv, bufs, _scatter_plan(axes), grad_x)
        n = len(keys)
        for k, ax, grad, landed in zip(keys, axes, bufs[:n], bufs[n:]):
            pkeys.append(k)
            p_sum = _sum_landed(grad, landed, ax, me)
            partial.append(p_sum.reshape(-1, p_sum.shape[-1]))
    mine = dict(zip(pkeys, partial))
    theirs = dict(zip(pkeys, _swap_with_sibling(partial)))
    for k in BIG:
        n_l = len([pk_ for pk_ in pkeys if pk_[0] == k])
        shp = W[k].shape
        rows, cols = mine[k, 0].shape
        three_d = lambda a: a.reshape(n_l, rows, cols)
        res = _adamw([[mine[k, l], theirs[k, l]] for l in range(n_l)], three_d(W[k]), three_d(M[k]), three_d(V[k]))
        out_g[k], out_d[k], out_m[k], out_v[k] = [r.reshape(shp) for r in res]

    return (loss_out, grad_x.reshape(x.shape), *[out_g[k] for k in ORDER], *[out_d[k] for k in ORDER],
            *[out_m[k] for k in ORDER], *[out_v[k] for k in ORDER])
```

```python
import functools

import jax
import jax.numpy as jnp
from jax import lax
from jax.experimental import pallas as pl
from jax.experimental.pallas import tpu as pltpu

D = 1024
DEPTH = 4
N_A = 2
N_B = 2
WINS = (2, 4, 8, 16)
GC = 256
HD = 64
NH = 16
BLK = 128
F = 2816
F2 = 2 * F
EPS = 1e-5
SCALE = HD ** -0.5
NEG = -1e30
HALO = 16
TN = 256
TM_STREAM = 1024
UP_GROUP = 3
LANES = 128
SUBLANES = 8
VMEM_LIMIT = 56 * 1024 * 1024
FFN_VMEM_LIMIT = 60 * 1024 * 1024
ACC_BYTES = 6 * 1024 * 1024
EW_BYTES = 1024 * 1024

LR, B1, B2, AEPS, WD, STEP = 0.001, 0.9, 0.999, 1e-08, 0.01, 10

MX = jnp.bfloat16
F32 = jnp.float32
MESH = pl.DeviceIdType.MESH


def _cp(n_axes=1, vmem=VMEM_LIMIT):
    return pltpu.CompilerParams(dimension_semantics=("arbitrary",) * n_axes, vmem_limit_bytes=vmem)


def _dot(a, b):
    return jnp.dot(a, b, preferred_element_type=F32)


def _dot_nt(a, b):
    return lax.dot_general(a, b, (((1,), (1,)), ((), ())), preferred_element_type=F32)


def _dot_tn(a, b):
    return lax.dot_general(a, b, (((0,), (0,)), ((), ())), preferred_element_type=F32)


def _rms_fwd(x, g):
    r = lax.rsqrt(jnp.mean(x * x, axis=-1, keepdims=True) + EPS)
    xh = x * r
    return xh * g, xh, r


def _rms_bwd(dh, xh, r, g):
    dxh = dh * g
    return r * (dxh - xh * jnp.mean(dxh * xh, axis=-1, keepdims=True))


def _row_tile(s, want):
    return min(s, want)


def _pool_pm(e, h, row, tm):
    out = []
    for gi, win in enumerate(WINS):
        cols = slice(gi * GC, (gi + 1) * GC)
        s = e[:, cols]
        sh = 1
        while sh < win:
            s = s + pltpu.roll(s, sh, 0)
            sh *= 2
        inv = 1.0 / jnp.minimum(row + 1, win).astype(F32)
        out.append(s[HALO:] * inv - h[:, cols])
    return out


def _pool_fwd(x, g, pw, ps, layer):
    S = x.shape[0]
    tm = _row_tile(S, 512)
    hb = tm // HALO

    def body(x_ref, xh_ref, g_ref, pw_ref, ps_ref, o_ref):
        i = pl.program_id(0)
        x = x_ref[...]
        gg = g_ref[...]
        h, _, _ = _rms_fwd(x, gg)
        hh, _, _ = _rms_fwd(xh_ref[...], gg)
        hh = jnp.where(i > 0, hh, 0.0)
        e = jnp.concatenate([hh, h], axis=0)
        row = i * tm + lax.broadcasted_iota(jnp.int32, (tm, 1), 0)
        pm = _pool_pm(e, h, row, tm)
        for gi in range(len(WINS)):
            cols = slice(gi * GC, (gi + 1) * GC)
            z = _dot(pm[gi].astype(MX), pw_ref[gi])
            o_ref[:, cols] = x[:, cols] + z * ps_ref[:, cols]

    return pl.pallas_call(
        body, name="pool_fwd",
        grid=(S // tm,),
        in_specs=[
            pl.BlockSpec((tm, D), lambda i: (i, 0)),
            pl.BlockSpec((HALO, D), lambda i: (jnp.maximum(i * hb - 1, 0), 0)),
            pl.BlockSpec((None, 1, D), lambda i: (layer, 0, 0)),
            pl.BlockSpec((None, 4, GC, GC), lambda i: (layer, 0, 0, 0)),
            pl.BlockSpec((None, 1, D), lambda i: (layer, 0, 0)),
        ],
        out_specs=pl.BlockSpec((tm, D), lambda i: (i, 0)),
        out_shape=jax.ShapeDtypeStruct((S, D), F32),
        compiler_params=_cp(),
    )(x, x, g, pw, ps)


def _pool_bwd(x, dy, g, pw, ps, layer):
    S = x.shape[0]
    tm = _row_tile(S, 256)
    hb = tm // HALO
    n_i = S // tm
    n_h = S // HALO

    def body(x_ref, xh_ref, dy_ref, dyn_ref, g_ref, pw_ref, ps_ref, dx_ref, dpw_ref, dps_ref, dg_ref):
        i = pl.program_id(0)

        @pl.when(i == 0)
        def _():
            dpw_ref[...] = jnp.zeros_like(dpw_ref)
            dps_ref[...] = jnp.zeros_like(dps_ref)
            dg_ref[...] = jnp.zeros_like(dg_ref)

        x = x_ref[...]
        gg = g_ref[...]
        ps = ps_ref[...]
        h, xh, r = _rms_fwd(x, gg)
        hh, _, _ = _rms_fwd(xh_ref[...], gg)
        hh = jnp.where(i > 0, hh, 0.0)
        e = jnp.concatenate([hh, h], axis=0)
        row = i * tm + lax.broadcasted_iota(jnp.int32, (tm, 1), 0)
        rown = (i + 1) * tm + lax.broadcasted_iota(jnp.int32, (HALO, 1), 0)
        pm = _pool_pm(e, h, row, tm)
        dy = dy_ref[...]
        dz = dy * ps
        dzn = jnp.where(i < n_i - 1, dyn_ref[...] * ps, 0.0)
        parts = []
        for gi, win in enumerate(WINS):
            cols = slice(gi * GC, (gi + 1) * GC)
            w = pw_ref[gi]
            pmb = pm[gi].astype(MX)
            z = _dot(pmb, w)
            dps_ref[:, cols] += jnp.sum(dy[:, cols] * z, axis=0, keepdims=True)
            dzb = dz[:, cols].astype(MX)
            dpw_ref[gi] += _dot_tn(pmb, dzb)
            dpm = _dot_nt(dzb, w)
            dpmn = _dot_nt(dzn[:, cols].astype(MX), w)
            q = dpm * (1.0 / jnp.minimum(row + 1, win).astype(F32))
            qn = dpmn * (1.0 / jnp.minimum(rown + 1, win).astype(F32))
            s = jnp.concatenate([q, qn], axis=0)
            sh = 1
            while sh < win:
                s = s + pltpu.roll(s, tm + HALO - sh, 0)
                sh *= 2
            parts.append(s[:tm] - dpm)
        dh = jnp.concatenate(parts, axis=1)
        dg_ref[...] += jnp.sum(dh * xh, axis=0, keepdims=True)
        dx_ref[...] = dy + _rms_bwd(dh, xh, r, gg)

    return pl.pallas_call(
        body, name="pool_bwd",
        grid=(n_i,),
        in_specs=[
            pl.BlockSpec((tm, D), lambda i: (i, 0)),
            pl.BlockSpec((HALO, D), lambda i: (jnp.maximum(i * hb - 1, 0), 0)),
            pl.BlockSpec((tm, D), lambda i: (i, 0)),
            pl.BlockSpec((HALO, D), lambda i: (jnp.minimum((i + 1) * hb, n_h - 1), 0)),
            pl.BlockSpec((None, 1, D), lambda i: (layer, 0, 0)),
            pl.BlockSpec((None, 4, GC, GC), lambda i: (layer, 0, 0, 0)),
            pl.BlockSpec((None, 1, D), lambda i: (layer, 0, 0)),
        ],
        out_specs=[
            pl.BlockSpec((tm, D), lambda i: (i, 0)),
            pl.BlockSpec((4, GC, GC), lambda i: (0, 0, 0)),
            pl.BlockSpec((1, D), lambda i: (0, 0)),
            pl.BlockSpec((1, D), lambda i: (0, 0)),
        ],
        out_shape=[
            jax.ShapeDtypeStruct((S, D), F32),
            jax.ShapeDtypeStruct((4, GC, GC), F32),
            jax.ShapeDtypeStruct((1, D), F32),
            jax.ShapeDtypeStruct((1, D), F32),
        ],
        compiler_params=_cp(),
    )(x, x, dy, dy, g, pw, ps)


N_STAGE = 4


def _rows_before(slot, u, prev8):
    tm = u.shape[0]
    m1, m2 = [], []
    for c in range(TN // LANES):
        lanes = slice(c * LANES, (c + 1) * LANES)
        slot[c, 0:SUBLANES, :] = prev8[:, lanes]
        slot[c, SUBLANES:SUBLANES + tm, :] = u[:, lanes]
        m1.append(slot[c, pl.ds(SUBLANES - 1, tm), :])
        m2.append(slot[c, pl.ds(SUBLANES - 2, tm), :])
    return jnp.concatenate(m1, axis=1), jnp.concatenate(m2, axis=1)


def _rows_after(slot, d, next8):
    tm = d.shape[0]
    p1, p2 = [], []
    for c in range(TN // LANES):
        lanes = slice(c * LANES, (c + 1) * LANES)
        slot[c, 0:tm, :] = d[:, lanes]
        slot[c, tm:tm + SUBLANES, :] = next8[:, lanes]
        p1.append(slot[c, pl.ds(1, tm), :])
        p2.append(slot[c, pl.ds(2, tm), :])
    return jnp.concatenate(p1, axis=1), jnp.concatenate(p2, axis=1)


def _conv(slot, u, prev8, cw):
    um1, um2 = _rows_before(slot, u, prev8)
    return cw[0:1] * um2 + cw[1:2] * um1 + cw[2:3] * u


def _ffn_fwd(x, g, wup, wdn, cw, cb, layer):
    S = x.shape[0]
    tm = _row_tile(S, 512)

    def body(x_ref, g_ref, wup_hbm, wdn_hbm, cw_ref, cb_ref, o_ref, u_ref, uc_ref, wup_v, wdn_v, carry, act, stage):
        i = pl.program_id(0)

        @pl.when(i == 0)
        def _():
            pltpu.sync_copy(wup_hbm.at[0], wup_v)
            pltpu.sync_copy(wdn_hbm.at[0], wdn_v)
            carry[...] = jnp.zeros_like(carry)

        x = x_ref[...]
        h, _, _ = _rms_fwd(x, g_ref[...])
        hb = h.astype(MX)
        for j in range(F // TN):
            cg = slice(j * TN, (j + 1) * TN)
            cv = slice(F + j * TN, F + (j + 1) * TN)
            ug = _dot(hb, wup_v[:, cg])
            uv = _dot(hb, wup_v[:, cv])
            u_ref[:, cg] = ug.astype(u_ref.dtype)
            u_ref[:, cv] = uv.astype(u_ref.dtype)
            gt = _conv(stage.at[2 * (j % 2)], ug, carry[:, cg], cw_ref[:, cg])
            vl = _conv(stage.at[2 * (j % 2) + 1], uv, carry[:, cv], cw_ref[:, cv])
            carry[:, cg] = ug[tm - SUBLANES:]
            carry[:, cv] = uv[tm - SUBLANES:]
            gt = gt + cb_ref[:, cg]
            vl = vl + cb_ref[:, cv]
            uc_ref[:, cg] = gt.astype(uc_ref.dtype)
            uc_ref[:, cv] = vl.astype(uc_ref.dtype)
            act[:, cg] = (gt * jax.nn.sigmoid(gt) * vl).astype(act.dtype)
        o_ref[...] = x + _dot(act[...], wdn_v[...])

    return pl.pallas_call(
        body, name="ffn_fwd",
        grid=(S // tm,),
        in_specs=[
            pl.BlockSpec((tm, D), lambda i: (i, 0)),
            pl.BlockSpec((None, 1, D), lambda i: (layer, 0, 0)),
            pl.BlockSpec(memory_space=pl.ANY),
            pl.BlockSpec(memory_space=pl.ANY),
            pl.BlockSpec((None, 3, F2), lambda i: (layer, 0, 0)),
            pl.BlockSpec((None, 1, F2), lambda i: (layer, 0, 0)),
        ],
        out_specs=[
            pl.BlockSpec((tm, D), lambda i: (i, 0)),
            pl.BlockSpec((tm, F2), lambda i: (i, 0)),
            pl.BlockSpec((tm, F2), lambda i: (i, 0)),
        ],
        out_shape=[
            jax.ShapeDtypeStruct((S, D), F32),
            jax.ShapeDtypeStruct((S, F2), MX),
            jax.ShapeDtypeStruct((S, F2), MX),
        ],
        scratch_shapes=[
            pltpu.VMEM((D, F2), MX),
            pltpu.VMEM((F, D), MX),
            pltpu.VMEM((SUBLANES, F2), F32),
            pltpu.VMEM((tm, F), MX),
            pltpu.VMEM((N_STAGE, TN // LANES, tm + SUBLANES, LANES), F32),
        ],
        compiler_params=_cp(vmem=FFN_VMEM_LIMIT),
    )(x, g, wup, wdn, cw, cb)


def _ffn_bwd(x, dy, u, uc, g, wup, wdn, cw, layer):
    S = x.shape[0]
    tm = _row_tile(S, 256)
    n_i = S // tm

    def body(x_ref, dy_ref, u_ref, uc_ref, g_ref, wup_hbm, wdn_hbm, cw_ref,
             dx_ref, du_ref, a_ref, h_ref, dg_ref, dcw_ref, dcb_ref, wup_v, wdn_v, carry, stage):
        i = pl.program_id(0)

        @pl.when(i == 0)
        def _():
            pltpu.sync_copy(wup_hbm.at[0], wup_v)
            pltpu.sync_copy(wdn_hbm.at[0], wdn_v)
            carry[...] = jnp.zeros_like(carry)
            dg_ref[...] = jnp.zeros_like(dg_ref)
            dcw_ref[...] = jnp.zeros_like(dcw_ref)
            dcb_ref[...] = jnp.zeros_like(dcb_ref)

        x = x_ref[...]
        gg = g_ref[...]
        h, xh, r = _rms_fwd(x, gg)
        h_ref[...] = h.T.astype(h_ref.dtype)
        dy = dy_ref[...]
        dyb = dy.astype(MX)
        dh, dus = None, ([], [])
        for j in range(F // TN):
            cg = slice(j * TN, (j + 1) * TN)
            cv = slice(F + j * TN, F + (j + 1) * TN)
            gt = uc_ref[:, cg].astype(F32)
            vl = uc_ref[:, cv].astype(F32)
            sg = jax.nn.sigmoid(gt)
            sil = gt * sg
            a_ref[cg, :] = (sil * vl).T.astype(a_ref.dtype)
            da = _dot_nt(dyb, wdn_v[cg, :])
            dvl = da * sil
            dgt = (da * vl) * (sg + sil * (1.0 - sg))
            for cc, dd in ((cg, dgt), (cv, dvl)):
                dp1, dp2 = _rows_after(stage.at[2 * (j % 2) + (cc is cv)], dd, carry[:, cc])
                carry[:, cc] = dd[0:SUBLANES]
                uu = u_ref[:, cc].astype(F32)
                dcb_ref[:, cc] += jnp.sum(dd, axis=0, keepdims=True)
                dcw_ref[0:1, cc] += jnp.sum(dp2 * uu, axis=0, keepdims=True)
                dcw_ref[1:2, cc] += jnp.sum(dp1 * uu, axis=0, keepdims=True)
                dcw_ref[2:3, cc] += jnp.sum(dd * uu, axis=0, keepdims=True)
                cwc = cw_ref[:, cc]
                duu = (cwc[2:3] * dd + cwc[1:2] * dp1 + cwc[0:1] * dp2).astype(MX)
                du_ref[:, cc] = duu
                dus[cc is cv].append(duu)
            if len(dus[0]) == UP_GROUP or j == F // TN - 1:
                first = j + 1 - len(dus[0])
                for side, base in ((0, 0), (1, F)):
                    cols = slice(base + first * TN, base + (j + 1) * TN)
                    part = _dot_nt(jnp.concatenate(dus[side], axis=1), wup_v[:, cols])
                    dh = part if dh is None else dh + part
                dus = ([], [])
        dg_ref[...] += jnp.sum(dh * xh, axis=0, keepdims=True)
        dx_ref[...] = dy + _rms_bwd(dh, xh, r, gg)

    rev = lambda i: (n_i - 1 - i, 0)
    return pl.pallas_call(
        body, name="ffn_bwd",
        grid=(n_i,),
        in_specs=[
            pl.BlockSpec((tm, D), rev),
            pl.BlockSpec((tm, D), rev),
            pl.BlockSpec((tm, F2), rev),
            pl.BlockSpec((tm, F2), rev),
            pl.BlockSpec((None, 1, D), lambda i: (layer, 0, 0)),
            pl.BlockSpec(memory_space=pl.ANY),
            pl.BlockSpec(memory_space=pl.ANY),
            pl.BlockSpec((None, 3, F2), lambda i: (layer, 0, 0)),
        ],
        out_specs=[
            pl.BlockSpec((tm, D), rev),
            pl.BlockSpec((tm, F2), rev),
            pl.BlockSpec((F, tm), lambda i: (0, n_i - 1 - i)),
            pl.BlockSpec((D, tm), lambda i: (0, n_i - 1 - i)),
            pl.BlockSpec((1, D), lambda i: (0, 0)),
            pl.BlockSpec((3, F2), lambda i: (0, 0)),
            pl.BlockSpec((1, F2), lambda i: (0, 0)),
        ],
        out_shape=[
            jax.ShapeDtypeStruct((S, D), F32),
            jax.ShapeDtypeStruct((S, F2), MX),
            jax.ShapeDtypeStruct((F, S), MX),
            jax.ShapeDtypeStruct((D, S), MX),
            jax.ShapeDtypeStruct((1, D), F32),
            jax.ShapeDtypeStruct((3, F2), F32),
            jax.ShapeDtypeStruct((1, F2), F32),
        ],
        scratch_shapes=[
            pltpu.VMEM((D, F2), MX),
            pltpu.VMEM((F, D), MX),
            pltpu.VMEM((SUBLANES, F2), F32),
            pltpu.VMEM((N_STAGE, TN // LANES, tm + SUBLANES, LANES), F32),
        ],
        compiler_params=_cp(vmem=FFN_VMEM_LIMIT),
    )(x, dy, u, uc, g, wup, wdn, cw)


def _tn_matmul(a, b, out_dtype, name, a_is_transposed=False):
    S, N = b.shape
    M = a.shape[0] if a_is_transposed else a.shape[1]
    bn = N
    while M * bn * 4 > ACC_BYTES and bn % (2 * LANES) == 0:
        bn //= 2
    bk = _row_tile(S, 1024)
    nk = S // bk
    a_spec = pl.BlockSpec((M, bk), lambda j, k: (0, k)) if a_is_transposed else pl.BlockSpec((bk, M), lambda j, k: (k, 0))

    def body(a_ref, b_ref, o_ref, acc):
        k = pl.program_id(1)
        if a_is_transposed:
            p = _dot(a_ref[...].astype(MX), b_ref[...].astype(MX))
        else:
            p = _dot_tn(a_ref[...].astype(MX), b_ref[...].astype(MX))

        @pl.when(k == 0)
        def _():
            acc[...] = p

        @pl.when(k > 0)
        def _():
            acc[...] += p

        @pl.when(k == nk - 1)
        def _():
            o_ref[...] = acc[...].astype(o_ref.dtype)

    return pl.pallas_call(
        body, name=name,
        grid=(N // bn, nk),
        in_specs=[
            a_spec,
            pl.BlockSpec((bk, bn), lambda j, k: (k, j)),
        ],
        out_specs=pl.BlockSpec((M, bn), lambda j, k: (0, j)),
        out_shape=jax.ShapeDtypeStruct((M, N), out_dtype),
        scratch_shapes=[pltpu.VMEM((M, bn), F32)],
        compiler_params=_cp(2),
    )(a, b)


def _rms_linear(x, g, w, b, g_layer, b_layer, name):
    S = x.shape[0]
    N = w.shape[-1]
    tm = _row_tile(S, TM_STREAM)

    def body(x_ref, g_ref, w_ref, b_ref, o_ref):
        h, _, _ = _rms_fwd(x_ref[...], g_ref[...])
        o_ref[...] = (_dot(h.astype(MX), w_ref[...]) + b_ref[...]).astype(o_ref.dtype)

    return pl.pallas_call(
        body, name=name,
        grid=(S // tm,),
        in_specs=[
            pl.BlockSpec((tm, D), lambda i: (i, 0)),
            pl.BlockSpec((None, 1, D), lambda i: (g_layer, 0, 0)),
            pl.BlockSpec((None, D, N), lambda i: (0, 0, 0)),
            pl.BlockSpec((None, 1, N), lambda i: (b_layer, 0, 0)),
        ],
        out_specs=pl.BlockSpec((tm, N), lambda i: (i, 0)),
        out_shape=jax.ShapeDtypeStruct((S, N), MX),
        compiler_params=_cp(),
    )(x, g, w, b)


def _linear_res(o, w, b, xres, layer):
    S = o.shape[0]
    tm = _row_tile(S, TM_STREAM)

    def body(o_ref, w_ref, b_ref, x_ref, y_ref):
        y_ref[...] = x_ref[...] + _dot(o_ref[...], w_ref[...]) + b_ref[...]

    return pl.pallas_call(
        body, name="o_proj",
        grid=(S // tm,),
        in_specs=[
            pl.BlockSpec((tm, D), lambda i: (i, 0)),
            pl.BlockSpec((None, D, D), lambda i: (0, 0, 0)),
            pl.BlockSpec((None, 1, D), lambda i: (layer, 0, 0)),
            pl.BlockSpec((tm, D), lambda i: (i, 0)),
        ],
        out_specs=pl.BlockSpec((tm, D), lambda i: (i, 0)),
        out_shape=jax.ShapeDtypeStruct((S, D), F32),
        compiler_params=_cp(),
    )(o, w, b, xres)


def _linear_nt(dy, w):
    S = dy.shape[0]
    tm = _row_tile(S, TM_STREAM)

    def body(dy_ref, w_ref, o_ref, db_ref):
        @pl.when(pl.program_id(0) == 0)
        def _():
            db_ref[...] = jnp.zeros_like(db_ref)

        dy = dy_ref[...]
        db_ref[...] += jnp.sum(dy, axis=0, keepdims=True)
        o_ref[...] = _dot_nt(dy.astype(MX), w_ref[...]).astype(o_ref.dtype)

    return pl.pallas_call(
        body, name="o_proj_bwd",
        grid=(S // tm,),
        in_specs=[
            pl.BlockSpec((tm, D), lambda i: (i, 0)),
            pl.BlockSpec((None, D, D), lambda i: (0, 0, 0)),
        ],
        out_specs=[
            pl.BlockSpec((tm, D), lambda i: (i, 0)),
            pl.BlockSpec((1, D), lambda i: (0, 0)),
        ],
        out_shape=[
            jax.ShapeDtypeStruct((S, D), MX),
            jax.ShapeDtypeStruct((1, D), F32),
        ],
        compiler_params=_cp(),
    )(dy, w)


def _rms_linear_bwd(x, g, dzs, w, dy, g_layer, name):
    S = x.shape[0]
    N = w.shape[-1]
    tm = _row_tile(S, TM_STREAM)
    nz = len(dzs)

    def body(*refs):
        x_ref, g_ref = refs[0], refs[1]
        dz_refs = refs[2:2 + nz]
        w_ref, dy_ref, dx_ref, dg_ref, db_ref, h_ref, dzb_ref = refs[2 + nz:]

        @pl.when(pl.program_id(0) == 0)
        def _():
            dg_ref[...] = jnp.zeros_like(dg_ref)
            db_ref[...] = jnp.zeros_like(db_ref)

        gg = g_ref[...]
        h, xh, r = _rms_fwd(x_ref[...], gg)
        h_ref[...] = h.astype(h_ref.dtype)
        dz = dz_refs[0][...].astype(F32)
        for zr in dz_refs[1:]:
            dz = dz + zr[...].astype(F32)
        db_ref[...] += jnp.sum(dz, axis=0, keepdims=True)
        dzb = dz.astype(MX)
        dzb_ref[...] = dzb
        dh = _dot_nt(dzb, w_ref[...])
        dg_ref[...] += jnp.sum(dh * xh, axis=0, keepdims=True)
        dx_ref[...] = dy_ref[...] + _rms_bwd(dh, xh, r, gg)

    return pl.pallas_call(
        body, name=name,
        grid=(S // tm,),
        in_specs=[
            pl.BlockSpec((tm, D), lambda i: (i, 0)),
            pl.BlockSpec((None, 1, D), lambda i: (g_layer, 0, 0)),
        ] + [pl.BlockSpec((tm, N), lambda i: (i, 0))] * nz + [
            pl.BlockSpec((None, D, N), lambda i: (0, 0, 0)),
            pl.BlockSpec((tm, D), lambda i: (i, 0)),
        ],
        out_specs=[
            pl.BlockSpec((tm, D), lambda i: (i, 0)),
            pl.BlockSpec((1, D), lambda i: (0, 0)),
            pl.BlockSpec((1, N), lambda i: (0, 0)),
            pl.BlockSpec((tm, D), lambda i: (i, 0)),
            pl.BlockSpec((tm, N), lambda i: (i, 0)),
        ],
        out_shape=[
            jax.ShapeDtypeStruct((S, D), F32),
            jax.ShapeDtypeStruct((1, D), F32),
            jax.ShapeDtypeStruct((1, N), F32),
            jax.ShapeDtypeStruct((S, D), MX),
            jax.ShapeDtypeStruct((S, N), MX),
        ],
        compiler_params=_cp(),
    )(x, g, *dzs, w, dy)


def _loss_head(x, g, tgt):
    S = x.shape[0]
    tm = _row_tile(S, TM_STREAM)

    def body(x_ref, g_ref, t_ref, dx_ref, dg_ref, l_ref):
        @pl.when(pl.program_id(0) == 0)
        def _():
            dg_ref[...] = jnp.zeros_like(dg_ref)
            l_ref[...] = jnp.zeros_like(l_ref)

        gg = g_ref[...]
        y, xh, r = _rms_fwd(x_ref[...], gg)
        err = y - t_ref[...]
        tok = jnp.sum(err * err, axis=-1, keepdims=True) * (1.0 / D)
        l_ref[...] += 0.5 * jnp.sum(tok, axis=0, keepdims=True)
        dyv = err * (1.0 / D)
        dg_ref[...] += jnp.sum(dyv * xh, axis=0, keepdims=True)
        dx_ref[...] = _rms_bwd(dyv, xh, r, gg)

    return pl.pallas_call(
        body, name="loss_head",
        grid=(S // tm,),
        in_specs=[
            pl.BlockSpec((tm, D), lambda i: (i, 0)),
            pl.BlockSpec((1, D), lambda i: (0, 0)),
            pl.BlockSpec((tm, D), lambda i: (i, 0)),
        ],
        out_specs=[
            pl.BlockSpec((tm, D), lambda i: (i, 0)),
            pl.BlockSpec((1, D), lambda i: (0, 0)),
            pl.BlockSpec((1, LANES), lambda i: (0, 0)),
        ],
        out_shape=[
            jax.ShapeDtypeStruct((S, D), F32),
            jax.ShapeDtypeStruct((1, D), F32),
            jax.ShapeDtypeStruct((1, LANES), F32),
        ],
        compiler_params=_cp(),
    )(x, g, tgt)


HPG = NH // 2
ROWS = HPG * BLK


def _attn_setup(kvp_ref, kvc_ref, n):
    kw = jnp.concatenate([kvp_ref[...], kvc_ref[...]], axis=0).astype(F32)
    kk, vv = kw[:, :LANES], kw[:, LANES:]
    lo = lax.broadcasted_iota(jnp.int32, (1, LANES), 1) < HD
    kr, vr = pltpu.roll(kk, HD, 1), pltpu.roll(vv, HD, 1)
    ks = [jnp.where(lo, kk, kr).astype(MX), jnp.where(lo, kr, kk).astype(MX)]
    vs = [jnp.where(lo, vv, vr).astype(MX), jnp.where(lo, vr, vv).astype(MX)]
    qi = lax.broadcasted_iota(jnp.int32, (ROWS, 2 * BLK), 0) & (BLK - 1)
    si = lax.broadcasted_iota(jnp.int32, (ROWS, 2 * BLK), 1)
    mask = (si > qi) & (si <= qi + BLK) & jnp.logical_or(n > 0, si >= BLK)
    return ks, vs, lo, mask


def _stack_heads(ref, grp, lo):
    parts = []
    for j in range(4 * grp, 4 * grp + 4):
        slab = ref[:, j * LANES:(j + 1) * LANES]
        zero = jnp.zeros_like(slab)
        parts += [jnp.where(lo, slab, zero), jnp.where(lo, zero, slab)]
    return jnp.concatenate(parts, axis=0)


def _unstack_heads(st, lo):
    return [jnp.where(lo, st[2 * i * BLK:(2 * i + 1) * BLK], st[(2 * i + 1) * BLK:(2 * i + 2) * BLK])
            for i in range(4)]


def _sink_column(sk_ref, layer, grp):
    head = lax.broadcasted_iota(jnp.int32, (ROWS, 1), 0) // BLK
    col = jnp.zeros((ROWS, 1), F32)
    for h in range(HPG):
        col = jnp.where(head == h, sk_ref[layer, HPG * grp + h], col)
    return col


def _attn_probs(qs, kg, mask, sink):
    s = jnp.where(mask, _dot_nt(qs, kg) * SCALE, NEG)
    m = jnp.maximum(jnp.max(s, axis=-1, keepdims=True), sink)
    p = jnp.exp(s - m)
    es = jnp.exp(sink - m)
    inv = 1.0 / (jnp.sum(p, axis=-1, keepdims=True) + es)
    return p * inv, es * inv


def _attn_specs(n_extra_q):
    q_spec = pl.BlockSpec((BLK, D), lambda n: (n, 0))
    return [q_spec] * n_extra_q + [
        pl.BlockSpec((BLK, 4 * HD), lambda n: (jnp.maximum(n - 1, 0), 0)),
        pl.BlockSpec((BLK, 4 * HD), lambda n: (n, 0)),
        pl.BlockSpec(memory_space=pltpu.SMEM),
    ]


def _attn_fwd(q, kv, sinks, layer):
    S = q.shape[0]

    def body(q_ref, kvp_ref, kvc_ref, sk_ref, o_ref):
        n = pl.program_id(0)
        ks, vs, lo, mask = _attn_setup(kvp_ref, kvc_ref, n)
        for grp in range(2):
            qs = _stack_heads(q_ref, grp, lo)
            pr, _ = _attn_probs(qs, ks[grp], mask, _sink_column(sk_ref, layer, grp))
            outs = _unstack_heads(_dot(pr.astype(MX), vs[grp]), lo)
            for i in range(4):
                j = 4 * grp + i
                o_ref[:, j * LANES:(j + 1) * LANES] = outs[i].astype(o_ref.dtype)

    return pl.pallas_call(
        body, name="attn_fwd",
        grid=(S // BLK,),
        in_specs=_attn_specs(1),
        out_specs=pl.BlockSpec((BLK, D), lambda n: (n, 0)),
        out_shape=jax.ShapeDtypeStruct((S, D), MX),
        compiler_params=_cp(),
    )(q, kv, kv, sinks)


def _attn_bwd(q, do, kv, sinks, layer):
    S = q.shape[0]

    def body(q_ref, do_ref, kvp_ref, kvc_ref, sk_ref, dq_ref, dkv_ref, dsk_ref):
        n = pl.program_id(0)

        @pl.when(n == 0)
        def _():
            dkv_ref[...] = jnp.zeros_like(dkv_ref)
            dsk_ref[...] = jnp.zeros_like(dsk_ref)

        ks, vs, lo, mask = _attn_setup(kvp_ref, kvc_ref, n)
        lane = lax.broadcasted_iota(jnp.int32, (1, LANES), 1)
        dsk = jnp.zeros((1, LANES), F32)
        tk, tv = [], []
        for grp in range(2):
            qs = _stack_heads(q_ref, grp, lo)
            dos = _stack_heads(do_ref, grp, lo)
            pr, psink = _attn_probs(qs, ks[grp], mask, _sink_column(sk_ref, layer, grp))
            dpr = _dot_nt(dos, vs[grp])
            delta = jnp.sum(pr * dpr, axis=-1, keepdims=True)
            ds = (pr * (dpr - delta) * SCALE).astype(MX)
            sd = psink * delta
            for h in range(HPG):
                dsk = dsk + jnp.where(lane == HPG * grp + h,
                                      -jnp.sum(sd[h * BLK:(h + 1) * BLK], axis=0, keepdims=True), 0.0)
            dqs = _unstack_heads(_dot(ds, ks[grp]), lo)
            for i in range(4):
                j = 4 * grp + i
                dq_ref[:, j * LANES:(j + 1) * LANES] = dqs[i].astype(dq_ref.dtype)
            dk = _dot_tn(qs, ds).T
            dv = _dot_tn(dos, pr.astype(MX)).T
            tk.append(dk + pltpu.roll(dk, HD, 1))
            tv.append(dv + pltpu.roll(dv, HD, 1))
        dsk_ref[...] += dsk
        contrib = jnp.concatenate([jnp.where(lo, tk[0], tk[1]), jnp.where(lo, tv[0], tv[1])], axis=1)

        @pl.when(n > 0)
        def _():
            rows = pl.ds(pl.multiple_of((n - 1) * BLK, BLK), 2 * BLK)
            dkv_ref[rows, :] += contrib

        @pl.when(n == 0)
        def _():
            dkv_ref[0:BLK, :] += contrib[BLK:]

    return pl.pallas_call(
        body, name="attn_bwd",
        grid=(S // BLK,),
        in_specs=_attn_specs(2),
        out_specs=[
            pl.BlockSpec((BLK, D), lambda n: (n, 0)),
            pl.BlockSpec((S, 4 * HD), lambda n: (0, 0)),
            pl.BlockSpec((1, LANES), lambda n: (0, 0)),
        ],
        out_shape=[
            jax.ShapeDtypeStruct((S, D), MX),
            jax.ShapeDtypeStruct((S, 4 * HD), F32),
            jax.ShapeDtypeStruct((1, LANES), F32),
        ],
        compiler_params=_cp(),
    )(q, do, kv, kv, sinks)


def _ew_rows(rows, cols, n_bufs=1):
    br = rows
    while br * cols * 4 * n_bufs > EW_BYTES and br % (2 * SUBLANES) == 0:
        br //= 2
    return br


def _adamw(parts, w, m, v):
    L, R, C = w.shape
    br = _ew_rows(R, C)
    npart = len(parts[0])

    def body(*refs):
        p_refs = refs[:L * npart]
        w_ref, m_ref, v_ref, g_ref, d_ref, nm_ref, nv_ref = refs[L * npart:]
        lyr = pl.program_id(0)
        for l in range(L):
            @pl.when(lyr == l)
            def _(l=l):
                g = p_refs[l * npart][...]
                for pr in p_refs[l * npart + 1:(l + 1) * npart]:
                    g = g + pr[...]
                nm = B1 * m_ref[...] + (1.0 - B1) * g
                nv = B2 * v_ref[...] + (1.0 - B2) * (g * g)
                m_hat = nm / (1.0 - B1 ** STEP)
                v_hat = nv / (1.0 - B2 ** STEP)
                g_ref[...] = g
                d_ref[...] = -LR * (m_hat / (jnp.sqrt(v_hat) + AEPS) + WD * w_ref[...])
                nm_ref[...] = nm
                nv_ref[...] = nv

    spec = pl.BlockSpec((None, br, C), lambda a, i: (a, i, 0))
    part_specs = [pl.BlockSpec((br, C), lambda a, i, l=l: (jnp.where(a == l, i, 0), 0))
                  for l in range(L) for _ in range(npart)]
    return pl.pallas_call(
        body, name="adamw",
        grid=(L, R // br),
        in_specs=part_specs + [spec] * 3,
        out_specs=[spec] * 4,
        out_shape=[jax.ShapeDtypeStruct((L, R, C), F32)] * 4,
        compiler_params=_cp(2),
    )(*[a for lp in parts for a in lp], w, m, v)


def _coords():
    return lax.axis_index("x"), lax.axis_index("y"), lax.axis_index("c")


def _other_chips(x, y):
    return [(1 - x, y), (x, 1 - y), (1 - x, 1 - y)]


def _slot(ref, axis, chip, size):
    idx = [slice(None)] * 3
    idx[axis] = pl.ds(pl.multiple_of(chip * size, size), size)
    return ref.at[tuple(idx)]


HBM_SPEC = pl.BlockSpec(memory_space=pltpu.HBM)
SEM_SPEC = pl.BlockSpec(memory_space=pltpu.SEMAPHORE)
ANY_SPEC = pl.BlockSpec(memory_space=pl.ANY)
EFFECT = pltpu.SideEffectType.DATAFLOW_SIDE_EFFECTING


def _slot_specs(shape, axis, br, lead):
    _, b, c = shape
    nrb = b // br
    first = (lambda a: a) if lead is None else (lambda a: lead)
    shard = pl.BlockSpec((None, br, c), lambda a, i, me: (first(a), i, 0))
    if axis == 1:
        slot = pl.BlockSpec((None, br, c), lambda a, i, me: (a, me[0] * nrb + i, 0))
    else:
        slot = pl.BlockSpec((None, br, c), lambda a, i, me: (a, i, me[0]))
    return shard, slot


def _shard_rows(b, c):
    br = b
    while br * c * 4 > 2 * EW_BYTES and br % (4 * SUBLANES) == 0:
        br //= 2
    return br


def _gather_place(shard, axis, me, dtype, lead=None):
    a_dim, b, c = shard.shape
    if lead is not None:
        a_dim = 1
    br = _shard_rows(b, c)
    shp = [a_dim, b, c]
    shp[axis] *= 4
    shard_spec, slot_spec = _slot_specs((a_dim, b, c), axis, br, lead)

    def body(me_ref, s_ref, o_ref):
        o_ref[...] = s_ref[...].astype(o_ref.dtype)

    return pl.pallas_call(
        body, name="gather_place",
        grid_spec=pltpu.PrefetchScalarGridSpec(
            num_scalar_prefetch=1, grid=(a_dim, b // br), in_specs=[shard_spec], out_specs=slot_spec),
        out_shape=jax.ShapeDtypeStruct(tuple(shp), dtype),
        compiler_params=_cp(2),
    )(me, shard)


def _sum_landed(grad, landed, axis, me):
    a_dim, b, c = landed.shape[1:]
    br = _shard_rows(b, c)
    shard_spec, slot_spec = _slot_specs((a_dim, b, c), axis, br, None)

    def body(me_ref, own_ref, r_ref, o_ref):
        o_ref[...] = ((own_ref[...].astype(F32) + r_ref[0].astype(F32)) + r_ref[1].astype(F32)) + r_ref[2].astype(F32)

    return pl.pallas_call(
        body, name="sum_landed",
        grid_spec=pltpu.PrefetchScalarGridSpec(
            num_scalar_prefetch=1, grid=(a_dim, b // br),
            in_specs=[slot_spec, pl.BlockSpec((3, None, br, c), lambda a, i, me: (0, a, i, 0))],
            out_specs=shard_spec),
        out_shape=jax.ShapeDtypeStruct((a_dim, b, c), F32),
        compiler_params=_cp(2),
    )(me, grad, landed)


def _copies(refs, plan, send, recv, to_sibling):
    x, y, c = _coords()
    me = 2 * x + y
    if to_sibling:
        targets = [((x, y, 1 - c), me)]
    else:
        targets = [((px, py, c), 2 * px + py) for px, py in _other_chips(x, y)]
    out, t = [], 0
    while plan(refs, me, t, 0, me) is not None:
        for k, (device, peer) in enumerate(targets):
            sv, dv = plan(refs, me, t, k, peer)
            n = len(targets) * t + k
            out.append(pltpu.make_async_remote_copy(
                src_ref=sv, dst_ref=dv, send_sem=send.at[n], recv_sem=recv.at[n],
                device_id=device, device_id_type=MESH))
        t += 1
    return out


def _push_start(name, bufs, n_copies, plan, to_sibling=False):
    nb = len(bufs)

    def body(*refs):
        send, recv, token = refs[nb], refs[nb + 1], refs[-1]
        for cp in _copies(refs[:nb], plan, send, recv, to_sibling):
            cp.start()
        token[...] = jnp.zeros_like(token)

    res = pl.pallas_call(
        body, name=name,
        in_specs=[HBM_SPEC] * nb,
        out_specs=[SEM_SPEC, SEM_SPEC] + [HBM_SPEC] * nb + [pl.BlockSpec(memory_space=pltpu.VMEM)],
        out_shape=[pltpu.SemaphoreType.DMA((n_copies,)), pltpu.SemaphoreType.DMA((n_copies,))]
        + [pltpu.HBM(a.shape, a.dtype) for a in bufs] + [jax.ShapeDtypeStruct((SUBLANES, LANES), F32)],
        input_output_aliases={i: 2 + i for i in range(nb)},
        compiler_params=pltpu.CompilerParams(has_side_effects=EFFECT),
    )(*[pltpu.with_memory_space_constraint(a, pltpu.HBM) for a in bufs])
    return res[0], res[1], res[2:2 + nb], res[-1]


def _push_wait(name, send, recv, bufs, plan, after, to_sibling=False):
    nb = len(bufs)

    def body(*refs):
        for cp in _copies(refs[:nb], plan, refs[nb], refs[nb + 1], to_sibling):
            cp.wait_send()
            cp.wait_recv()

    return pl.pallas_call(
        body, name=name,
        in_specs=[HBM_SPEC] * nb + [SEM_SPEC, SEM_SPEC, ANY_SPEC],
        out_specs=[HBM_SPEC] * nb,
        out_shape=[pltpu.HBM(a.shape, a.dtype) for a in bufs],
        input_output_aliases={i: i for i in range(nb)},
        compiler_params=pltpu.CompilerParams(has_side_effects=EFFECT),
    )(*bufs, send, recv, after)


def _gather_plan(axes):
    def plan(refs, me, t, k, peer):
        if t >= len(axes):
            return None
        size = refs[t].shape[axes[t]] // 4
        mine = _slot(refs[t], axes[t], me, size)
        return mine, mine
    return plan


def _half_slot(ref, axis, chip):
    c = lax.axis_index("c")
    if axis == 1:
        half = ref.shape[1] // 8
        return ref.at[:, pl.ds(pl.multiple_of(chip * 2 * half + c * half, 2 * SUBLANES), half), :]
    half = ref.shape[1] // 2
    size = ref.shape[2] // 4
    return ref.at[:, pl.ds(pl.multiple_of(c * half, 2 * SUBLANES), half), pl.ds(pl.multiple_of(chip * size, LANES), size)]


def _gather_half_plan(axes):
    def plan(refs, me, t, k, peer):
        if t >= len(axes):
            return None
        mine = _half_slot(refs[t], axes[t], me)
        return mine, mine
    return plan


def _gather_pass_plan(axes):
    def plan(refs, me, t, k, peer):
        if t >= 3 * len(axes):
            return None
        x, y, _ = _coords()
        px, py = _other_chips(x, y)[t % 3]
        landed = _half_slot(refs[t // 3], axes[t // 3], 2 * px + py)
        return landed, landed
    return plan


def _scatter_plan(axes):
    n = len(axes)

    def plan(refs, me, t, k, peer):
        if t >= n:
            return None
        size = refs[t].shape[axes[t]] // 4
        return _slot(refs[t], axes[t], peer, size), refs[n + t].at[k]
    return plan


def _swap_plan(n):
    def plan(refs, me, t, k, peer):
        if t >= n:
            return None
        return refs[t], refs[n + t]
    return plan


def _all_reduce_small(v):
    R = v.shape[0]

    def body(v_ref, o_ref, buf, send, recv, local):
        x, y, c = _coords()
        me = 4 * x + 2 * y + c
        cp = pltpu.make_async_copy(v_ref, buf.at[me], local)
        cp.start()
        pushes = []
        for k in range(1, 8):
            peer = (x ^ (k >> 2), y ^ ((k >> 1) & 1), c ^ (k & 1))
            rc = pltpu.make_async_remote_copy(
                src_ref=v_ref, dst_ref=buf.at[me], send_sem=send.at[k - 1], recv_sem=recv.at[k - 1],
                device_id=peer, device_id_type=MESH)
            rc.start()
            pushes.append(rc)
        for k in range(1, 8):
            px, py, pc = x ^ (k >> 2), y ^ ((k >> 1) & 1), c ^ (k & 1)
            pltpu.make_async_remote_copy(
                src_ref=v_ref, dst_ref=buf.at[4 * px + 2 * py + pc], send_sem=send.at[k - 1],
                recv_sem=recv.at[k - 1], device_id=(px, py, pc), device_id_type=MESH).wait_recv()
        for rc in pushes:
            rc.wait_send()
        cp.wait()
        tot = buf[0]
        for k in range(1, 8):
            tot = tot + buf[k]
        o_ref[...] = tot

    vm = pl.BlockSpec(memory_space=pltpu.VMEM)
    return pl.pallas_call(
        body, name="all_reduce_small",
        in_specs=[vm],
        out_specs=vm,
        out_shape=jax.ShapeDtypeStruct((R, LANES), F32),
        scratch_shapes=[
            pltpu.VMEM((8, R, LANES), F32),
            pltpu.SemaphoreType.DMA((7,)),
            pltpu.SemaphoreType.DMA((7,)),
            pltpu.SemaphoreType.DMA,
        ],
        compiler_params=pltpu.CompilerParams(vmem_limit_bytes=VMEM_LIMIT),
    )(v)


def _pack(arrs):
    flat = []
    for a in arrs:
        f = a.reshape(-1).astype(F32)
        flat.append(jnp.pad(f, (0, (-f.shape[0]) % LANES)))
    v = jnp.concatenate(flat)
    v = jnp.pad(v, (0, (-v.shape[0]) % (SUBLANES * LANES)))
    return v.reshape(-1, LANES)


def _unpack(v, shapes):
    flat = v.reshape(-1)
    out, off = [], 0
    for shp in shapes:
        n = 1
        for d in shp:
            n *= d
        out.append(flat[off:off + n].reshape(shp))
        off += n + (-n) % LANES
    return out


def _local_step(x, tgt, sp, weights_for, on_grads):
    n1, n2 = sp["norm1_g"], sp["norm2_g"]
    w = dict(weights_for(0, x))
    saved = []
    xs = x
    kv = None
    for l in range(DEPTH):
        x_in = xs
        if l >= N_A:
            w.update(weights_for(1 + 2 * l - N_A, x_in))
        if l == N_A:
            kv = _rms_linear(x_in, sp["kv_norm_g"], w["w_kv"], sp["b_kv"], 0, 0, "kv_proj")
        if l < N_A:
            xa = _pool_fwd(x_in, n1, w["pool_w"], w["pool_scale"], l)
            q = o = None
        else:
            j = l - N_A
            q = _rms_linear(x_in, n1, w["w_q", j], sp["b_q"], l, j, "q_proj")
            o = _attn_fwd(q, kv, sp["sinks"], j)
            xa = _linear_res(o, w["w_o", j], sp["b_o"], x_in, j)
        w.update(weights_for(1 + l if l < N_A else 2 + 2 * l - N_A, xa))
        xs, u, uc = _ffn_fwd(xa, n2, w["ffn_up", l], w["ffn_down", l], w["ffn_conv_w"], sp["ffn_conv_b"], l)
        saved.append((x_in, xa, u, uc, q, o))

    dx, d_final_g, loss = _loss_head(xs, sp["final_g"], tgt)

    g = {k: [None] * DEPTH for k in ("norm1_g", "norm2_g", "ffn_conv_w", "ffn_conv_b")}
    for k in ("pool_scale", "b_q", "sinks", "b_o"):
        g[k] = [None] * N_A
    g["final_g"] = d_final_g
    dkvs = []
    pending = {}
    for l in reversed(range(DEPTH)):
        x_in, xa, u, uc, q, o = saved[l]
        dxa, du, a, hb, g["norm2_g"][l], g["ffn_conv_w"][l], g["ffn_conv_b"][l] = _ffn_bwd(
            xa, dx, u, uc, n2, w["ffn_up", l], w["ffn_down", l], w["ffn_conv_w"], l)
        pending["ffn_up", l] = _tn_matmul(hb, du, MX, "d_ffn_up", a_is_transposed=True)
        if l == 0:
            n1 = n1 + on_grads(DEPTH + 1, pending)
            pending = {}
        pending["ffn_down", l] = _tn_matmul(a, dx, MX, "d_ffn_down", a_is_transposed=True)
        zero = on_grads(DEPTH - 1 - l, pending)
        pending = {}
        n1, n2 = n1 + zero, n2 + zero
        if l < N_A:
            dx, d_pw, g["pool_scale"][l], g["norm1_g"][l] = _pool_bwd(
                x_in, dxa, n1, w["pool_w"], w["pool_scale"], l)
            pending["pool_w", l] = d_pw.astype(MX)
        else:
            j = l - N_A
            d_o, g["b_o"][j] = _linear_nt(dxa, w["w_o", j])
            pending["w_o", j] = _tn_matmul(o, dxa, MX, "d_w_o")
            dq, dkv, g["sinks"][j] = _attn_bwd(q, d_o, kv, sp["sinks"], j)
            dkvs.append(dkv)
            dx, g["norm1_g"][l], g["b_q"][j], hq, dqb = _rms_linear_bwd(
                x_in, n1, [dq], w["w_q", j], dxa, l, "q_proj_bwd")
            pending["w_q", j] = _tn_matmul(hq, dqb, MX, "d_w_q")
        if l == N_A:
            dx, g["kv_norm_g"], g["b_kv"], hk, dkvb = _rms_linear_bwd(
                x_in, sp["kv_norm_g"], dkvs, w["w_kv"], dx, 0, "kv_proj_bwd")
            pending["w_kv", 0] = _tn_matmul(hk, dkvb, MX, "d_w_kv")
    on_grads(DEPTH, pending)
    return loss, dx, g


SMALL = ("norm1_g", "norm2_g", "kv_norm_g", "b_kv", "b_q", "sinks", "b_o", "ffn_conv_b", "final_g")
SMALL_SHARDED = ("pool_scale", "ffn_conv_w")
BIG = ("pool_w", "w_kv", "w_q", "w_o", "ffn_up", "ffn_down")
EARLY_SWAP = 3
ORDER = ("norm1_g", "norm2_g", "pool_w", "pool_scale", "kv_norm_g", "w_kv", "b_kv", "w_q", "b_q", "sinks",
         "w_o", "b_o", "ffn_up", "ffn_conv_w", "ffn_conv_b", "ffn_down", "final_g")


def _as3d(a):
    return a.reshape((-1,) + a.shape[-2:])


def kernel(x, norm1_g, norm2_g, pool_w, pool_scale, kv_norm_g, w_kv, b_kv, w_q, b_q, sinks, w_o, b_o, ffn_up, ffn_conv_w, ffn_conv_b, ffn_down, final_g, loss_target, m_norm1_g, m_norm2_g, m_pool_w, m_pool_scale, m_kv_norm_g, m_w_kv, m_b_kv, m_w_q, m_b_q, m_sinks, m_w_o, m_b_o, m_ffn_up, m_ffn_conv_w, m_ffn_conv_b, m_ffn_down, m_final_g, v_norm1_g, v_norm2_g, v_pool_w, v_pool_scale, v_kv_norm_g, v_w_kv, v_b_kv, v_w_q, v_b_q, v_sinks, v_w_o, v_b_o, v_ffn_up, v_ffn_conv_w, v_ffn_conv_b, v_ffn_down, v_final_g):
    W = dict(norm1_g=norm1_g, norm2_g=norm2_g, pool_w=pool_w, pool_scale=pool_scale, kv_norm_g=kv_norm_g,
             w_kv=w_kv, b_kv=b_kv, w_q=w_q, b_q=b_q, sinks=sinks, w_o=w_o, b_o=b_o, ffn_up=ffn_up,
             ffn_conv_w=ffn_conv_w, ffn_conv_b=ffn_conv_b, ffn_down=ffn_down, final_g=final_g)
    M = dict(norm1_g=m_norm1_g, norm2_g=m_norm2_g, pool_w=m_pool_w, pool_scale=m_pool_scale,
             kv_norm_g=m_kv_norm_g, w_kv=m_w_kv, b_kv=m_b_kv, w_q=m_w_q, b_q=m_b_q, sinks=m_sinks, w_o=m_w_o,
             b_o=m_b_o, ffn_up=m_ffn_up, ffn_conv_w=m_ffn_conv_w, ffn_conv_b=m_ffn_conv_b, ffn_down=m_ffn_down,
             final_g=m_final_g)
    V = dict(norm1_g=v_norm1_g, norm2_g=v_norm2_g, pool_w=v_pool_w, pool_scale=v_pool_scale,
             kv_norm_g=v_kv_norm_g, w_kv=v_w_kv, b_kv=v_b_kv, w_q=v_w_q, b_q=v_b_q, sinks=v_sinks, w_o=v_w_o,
             b_o=v_b_o, ffn_up=v_ffn_up, ffn_conv_w=v_ffn_conv_w, ffn_conv_b=v_ffn_conv_b, ffn_down=v_ffn_down,
             final_g=v_final_g)
    S = x.shape[1]
    chip = 2 * lax.axis_index("x") + lax.axis_index("y")

    gather_axis = dict(pool_w=1, w_kv=1, w_q=1, w_o=1, ffn_up=2, ffn_down=1, pool_scale=2, ffn_conv_w=2)
    me = chip.reshape(1).astype(jnp.int32)
    axis_of = lambda key: gather_axis[key if isinstance(key, str) else key[0]]

    def placed(key, dtype):
        if isinstance(key, str):
            return _gather_place(_as3d(W[key]), axis_of(key), me, dtype)
        return _gather_place(W[key[0]], axis_of(key), me, dtype, lead=key[1])

    stages = [
        ["pool_w", "pool_scale", "ffn_conv_w"],
        [("ffn_up", 0), ("ffn_down", 0)],
        [("ffn_up", 1), ("ffn_down", 1)],
        ["w_kv", ("w_q", 0), ("w_o", 0)],
        [("ffn_up", 2), ("ffn_down", 2)],
        [("w_q", 1), ("w_o", 1)],
        [("ffn_up", 3), ("ffn_down", 3)],
    ]
    TWO_LEVEL = 1
    gathers, zero = [], 0.0
    for si, keys in enumerate(stages):
        axes = [axis_of(k) for k in keys]
        bufs = [placed(k, F32 if k in SMALL_SHARDED else MX) for k in keys]
        plan = _gather_half_plan(axes) if si == TWO_LEVEL else _gather_plan(axes)
        send, recv, bufs, token = _push_start(f"gather_start_{si}", bufs, 3 * len(keys), plan)
        gathers.append((keys, axes, send, recv, bufs))
        zero = zero + token[0, 0]

    def weights_for(stage, after):
        keys, axes, send, recv, bufs = gathers[stage]
        if stage == TWO_LEVEL:
            bufs = _push_wait(f"gather_wait_{stage}", send, recv, bufs, _gather_half_plan(axes), after)
            send, recv, bufs, _ = _push_start("gather_pass_start", bufs, 3 * len(keys), _gather_pass_plan(axes), True)
            bufs = _push_wait("gather_pass_wait", send, recv, bufs, _gather_pass_plan(axes), after, True)
        else:
            bufs = _push_wait(f"gather_wait_{stage}", send, recv, bufs, _gather_plan(axes), after)
        out = dict(zip(keys, bufs))
        if stage == 0:
            out["pool_w"] = out["pool_w"].reshape(N_A, 4, GC, GC)
            out["pool_scale"] = out["pool_scale"].reshape(N_A, 1, D)
        return out

    scatters = []

    def on_grads(stage, grads):
        keys = list(grads)
        axes = [axis_of(k) for k in keys]
        arrs = [_as3d(grads[k]) for k in keys]
        lands = []
        for a, ax in zip(arrs, axes):
            shp = list(a.shape)
            shp[ax] //= 4
            lands.append(lax.empty((3,) + tuple(shp), a.dtype))
        send, recv, bufs, token = _push_start(f"scatter_start_{stage}", arrs + lands, 3 * len(keys), _scatter_plan(axes))
        scatters.append((stage, keys, axes, send, recv, bufs))
        return token[0, 0]

    sp = dict(
        norm1_g=norm1_g.reshape(DEPTH, 1, D) + zero, norm2_g=norm2_g.reshape(DEPTH, 1, D),
        kv_norm_g=kv_norm_g.reshape(1, 1, D), b_kv=b_kv.reshape(1, 1, 4 * HD), b_q=b_q.reshape(N_B, 1, D),
        sinks=sinks, b_o=b_o.reshape(N_B, 1, D), ffn_conv_b=ffn_conv_b.reshape(DEPTH, 1, F2),
        final_g=final_g.reshape(1, D))

    x2d = x.reshape(S, D)
    loss, grad_x, g = _local_step(x2d, loss_target.reshape(S, D), sp, weights_for, on_grads)

    small_full = dict(
        norm1_g=jnp.stack(g["norm1_g"]), norm2_g=jnp.stack(g["norm2_g"]), kv_norm_g=g["kv_norm_g"],
        b_kv=g["b_kv"], b_q=jnp.stack(g["b_q"]), sinks=jnp.stack([s[0, :NH] for s in g["sinks"]]),
        b_o=jnp.stack(g["b_o"]), ffn_conv_b=jnp.stack(g["ffn_conv_b"]), final_g=g["final_g"],
        pool_scale=jnp.stack(g["pool_scale"]), ffn_conv_w=jnp.stack(g["ffn_conv_w"]))
    small_names = SMALL + SMALL_SHARDED
    small_shapes = [tuple(W[k].shape) for k in SMALL] + [(N_A, D), (DEPTH, 3, F2)]
    packed = _pack([small_full[k] for k in small_names] + [loss])
    red = _unpack(_all_reduce_small(packed), small_shapes + [(1, LANES)])
    red_g = dict(zip(small_names, red[:-1]))
    loss_out = red[-1][0, 0]
    red_g["pool_scale"] = lax.dynamic_slice_in_dim(red_g["pool_scale"], chip * (D // 4), D // 4, axis=1)
    red_g["ffn_conv_w"] = lax.dynamic_slice_in_dim(red_g["ffn_conv_w"], chip * (F2 // 4), F2 // 4, axis=2)
    small_w_shapes = [tuple(W[k].shape) for k in small_names]
    pk = lambda d: _pack([d[k] for k in small_names])[None]
    res = _adamw([[_pack([red_g[k] for k in small_names])]], pk(W), pk(M), pk(V))
    out_g, out_d, out_m, out_v = [dict(zip(small_names, _unpack(r, small_w_shapes))) for r in res]

    pkeys, partial, swaps, after = [], [], [], grad_x

    def swap_start(tag, first):
        mine_now = partial[first:]
        lands = [lax.empty(p.shape, p.dtype) for p in mine_now]
        n = len(mine_now)
        send, recv, bufs, token = _push_start(f"swap_start_{tag}", mine_now + lands, n, _swap_plan(n), True)
        swaps.append((tag, n, send, recv, bufs))
        return token

    for i, (stage, keys, axes, send, recv, bufs) in enumerate(scatters):
        bufs = _push_wait(f"scatter_wait_{stage}", send, recv, bufs, _scatter_plan(axes), after)
        n = len(keys)
        for k, ax, grad, landed in zip(keys, axes, bufs[:n], bufs[n:]):
            pkeys.append(k)
            p_sum = _sum_landed(grad, landed, ax, me)
            partial.append(p_sum.reshape(-1, p_sum.shape[-1]))
        if i == EARLY_SWAP - 1:
            after = swap_start("early", 0)
    after = swap_start("late", sum(n for _, n, *_ in swaps))
    mine, theirs = [], []
    for tag, n, send, recv, bufs in swaps:
        bufs = _push_wait(f"swap_wait_{tag}", send, recv, bufs, _swap_plan(n), after, True)
        mine += bufs[:n]
        theirs += bufs[n:]
    mine = dict(zip(pkeys, mine))
    theirs = dict(zip(pkeys, theirs))
    for k in BIG:
        n_l = len([pk_ for pk_ in pkeys if pk_[0] == k])
        shp = W[k].shape
        rows, cols = mine[k, 0].shape
        three_d = lambda a: a.reshape(n_l, rows, cols)
        res = _adamw([[mine[k, l], theirs[k, l]] for l in range(n_l)], three_d(W[k]), three_d(M[k]), three_d(V[k]))
        out_g[k], out_d[k], out_m[k], out_v[k] = [r.reshape(shp) for r in res]

    return (loss_out, grad_x.reshape(x.shape), *[out_g[k] for k in ORDER], *[out_d[k] for k in ORDER],
            *[out_m[k] for k in ORDER], *[out_v[k] for k in ORDER])
```

```python
import functools

import jax
import jax.numpy as jnp
from jax import lax
from jax.experimental import pallas as pl
from jax.experimental.pallas import tpu as pltpu

D = 1024
DEPTH = 4
N_A = 2
N_B = 2
WINS = (2, 4, 8, 16)
GC = 256
HD = 64
NH = 16
BLK = 128
F = 2816
F2 = 2 * F
EPS = 1e-5
SCALE = HD ** -0.5
NEG = -1e30
HALO = 16
TN = 256
TM_STREAM = 1024
UP_GROUP = 3
LANES = 128
SUBLANES = 8
VMEM_LIMIT = 56 * 1024 * 1024
FFN_VMEM_LIMIT = 60 * 1024 * 1024
ACC_BYTES = 6 * 1024 * 1024
EW_BYTES = 1024 * 1024

LR, B1, B2, AEPS, WD, STEP = 0.001, 0.9, 0.999, 1e-08, 0.01, 10

MX = jnp.bfloat16
F32 = jnp.float32
MESH = pl.DeviceIdType.MESH


def _cp(n_axes=1, vmem=VMEM_LIMIT):
    return pltpu.CompilerParams(dimension_semantics=("arbitrary",) * n_axes, vmem_limit_bytes=vmem)


def _dot(a, b):
    return jnp.dot(a, b, preferred_element_type=F32)


def _dot_nt(a, b):
    return lax.dot_general(a, b, (((1,), (1,)), ((), ())), preferred_element_type=F32)


def _dot_tn(a, b):
    return lax.dot_general(a, b, (((0,), (0,)), ((), ())), preferred_element_type=F32)


def _rms_fwd(x, g):
    r = lax.rsqrt(jnp.mean(x * x, axis=-1, keepdims=True) + EPS)
    xh = x * r
    return xh * g, xh, r


def _rms_bwd(dh, xh, r, g):
    dxh = dh * g
    return r * (dxh - xh * jnp.mean(dxh * xh, axis=-1, keepdims=True))


def _row_tile(s, want):
    return min(s, want)


def _pool_pm(e, h, row, tm):
    out = []
    for gi, win in enumerate(WINS):
        cols = slice(gi * GC, (gi + 1) * GC)
        s = e[:, cols]
        sh = 1
        while sh < win:
            s = s + pltpu.roll(s, sh, 0)
            sh *= 2
        inv = 1.0 / jnp.minimum(row + 1, win).astype(F32)
        out.append(s[HALO:] * inv - h[:, cols])
    return out


def _pool_fwd(x, g, pw, ps, layer):
    S = x.shape[0]
    tm = _row_tile(S, 512)
    hb = tm // HALO

    def body(x_ref, xh_ref, g_ref, pw_ref, ps_ref, o_ref):
        i = pl.program_id(0)
        x = x_ref[...]
        gg = g_ref[...]
        h, _, _ = _rms_fwd(x, gg)
        hh, _, _ = _rms_fwd(xh_ref[...], gg)
        hh = jnp.where(i > 0, hh, 0.0)
        e = jnp.concatenate([hh, h], axis=0)
        row = i * tm + lax.broadcasted_iota(jnp.int32, (tm, 1), 0)
        pm = _pool_pm(e, h, row, tm)
        for gi in range(len(WINS)):
            cols = slice(gi * GC, (gi + 1) * GC)
            z = _dot(pm[gi].astype(MX), pw_ref[gi])
            o_ref[:, cols] = x[:, cols] + z * ps_ref[:, cols]

    return pl.pallas_call(
        body, name="pool_fwd",
        grid=(S // tm,),
        in_specs=[
            pl.BlockSpec((tm, D), lambda i: (i, 0)),
            pl.BlockSpec((HALO, D), lambda i: (jnp.maximum(i * hb - 1, 0), 0)),
            pl.BlockSpec((None, 1, D), lambda i: (layer, 0, 0)),
            pl.BlockSpec((None, 4, GC, GC), lambda i: (layer, 0, 0, 0)),
            pl.BlockSpec((None, 1, D), lambda i: (layer, 0, 0)),
        ],
        out_specs=pl.BlockSpec((tm, D), lambda i: (i, 0)),
        out_shape=jax.ShapeDtypeStruct((S, D), F32),
        compiler_params=_cp(),
    )(x, x, g, pw, ps)


def _pool_bwd(x, dy, g, pw, ps, layer):
    S = x.shape[0]
    tm = _row_tile(S, 256)
    hb = tm // HALO
    n_i = S // tm
    n_h = S // HALO

    def body(x_ref, xh_ref, dy_ref, dyn_ref, g_ref, pw_ref, ps_ref, dx_ref, dpw_ref, dps_ref, dg_ref):
        i = pl.program_id(0)

        @pl.when(i == 0)
        def _():
            dpw_ref[...] = jnp.zeros_like(dpw_ref)
            dps_ref[...] = jnp.zeros_like(dps_ref)
            dg_ref[...] = jnp.zeros_like(dg_ref)

        x = x_ref[...]
        gg = g_ref[...]
        ps = ps_ref[...]
        h, xh, r = _rms_fwd(x, gg)
        hh, _, _ = _rms_fwd(xh_ref[...], gg)
        hh = jnp.where(i > 0, hh, 0.0)
        e = jnp.concatenate([hh, h], axis=0)
        row = i * tm + lax.broadcasted_iota(jnp.int32, (tm, 1), 0)
        rown = (i + 1) * tm + lax.broadcasted_iota(jnp.int32, (HALO, 1), 0)
        pm = _pool_pm(e, h, row, tm)
        dy = dy_ref[...]
        dz = dy * ps
        dzn = jnp.where(i < n_i - 1, dyn_ref[...] * ps, 0.0)
        parts = []
        for gi, win in enumerate(WINS):
            cols = slice(gi * GC, (gi + 1) * GC)
            w = pw_ref[gi]
            pmb = pm[gi].astype(MX)
            z = _dot(pmb, w)
            dps_ref[:, cols] += jnp.sum(dy[:, cols] * z, axis=0, keepdims=True)
            dzb = dz[:, cols].astype(MX)
            dpw_ref[gi] += _dot_tn(pmb, dzb)
            dpm = _dot_nt(dzb, w)
            dpmn = _dot_nt(dzn[:, cols].astype(MX), w)
            q = dpm * (1.0 / jnp.minimum(row + 1, win).astype(F32))
            qn = dpmn * (1.0 / jnp.minimum(rown + 1, win).astype(F32))
            s = jnp.concatenate([q, qn], axis=0)
            sh = 1
            while sh < win:
                s = s + pltpu.roll(s, tm + HALO - sh, 0)
                sh *= 2
            parts.append(s[:tm] - dpm)
        dh = jnp.concatenate(parts, axis=1)
        dg_ref[...] += jnp.sum(dh * xh, axis=0, keepdims=True)
        dx_ref[...] = dy + _rms_bwd(dh, xh, r, gg)

    return pl.pallas_call(
        body, name="pool_bwd",
        grid=(n_i,),
        in_specs=[
            pl.BlockSpec((tm, D), lambda i: (i, 0)),
            pl.BlockSpec((HALO, D), lambda i: (jnp.maximum(i * hb - 1, 0), 0)),
            pl.BlockSpec((tm, D), lambda i: (i, 0)),
            pl.BlockSpec((HALO, D), lambda i: (jnp.minimum((i + 1) * hb, n_h - 1), 0)),
            pl.BlockSpec((None, 1, D), lambda i: (layer, 0, 0)),
            pl.BlockSpec((None, 4, GC, GC), lambda i: (layer, 0, 0, 0)),
            pl.BlockSpec((None, 1, D), lambda i: (layer, 0, 0)),
        ],
        out_specs=[
            pl.BlockSpec((tm, D), lambda i: (i, 0)),
            pl.BlockSpec((4, GC, GC), lambda i: (0, 0, 0)),
            pl.BlockSpec((1, D), lambda i: (0, 0)),
            pl.BlockSpec((1, D), lambda i: (0, 0)),
        ],
        out_shape=[
            jax.ShapeDtypeStruct((S, D), F32),
            jax.ShapeDtypeStruct((4, GC, GC), F32),
            jax.ShapeDtypeStruct((1, D), F32),
            jax.ShapeDtypeStruct((1, D), F32),
        ],
        compiler_params=_cp(),
    )(x, x, dy, dy, g, pw, ps)


N_STAGE = 4


def _rows_before(slot, u, prev8):
    tm = u.shape[0]
    m1, m2 = [], []
    for c in range(TN // LANES):
        lanes = slice(c * LANES, (c + 1) * LANES)
        slot[c, 0:SUBLANES, :] = prev8[:, lanes]
        slot[c, SUBLANES:SUBLANES + tm, :] = u[:, lanes]
        m1.append(slot[c, pl.ds(SUBLANES - 1, tm), :])
        m2.append(slot[c, pl.ds(SUBLANES - 2, tm), :])
    return jnp.concatenate(m1, axis=1), jnp.concatenate(m2, axis=1)


def _rows_after(slot, d, next8):
    tm = d.shape[0]
    p1, p2 = [], []
    for c in range(TN // LANES):
        lanes = slice(c * LANES, (c + 1) * LANES)
        slot[c, 0:tm, :] = d[:, lanes]
        slot[c, tm:tm + SUBLANES, :] = next8[:, lanes]
        p1.append(slot[c, pl.ds(1, tm), :])
        p2.append(slot[c, pl.ds(2, tm), :])
    return jnp.concatenate(p1, axis=1), jnp.concatenate(p2, axis=1)


def _conv(slot, u, prev8, cw):
    um1, um2 = _rows_before(slot, u, prev8)
    return cw[0:1] * um2 + cw[1:2] * um1 + cw[2:3] * u


def _ffn_fwd(x, g, wup, wdn, cw, cb, layer):
    S = x.shape[0]
    tm = _row_tile(S, 512)

    def body(x_ref, g_ref, wup_hbm, wdn_hbm, cw_ref, cb_ref, o_ref, u_ref, uc_ref, wup_v, wdn_v, carry, act, stage):
        i = pl.program_id(0)

        @pl.when(i == 0)
        def _():
            pltpu.sync_copy(wup_hbm.at[0], wup_v)
            pltpu.sync_copy(wdn_hbm.at[0], wdn_v)
            carry[...] = jnp.zeros_like(carry)

        x = x_ref[...]
        h, _, _ = _rms_fwd(x, g_ref[...])
        hb = h.astype(MX)
        for j in range(F // TN):
            cg = slice(j * TN, (j + 1) * TN)
            cv = slice(F + j * TN, F + (j + 1) * TN)
            ug = _dot(hb, wup_v[:, cg])
            uv = _dot(hb, wup_v[:, cv])
            u_ref[:, cg] = ug.astype(u_ref.dtype)
            u_ref[:, cv] = uv.astype(u_ref.dtype)
            gt = _conv(stage.at[2 * (j % 2)], ug, carry[:, cg], cw_ref[:, cg])
            vl = _conv(stage.at[2 * (j % 2) + 1], uv, carry[:, cv], cw_ref[:, cv])
            carry[:, cg] = ug[tm - SUBLANES:]
            carry[:, cv] = uv[tm - SUBLANES:]
            gt = gt + cb_ref[:, cg]
            vl = vl + cb_ref[:, cv]
            uc_ref[:, cg] = gt.astype(uc_ref.dtype)
            uc_ref[:, cv] = vl.astype(uc_ref.dtype)
            act[:, cg] = (gt * jax.nn.sigmoid(gt) * vl).astype(act.dtype)
        o_ref[...] = x + _dot(act[...], wdn_v[...])

    return pl.pallas_call(
        body, name="ffn_fwd",
        grid=(S // tm,),
        in_specs=[
            pl.BlockSpec((tm, D), lambda i: (i, 0)),
            pl.BlockSpec((None, 1, D), lambda i: (layer, 0, 0)),
            pl.BlockSpec(memory_space=pl.ANY),
            pl.BlockSpec(memory_space=pl.ANY),
            pl.BlockSpec((None, 3, F2), lambda i: (layer, 0, 0)),
            pl.BlockSpec((None, 1, F2), lambda i: (layer, 0, 0)),
        ],
        out_specs=[
            pl.BlockSpec((tm, D), lambda i: (i, 0)),
            pl.BlockSpec((tm, F2), lambda i: (i, 0)),
            pl.BlockSpec((tm, F2), lambda i: (i, 0)),
        ],
        out_shape=[
            jax.ShapeDtypeStruct((S, D), F32),
            jax.ShapeDtypeStruct((S, F2), MX),
            jax.ShapeDtypeStruct((S, F2), MX),
        ],
        scratch_shapes=[
            pltpu.VMEM((D, F2), MX),
            pltpu.VMEM((F, D), MX),
            pltpu.VMEM((SUBLANES, F2), F32),
            pltpu.VMEM((tm, F), MX),
            pltpu.VMEM((N_STAGE, TN // LANES, tm + SUBLANES, LANES), F32),
        ],
        compiler_params=_cp(vmem=FFN_VMEM_LIMIT),
    )(x, g, wup, wdn, cw, cb)


def _ffn_bwd(x, dy, u, uc, g, wup, wdn, cw, layer):
    S = x.shape[0]
    tm = _row_tile(S, 256)
    n_i = S // tm

    def body(x_ref, dy_ref, u_ref, uc_ref, g_ref, wup_hbm, wdn_hbm, cw_ref,
             dx_ref, du_ref, a_ref, h_ref, dg_ref, dcw_ref, dcb_ref, wup_v, wdn_v, carry, stage):
        i = pl.program_id(0)

        @pl.when(i == 0)
        def _():
            pltpu.sync_copy(wup_hbm.at[0], wup_v)
            pltpu.sync_copy(wdn_hbm.at[0], wdn_v)
            carry[...] = jnp.zeros_like(carry)
            dg_ref[...] = jnp.zeros_like(dg_ref)
            dcw_ref[...] = jnp.zeros_like(dcw_ref)
            dcb_ref[...] = jnp.zeros_like(dcb_ref)

        x = x_ref[...]
        gg = g_ref[...]
        h, xh, r = _rms_fwd(x, gg)
        h_ref[...] = h.T.astype(h_ref.dtype)
        dy = dy_ref[...]
        dyb = dy.astype(MX)
        dh, dus = None, ([], [])
        for j in range(F // TN):
            cg = slice(j * TN, (j + 1) * TN)
            cv = slice(F + j * TN, F + (j + 1) * TN)
            gt = uc_ref[:, cg].astype(F32)
            vl = uc_ref[:, cv].astype(F32)
            sg = jax.nn.sigmoid(gt)
            sil = gt * sg
            a_ref[cg, :] = (sil * vl).T.astype(a_ref.dtype)
            da = _dot_nt(dyb, wdn_v[cg, :])
            dvl = da * sil
            dgt = (da * vl) * (sg + sil * (1.0 - sg))
            for cc, dd in ((cg, dgt), (cv, dvl)):
                dp1, dp2 = _rows_after(stage.at[2 * (j % 2) + (cc is cv)], dd, carry[:, cc])
                carry[:, cc] = dd[0:SUBLANES]
                uu = u_ref[:, cc].astype(F32)
                dcb_ref[:, cc] += jnp.sum(dd, axis=0, keepdims=True)
                dcw_ref[0:1, cc] += jnp.sum(dp2 * uu, axis=0, keepdims=True)
                dcw_ref[1:2, cc] += jnp.sum(dp1 * uu, axis=0, keepdims=True)
                dcw_ref[2:3, cc] += jnp.sum(dd * uu, axis=0, keepdims=True)
                cwc = cw_ref[:, cc]
                duu = (cwc[2:3] * dd + cwc[1:2] * dp1 + cwc[0:1] * dp2).astype(MX)
                du_ref[:, cc] = duu
                dus[cc is cv].append(duu)
            if len(dus[0]) == UP_GROUP or j == F // TN - 1:
                first = j + 1 - len(dus[0])
                for side, base in ((0, 0), (1, F)):
                    cols = slice(base + first * TN, base + (j + 1) * TN)
                    part = _dot_nt(jnp.concatenate(dus[side], axis=1), wup_v[:, cols])
                    dh = part if dh is None else dh + part
                dus = ([], [])
        dg_ref[...] += jnp.sum(dh * xh, axis=0, keepdims=True)
        dx_ref[...] = dy + _rms_bwd(dh, xh, r, gg)

    rev = lambda i: (n_i - 1 - i, 0)
    return pl.pallas_call(
        body, name="ffn_bwd",
        grid=(n_i,),
        in_specs=[
            pl.BlockSpec((tm, D), rev),
            pl.BlockSpec((tm, D), rev),
            pl.BlockSpec((tm, F2), rev),
            pl.BlockSpec((tm, F2), rev),
            pl.BlockSpec((None, 1, D), lambda i: (layer, 0, 0)),
            pl.BlockSpec(memory_space=pl.ANY),
            pl.BlockSpec(memory_space=pl.ANY),
            pl.BlockSpec((None, 3, F2), lambda i: (layer, 0, 0)),
        ],
        out_specs=[
            pl.BlockSpec((tm, D), rev),
            pl.BlockSpec((tm, F2), rev),
            pl.BlockSpec((F, tm), lambda i: (0, n_i - 1 - i)),
            pl.BlockSpec((D, tm), lambda i: (0, n_i - 1 - i)),
            pl.BlockSpec((1, D), lambda i: (0, 0)),
            pl.BlockSpec((3, F2), lambda i: (0, 0)),
            pl.BlockSpec((1, F2), lambda i: (0, 0)),
        ],
        out_shape=[
            jax.ShapeDtypeStruct((S, D), F32),
            jax.ShapeDtypeStruct((S, F2), MX),
            jax.ShapeDtypeStruct((F, S), MX),
            jax.ShapeDtypeStruct((D, S), MX),
            jax.ShapeDtypeStruct((1, D), F32),
            jax.ShapeDtypeStruct((3, F2), F32),
            jax.ShapeDtypeStruct((1, F2), F32),
        ],
        scratch_shapes=[
            pltpu.VMEM((D, F2), MX),
            pltpu.VMEM((F, D), MX),
            pltpu.VMEM((SUBLANES, F2), F32),
            pltpu.VMEM((N_STAGE, TN // LANES, tm + SUBLANES, LANES), F32),
        ],
        compiler_params=_cp(vmem=FFN_VMEM_LIMIT),
    )(x, dy, u, uc, g, wup, wdn, cw)


def _tn_matmul(a, b, out_dtype, name, a_is_transposed=False):
    S, N = b.shape
    M = a.shape[0] if a_is_transposed else a.shape[1]
    bn = N
    while M * bn * 4 > ACC_BYTES and bn % (2 * LANES) == 0:
        bn //= 2
    bk = _row_tile(S, 1024)
    nk = S // bk
    a_spec = pl.BlockSpec((M, bk), lambda j, k: (0, k)) if a_is_transposed else pl.BlockSpec((bk, M), lambda j, k: (k, 0))

    def body(a_ref, b_ref, o_ref, acc):
        k = pl.program_id(1)
        if a_is_transposed:
            p = _dot(a_ref[...].astype(MX), b_ref[...].astype(MX))
        else:
            p = _dot_tn(a_ref[...].astype(MX), b_ref[...].astype(MX))

        @pl.when(k == 0)
        def _():
            acc[...] = p

        @pl.when(k > 0)
        def _():
            acc[...] += p

        @pl.when(k == nk - 1)
        def _():
            o_ref[...] = acc[...].astype(o_ref.dtype)

    return pl.pallas_call(
        body, name=name,
        grid=(N // bn, nk),
        in_specs=[
            a_spec,
            pl.BlockSpec((bk, bn), lambda j, k: (k, j)),
        ],
        out_specs=pl.BlockSpec((M, bn), lambda j, k: (0, j)),
        out_shape=jax.ShapeDtypeStruct((M, N), out_dtype),
        scratch_shapes=[pltpu.VMEM((M, bn), F32)],
        compiler_params=_cp(2),
    )(a, b)


def _rms_linear(x, g, w, b, g_layer, b_layer, name):
    S = x.shape[0]
    N = w.shape[-1]
    tm = _row_tile(S, TM_STREAM)

    def body(x_ref, g_ref, w_ref, b_ref, o_ref):
        h, _, _ = _rms_fwd(x_ref[...], g_ref[...])
        o_ref[...] = (_dot(h.astype(MX), w_ref[...]) + b_ref[...]).astype(o_ref.dtype)

    return pl.pallas_call(
        body, name=name,
        grid=(S // tm,),
        in_specs=[
            pl.BlockSpec((tm, D), lambda i: (i, 0)),
            pl.BlockSpec((None, 1, D), lambda i: (g_layer, 0, 0)),
            pl.BlockSpec((None, D, N), lambda i: (0, 0, 0)),
            pl.BlockSpec((None, 1, N), lambda i: (b_layer, 0, 0)),
        ],
        out_specs=pl.BlockSpec((tm, N), lambda i: (i, 0)),
        out_shape=jax.ShapeDtypeStruct((S, N), MX),
        compiler_params=_cp(),
    )(x, g, w, b)


def _linear_res(o, w, b, xres, layer):
    S = o.shape[0]
    tm = _row_tile(S, TM_STREAM)

    def body(o_ref, w_ref, b_ref, x_ref, y_ref):
        y_ref[...] = x_ref[...] + _dot(o_ref[...], w_ref[...]) + b_ref[...]

    return pl.pallas_call(
        body, name="o_proj",
        grid=(S // tm,),
        in_specs=[
            pl.BlockSpec((tm, D), lambda i: (i, 0)),
            pl.BlockSpec((None, D, D), lambda i: (0, 0, 0)),
            pl.BlockSpec((None, 1, D), lambda i: (layer, 0, 0)),
            pl.BlockSpec((tm, D), lambda i: (i, 0)),
        ],
        out_specs=pl.BlockSpec((tm, D), lambda i: (i, 0)),
        out_shape=jax.ShapeDtypeStruct((S, D), F32),
        compiler_params=_cp(),
    )(o, w, b, xres)


def _linear_nt(dy, w):
    S = dy.shape[0]
    tm = _row_tile(S, TM_STREAM)

    def body(dy_ref, w_ref, o_ref, db_ref):
        @pl.when(pl.program_id(0) == 0)
        def _():
            db_ref[...] = jnp.zeros_like(db_ref)

        dy = dy_ref[...]
        db_ref[...] += jnp.sum(dy, axis=0, keepdims=True)
        o_ref[...] = _dot_nt(dy.astype(MX), w_ref[...]).astype(o_ref.dtype)

    return pl.pallas_call(
        body, name="o_proj_bwd",
        grid=(S // tm,),
        in_specs=[
            pl.BlockSpec((tm, D), lambda i: (i, 0)),
            pl.BlockSpec((None, D, D), lambda i: (0, 0, 0)),
        ],
        out_specs=[
            pl.BlockSpec((tm, D), lambda i: (i, 0)),
            pl.BlockSpec((1, D), lambda i: (0, 0)),
        ],
        out_shape=[
            jax.ShapeDtypeStruct((S, D), MX),
            jax.ShapeDtypeStruct((1, D), F32),
        ],
        compiler_params=_cp(),
    )(dy, w)


def _rms_linear_bwd(x, g, dzs, w, dy, g_layer, name):
    S = x.shape[0]
    N = w.shape[-1]
    tm = _row_tile(S, TM_STREAM)
    nz = len(dzs)

    def body(*refs):
        x_ref, g_ref = refs[0], refs[1]
        dz_refs = refs[2:2 + nz]
        w_ref, dy_ref, dx_ref, dg_ref, db_ref, h_ref, dzb_ref = refs[2 + nz:]

        @pl.when(pl.program_id(0) == 0)
        def _():
            dg_ref[...] = jnp.zeros_like(dg_ref)
            db_ref[...] = jnp.zeros_like(db_ref)

        gg = g_ref[...]
        h, xh, r = _rms_fwd(x_ref[...], gg)
        h_ref[...] = h.astype(h_ref.dtype)
        dz = dz_refs[0][...].astype(F32)
        for zr in dz_refs[1:]:
            dz = dz + zr[...].astype(F32)
        db_ref[...] += jnp.sum(dz, axis=0, keepdims=True)
        dzb = dz.astype(MX)
        dzb_ref[...] = dzb
        dh = _dot_nt(dzb, w_ref[...])
        dg_ref[...] += jnp.sum(dh * xh, axis=0, keepdims=True)
        dx_ref[...] = dy_ref[...] + _rms_bwd(dh, xh, r, gg)

    return pl.pallas_call(
        body, name=name,
        grid=(S // tm,),
        in_specs=[
            pl.BlockSpec((tm, D), lambda i: (i, 0)),
            pl.BlockSpec((None, 1, D), lambda i: (g_layer, 0, 0)),
        ] + [pl.BlockSpec((tm, N), lambda i: (i, 0))] * nz + [
            pl.BlockSpec((None, D, N), lambda i: (0, 0, 0)),
            pl.BlockSpec((tm, D), lambda i: (i, 0)),
        ],
        out_specs=[
            pl.BlockSpec((tm, D), lambda i: (i, 0)),
            pl.BlockSpec((1, D), lambda i: (0, 0)),
            pl.BlockSpec((1, N), lambda i: (0, 0)),
            pl.BlockSpec((tm, D), lambda i: (i, 0)),
            pl.BlockSpec((tm, N), lambda i: (i, 0)),
        ],
        out_shape=[
            jax.ShapeDtypeStruct((S, D), F32),
            jax.ShapeDtypeStruct((1, D), F32),
            jax.ShapeDtypeStruct((1, N), F32),
            jax.ShapeDtypeStruct((S, D), MX),
            jax.ShapeDtypeStruct((S, N), MX),
        ],
        compiler_params=_cp(),
    )(x, g, *dzs, w, dy)


def _loss_head(x, g, tgt):
    S = x.shape[0]
    tm = _row_tile(S, TM_STREAM)

    def body(x_ref, g_ref, t_ref, dx_ref, dg_ref, l_ref):
        @pl.when(pl.program_id(0) == 0)
        def _():
            dg_ref[...] = jnp.zeros_like(dg_ref)
            l_ref[...] = jnp.zeros_like(l_ref)

        gg = g_ref[...]
        y, xh, r = _rms_fwd(x_ref[...], gg)
        err = y - t_ref[...]
        tok = jnp.sum(err * err, axis=-1, keepdims=True) * (1.0 / D)
        l_ref[...] += 0.5 * jnp.sum(tok, axis=0, keepdims=True)
        dyv = err * (1.0 / D)
        dg_ref[...] += jnp.sum(dyv * xh, axis=0, keepdims=True)
        dx_ref[...] = _rms_bwd(dyv, xh, r, gg)

    return pl.pallas_call(
        body, name="loss_head",
        grid=(S // tm,),
        in_specs=[
            pl.BlockSpec((tm, D), lambda i: (i, 0)),
            pl.BlockSpec((1, D), lambda i: (0, 0)),
            pl.BlockSpec((tm, D), lambda i: (i, 0)),
        ],
        out_specs=[
            pl.BlockSpec((tm, D), lambda i: (i, 0)),
            pl.BlockSpec((1, D), lambda i: (0, 0)),
            pl.BlockSpec((1, LANES), lambda i: (0, 0)),
        ],
        out_shape=[
            jax.ShapeDtypeStruct((S, D), F32),
            jax.ShapeDtypeStruct((1, D), F32),
            jax.ShapeDtypeStruct((1, LANES), F32),
        ],
        compiler_params=_cp(),
    )(x, g, tgt)


HPG = NH // 2
QH = BLK // 2
KW = BLK + QH
COLS = HPG * QH


def _attn_setup(kvp_ref, kvc_ref):
    kw = jnp.concatenate([kvp_ref[...], kvc_ref[...]], axis=0).astype(F32)
    kk, vv = kw[:, :LANES], kw[:, LANES:]
    lo = lax.broadcasted_iota(jnp.int32, (1, LANES), 1) < HD
    kr, vr = pltpu.roll(kk, HD, 1), pltpu.roll(vv, HD, 1)
    ks = [jnp.where(lo, kk, kr).astype(MX), jnp.where(lo, kr, kk).astype(MX)]
    vs = [jnp.where(lo, vv, vr).astype(MX), jnp.where(lo, vr, vv).astype(MX)]
    return ks, vs, lo


def _stack_heads(ref, grp, lo, rows):
    parts = []
    for j in range(4 * grp, 4 * grp + 4):
        slab = ref[rows, j * LANES:(j + 1) * LANES]
        zero = jnp.zeros_like(slab)
        parts += [jnp.where(lo, slab, zero), jnp.where(lo, zero, slab)]
    return jnp.concatenate(parts, axis=0)


def _unstack_heads(st, lo):
    nq = st.shape[0] // HPG
    return [jnp.where(lo, st[2 * i * nq:(2 * i + 1) * nq], st[(2 * i + 1) * nq:(2 * i + 2) * nq])
            for i in range(4)]


def _attn_probs(qs, kg, n, sk_ref, layer, grp):
    rows = HPG * BLK
    qi = lax.broadcasted_iota(jnp.int32, (rows, 2 * BLK), 0) & (BLK - 1)
    si = lax.broadcasted_iota(jnp.int32, (rows, 2 * BLK), 1)
    ok = (si > qi) & (si <= qi + BLK) & jnp.logical_or(n > 0, si >= BLK)
    head = lax.broadcasted_iota(jnp.int32, (rows, 1), 0) // BLK
    sink = jnp.zeros((rows, 1), F32)
    for h in range(HPG):
        sink = jnp.where(head == h, sk_ref[layer, HPG * grp + h], sink)
    s = jnp.where(ok, _dot_nt(qs, kg) * SCALE, NEG)
    m = jnp.maximum(jnp.max(s, axis=-1, keepdims=True), sink)
    p = jnp.exp(s - m)
    return p * (1.0 / (jnp.sum(p, axis=-1, keepdims=True) + jnp.exp(sink - m)))


def _attn_mask_t(koff, n):
    si = lax.broadcasted_iota(jnp.int32, (KW, COLS), 0)
    qi = lax.broadcasted_iota(jnp.int32, (KW, COLS), 1) & (QH - 1)
    return (si > qi) & (si <= qi + BLK) & jnp.logical_or(n > 0, si >= BLK - koff)


def _attn_probs_t(qs, kg, ok, sk_ref, layer, grp):
    s = jnp.where(ok, _dot_nt(kg, qs) * SCALE, NEG)
    head = lax.broadcasted_iota(jnp.int32, (1, COLS), 1) // QH
    sink = jnp.zeros((1, COLS), F32)
    for h in range(HPG):
        sink = jnp.where(head == h, sk_ref[layer, HPG * grp + h], sink)
    m = jnp.maximum(jnp.max(s, axis=0, keepdims=True), sink)
    p = jnp.exp(s - m)
    es = jnp.exp(sink - m)
    inv = 1.0 / (jnp.sum(p, axis=0, keepdims=True) + es)
    return p * inv, es * inv, head


def _attn_specs(n_extra_q):
    q_spec = pl.BlockSpec((BLK, D), lambda n: (n, 0))
    return [q_spec] * n_extra_q + [
        pl.BlockSpec((BLK, 4 * HD), lambda n: (jnp.maximum(n - 1, 0), 0)),
        pl.BlockSpec((BLK, 4 * HD), lambda n: (n, 0)),
        pl.BlockSpec(memory_space=pltpu.SMEM),
    ]


def _attn_fwd(q, kv, sinks, layer):
    S = q.shape[0]

    def body(q_ref, kvp_ref, kvc_ref, sk_ref, o_ref):
        n = pl.program_id(0)
        ks, vs, lo = _attn_setup(kvp_ref, kvc_ref)
        for grp in range(2):
            qs = _stack_heads(q_ref, grp, lo, slice(None))
            pr = _attn_probs(qs, ks[grp], n, sk_ref, layer, grp)
            outs = _unstack_heads(_dot(pr.astype(MX), vs[grp]), lo)
            for i in range(4):
                j = 4 * grp + i
                o_ref[:, j * LANES:(j + 1) * LANES] = outs[i].astype(o_ref.dtype)

    return pl.pallas_call(
        body, name="attn_fwd",
        grid=(S // BLK,),
        in_specs=_attn_specs(1),
        out_specs=pl.BlockSpec((BLK, D), lambda n: (n, 0)),
        out_shape=jax.ShapeDtypeStruct((S, D), MX),
        compiler_params=_cp(),
    )(q, kv, kv, sinks)


def _attn_bwd(q, do, kv, sinks, layer):
    S = q.shape[0]

    def body(q_ref, do_ref, kvp_ref, kvc_ref, sk_ref, dq_ref, dkv_ref, dsk_ref):
        n = pl.program_id(0)

        @pl.when(n == 0)
        def _():
            dkv_ref[...] = jnp.zeros_like(dkv_ref)
            dsk_ref[...] = jnp.zeros_like(dsk_ref)

        ks, vs, lo = _attn_setup(kvp_ref, kvc_ref)
        lane = lax.broadcasted_iota(jnp.int32, (1, LANES), 1)
        dsk = jnp.zeros((1, LANES), F32)
        dk = [jnp.zeros((2 * BLK, LANES), F32) for _ in range(2)]
        dv = [jnp.zeros((2 * BLK, LANES), F32) for _ in range(2)]
        for half in range(2):
            rows = slice(half * QH, (half + 1) * QH)
            koff = half * QH
            ok = _attn_mask_t(koff, n)
            above = [jnp.zeros((koff, LANES), F32)] if koff else []
            below = [jnp.zeros((2 * BLK - KW - koff, LANES), F32)] if 2 * BLK - KW - koff else []
            for grp in range(2):
                qs = _stack_heads(q_ref, grp, lo, rows)
                dos = _stack_heads(do_ref, grp, lo, rows)
                kg, vg = ks[grp][koff:koff + KW], vs[grp][koff:koff + KW]
                pr, psink, head = _attn_probs_t(qs, kg, ok, sk_ref, layer, grp)
                dpr = _dot_nt(vg, dos)
                delta = jnp.sum(pr * dpr, axis=0, keepdims=True)
                ds = (pr * (dpr - delta) * SCALE).astype(MX)
                sd = psink * delta
                for h in range(HPG):
                    dsk = dsk + jnp.where(lane == HPG * grp + h,
                                          -jnp.sum(jnp.where(head == h, sd, 0.0), axis=1, keepdims=True), 0.0)
                dqs = _unstack_heads(_dot_tn(kg, ds).T, lo)
                for i in range(4):
                    j = 4 * grp + i
                    dq_ref[rows, j * LANES:(j + 1) * LANES] = dqs[i].astype(dq_ref.dtype)
                dk[grp] = dk[grp] + jnp.concatenate(above + [_dot(ds, qs)] + below, axis=0)
                dv[grp] = dv[grp] + jnp.concatenate(above + [_dot(pr.astype(MX), dos)] + below, axis=0)
        dsk_ref[...] += dsk
        tk = [a + pltpu.roll(a, HD, 1) for a in dk]
        tv = [a + pltpu.roll(a, HD, 1) for a in dv]
        contrib = jnp.concatenate([jnp.where(lo, tk[0], tk[1]), jnp.where(lo, tv[0], tv[1])], axis=1)

        @pl.when(n > 0)
        def _():
            rows = pl.ds(pl.multiple_of((n - 1) * BLK, BLK), 2 * BLK)
            dkv_ref[rows, :] += contrib

        @pl.when(n == 0)
        def _():
            dkv_ref[0:BLK, :] += contrib[BLK:]

    return pl.pallas_call(
        body, name="attn_bwd",
        grid=(S // BLK,),
        in_specs=_attn_specs(2),
        out_specs=[
            pl.BlockSpec((BLK, D), lambda n: (n, 0)),
            pl.BlockSpec((S, 4 * HD), lambda n: (0, 0)),
            pl.BlockSpec((1, LANES), lambda n: (0, 0)),
        ],
        out_shape=[
            jax.ShapeDtypeStruct((S, D), MX),
            jax.ShapeDtypeStruct((S, 4 * HD), F32),
            jax.ShapeDtypeStruct((1, LANES), F32),
        ],
        compiler_params=_cp(),
    )(q, do, kv, kv, sinks)


def _ew_rows(rows, cols, n_bufs=1):
    br = rows
    while br * cols * 4 * n_bufs > EW_BYTES and br % (2 * SUBLANES) == 0:
        br //= 2
    return br


def _adamw(parts, w, m, v):
    L, R, C = w.shape
    br = _ew_rows(R, C)
    npart = len(parts[0])

    def body(*refs):
        p_refs = refs[:L * npart]
        w_ref, m_ref, v_ref, g_ref, d_ref, nm_ref, nv_ref = refs[L * npart:]
        lyr = pl.program_id(0)
        for l in range(L):
            @pl.when(lyr == l)
            def _(l=l):
                g = p_refs[l * npart][...]
                for pr in p_refs[l * npart + 1:(l + 1) * npart]:
                    g = g + pr[...]
                nm = B1 * m_ref[...] + (1.0 - B1) * g
                nv = B2 * v_ref[...] + (1.0 - B2) * (g * g)
                m_hat = nm / (1.0 - B1 ** STEP)
                v_hat = nv / (1.0 - B2 ** STEP)
                g_ref[...] = g
                d_ref[...] = -LR * (m_hat / (jnp.sqrt(v_hat) + AEPS) + WD * w_ref[...])
                nm_ref[...] = nm
                nv_ref[...] = nv

    spec = pl.BlockSpec((None, br, C), lambda a, i: (a, i, 0))
    part_specs = [pl.BlockSpec((br, C), lambda a, i, l=l: (jnp.where(a == l, i, 0), 0))
                  for l in range(L) for _ in range(npart)]
    return pl.pallas_call(
        body, name="adamw",
        grid=(L, R // br),
        in_specs=part_specs + [spec] * 3,
        out_specs=[spec] * 4,
        out_shape=[jax.ShapeDtypeStruct((L, R, C), F32)] * 4,
        compiler_params=_cp(2),
    )(*[a for lp in parts for a in lp], w, m, v)


def _coords():
    return lax.axis_index("x"), lax.axis_index("y"), lax.axis_index("c")


def _other_chips(x, y):
    return [(1 - x, y), (x, 1 - y), (1 - x, 1 - y)]


def _slot(ref, axis, chip, size):
    idx = [slice(None)] * 3
    idx[axis] = pl.ds(pl.multiple_of(chip * size, size), size)
    return ref.at[tuple(idx)]


HBM_SPEC = pl.BlockSpec(memory_space=pltpu.HBM)
SEM_SPEC = pl.BlockSpec(memory_space=pltpu.SEMAPHORE)
ANY_SPEC = pl.BlockSpec(memory_space=pl.ANY)
EFFECT = pltpu.SideEffectType.DATAFLOW_SIDE_EFFECTING


def _slot_specs(shape, axis, br, lead):
    _, b, c = shape
    nrb = b // br
    first = (lambda a: a) if lead is None else (lambda a: lead)
    shard = pl.BlockSpec((None, br, c), lambda a, i, me: (first(a), i, 0))
    if axis == 1:
        slot = pl.BlockSpec((None, br, c), lambda a, i, me: (a, me[0] * nrb + i, 0))
    else:
        slot = pl.BlockSpec((None, br, c), lambda a, i, me: (a, i, me[0]))
    return shard, slot


def _shard_rows(b, c):
    br = b
    while br * c * 4 > 2 * EW_BYTES and br % (4 * SUBLANES) == 0:
        br //= 2
    return br


def _gather_place(shard, axis, me, dtype, lead=None):
    a_dim, b, c = shard.shape
    if lead is not None:
        a_dim = 1
    br = _shard_rows(b, c)
    shp = [a_dim, b, c]
    shp[axis] *= 4
    shard_spec, slot_spec = _slot_specs((a_dim, b, c), axis, br, lead)

    def body(me_ref, s_ref, o_ref):
        o_ref[...] = s_ref[...].astype(o_ref.dtype)

    return pl.pallas_call(
        body, name="gather_place",
        grid_spec=pltpu.PrefetchScalarGridSpec(
            num_scalar_prefetch=1, grid=(a_dim, b // br), in_specs=[shard_spec], out_specs=slot_spec),
        out_shape=jax.ShapeDtypeStruct(tuple(shp), dtype),
        compiler_params=_cp(2),
    )(me, shard)


def _sum_landed(grad, landed, axis, me):
    a_dim, b, c = landed.shape[1:]
    br = _shard_rows(b, c)
    shard_spec, slot_spec = _slot_specs((a_dim, b, c), axis, br, None)

    def body(me_ref, own_ref, r_ref, o_ref):
        o_ref[...] = ((own_ref[...].astype(F32) + r_ref[0].astype(F32)) + r_ref[1].astype(F32)) + r_ref[2].astype(F32)

    return pl.pallas_call(
        body, name="sum_landed",
        grid_spec=pltpu.PrefetchScalarGridSpec(
            num_scalar_prefetch=1, grid=(a_dim, b // br),
            in_specs=[slot_spec, pl.BlockSpec((3, None, br, c), lambda a, i, me: (0, a, i, 0))],
            out_specs=shard_spec),
        out_shape=jax.ShapeDtypeStruct((a_dim, b, c), F32),
        compiler_params=_cp(2),
    )(me, grad, landed)


def _copies(refs, plan, send, recv, to_sibling):
    x, y, c = _coords()
    me = 2 * x + y
    if to_sibling:
        targets = [((x, y, 1 - c), me)]
    else:
        targets = [((px, py, c), 2 * px + py) for px, py in _other_chips(x, y)]
    out, t = [], 0
    while plan(refs, me, t, 0, me) is not None:
        for k, (device, peer) in enumerate(targets):
            sv, dv = plan(refs, me, t, k, peer)
            n = len(targets) * t + k
            out.append(pltpu.make_async_remote_copy(
                src_ref=sv, dst_ref=dv, send_sem=send.at[n], recv_sem=recv.at[n],
                device_id=device, device_id_type=MESH))
        t += 1
    return out


def _push_start(name, bufs, n_copies, plan, to_sibling=False):
    nb = len(bufs)

    def body(*refs):
        send, recv, token = refs[nb], refs[nb + 1], refs[-1]
        for cp in _copies(refs[:nb], plan, send, recv, to_sibling):
            cp.start()
        token[...] = jnp.zeros_like(token)

    res = pl.pallas_call(
        body, name=name,
        in_specs=[HBM_SPEC] * nb,
        out_specs=[SEM_SPEC, SEM_SPEC] + [HBM_SPEC] * nb + [pl.BlockSpec(memory_space=pltpu.VMEM)],
        out_shape=[pltpu.SemaphoreType.DMA((n_copies,)), pltpu.SemaphoreType.DMA((n_copies,))]
        + [pltpu.HBM(a.shape, a.dtype) for a in bufs] + [jax.ShapeDtypeStruct((SUBLANES, LANES), F32)],
        input_output_aliases={i: 2 + i for i in range(nb)},
        compiler_params=pltpu.CompilerParams(has_side_effects=EFFECT),
    )(*[pltpu.with_memory_space_constraint(a, pltpu.HBM) for a in bufs])
    return res[0], res[1], res[2:2 + nb], res[-1]


def _push_wait(name, send, recv, bufs, plan, after, to_sibling=False):
    nb = len(bufs)

    def body(*refs):
        for cp in _copies(refs[:nb], plan, refs[nb], refs[nb + 1], to_sibling):
            cp.wait_send()
            cp.wait_recv()

    return pl.pallas_call(
        body, name=name,
        in_specs=[HBM_SPEC] * nb + [SEM_SPEC, SEM_SPEC, ANY_SPEC],
        out_specs=[HBM_SPEC] * nb,
        out_shape=[pltpu.HBM(a.shape, a.dtype) for a in bufs],
        input_output_aliases={i: i for i in range(nb)},
        compiler_params=pltpu.CompilerParams(has_side_effects=EFFECT),
    )(*bufs, send, recv, after)


def _gather_plan(axes):
    def plan(refs, me, t, k, peer):
        if t >= len(axes):
            return None
        size = refs[t].shape[axes[t]] // 4
        mine = _slot(refs[t], axes[t], me, size)
        return mine, mine
    return plan


def _half_slot(ref, axis, chip):
    c = lax.axis_index("c")
    if axis == 1:
        half = ref.shape[1] // 8
        return ref.at[:, pl.ds(pl.multiple_of(chip * 2 * half + c * half, 2 * SUBLANES), half), :]
    half = ref.shape[1] // 2
    size = ref.shape[2] // 4
    return ref.at[:, pl.ds(pl.multiple_of(c * half, 2 * SUBLANES), half), pl.ds(pl.multiple_of(chip * size, LANES), size)]


def _gather_half_plan(axes):
    def plan(refs, me, t, k, peer):
        if t >= len(axes):
            return None
        mine = _half_slot(refs[t], axes[t], me)
        return mine, mine
    return plan


def _gather_pass_plan(axes):
    def plan(refs, me, t, k, peer):
        if t >= 3 * len(axes):
            return None
        x, y, _ = _coords()
        px, py = _other_chips(x, y)[t % 3]
        landed = _half_slot(refs[t // 3], axes[t // 3], 2 * px + py)
        return landed, landed
    return plan


def _scatter_plan(axes):
    n = len(axes)

    def plan(refs, me, t, k, peer):
        if t >= n:
            return None
        size = refs[t].shape[axes[t]] // 4
        return _slot(refs[t], axes[t], peer, size), refs[n + t].at[k]
    return plan


def _swap_plan(n):
    def plan(refs, me, t, k, peer):
        if t >= n:
            return None
        return refs[t], refs[n + t]
    return plan


def _all_reduce_small(v):
    R = v.shape[0]

    def body(v_ref, o_ref, buf, send, recv, local):
        x, y, c = _coords()
        me = 4 * x + 2 * y + c
        cp = pltpu.make_async_copy(v_ref, buf.at[me], local)
        cp.start()
        pushes = []
        for k in range(1, 8):
            peer = (x ^ (k >> 2), y ^ ((k >> 1) & 1), c ^ (k & 1))
            rc = pltpu.make_async_remote_copy(
                src_ref=v_ref, dst_ref=buf.at[me], send_sem=send.at[k - 1], recv_sem=recv.at[k - 1],
                device_id=peer, device_id_type=MESH)
            rc.start()
            pushes.append(rc)
        for k in range(1, 8):
            px, py, pc = x ^ (k >> 2), y ^ ((k >> 1) & 1), c ^ (k & 1)
            pltpu.make_async_remote_copy(
                src_ref=v_ref, dst_ref=buf.at[4 * px + 2 * py + pc], send_sem=send.at[k - 1],
                recv_sem=recv.at[k - 1], device_id=(px, py, pc), device_id_type=MESH).wait_recv()
        for rc in pushes:
            rc.wait_send()
        cp.wait()
        tot = buf[0]
        for k in range(1, 8):
            tot = tot + buf[k]
        o_ref[...] = tot

    vm = pl.BlockSpec(memory_space=pltpu.VMEM)
    return pl.pallas_call(
        body, name="all_reduce_small",
        in_specs=[vm],
        out_specs=vm,
        out_shape=jax.ShapeDtypeStruct((R, LANES), F32),
        scratch_shapes=[
            pltpu.VMEM((8, R, LANES), F32),
            pltpu.SemaphoreType.DMA((7,)),
            pltpu.SemaphoreType.DMA((7,)),
            pltpu.SemaphoreType.DMA,
        ],
        compiler_params=pltpu.CompilerParams(vmem_limit_bytes=VMEM_LIMIT),
    )(v)


def _pack(arrs):
    flat = []
    for a in arrs:
        f = a.reshape(-1).astype(F32)
        flat.append(jnp.pad(f, (0, (-f.shape[0]) % LANES)))
    v = jnp.concatenate(flat)
    v = jnp.pad(v, (0, (-v.shape[0]) % (SUBLANES * LANES)))
    return v.reshape(-1, LANES)


def _unpack(v, shapes):
    flat = v.reshape(-1)
    out, off = [], 0
    for shp in shapes:
        n = 1
        for d in shp:
            n *= d
        out.append(flat[off:off + n].reshape(shp))
        off += n + (-n) % LANES
    return out


def _local_step(x, tgt, sp, weights_for, on_grads):
    n1, n2 = sp["norm1_g"], sp["norm2_g"]
    w = dict(weights_for(0, x))
    saved = []
    xs = x
    kv = None
    for l in range(DEPTH):
        x_in = xs
        if l >= N_A:
            w.update(weights_for(1 + 2 * l - N_A, x_in))
        if l == N_A:
            kv = _rms_linear(x_in, sp["kv_norm_g"], w["w_kv"], sp["b_kv"], 0, 0, "kv_proj")
        if l < N_A:
            xa = _pool_fwd(x_in, n1, w["pool_w"], w["pool_scale"], l)
            q = o = None
        else:
            j = l - N_A
            q = _rms_linear(x_in, n1, w["w_q", j], sp["b_q"], l, j, "q_proj")
            o = _attn_fwd(q, kv, sp["sinks"], j)
            xa = _linear_res(o, w["w_o", j], sp["b_o"], x_in, j)
        w.update(weights_for(1 + l if l < N_A else 2 + 2 * l - N_A, xa))
        xs, u, uc = _ffn_fwd(xa, n2, w["ffn_up", l], w["ffn_down", l], w["ffn_conv_w"], sp["ffn_conv_b"], l)
        saved.append((x_in, xa, u, uc, q, o))

    dx, d_final_g, loss = _loss_head(xs, sp["final_g"], tgt)

    g = {k: [None] * DEPTH for k in ("norm1_g", "norm2_g", "ffn_conv_w", "ffn_conv_b")}
    for k in ("pool_scale", "b_q", "sinks", "b_o"):
        g[k] = [None] * N_A
    g["final_g"] = d_final_g
    dkvs = []
    pending = {}
    for l in reversed(range(DEPTH)):
        x_in, xa, u, uc, q, o = saved[l]
        dxa, du, a, hb, g["norm2_g"][l], g["ffn_conv_w"][l], g["ffn_conv_b"][l] = _ffn_bwd(
            xa, dx, u, uc, n2, w["ffn_up", l], w["ffn_down", l], w["ffn_conv_w"], l)
        pending["ffn_up", l] = _tn_matmul(hb, du, MX, "d_ffn_up", a_is_transposed=True)
        if l == 0:
            n1 = n1 + on_grads(DEPTH + 1, pending)
            pending = {}
        pending["ffn_down", l] = _tn_matmul(a, dx, MX, "d_ffn_down", a_is_transposed=True)
        zero = on_grads(DEPTH - 1 - l, pending)
        pending = {}
        n1, n2 = n1 + zero, n2 + zero
        if l < N_A:
            dx, d_pw, g["pool_scale"][l], g["norm1_g"][l] = _pool_bwd(
                x_in, dxa, n1, w["pool_w"], w["pool_scale"], l)
            pending["pool_w", l] = d_pw.astype(MX)
        else:
            j = l - N_A
            d_o, g["b_o"][j] = _linear_nt(dxa, w["w_o", j])
            pending["w_o", j] = _tn_matmul(o, dxa, MX, "d_w_o")
            dq, dkv, g["sinks"][j] = _attn_bwd(q, d_o, kv, sp["sinks"], j)
            dkvs.append(dkv)
            dx, g["norm1_g"][l], g["b_q"][j], hq, dqb = _rms_linear_bwd(
                x_in, n1, [dq], w["w_q", j], dxa, l, "q_proj_bwd")
            pending["w_q", j] = _tn_matmul(hq, dqb, MX, "d_w_q")
        if l == N_A:
            dx, g["kv_norm_g"], g["b_kv"], hk, dkvb = _rms_linear_bwd(
                x_in, sp["kv_norm_g"], dkvs, w["w_kv"], dx, 0, "kv_proj_bwd")
            pending["w_kv", 0] = _tn_matmul(hk, dkvb, MX, "d_w_kv")
    on_grads(DEPTH, pending)
    return loss, dx, g


SMALL = ("norm1_g", "norm2_g", "kv_norm_g", "b_kv", "b_q", "sinks", "b_o", "ffn_conv_b", "final_g")
SMALL_SHARDED = ("pool_scale", "ffn_conv_w")
BIG = ("pool_w", "w_kv", "w_q", "w_o", "ffn_up", "ffn_down")
EARLY_SWAP = 3
ORDER = ("norm1_g", "norm2_g", "pool_w", "pool_scale", "kv_norm_g", "w_kv", "b_kv", "w_q", "b_q", "sinks",
         "w_o", "b_o", "ffn_up", "ffn_conv_w", "ffn_conv_b", "ffn_down", "final_g")


def _as3d(a):
    return a.reshape((-1,) + a.shape[-2:])


def kernel(x, norm1_g, norm2_g, pool_w, pool_scale, kv_norm_g, w_kv, b_kv, w_q, b_q, sinks, w_o, b_o, ffn_up, ffn_conv_w, ffn_conv_b, ffn_down, final_g, loss_target, m_norm1_g, m_norm2_g, m_pool_w, m_pool_scale, m_kv_norm_g, m_w_kv, m_b_kv, m_w_q, m_b_q, m_sinks, m_w_o, m_b_o, m_ffn_up, m_ffn_conv_w, m_ffn_conv_b, m_ffn_down, m_final_g, v_norm1_g, v_norm2_g, v_pool_w, v_pool_scale, v_kv_norm_g, v_w_kv, v_b_kv, v_w_q, v_b_q, v_sinks, v_w_o, v_b_o, v_ffn_up, v_ffn_conv_w, v_ffn_conv_b, v_ffn_down, v_final_g):
    W = dict(norm1_g=norm1_g, norm2_g=norm2_g, pool_w=pool_w, pool_scale=pool_scale, kv_norm_g=kv_norm_g,
             w_kv=w_kv, b_kv=b_kv, w_q=w_q, b_q=b_q, sinks=sinks, w_o=w_o, b_o=b_o, ffn_up=ffn_up,
             ffn_conv_w=ffn_conv_w, ffn_conv_b=ffn_conv_b, ffn_down=ffn_down, final_g=final_g)
    M = dict(norm1_g=m_norm1_g, norm2_g=m_norm2_g, pool_w=m_pool_w, pool_scale=m_pool_scale,
             kv_norm_g=m_kv_norm_g, w_kv=m_w_kv, b_kv=m_b_kv, w_q=m_w_q, b_q=m_b_q, sinks=m_sinks, w_o=m_w_o,
             b_o=m_b_o, ffn_up=m_ffn_up, ffn_conv_w=m_ffn_conv_w, ffn_conv_b=m_ffn_conv_b, ffn_down=m_ffn_down,
             final_g=m_final_g)
    V = dict(norm1_g=v_norm1_g, norm2_g=v_norm2_g, pool_w=v_pool_w, pool_scale=v_pool_scale,
             kv_norm_g=v_kv_norm_g, w_kv=v_w_kv, b_kv=v_b_kv, w_q=v_w_q, b_q=v_b_q, sinks=v_sinks, w_o=v_w_o,
             b_o=v_b_o, ffn_up=v_ffn_up, ffn_conv_w=v_ffn_conv_w, ffn_conv_b=v_ffn_conv_b, ffn_down=v_ffn_down,
             final_g=v_final_g)
    S = x.shape[1]
    chip = 2 * lax.axis_index("x") + lax.axis_index("y")

    gather_axis = dict(pool_w=1, w_kv=1, w_q=1, w_o=1, ffn_up=2, ffn_down=1, pool_scale=2, ffn_conv_w=2)
    me = chip.reshape(1).astype(jnp.int32)
    axis_of = lambda key: gather_axis[key if isinstance(key, str) else key[0]]

    def placed(key, dtype):
        if isinstance(key, str):
            return _gather_place(_as3d(W[key]), axis_of(key), me, dtype)
        return _gather_place(W[key[0]], axis_of(key), me, dtype, lead=key[1])

    stages = [
        ["pool_w", "pool_scale", "ffn_conv_w"],
        [("ffn_up", 0), ("ffn_down", 0)],
        [("ffn_up", 1), ("ffn_down", 1)],
        ["w_kv", ("w_q", 0), ("w_o", 0)],
        [("ffn_up", 2), ("ffn_down", 2)],
        [("w_q", 1), ("w_o", 1)],
        [("ffn_up", 3), ("ffn_down", 3)],
    ]
    TWO_LEVEL = 1
    gathers, zero = [], 0.0
    for si, keys in enumerate(stages):
        axes = [axis_of(k) for k in keys]
        bufs = [placed(k, F32 if k in SMALL_SHARDED else MX) for k in keys]
        plan = _gather_half_plan(axes) if si == TWO_LEVEL else _gather_plan(axes)
        send, recv, bufs, token = _push_start(f"gather_start_{si}", bufs, 3 * len(keys), plan)
        gathers.append((keys, axes, send, recv, bufs))
        zero = zero + token[0, 0]

    def weights_for(stage, after):
        keys, axes, send, recv, bufs = gathers[stage]
        if stage == TWO_LEVEL:
            bufs = _push_wait(f"gather_wait_{stage}", send, recv, bufs, _gather_half_plan(axes), after)
            send, recv, bufs, _ = _push_start("gather_pass_start", bufs, 3 * len(keys), _gather_pass_plan(axes), True)
            bufs = _push_wait("gather_pass_wait", send, recv, bufs, _gather_pass_plan(axes), after, True)
        else:
            bufs = _push_wait(f"gather_wait_{stage}", send, recv, bufs, _gather_plan(axes), after)
        out = dict(zip(keys, bufs))
        if stage == 0:
            out["pool_w"] = out["pool_w"].reshape(N_A, 4, GC, GC)
            out["pool_scale"] = out["pool_scale"].reshape(N_A, 1, D)
        return out

    scatters = []

    def on_grads(stage, grads):
        keys = list(grads)
        axes = [axis_of(k) for k in keys]
        arrs = [_as3d(grads[k]) for k in keys]
        lands = []
        for a, ax in zip(arrs, axes):
            shp = list(a.shape)
            shp[ax] //= 4
            lands.append(lax.empty((3,) + tuple(shp), a.dtype))
        send, recv, bufs, token = _push_start(f"scatter_start_{stage}", arrs + lands, 3 * len(keys), _scatter_plan(axes))
        scatters.append((stage, keys, axes, send, recv, bufs))
        return token[0, 0]

    sp = dict(
        norm1_g=norm1_g.reshape(DEPTH, 1, D) + zero, norm2_g=norm2_g.reshape(DEPTH, 1, D),
        kv_norm_g=kv_norm_g.reshape(1, 1, D), b_kv=b_kv.reshape(1, 1, 4 * HD), b_q=b_q.reshape(N_B, 1, D),
        sinks=sinks, b_o=b_o.reshape(N_B, 1, D), ffn_conv_b=ffn_conv_b.reshape(DEPTH, 1, F2),
        final_g=final_g.reshape(1, D))

    x2d = x.reshape(S, D)
    loss, grad_x, g = _local_step(x2d, loss_target.reshape(S, D), sp, weights_for, on_grads)

    small_full = dict(
        norm1_g=jnp.stack(g["norm1_g"]), norm2_g=jnp.stack(g["norm2_g"]), kv_norm_g=g["kv_norm_g"],
        b_kv=g["b_kv"], b_q=jnp.stack(g["b_q"]), sinks=jnp.stack([s[0, :NH] for s in g["sinks"]]),
        b_o=jnp.stack(g["b_o"]), ffn_conv_b=jnp.stack(g["ffn_conv_b"]), final_g=g["final_g"],
        pool_scale=jnp.stack(g["pool_scale"]), ffn_conv_w=jnp.stack(g["ffn_conv_w"]))
    small_names = SMALL + SMALL_SHARDED
    small_shapes = [tuple(W[k].shape) for k in SMALL] + [(N_A, D), (DEPTH, 3, F2)]
    packed = _pack([small_full[k] for k in small_names] + [loss])
    red = _unpack(_all_reduce_small(packed), small_shapes + [(1, LANES)])
    red_g = dict(zip(small_names, red[:-1]))
    loss_out = red[-1][0, 0]
    red_g["pool_scale"] = lax.dynamic_slice_in_dim(red_g["pool_scale"], chip * (D // 4), D // 4, axis=1)
    red_g["ffn_conv_w"] = lax.dynamic_slice_in_dim(red_g["ffn_conv_w"], chip * (F2 // 4), F2 // 4, axis=2)
    small_w_shapes = [tuple(W[k].shape) for k in small_names]
    pk = lambda d: _pack([d[k] for k in small_names])[None]
    res = _adamw([[_pack([red_g[k] for k in small_names])]], pk(W), pk(M), pk(V))
    out_g, out_d, out_m, out_v = [dict(zip(small_names, _unpack(r, small_w_shapes))) for r in res]

    pkeys, partial, swaps, after = [], [], [], grad_x

    def swap_start(tag, first):
        mine_now = partial[first:]
        lands = [lax.empty(p.shape, p.dtype) for p in mine_now]
        n = len(mine_now)
        send, recv, bufs, token = _push_start(f"swap_start_{tag}", mine_now + lands, n, _swap_plan(n), True)
        swaps.append((tag, n, send, recv, bufs))
        return token

    for i, (stage, keys, axes, send, recv, bufs) in enumerate(scatters):
        bufs = _push_wait(f"scatter_wait_{stage}", send, recv, bufs, _scatter_plan(axes), after)
        n = len(keys)
        for k, ax, grad, landed in zip(keys, axes, bufs[:n], bufs[n:]):
            pkeys.append(k)
            p_sum = _sum_landed(grad, landed, ax, me)
            partial.append(p_sum.reshape(-1, p_sum.shape[-1]))
        if i == EARLY_SWAP - 1:
            after = swap_start("early", 0)
    after = swap_start("late", sum(n for _, n, *_ in swaps))
    mine, theirs = [], []
    for tag, n, send, recv, bufs in swaps:
        bufs = _push_wait(f"swap_wait_{tag}", send, recv, bufs, _swap_plan(n), after, True)
        mine += bufs[:n]
        theirs += bufs[n:]
    mine = dict(zip(pkeys, mine))
    theirs = dict(zip(pkeys, theirs))
    for k in BIG:
        n_l = len([pk_ for pk_ in pkeys if pk_[0] == k])
        shp = W[k].shape
        rows, cols = mine[k, 0].shape
        three_d = lambda a: a.reshape(n_l, rows, cols)
        res = _adamw([[mine[k, l], theirs[k, l]] for l in range(n_l)], three_d(W[k]), three_d(M[k]), three_d(V[k]))
        out_g[k], out_d[k], out_m[k], out_v[k] = [r.reshape(shp) for r in res]

    return (loss_out, grad_x.reshape(x.shape), *[out_g[k] for k in ORDER], *[out_d[k] for k in ORDER],
            *[out_m[k] for k in ORDER], *[out_v[k] for k in ORDER])
```

```python
import functools

import jax
import jax.numpy as jnp
from jax import lax
from jax.experimental import pallas as pl
from jax.experimental.pallas import tpu as pltpu

D = 1024
DEPTH = 4
N_A = 2
N_B = 2
WINS = (2, 4, 8, 16)
GC = 256
HD = 64
NH = 16
BLK = 128
F = 2816
F2 = 2 * F
EPS = 1e-5
SCALE = HD ** -0.5
NEG = -1e30
HALO = 16
TN = 256
TM_STREAM = 1024
UP_GROUP = 3
LANES = 128
SUBLANES = 8
VMEM_LIMIT = 56 * 1024 * 1024
FFN_VMEM_LIMIT = 60 * 1024 * 1024
ACC_BYTES = 6 * 1024 * 1024
EW_BYTES = 1024 * 1024

LR, B1, B2, AEPS, WD, STEP = 0.001, 0.9, 0.999, 1e-08, 0.01, 10

MX = jnp.bfloat16
F32 = jnp.float32
MESH = pl.DeviceIdType.MESH


def _cp(n_axes=1, vmem=VMEM_LIMIT):
    return pltpu.CompilerParams(dimension_semantics=("arbitrary",) * n_axes, vmem_limit_bytes=vmem)


def _dot(a, b):
    return jnp.dot(a, b, preferred_element_type=F32)


def _dot_nt(a, b):
    return lax.dot_general(a, b, (((1,), (1,)), ((), ())), preferred_element_type=F32)


def _dot_tn(a, b):
    return lax.dot_general(a, b, (((0,), (0,)), ((), ())), preferred_element_type=F32)


def _rms_fwd(x, g):
    r = lax.rsqrt(jnp.mean(x * x, axis=-1, keepdims=True) + EPS)
    xh = x * r
    return xh * g, xh, r


def _rms_bwd(dh, xh, r, g):
    dxh = dh * g
    return r * (dxh - xh * jnp.mean(dxh * xh, axis=-1, keepdims=True))


def _row_tile(s, want):
    return min(s, want)


def _pool_pm(e, h, row, tm):
    out = []
    for gi, win in enumerate(WINS):
        cols = slice(gi * GC, (gi + 1) * GC)
        s = e[:, cols]
        sh = 1
        while sh < win:
            s = s + pltpu.roll(s, sh, 0)
            sh *= 2
        inv = 1.0 / jnp.minimum(row + 1, win).astype(F32)
        out.append(s[HALO:] * inv - h[:, cols])
    return out


def _pool_fwd(x, g, pw, ps, layer):
    S = x.shape[0]
    tm = _row_tile(S, 512)
    hb = tm // HALO

    def body(x_ref, xh_ref, g_ref, pw_ref, ps_ref, o_ref):
        i = pl.program_id(0)
        x = x_ref[...]
        gg = g_ref[...]
        h, _, _ = _rms_fwd(x, gg)
        hh, _, _ = _rms_fwd(xh_ref[...], gg)
        hh = jnp.where(i > 0, hh, 0.0)
        e = jnp.concatenate([hh, h], axis=0)
        row = i * tm + lax.broadcasted_iota(jnp.int32, (tm, 1), 0)
        pm = _pool_pm(e, h, row, tm)
        for gi in range(len(WINS)):
            cols = slice(gi * GC, (gi + 1) * GC)
            z = _dot(pm[gi].astype(MX), pw_ref[gi])
            o_ref[:, cols] = x[:, cols] + z * ps_ref[:, cols]

    return pl.pallas_call(
        body, name="pool_fwd",
        grid=(S // tm,),
        in_specs=[
            pl.BlockSpec((tm, D), lambda i: (i, 0)),
            pl.BlockSpec((HALO, D), lambda i: (jnp.maximum(i * hb - 1, 0), 0)),
            pl.BlockSpec((None, 1, D), lambda i: (layer, 0, 0)),
            pl.BlockSpec((None, 4, GC, GC), lambda i: (layer, 0, 0, 0)),
            pl.BlockSpec((None, 1, D), lambda i: (layer, 0, 0)),
        ],
        out_specs=pl.BlockSpec((tm, D), lambda i: (i, 0)),
        out_shape=jax.ShapeDtypeStruct((S, D), F32),
        compiler_params=_cp(),
    )(x, x, g, pw, ps)


def _pool_bwd(x, dy, g, pw, ps, layer):
    S = x.shape[0]
    tm = _row_tile(S, 256)
    hb = tm // HALO
    n_i = S // tm
    n_h = S // HALO

    def body(x_ref, xh_ref, dy_ref, dyn_ref, g_ref, pw_ref, ps_ref, dx_ref, dpw_ref, dps_ref, dg_ref):
        i = pl.program_id(0)

        @pl.when(i == 0)
        def _():
            dpw_ref[...] = jnp.zeros_like(dpw_ref)
            dps_ref[...] = jnp.zeros_like(dps_ref)
            dg_ref[...] = jnp.zeros_like(dg_ref)

        x = x_ref[...]
        gg = g_ref[...]
        ps = ps_ref[...]
        h, xh, r = _rms_fwd(x, gg)
        hh, _, _ = _rms_fwd(xh_ref[...], gg)
        hh = jnp.where(i > 0, hh, 0.0)
        e = jnp.concatenate([hh, h], axis=0)
        row = i * tm + lax.broadcasted_iota(jnp.int32, (tm, 1), 0)
        rown = (i + 1) * tm + lax.broadcasted_iota(jnp.int32, (HALO, 1), 0)
        pm = _pool_pm(e, h, row, tm)
        dy = dy_ref[...]
        dz = dy * ps
        dzn = jnp.where(i < n_i - 1, dyn_ref[...] * ps, 0.0)
        parts = []
        for gi, win in enumerate(WINS):
            cols = slice(gi * GC, (gi + 1) * GC)
            w = pw_ref[gi]
            pmb = pm[gi].astype(MX)
            z = _dot(pmb, w)
            dps_ref[:, cols] += jnp.sum(dy[:, cols] * z, axis=0, keepdims=True)
            dzb = dz[:, cols].astype(MX)
            dpw_ref[gi] += _dot_tn(pmb, dzb)
            dpm = _dot_nt(dzb, w)
            dpmn = _dot_nt(dzn[:, cols].astype(MX), w)
            q = dpm * (1.0 / jnp.minimum(row + 1, win).astype(F32))
            qn = dpmn * (1.0 / jnp.minimum(rown + 1, win).astype(F32))
            s = jnp.concatenate([q, qn], axis=0)
            sh = 1
            while sh < win:
                s = s + pltpu.roll(s, tm + HALO - sh, 0)
                sh *= 2
            parts.append(s[:tm] - dpm)
        dh = jnp.concatenate(parts, axis=1)
        dg_ref[...] += jnp.sum(dh * xh, axis=0, keepdims=True)
        dx_ref[...] = dy + _rms_bwd(dh, xh, r, gg)

    return pl.pallas_call(
        body, name="pool_bwd",
        grid=(n_i,),
        in_specs=[
            pl.BlockSpec((tm, D), lambda i: (i, 0)),
            pl.BlockSpec((HALO, D), lambda i: (jnp.maximum(i * hb - 1, 0), 0)),
            pl.BlockSpec((tm, D), lambda i: (i, 0)),
            pl.BlockSpec((HALO, D), lambda i: (jnp.minimum((i + 1) * hb, n_h - 1), 0)),
            pl.BlockSpec((None, 1, D), lambda i: (layer, 0, 0)),
            pl.BlockSpec((None, 4, GC, GC), lambda i: (layer, 0, 0, 0)),
            pl.BlockSpec((None, 1, D), lambda i: (layer, 0, 0)),
        ],
        out_specs=[
            pl.BlockSpec((tm, D), lambda i: (i, 0)),
            pl.BlockSpec((4, GC, GC), lambda i: (0, 0, 0)),
            pl.BlockSpec((1, D), lambda i: (0, 0)),
            pl.BlockSpec((1, D), lambda i: (0, 0)),
        ],
        out_shape=[
            jax.ShapeDtypeStruct((S, D), F32),
            jax.ShapeDtypeStruct((4, GC, GC), F32),
            jax.ShapeDtypeStruct((1, D), F32),
            jax.ShapeDtypeStruct((1, D), F32),
        ],
        compiler_params=_cp(),
    )(x, x, dy, dy, g, pw, ps)


N_STAGE = 4


def _rows_before(slot, u, prev8):
    tm = u.shape[0]
    m1, m2 = [], []
    for c in range(TN // LANES):
        lanes = slice(c * LANES, (c + 1) * LANES)
        slot[c, 0:SUBLANES, :] = prev8[:, lanes]
        slot[c, SUBLANES:SUBLANES + tm, :] = u[:, lanes]
        m1.append(slot[c, pl.ds(SUBLANES - 1, tm), :])
        m2.append(slot[c, pl.ds(SUBLANES - 2, tm), :])
    return jnp.concatenate(m1, axis=1), jnp.concatenate(m2, axis=1)


def _rows_after(slot, d, next8):
    tm = d.shape[0]
    p1, p2 = [], []
    for c in range(TN // LANES):
        lanes = slice(c * LANES, (c + 1) * LANES)
        slot[c, 0:tm, :] = d[:, lanes]
        slot[c, tm:tm + SUBLANES, :] = next8[:, lanes]
        p1.append(slot[c, pl.ds(1, tm), :])
        p2.append(slot[c, pl.ds(2, tm), :])
    return jnp.concatenate(p1, axis=1), jnp.concatenate(p2, axis=1)


def _conv(slot, u, prev8, cw):
    um1, um2 = _rows_before(slot, u, prev8)
    return cw[0:1] * um2 + cw[1:2] * um1 + cw[2:3] * u


def _ffn_fwd(x, g, wup, wdn, cw, cb, layer):
    S = x.shape[0]
    tm = _row_tile(S, 512)

    def body(x_ref, g_ref, wup_hbm, wdn_hbm, cw_ref, cb_ref, o_ref, u_ref, uc_ref, wup_v, wdn_v, carry, act, stage):
        i = pl.program_id(0)

        @pl.when(i == 0)
        def _():
            pltpu.sync_copy(wup_hbm.at[0], wup_v)
            pltpu.sync_copy(wdn_hbm.at[0], wdn_v)
            carry[...] = jnp.zeros_like(carry)

        x = x_ref[...]
        h, _, _ = _rms_fwd(x, g_ref[...])
        hb = h.astype(MX)
        for j in range(F // TN):
            cg = slice(j * TN, (j + 1) * TN)
            cv = slice(F + j * TN, F + (j + 1) * TN)
            ug = _dot(hb, wup_v[:, cg])
            uv = _dot(hb, wup_v[:, cv])
            u_ref[:, cg] = ug.astype(u_ref.dtype)
            u_ref[:, cv] = uv.astype(u_ref.dtype)
            gt = _conv(stage.at[2 * (j % 2)], ug, carry[:, cg], cw_ref[:, cg])
            vl = _conv(stage.at[2 * (j % 2) + 1], uv, carry[:, cv], cw_ref[:, cv])
            carry[:, cg] = ug[tm - SUBLANES:]
            carry[:, cv] = uv[tm - SUBLANES:]
            gt = gt + cb_ref[:, cg]
            vl = vl + cb_ref[:, cv]
            uc_ref[:, cg] = gt.astype(uc_ref.dtype)
            uc_ref[:, cv] = vl.astype(uc_ref.dtype)
            act[:, cg] = (gt * jax.nn.sigmoid(gt) * vl).astype(act.dtype)
        o_ref[...] = x + _dot(act[...], wdn_v[...])

    return pl.pallas_call(
        body, name="ffn_fwd",
        grid=(S // tm,),
        in_specs=[
            pl.BlockSpec((tm, D), lambda i: (i, 0)),
            pl.BlockSpec((None, 1, D), lambda i: (layer, 0, 0)),
            pl.BlockSpec(memory_space=pl.ANY),
            pl.BlockSpec(memory_space=pl.ANY),
            pl.BlockSpec((None, 3, F2), lambda i: (layer, 0, 0)),
            pl.BlockSpec((None, 1, F2), lambda i: (layer, 0, 0)),
        ],
        out_specs=[
            pl.BlockSpec((tm, D), lambda i: (i, 0)),
            pl.BlockSpec((tm, F2), lambda i: (i, 0)),
            pl.BlockSpec((tm, F2), lambda i: (i, 0)),
        ],
        out_shape=[
            jax.ShapeDtypeStruct((S, D), F32),
            jax.ShapeDtypeStruct((S, F2), MX),
            jax.ShapeDtypeStruct((S, F2), MX),
        ],
        scratch_shapes=[
            pltpu.VMEM((D, F2), MX),
            pltpu.VMEM((F, D), MX),
            pltpu.VMEM((SUBLANES, F2), F32),
            pltpu.VMEM((tm, F), MX),
            pltpu.VMEM((N_STAGE, TN // LANES, tm + SUBLANES, LANES), F32),
        ],
        compiler_params=_cp(vmem=FFN_VMEM_LIMIT),
    )(x, g, wup, wdn, cw, cb)


def _ffn_bwd(x, dy, u, uc, g, wup, wdn, cw, layer):
    S = x.shape[0]
    tm = _row_tile(S, 256)
    n_i = S // tm

    def body(x_ref, dy_ref, u_ref, uc_ref, g_ref, wup_hbm, wdn_hbm, cw_ref,
             dx_ref, du_ref, a_ref, h_ref, dg_ref, dcw_ref, dcb_ref, wup_v, wdn_v, carry, stage):
        i = pl.program_id(0)

        @pl.when(i == 0)
        def _():
            pltpu.sync_copy(wup_hbm.at[0], wup_v)
            pltpu.sync_copy(wdn_hbm.at[0], wdn_v)
            carry[...] = jnp.zeros_like(carry)
            dg_ref[...] = jnp.zeros_like(dg_ref)
            dcw_ref[...] = jnp.zeros_like(dcw_ref)
            dcb_ref[...] = jnp.zeros_like(dcb_ref)

        x = x_ref[...]
        gg = g_ref[...]
        h, xh, r = _rms_fwd(x, gg)
        h_ref[...] = h.T.astype(h_ref.dtype)
        dy = dy_ref[...]
        dyb = dy.astype(MX)
        dh, dus = None, ([], [])
        for j in range(F // TN):
            cg = slice(j * TN, (j + 1) * TN)
            cv = slice(F + j * TN, F + (j + 1) * TN)
            gt = uc_ref[:, cg].astype(F32)
            vl = uc_ref[:, cv].astype(F32)
            sg = jax.nn.sigmoid(gt)
            sil = gt * sg
            a_ref[cg, :] = (sil * vl).T.astype(a_ref.dtype)
            da = _dot_nt(dyb, wdn_v[cg, :])
            dvl = da * sil
            dgt = (da * vl) * (sg + sil * (1.0 - sg))
            for cc, dd in ((cg, dgt), (cv, dvl)):
                dp1, dp2 = _rows_after(stage.at[2 * (j % 2) + (cc is cv)], dd, carry[:, cc])
                carry[:, cc] = dd[0:SUBLANES]
                uu = u_ref[:, cc].astype(F32)
                dcb_ref[:, cc] += jnp.sum(dd, axis=0, keepdims=True)
                dcw_ref[0:1, cc] += jnp.sum(dp2 * uu, axis=0, keepdims=True)
                dcw_ref[1:2, cc] += jnp.sum(dp1 * uu, axis=0, keepdims=True)
                dcw_ref[2:3, cc] += jnp.sum(dd * uu, axis=0, keepdims=True)
                cwc = cw_ref[:, cc]
                duu = (cwc[2:3] * dd + cwc[1:2] * dp1 + cwc[0:1] * dp2).astype(MX)
                du_ref[:, cc] = duu
                dus[cc is cv].append(duu)
            if len(dus[0]) == UP_GROUP or j == F // TN - 1:
                first = j + 1 - len(dus[0])
                for side, base in ((0, 0), (1, F)):
                    cols = slice(base + first * TN, base + (j + 1) * TN)
                    part = _dot_nt(jnp.concatenate(dus[side], axis=1), wup_v[:, cols])
                    dh = part if dh is None else dh + part
                dus = ([], [])
        dg_ref[...] += jnp.sum(dh * xh, axis=0, keepdims=True)
        dx_ref[...] = dy + _rms_bwd(dh, xh, r, gg)

    rev = lambda i: (n_i - 1 - i, 0)
    return pl.pallas_call(
        body, name="ffn_bwd",
        grid=(n_i,),
        in_specs=[
            pl.BlockSpec((tm, D), rev),
            pl.BlockSpec((tm, D), rev),
            pl.BlockSpec((tm, F2), rev),
            pl.BlockSpec((tm, F2), rev),
            pl.BlockSpec((None, 1, D), lambda i: (layer, 0, 0)),
            pl.BlockSpec(memory_space=pl.ANY),
            pl.BlockSpec(memory_space=pl.ANY),
            pl.BlockSpec((None, 3, F2), lambda i: (layer, 0, 0)),
        ],
        out_specs=[
            pl.BlockSpec((tm, D), rev),
            pl.BlockSpec((tm, F2), rev),
            pl.BlockSpec((F, tm), lambda i: (0, n_i - 1 - i)),
            pl.BlockSpec((D, tm), lambda i: (0, n_i - 1 - i)),
            pl.BlockSpec((1, D), lambda i: (0, 0)),
            pl.BlockSpec((3, F2), lambda i: (0, 0)),
            pl.BlockSpec((1, F2), lambda i: (0, 0)),
        ],
        out_shape=[
            jax.ShapeDtypeStruct((S, D), F32),
            jax.ShapeDtypeStruct((S, F2), MX),
            jax.ShapeDtypeStruct((F, S), MX),
            jax.ShapeDtypeStruct((D, S), MX),
            jax.ShapeDtypeStruct((1, D), F32),
            jax.ShapeDtypeStruct((3, F2), F32),
            jax.ShapeDtypeStruct((1, F2), F32),
        ],
        scratch_shapes=[
            pltpu.VMEM((D, F2), MX),
            pltpu.VMEM((F, D), MX),
            pltpu.VMEM((SUBLANES, F2), F32),
            pltpu.VMEM((N_STAGE, TN // LANES, tm + SUBLANES, LANES), F32),
        ],
        compiler_params=_cp(vmem=FFN_VMEM_LIMIT),
    )(x, dy, u, uc, g, wup, wdn, cw)


def _tn_matmul(a, b, out_dtype, name, a_is_transposed=False):
    S, N = b.shape
    M = a.shape[0] if a_is_transposed else a.shape[1]
    bn = N
    while M * bn * 4 > ACC_BYTES and bn % (2 * LANES) == 0:
        bn //= 2
    bk = _row_tile(S, 1024)
    nk = S // bk
    a_spec = pl.BlockSpec((M, bk), lambda j, k: (0, k)) if a_is_transposed else pl.BlockSpec((bk, M), lambda j, k: (k, 0))

    def body(a_ref, b_ref, o_ref, acc):
        k = pl.program_id(1)
        if a_is_transposed:
            p = _dot(a_ref[...].astype(MX), b_ref[...].astype(MX))
        else:
            p = _dot_tn(a_ref[...].astype(MX), b_ref[...].astype(MX))

        @pl.when(k == 0)
        def _():
            acc[...] = p

        @pl.when(k > 0)
        def _():
            acc[...] += p

        @pl.when(k == nk - 1)
        def _():
            o_ref[...] = acc[...].astype(o_ref.dtype)

    return pl.pallas_call(
        body, name=name,
        grid=(N // bn, nk),
        in_specs=[
            a_spec,
            pl.BlockSpec((bk, bn), lambda j, k: (k, j)),
        ],
        out_specs=pl.BlockSpec((M, bn), lambda j, k: (0, j)),
        out_shape=jax.ShapeDtypeStruct((M, N), out_dtype),
        scratch_shapes=[pltpu.VMEM((M, bn), F32)],
        compiler_params=_cp(2),
    )(a, b)


def _rms_linear(x, g, w, b, g_layer, b_layer, name):
    S = x.shape[0]
    N = w.shape[-1]
    tm = _row_tile(S, TM_STREAM)

    def body(x_ref, g_ref, w_ref, b_ref, o_ref):
        h, _, _ = _rms_fwd(x_ref[...], g_ref[...])
        o_ref[...] = (_dot(h.astype(MX), w_ref[...]) + b_ref[...]).astype(o_ref.dtype)

    return pl.pallas_call(
        body, name=name,
        grid=(S // tm,),
        in_specs=[
            pl.BlockSpec((tm, D), lambda i: (i, 0)),
            pl.BlockSpec((None, 1, D), lambda i: (g_layer, 0, 0)),
            pl.BlockSpec((None, D, N), lambda i: (0, 0, 0)),
            pl.BlockSpec((None, 1, N), lambda i: (b_layer, 0, 0)),
        ],
        out_specs=pl.BlockSpec((tm, N), lambda i: (i, 0)),
        out_shape=jax.ShapeDtypeStruct((S, N), MX),
        compiler_params=_cp(),
    )(x, g, w, b)


def _linear_res(o, w, b, xres, layer):
    S = o.shape[0]
    tm = _row_tile(S, TM_STREAM)

    def body(o_ref, w_ref, b_ref, x_ref, y_ref):
        y_ref[...] = x_ref[...] + _dot(o_ref[...], w_ref[...]) + b_ref[...]

    return pl.pallas_call(
        body, name="o_proj",
        grid=(S // tm,),
        in_specs=[
            pl.BlockSpec((tm, D), lambda i: (i, 0)),
            pl.BlockSpec((None, D, D), lambda i: (0, 0, 0)),
            pl.BlockSpec((None, 1, D), lambda i: (layer, 0, 0)),
            pl.BlockSpec((tm, D), lambda i: (i, 0)),
        ],
        out_specs=pl.BlockSpec((tm, D), lambda i: (i, 0)),
        out_shape=jax.ShapeDtypeStruct((S, D), F32),
        compiler_params=_cp(),
    )(o, w, b, xres)


def _linear_nt(dy, w):
    S = dy.shape[0]
    tm = _row_tile(S, TM_STREAM)

    def body(dy_ref, w_ref, o_ref, db_ref):
        @pl.when(pl.program_id(0) == 0)
        def _():
            db_ref[...] = jnp.zeros_like(db_ref)

        dy = dy_ref[...]
        db_ref[...] += jnp.sum(dy, axis=0, keepdims=True)
        o_ref[...] = _dot_nt(dy.astype(MX), w_ref[...]).astype(o_ref.dtype)

    return pl.pallas_call(
        body, name="o_proj_bwd",
        grid=(S // tm,),
        in_specs=[
            pl.BlockSpec((tm, D), lambda i: (i, 0)),
            pl.BlockSpec((None, D, D), lambda i: (0, 0, 0)),
        ],
        out_specs=[
            pl.BlockSpec((tm, D), lambda i: (i, 0)),
            pl.BlockSpec((1, D), lambda i: (0, 0)),
        ],
        out_shape=[
            jax.ShapeDtypeStruct((S, D), MX),
            jax.ShapeDtypeStruct((1, D), F32),
        ],
        compiler_params=_cp(),
    )(dy, w)


def _rms_linear_bwd(x, g, dzs, w, dy, g_layer, name):
    S = x.shape[0]
    N = w.shape[-1]
    tm = _row_tile(S, TM_STREAM)
    nz = len(dzs)

    def body(*refs):
        x_ref, g_ref = refs[0], refs[1]
        dz_refs = refs[2:2 + nz]
        w_ref, dy_ref, dx_ref, dg_ref, db_ref, h_ref, dzb_ref = refs[2 + nz:]

        @pl.when(pl.program_id(0) == 0)
        def _():
            dg_ref[...] = jnp.zeros_like(dg_ref)
            db_ref[...] = jnp.zeros_like(db_ref)

        gg = g_ref[...]
        h, xh, r = _rms_fwd(x_ref[...], gg)
        h_ref[...] = h.astype(h_ref.dtype)
        dz = dz_refs[0][...].astype(F32)
        for zr in dz_refs[1:]:
            dz = dz + zr[...].astype(F32)
        db_ref[...] += jnp.sum(dz, axis=0, keepdims=True)
        dzb = dz.astype(MX)
        dzb_ref[...] = dzb
        dh = _dot_nt(dzb, w_ref[...])
        dg_ref[...] += jnp.sum(dh * xh, axis=0, keepdims=True)
        dx_ref[...] = dy_ref[...] + _rms_bwd(dh, xh, r, gg)

    return pl.pallas_call(
        body, name=name,
        grid=(S // tm,),
        in_specs=[
            pl.BlockSpec((tm, D), lambda i: (i, 0)),
            pl.BlockSpec((None, 1, D), lambda i: (g_layer, 0, 0)),
        ] + [pl.BlockSpec((tm, N), lambda i: (i, 0))] * nz + [
            pl.BlockSpec((None, D, N), lambda i: (0, 0, 0)),
            pl.BlockSpec((tm, D), lambda i: (i, 0)),
        ],
        out_specs=[
            pl.BlockSpec((tm, D), lambda i: (i, 0)),
            pl.BlockSpec((1, D), lambda i: (0, 0)),
            pl.BlockSpec((1, N), lambda i: (0, 0)),
            pl.BlockSpec((tm, D), lambda i: (i, 0)),
            pl.BlockSpec((tm, N), lambda i: (i, 0)),
        ],
        out_shape=[
            jax.ShapeDtypeStruct((S, D), F32),
            jax.ShapeDtypeStruct((1, D), F32),
            jax.ShapeDtypeStruct((1, N), F32),
            jax.ShapeDtypeStruct((S, D), MX),
            jax.ShapeDtypeStruct((S, N), MX),
        ],
        compiler_params=_cp(),
    )(x, g, *dzs, w, dy)


def _loss_head(x, g, tgt):
    S = x.shape[0]
    tm = _row_tile(S, TM_STREAM)

    def body(x_ref, g_ref, t_ref, dx_ref, dg_ref, l_ref):
        @pl.when(pl.program_id(0) == 0)
        def _():
            dg_ref[...] = jnp.zeros_like(dg_ref)
            l_ref[...] = jnp.zeros_like(l_ref)

        gg = g_ref[...]
        y, xh, r = _rms_fwd(x_ref[...], gg)
        err = y - t_ref[...]
        tok = jnp.sum(err * err, axis=-1, keepdims=True) * (1.0 / D)
        l_ref[...] += 0.5 * jnp.sum(tok, axis=0, keepdims=True)
        dyv = err * (1.0 / D)
        dg_ref[...] += jnp.sum(dyv * xh, axis=0, keepdims=True)
        dx_ref[...] = _rms_bwd(dyv, xh, r, gg)

    return pl.pallas_call(
        body, name="loss_head",
        grid=(S // tm,),
        in_specs=[
            pl.BlockSpec((tm, D), lambda i: (i, 0)),
            pl.BlockSpec((1, D), lambda i: (0, 0)),
            pl.BlockSpec((tm, D), lambda i: (i, 0)),
        ],
        out_specs=[
            pl.BlockSpec((tm, D), lambda i: (i, 0)),
            pl.BlockSpec((1, D), lambda i: (0, 0)),
            pl.BlockSpec((1, LANES), lambda i: (0, 0)),
        ],
        out_shape=[
            jax.ShapeDtypeStruct((S, D), F32),
            jax.ShapeDtypeStruct((1, D), F32),
            jax.ShapeDtypeStruct((1, LANES), F32),
        ],
        compiler_params=_cp(),
    )(x, g, tgt)


HPG = NH // 2
QH = BLK // 2
KW = BLK + QH
COLS = HPG * QH


def _attn_setup(kvp_ref, kvc_ref):
    kw = jnp.concatenate([kvp_ref[...], kvc_ref[...]], axis=0).astype(F32)
    kk, vv = kw[:, :LANES], kw[:, LANES:]
    lo = lax.broadcasted_iota(jnp.int32, (1, LANES), 1) < HD
    kr, vr = pltpu.roll(kk, HD, 1), pltpu.roll(vv, HD, 1)
    ks = [jnp.where(lo, kk, kr).astype(MX), jnp.where(lo, kr, kk).astype(MX)]
    vs = [jnp.where(lo, vv, vr).astype(MX), jnp.where(lo, vr, vv).astype(MX)]
    return ks, vs, lo


def _stack_heads(ref, grp, lo, rows):
    parts = []
    for j in range(4 * grp, 4 * grp + 4):
        slab = ref[rows, j * LANES:(j + 1) * LANES]
        zero = jnp.zeros_like(slab)
        parts += [jnp.where(lo, slab, zero), jnp.where(lo, zero, slab)]
    return jnp.concatenate(parts, axis=0)


def _unstack_heads(st, lo):
    nq = st.shape[0] // HPG
    return [jnp.where(lo, st[2 * i * nq:(2 * i + 1) * nq], st[(2 * i + 1) * nq:(2 * i + 2) * nq])
            for i in range(4)]


def _attn_probs(qs, kg, n, sk_ref, layer, grp):
    rows = HPG * BLK
    qi = lax.broadcasted_iota(jnp.int32, (rows, 2 * BLK), 0) & (BLK - 1)
    si = lax.broadcasted_iota(jnp.int32, (rows, 2 * BLK), 1)
    ok = (si > qi) & (si <= qi + BLK) & jnp.logical_or(n > 0, si >= BLK)
    head = lax.broadcasted_iota(jnp.int32, (rows, 1), 0) // BLK
    sink = jnp.zeros((rows, 1), F32)
    for h in range(HPG):
        sink = jnp.where(head == h, sk_ref[layer, HPG * grp + h], sink)
    s = jnp.where(ok, _dot_nt(qs, kg) * SCALE, NEG)
    m = jnp.maximum(jnp.max(s, axis=-1, keepdims=True), sink)
    p = jnp.exp(s - m)
    return p * (1.0 / (jnp.sum(p, axis=-1, keepdims=True) + jnp.exp(sink - m)))


def _attn_mask_t(koff, n):
    si = lax.broadcasted_iota(jnp.int32, (KW, COLS), 0)
    qi = lax.broadcasted_iota(jnp.int32, (KW, COLS), 1) & (QH - 1)
    return (si > qi) & (si <= qi + BLK) & jnp.logical_or(n > 0, si >= BLK - koff)


def _attn_probs_t(qs, kg, ok, sk_ref, layer, grp):
    s = jnp.where(ok, _dot_nt(kg, qs) * SCALE, NEG)
    head = lax.broadcasted_iota(jnp.int32, (1, COLS), 1) // QH
    sink = jnp.zeros((1, COLS), F32)
    for h in range(HPG):
        sink = jnp.where(head == h, sk_ref[layer, HPG * grp + h], sink)
    m = jnp.maximum(jnp.max(s, axis=0, keepdims=True), sink)
    p = jnp.exp(s - m)
    es = jnp.exp(sink - m)
    inv = 1.0 / (jnp.sum(p, axis=0, keepdims=True) + es)
    return p * inv, es * inv, head


def _attn_specs(n_extra_q):
    q_spec = pl.BlockSpec((BLK, D), lambda n: (n, 0))
    return [q_spec] * n_extra_q + [
        pl.BlockSpec((BLK, 4 * HD), lambda n: (jnp.maximum(n - 1, 0), 0)),
        pl.BlockSpec((BLK, 4 * HD), lambda n: (n, 0)),
        pl.BlockSpec(memory_space=pltpu.SMEM),
    ]


def _attn_fwd(q, kv, sinks, layer):
    S = q.shape[0]

    def body(q_ref, kvp_ref, kvc_ref, sk_ref, o_ref):
        n = pl.program_id(0)
        ks, vs, lo = _attn_setup(kvp_ref, kvc_ref)
        for grp in range(2):
            qs = _stack_heads(q_ref, grp, lo, slice(None))
            pr = _attn_probs(qs, ks[grp], n, sk_ref, layer, grp)
            outs = _unstack_heads(_dot(pr.astype(MX), vs[grp]), lo)
            for i in range(4):
                j = 4 * grp + i
                o_ref[:, j * LANES:(j + 1) * LANES] = outs[i].astype(o_ref.dtype)

    return pl.pallas_call(
        body, name="attn_fwd",
        grid=(S // BLK,),
        in_specs=_attn_specs(1),
        out_specs=pl.BlockSpec((BLK, D), lambda n: (n, 0)),
        out_shape=jax.ShapeDtypeStruct((S, D), MX),
        compiler_params=_cp(),
    )(q, kv, kv, sinks)


def _attn_bwd(q, do, kv, sinks, layer):
    S = q.shape[0]

    def body(q_ref, do_ref, kvp_ref, kvc_ref, sk_ref, dq_ref, dkv_ref, dsk_ref):
        n = pl.program_id(0)

        @pl.when(n == 0)
        def _():
            dkv_ref[...] = jnp.zeros_like(dkv_ref)
            dsk_ref[...] = jnp.zeros_like(dsk_ref)

        ks, vs, lo = _attn_setup(kvp_ref, kvc_ref)
        lane = lax.broadcasted_iota(jnp.int32, (1, LANES), 1)
        dsk = jnp.zeros((1, LANES), F32)
        dk = [jnp.zeros((2 * BLK, LANES), F32) for _ in range(2)]
        dv = [jnp.zeros((2 * BLK, LANES), F32) for _ in range(2)]
        for half in range(2):
            rows = slice(half * QH, (half + 1) * QH)
            koff = half * QH
            ok = _attn_mask_t(koff, n)
            above = [jnp.zeros((koff, LANES), F32)] if koff else []
            below = [jnp.zeros((2 * BLK - KW - koff, LANES), F32)] if 2 * BLK - KW - koff else []
            for grp in range(2):
                qs = _stack_heads(q_ref, grp, lo, rows)
                dos = _stack_heads(do_ref, grp, lo, rows)
                kg, vg = ks[grp][koff:koff + KW], vs[grp][koff:koff + KW]
                pr, psink, head = _attn_probs_t(qs, kg, ok, sk_ref, layer, grp)
                dpr = _dot_nt(vg, dos)
                delta = jnp.sum(pr * dpr, axis=0, keepdims=True)
                ds = (pr * (dpr - delta) * SCALE).astype(MX)
                sd = psink * delta
                for h in range(HPG):
                    dsk = dsk + jnp.where(lane == HPG * grp + h,
                                          -jnp.sum(jnp.where(head == h, sd, 0.0), axis=1, keepdims=True), 0.0)
                dqs = _unstack_heads(_dot_tn(kg, ds).T, lo)
                for i in range(4):
                    j = 4 * grp + i
                    dq_ref[rows, j * LANES:(j + 1) * LANES] = dqs[i].astype(dq_ref.dtype)
                dk[grp] = dk[grp] + jnp.concatenate(above + [_dot(ds, qs)] + below, axis=0)
                dv[grp] = dv[grp] + jnp.concatenate(above + [_dot(pr.astype(MX), dos)] + below, axis=0)
        dsk_ref[...] += dsk
        tk = [a + pltpu.roll(a, HD, 1) for a in dk]
        tv = [a + pltpu.roll(a, HD, 1) for a in dv]
        contrib = jnp.concatenate([jnp.where(lo, tk[0], tk[1]), jnp.where(lo, tv[0], tv[1])], axis=1)

        @pl.when(n > 0)
        def _():
            rows = pl.ds(pl.multiple_of((n - 1) * BLK, BLK), 2 * BLK)
            dkv_ref[rows, :] += contrib

        @pl.when(n == 0)
        def _():
            dkv_ref[0:BLK, :] += contrib[BLK:]

    return pl.pallas_call(
        body, name="attn_bwd",
        grid=(S // BLK,),
        in_specs=_attn_specs(2),
        out_specs=[
            pl.BlockSpec((BLK, D), lambda n: (n, 0)),
            pl.BlockSpec((S, 4 * HD), lambda n: (0, 0)),
            pl.BlockSpec((1, LANES), lambda n: (0, 0)),
        ],
        out_shape=[
            jax.ShapeDtypeStruct((S, D), MX),
            jax.ShapeDtypeStruct((S, 4 * HD), F32),
            jax.ShapeDtypeStruct((1, LANES), F32),
        ],
        compiler_params=_cp(),
    )(q, do, kv, kv, sinks)


def _ew_rows(rows, cols, n_bufs=1):
    br = rows
    while br * cols * 4 * n_bufs > EW_BYTES and br % (2 * SUBLANES) == 0:
        br //= 2
    return br


def _adamw(parts, w, m, v):
    L, R, C = w.shape
    br = _ew_rows(R, C)
    npart = len(parts[0])

    def body(*refs):
        p_refs = refs[:L * npart]
        w_ref, m_ref, v_ref, g_ref, d_ref, nm_ref, nv_ref = refs[L * npart:]
        lyr = pl.program_id(0)
        for l in range(L):
            @pl.when(lyr == l)
            def _(l=l):
                g = p_refs[l * npart][...]
                for pr in p_refs[l * npart + 1:(l + 1) * npart]:
                    g = g + pr[...]
                nm = B1 * m_ref[...] + (1.0 - B1) * g
                nv = B2 * v_ref[...] + (1.0 - B2) * (g * g)
                m_hat = nm / (1.0 - B1 ** STEP)
                v_hat = nv / (1.0 - B2 ** STEP)
                g_ref[...] = g
                d_ref[...] = -LR * (m_hat / (jnp.sqrt(v_hat) + AEPS) + WD * w_ref[...])
                nm_ref[...] = nm
                nv_ref[...] = nv

    spec = pl.BlockSpec((None, br, C), lambda a, i: (a, i, 0))
    part_specs = [pl.BlockSpec((br, C), lambda a, i, l=l: (jnp.where(a == l, i, 0), 0))
                  for l in range(L) for _ in range(npart)]
    return pl.pallas_call(
        body, name="adamw",
        grid=(L, R // br),
        in_specs=part_specs + [spec] * 3,
        out_specs=[spec] * 4,
        out_shape=[jax.ShapeDtypeStruct((L, R, C), F32)] * 4,
        compiler_params=_cp(2),
    )(*[a for lp in parts for a in lp], w, m, v)


def _coords():
    return lax.axis_index("x"), lax.axis_index("y"), lax.axis_index("c")


def _other_chips(x, y):
    return [(1 - x, y), (x, 1 - y), (1 - x, 1 - y)]


def _slot(ref, axis, chip, size):
    idx = [slice(None)] * 3
    idx[axis] = pl.ds(pl.multiple_of(chip * size, size), size)
    return ref.at[tuple(idx)]


HBM_SPEC = pl.BlockSpec(memory_space=pltpu.HBM)
SEM_SPEC = pl.BlockSpec(memory_space=pltpu.SEMAPHORE)
ANY_SPEC = pl.BlockSpec(memory_space=pl.ANY)
EFFECT = pltpu.SideEffectType.DATAFLOW_SIDE_EFFECTING


def _slot_specs(shape, axis, br, lead):
    _, b, c = shape
    nrb = b // br
    first = (lambda a: a) if lead is None else (lambda a: lead)
    shard = pl.BlockSpec((None, br, c), lambda a, i, me: (first(a), i, 0))
    if axis == 1:
        slot = pl.BlockSpec((None, br, c), lambda a, i, me: (a, me[0] * nrb + i, 0))
    else:
        slot = pl.BlockSpec((None, br, c), lambda a, i, me: (a, i, me[0]))
    return shard, slot


def _shard_rows(b, c):
    br = b
    while br * c * 4 > 2 * EW_BYTES and br % (4 * SUBLANES) == 0:
        br //= 2
    return br


def _gather_place(shard, axis, me, dtype, lead=None):
    a_dim, b, c = shard.shape
    if lead is not None:
        a_dim = 1
    br = _shard_rows(b, c)
    shp = [a_dim, b, c]
    shp[axis] *= 4
    shard_spec, slot_spec = _slot_specs((a_dim, b, c), axis, br, lead)

    def body(me_ref, s_ref, o_ref):
        o_ref[...] = s_ref[...].astype(o_ref.dtype)

    return pl.pallas_call(
        body, name="gather_place",
        grid_spec=pltpu.PrefetchScalarGridSpec(
            num_scalar_prefetch=1, grid=(a_dim, b // br), in_specs=[shard_spec], out_specs=slot_spec),
        out_shape=jax.ShapeDtypeStruct(tuple(shp), dtype),
        compiler_params=_cp(2),
    )(me, shard)


def _sum_landed(grad, landed, axis, me):
    a_dim, b, c = landed.shape[1:]
    br = _shard_rows(b, c)
    shard_spec, slot_spec = _slot_specs((a_dim, b, c), axis, br, None)

    def body(me_ref, own_ref, r_ref, o_ref):
        o_ref[...] = ((own_ref[...].astype(F32) + r_ref[0].astype(F32)) + r_ref[1].astype(F32)) + r_ref[2].astype(F32)

    return pl.pallas_call(
        body, name="sum_landed",
        grid_spec=pltpu.PrefetchScalarGridSpec(
            num_scalar_prefetch=1, grid=(a_dim, b // br),
            in_specs=[slot_spec, pl.BlockSpec((3, None, br, c), lambda a, i, me: (0, a, i, 0))],
            out_specs=shard_spec),
        out_shape=jax.ShapeDtypeStruct((a_dim, b, c), F32),
        compiler_params=_cp(2),
    )(me, grad, landed)


def _copies(refs, plan, send, recv, to_sibling):
    x, y, c = _coords()
    me = 2 * x + y
    if to_sibling == "everyone":
        me = 4 * x + 2 * y + c
        flips = [(k >> 2, (k >> 1) & 1, k & 1) for k in range(1, 8)]
        targets = [((x ^ fx, y ^ fy, c ^ fc), 4 * (x ^ fx) + 2 * (y ^ fy) + (c ^ fc)) for fx, fy, fc in flips]
    elif to_sibling:
        targets = [((x, y, 1 - c), me)]
    else:
        targets = [((px, py, c), 2 * px + py) for px, py in _other_chips(x, y)]
    out, t = [], 0
    while plan(refs, me, t, 0, me) is not None:
        for k, (device, peer) in enumerate(targets):
            sv, dv = plan(refs, me, t, k, peer)
            n = len(targets) * t + k
            out.append(pltpu.make_async_remote_copy(
                src_ref=sv, dst_ref=dv, send_sem=send.at[n], recv_sem=recv.at[n],
                device_id=device, device_id_type=MESH))
        t += 1
    return out


def _push_start(name, bufs, n_copies, plan, to_sibling=False):
    nb = len(bufs)

    def body(*refs):
        send, recv, token = refs[nb], refs[nb + 1], refs[-1]
        for cp in _copies(refs[:nb], plan, send, recv, to_sibling):
            cp.start()
        token[...] = jnp.zeros_like(token)

    res = pl.pallas_call(
        body, name=name,
        in_specs=[HBM_SPEC] * nb,
        out_specs=[SEM_SPEC, SEM_SPEC] + [HBM_SPEC] * nb + [pl.BlockSpec(memory_space=pltpu.VMEM)],
        out_shape=[pltpu.SemaphoreType.DMA((n_copies,)), pltpu.SemaphoreType.DMA((n_copies,))]
        + [pltpu.HBM(a.shape, a.dtype) for a in bufs] + [jax.ShapeDtypeStruct((SUBLANES, LANES), F32)],
        input_output_aliases={i: 2 + i for i in range(nb)},
        compiler_params=pltpu.CompilerParams(has_side_effects=EFFECT),
    )(*[pltpu.with_memory_space_constraint(a, pltpu.HBM) for a in bufs])
    return res[0], res[1], res[2:2 + nb], res[-1]


def _push_wait(name, send, recv, bufs, plan, after, to_sibling=False):
    nb = len(bufs)

    def body(*refs):
        for cp in _copies(refs[:nb], plan, refs[nb], refs[nb + 1], to_sibling):
            cp.wait_send()
            cp.wait_recv()

    return pl.pallas_call(
        body, name=name,
        in_specs=[HBM_SPEC] * nb + [SEM_SPEC, SEM_SPEC, ANY_SPEC],
        out_specs=[HBM_SPEC] * nb,
        out_shape=[pltpu.HBM(a.shape, a.dtype) for a in bufs],
        input_output_aliases={i: i for i in range(nb)},
        compiler_params=pltpu.CompilerParams(has_side_effects=EFFECT),
    )(*bufs, send, recv, after)


def _gather_plan(axes):
    def plan(refs, me, t, k, peer):
        if t >= len(axes):
            return None
        size = refs[t].shape[axes[t]] // 4
        mine = _slot(refs[t], axes[t], me, size)
        return mine, mine
    return plan


def _half_slot(ref, axis, chip):
    c = lax.axis_index("c")
    if axis == 1:
        half = ref.shape[1] // 8
        return ref.at[:, pl.ds(pl.multiple_of(chip * 2 * half + c * half, 2 * SUBLANES), half), :]
    half = ref.shape[1] // 2
    size = ref.shape[2] // 4
    return ref.at[:, pl.ds(pl.multiple_of(c * half, 2 * SUBLANES), half), pl.ds(pl.multiple_of(chip * size, LANES), size)]


def _gather_half_plan(axes):
    def plan(refs, me, t, k, peer):
        if t >= len(axes):
            return None
        mine = _half_slot(refs[t], axes[t], me)
        return mine, mine
    return plan


def _gather_pass_plan(axes):
    def plan(refs, me, t, k, peer):
        if t >= 3 * len(axes):
            return None
        x, y, _ = _coords()
        px, py = _other_chips(x, y)[t % 3]
        landed = _half_slot(refs[t // 3], axes[t // 3], 2 * px + py)
        return landed, landed
    return plan


def _scatter_plan(axes):
    n = len(axes)

    def plan(refs, me, t, k, peer):
        if t >= n:
            return None
        size = refs[t].shape[axes[t]] // 4
        return _slot(refs[t], axes[t], peer, size), refs[n + t].at[k]
    return plan


def _swap_plan(n):
    def plan(refs, me, t, k, peer):
        if t >= n:
            return None
        return refs[t], refs[n + t]
    return plan


def _everyone_plan():
    def plan(refs, me, t, k, peer):
        if t >= 1:
            return None
        mine = refs[0].at[me]
        return mine, mine
    return plan


def _sum_slots(slots):
    R = slots.shape[1]

    def body(s_ref, o_ref):
        tot = s_ref[0]
        for k in range(1, 8):
            tot = tot + s_ref[k]
        o_ref[...] = tot

    vm = pl.BlockSpec(memory_space=pltpu.VMEM)
    return pl.pallas_call(
        body, name="sum_slots",
        in_specs=[vm],
        out_specs=vm,
        out_shape=jax.ShapeDtypeStruct((R, LANES), F32),
        compiler_params=pltpu.CompilerParams(vmem_limit_bytes=VMEM_LIMIT),
    )(slots)


def _pack(arrs):
    flat = []
    for a in arrs:
        f = a.reshape(-1).astype(F32)
        flat.append(jnp.pad(f, (0, (-f.shape[0]) % LANES)))
    v = jnp.concatenate(flat)
    v = jnp.pad(v, (0, (-v.shape[0]) % (SUBLANES * LANES)))
    return v.reshape(-1, LANES)


def _unpack(v, shapes):
    flat = v.reshape(-1)
    out, off = [], 0
    for shp in shapes:
        n = 1
        for d in shp:
            n *= d
        out.append(flat[off:off + n].reshape(shp))
        off += n + (-n) % LANES
    return out


def _local_step(x, tgt, sp, weights_for, on_grads):
    n1, n2 = sp["norm1_g"], sp["norm2_g"]
    w = dict(weights_for(0, x))
    saved = []
    xs = x
    kv = None
    for l in range(DEPTH):
        x_in = xs
        if l >= N_A:
            w.update(weights_for(1 + 2 * l - N_A, x_in))
        if l == N_A:
            kv = _rms_linear(x_in, sp["kv_norm_g"], w["w_kv"], sp["b_kv"], 0, 0, "kv_proj")
        if l < N_A:
            xa = _pool_fwd(x_in, n1, w["pool_w"], w["pool_scale"], l)
            q = o = None
        else:
            j = l - N_A
            q = _rms_linear(x_in, n1, w["w_q", j], sp["b_q"], l, j, "q_proj")
            o = _attn_fwd(q, kv, sp["sinks"], j)
            xa = _linear_res(o, w["w_o", j], sp["b_o"], x_in, j)
        w.update(weights_for(1 + l if l < N_A else 2 + 2 * l - N_A, xa))
        xs, u, uc = _ffn_fwd(xa, n2, w["ffn_up", l], w["ffn_down", l], w["ffn_conv_w"], sp["ffn_conv_b"], l)
        saved.append((x_in, xa, u, uc, q, o))

    dx, d_final_g, loss = _loss_head(xs, sp["final_g"], tgt)

    g = {k: [None] * DEPTH for k in ("norm1_g", "norm2_g", "ffn_conv_w", "ffn_conv_b")}
    for k in ("pool_scale", "b_q", "sinks", "b_o"):
        g[k] = [None] * N_A
    g["final_g"] = d_final_g
    dkvs = []
    pending = {}
    for l in reversed(range(DEPTH)):
        x_in, xa, u, uc, q, o = saved[l]
        dxa, du, a, hb, g["norm2_g"][l], g["ffn_conv_w"][l], g["ffn_conv_b"][l] = _ffn_bwd(
            xa, dx, u, uc, n2, w["ffn_up", l], w["ffn_down", l], w["ffn_conv_w"], l)
        pending["ffn_up", l] = _tn_matmul(hb, du, MX, "d_ffn_up", a_is_transposed=True)
        if l == 0:
            n1 = n1 + on_grads(DEPTH + 1, pending)
            pending = {}
        pending["ffn_down", l] = _tn_matmul(a, dx, MX, "d_ffn_down", a_is_transposed=True)
        zero = on_grads(DEPTH - 1 - l, pending)
        pending = {}
        n1, n2 = n1 + zero, n2 + zero
        if l < N_A:
            dx, d_pw, g["pool_scale"][l], g["norm1_g"][l] = _pool_bwd(
                x_in, dxa, n1, w["pool_w"], w["pool_scale"], l)
            pending["pool_w", l] = d_pw.astype(MX)
        else:
            j = l - N_A
            d_o, g["b_o"][j] = _linear_nt(dxa, w["w_o", j])
            pending["w_o", j] = _tn_matmul(o, dxa, MX, "d_w_o")
            dq, dkv, g["sinks"][j] = _attn_bwd(q, d_o, kv, sp["sinks"], j)
            dkvs.append(dkv)
            dx, g["norm1_g"][l], g["b_q"][j], hq, dqb = _rms_linear_bwd(
                x_in, n1, [dq], w["w_q", j], dxa, l, "q_proj_bwd")
            pending["w_q", j] = _tn_matmul(hq, dqb, MX, "d_w_q")
        if l == N_A:
            dx, g["kv_norm_g"], g["b_kv"], hk, dkvb = _rms_linear_bwd(
                x_in, sp["kv_norm_g"], dkvs, w["w_kv"], dx, 0, "kv_proj_bwd")
            pending["w_kv", 0] = _tn_matmul(hk, dkvb, MX, "d_w_kv")
    on_grads(DEPTH, pending)
    return loss, dx, g


SMALL = ("norm1_g", "norm2_g", "kv_norm_g", "b_kv", "b_q", "sinks", "b_o", "ffn_conv_b", "final_g")
SMALL_SHARDED = ("pool_scale", "ffn_conv_w")
BIG = ("pool_w", "w_kv", "w_q", "w_o", "ffn_up", "ffn_down")
EARLY_SWAP = 3
ORDER = ("norm1_g", "norm2_g", "pool_w", "pool_scale", "kv_norm_g", "w_kv", "b_kv", "w_q", "b_q", "sinks",
         "w_o", "b_o", "ffn_up", "ffn_conv_w", "ffn_conv_b", "ffn_down", "final_g")


def _as3d(a):
    return a.reshape((-1,) + a.shape[-2:])


def kernel(x, norm1_g, norm2_g, pool_w, pool_scale, kv_norm_g, w_kv, b_kv, w_q, b_q, sinks, w_o, b_o, ffn_up, ffn_conv_w, ffn_conv_b, ffn_down, final_g, loss_target, m_norm1_g, m_norm2_g, m_pool_w, m_pool_scale, m_kv_norm_g, m_w_kv, m_b_kv, m_w_q, m_b_q, m_sinks, m_w_o, m_b_o, m_ffn_up, m_ffn_conv_w, m_ffn_conv_b, m_ffn_down, m_final_g, v_norm1_g, v_norm2_g, v_pool_w, v_pool_scale, v_kv_norm_g, v_w_kv, v_b_kv, v_w_q, v_b_q, v_sinks, v_w_o, v_b_o, v_ffn_up, v_ffn_conv_w, v_ffn_conv_b, v_ffn_down, v_final_g):
    W = dict(norm1_g=norm1_g, norm2_g=norm2_g, pool_w=pool_w, pool_scale=pool_scale, kv_norm_g=kv_norm_g,
             w_kv=w_kv, b_kv=b_kv, w_q=w_q, b_q=b_q, sinks=sinks, w_o=w_o, b_o=b_o, ffn_up=ffn_up,
             ffn_conv_w=ffn_conv_w, ffn_conv_b=ffn_conv_b, ffn_down=ffn_down, final_g=final_g)
    M = dict(norm1_g=m_norm1_g, norm2_g=m_norm2_g, pool_w=m_pool_w, pool_scale=m_pool_scale,
             kv_norm_g=m_kv_norm_g, w_kv=m_w_kv, b_kv=m_b_kv, w_q=m_w_q, b_q=m_b_q, sinks=m_sinks, w_o=m_w_o,
             b_o=m_b_o, ffn_up=m_ffn_up, ffn_conv_w=m_ffn_conv_w, ffn_conv_b=m_ffn_conv_b, ffn_down=m_ffn_down,
             final_g=m_final_g)
    V = dict(norm1_g=v_norm1_g, norm2_g=v_norm2_g, pool_w=v_pool_w, pool_scale=v_pool_scale,
             kv_norm_g=v_kv_norm_g, w_kv=v_w_kv, b_kv=v_b_kv, w_q=v_w_q, b_q=v_b_q, sinks=v_sinks, w_o=v_w_o,
             b_o=v_b_o, ffn_up=v_ffn_up, ffn_conv_w=v_ffn_conv_w, ffn_conv_b=v_ffn_conv_b, ffn_down=v_ffn_down,
             final_g=v_final_g)
    S = x.shape[1]
    chip = 2 * lax.axis_index("x") + lax.axis_index("y")

    gather_axis = dict(pool_w=1, w_kv=1, w_q=1, w_o=1, ffn_up=2, ffn_down=1, pool_scale=2, ffn_conv_w=2)
    me = chip.reshape(1).astype(jnp.int32)
    axis_of = lambda key: gather_axis[key if isinstance(key, str) else key[0]]

    def placed(key, dtype):
        if isinstance(key, str):
            return _gather_place(_as3d(W[key]), axis_of(key), me, dtype)
        return _gather_place(W[key[0]], axis_of(key), me, dtype, lead=key[1])

    stages = [
        ["pool_w", "pool_scale", "ffn_conv_w"],
        [("ffn_up", 0), ("ffn_down", 0)],
        [("ffn_up", 1), ("ffn_down", 1)],
        ["w_kv", ("w_q", 0), ("w_o", 0)],
        [("ffn_up", 2), ("ffn_down", 2)],
        [("w_q", 1), ("w_o", 1)],
        [("ffn_up", 3), ("ffn_down", 3)],
    ]
    TWO_LEVEL = 1
    gathers, zero = [], 0.0
    for si, keys in enumerate(stages):
        axes = [axis_of(k) for k in keys]
        bufs = [placed(k, F32 if k in SMALL_SHARDED else MX) for k in keys]
        plan = _gather_half_plan(axes) if si == TWO_LEVEL else _gather_plan(axes)
        send, recv, bufs, token = _push_start(f"gather_start_{si}", bufs, 3 * len(keys), plan)
        gathers.append((keys, axes, send, recv, bufs))
        zero = zero + token[0, 0]

    def weights_for(stage, after):
        keys, axes, send, recv, bufs = gathers[stage]
        if stage == TWO_LEVEL:
            bufs = _push_wait(f"gather_wait_{stage}", send, recv, bufs, _gather_half_plan(axes), after)
            send, recv, bufs, _ = _push_start("gather_pass_start", bufs, 3 * len(keys), _gather_pass_plan(axes), True)
            bufs = _push_wait("gather_pass_wait", send, recv, bufs, _gather_pass_plan(axes), after, True)
        else:
            bufs = _push_wait(f"gather_wait_{stage}", send, recv, bufs, _gather_plan(axes), after)
        out = dict(zip(keys, bufs))
        if stage == 0:
            out["pool_w"] = out["pool_w"].reshape(N_A, 4, GC, GC)
            out["pool_scale"] = out["pool_scale"].reshape(N_A, 1, D)
        return out

    scatters = []

    def on_grads(stage, grads):
        keys = list(grads)
        axes = [axis_of(k) for k in keys]
        arrs = [_as3d(grads[k]) for k in keys]
        lands = []
        for a, ax in zip(arrs, axes):
            shp = list(a.shape)
            shp[ax] //= 4
            lands.append(lax.empty((3,) + tuple(shp), a.dtype))
        send, recv, bufs, token = _push_start(f"scatter_start_{stage}", arrs + lands, 3 * len(keys), _scatter_plan(axes))
        scatters.append((stage, keys, axes, send, recv, bufs))
        return token[0, 0]

    sp = dict(
        norm1_g=norm1_g.reshape(DEPTH, 1, D) + zero, norm2_g=norm2_g.reshape(DEPTH, 1, D),
        kv_norm_g=kv_norm_g.reshape(1, 1, D), b_kv=b_kv.reshape(1, 1, 4 * HD), b_q=b_q.reshape(N_B, 1, D),
        sinks=sinks, b_o=b_o.reshape(N_B, 1, D), ffn_conv_b=ffn_conv_b.reshape(DEPTH, 1, F2),
        final_g=final_g.reshape(1, D))

    x2d = x.reshape(S, D)
    loss, grad_x, g = _local_step(x2d, loss_target.reshape(S, D), sp, weights_for, on_grads)

    small_full = dict(
        norm1_g=jnp.stack(g["norm1_g"]), norm2_g=jnp.stack(g["norm2_g"]), kv_norm_g=g["kv_norm_g"],
        b_kv=g["b_kv"], b_q=jnp.stack(g["b_q"]), sinks=jnp.stack([s[0, :NH] for s in g["sinks"]]),
        b_o=jnp.stack(g["b_o"]), ffn_conv_b=jnp.stack(g["ffn_conv_b"]), final_g=g["final_g"],
        pool_scale=jnp.stack(g["pool_scale"]), ffn_conv_w=jnp.stack(g["ffn_conv_w"]))
    small_names = SMALL + SMALL_SHARDED
    small_shapes = [tuple(W[k].shape) for k in SMALL] + [(N_A, D), (DEPTH, 3, F2)]
    packed = _pack([small_full[k] for k in small_names] + [loss])
    slots = lax.dynamic_update_slice(lax.empty((8,) + packed.shape, F32), packed[None],
                                     (4 * lax.axis_index("x") + 2 * lax.axis_index("y") + lax.axis_index("c"), 0, 0))
    red_send, red_recv, slots, _ = _push_start("reduce_start", [slots], 7, _everyone_plan(), "everyone")

    pkeys, partial, swaps, after = [], [], [], grad_x

    def swap_start(tag, first):
        mine_now = partial[first:]
        lands = [lax.empty(p.shape, p.dtype) for p in mine_now]
        n = len(mine_now)
        send, recv, bufs, token = _push_start(f"swap_start_{tag}", mine_now + lands, n, _swap_plan(n), True)
        swaps.append((tag, n, send, recv, bufs))
        return token

    for i, (stage, keys, axes, send, recv, bufs) in enumerate(scatters):
        bufs = _push_wait(f"scatter_wait_{stage}", send, recv, bufs, _scatter_plan(axes), after)
        n = len(keys)
        for k, ax, grad, landed in zip(keys, axes, bufs[:n], bufs[n:]):
            pkeys.append(k)
            p_sum = _sum_landed(grad, landed, ax, me)
            partial.append(p_sum.reshape(-1, p_sum.shape[-1]))
        if i == EARLY_SWAP - 1:
            after = swap_start("early", 0)
    after = swap_start("late", sum(n for _, n, *_ in swaps))

    slots = _push_wait("reduce_wait", red_send, red_recv, slots, _everyone_plan(), after, "everyone")[0]
    red = _unpack(_sum_slots(slots), small_shapes + [(1, LANES)])
    red_g = dict(zip(small_names, red[:-1]))
    loss_out = red[-1][0, 0]
    red_g["pool_scale"] = lax.dynamic_slice_in_dim(red_g["pool_scale"], chip * (D // 4), D // 4, axis=1)
    red_g["ffn_conv_w"] = lax.dynamic_slice_in_dim(red_g["ffn_conv_w"], chip * (F2 // 4), F2 // 4, axis=2)
    small_w_shapes = [tuple(W[k].shape) for k in small_names]
    pk = lambda d: _pack([d[k] for k in small_names])[None]
    res = _adamw([[_pack([red_g[k] for k in small_names])]], pk(W), pk(M), pk(V))
    out_g, out_d, out_m, out_v = [dict(zip(small_names, _unpack(r, small_w_shapes))) for r in res]
    after = res[0]

    mine, theirs = [], []
    for tag, n, send, recv, bufs in swaps:
        bufs = _push_wait(f"swap_wait_{tag}", send, recv, bufs, _swap_plan(n), after, True)
        mine += bufs[:n]
        theirs += bufs[n:]
    mine = dict(zip(pkeys, mine))
    theirs = dict(zip(pkeys, theirs))
    for k in BIG:
        n_l = len([pk_ for pk_ in pkeys if pk_[0] == k])
        shp = W[k].shape
        rows, cols = mine[k, 0].shape
        three_d = lambda a: a.reshape(n_l, rows, cols)
        res = _adamw([[mine[k, l], theirs[k, l]] for l in range(n_l)], three_d(W[k]), three_d(M[k]), three_d(V[k]))
        out_g[k], out_d[k], out_m[k], out_v[k] = [r.reshape(shp) for r in res]

    return (loss_out, grad_x.reshape(x.shape), *[out_g[k] for k in ORDER], *[out_d[k] for k in ORDER],
            *[out_m[k] for k in ORDER], *[out_v[k] for k in ORDER])
```

```python
import functools

import jax
import jax.numpy as jnp
from jax import lax
from jax.experimental import pallas as pl
from jax.experimental.pallas import tpu as pltpu

D = 1024
DEPTH = 4
N_A = 2
N_B = 2
WINS = (2, 4, 8, 16)
GC = 256
HD = 64
NH = 16
BLK = 128
F = 2816
F2 = 2 * F
EPS = 1e-5
SCALE = HD ** -0.5
NEG = -1e30
HALO = 16
TN = 256
TM_STREAM = 1024
UP_GROUP = 3
LANES = 128
SUBLANES = 8
VMEM_LIMIT = 56 * 1024 * 1024
FFN_VMEM_LIMIT = 60 * 1024 * 1024
ACC_BYTES = 6 * 1024 * 1024
EW_BYTES = 1536 * 1024

LR, B1, B2, AEPS, WD, STEP = 0.001, 0.9, 0.999, 1e-08, 0.01, 10

MX = jnp.bfloat16
F32 = jnp.float32
MESH = pl.DeviceIdType.MESH


def _cp(n_axes=1, vmem=VMEM_LIMIT):
    return pltpu.CompilerParams(dimension_semantics=("arbitrary",) * n_axes, vmem_limit_bytes=vmem)


def _dot(a, b):
    return jnp.dot(a, b, preferred_element_type=F32)


def _dot_nt(a, b):
    return lax.dot_general(a, b, (((1,), (1,)), ((), ())), preferred_element_type=F32)


def _dot_tn(a, b):
    return lax.dot_general(a, b, (((0,), (0,)), ((), ())), preferred_element_type=F32)


def _rms_fwd(x, g):
    r = lax.rsqrt(jnp.mean(x * x, axis=-1, keepdims=True) + EPS)
    xh = x * r
    return xh * g, xh, r


def _rms_bwd(dh, xh, r, g):
    dxh = dh * g
    return r * (dxh - xh * jnp.mean(dxh * xh, axis=-1, keepdims=True))


def _row_tile(s, want):
    return min(s, want)


def _pool_pm(e, h, row, tm):
    out = []
    for gi, win in enumerate(WINS):
        cols = slice(gi * GC, (gi + 1) * GC)
        s = e[:, cols]
        sh = 1
        while sh < win:
            s = s + pltpu.roll(s, sh, 0)
            sh *= 2
        inv = 1.0 / jnp.minimum(row + 1, win).astype(F32)
        out.append(s[HALO:] * inv - h[:, cols])
    return out


def _pool_fwd(x, g, pw, ps, layer):
    S = x.shape[0]
    tm = _row_tile(S, 512)
    hb = tm // HALO

    def body(x_ref, xh_ref, g_ref, pw_ref, ps_ref, o_ref):
        i = pl.program_id(0)
        x = x_ref[...]
        gg = g_ref[...]
        h, _, _ = _rms_fwd(x, gg)
        hh, _, _ = _rms_fwd(xh_ref[...], gg)
        hh = jnp.where(i > 0, hh, 0.0)
        e = jnp.concatenate([hh, h], axis=0)
        row = i * tm + lax.broadcasted_iota(jnp.int32, (tm, 1), 0)
        pm = _pool_pm(e, h, row, tm)
        for gi in range(len(WINS)):
            cols = slice(gi * GC, (gi + 1) * GC)
            z = _dot(pm[gi].astype(MX), pw_ref[gi])
            o_ref[:, cols] = x[:, cols] + z * ps_ref[:, cols]

    return pl.pallas_call(
        body, name="pool_fwd",
        grid=(S // tm,),
        in_specs=[
            pl.BlockSpec((tm, D), lambda i: (i, 0)),
            pl.BlockSpec((HALO, D), lambda i: (jnp.maximum(i * hb - 1, 0), 0)),
            pl.BlockSpec((None, 1, D), lambda i: (layer, 0, 0)),
            pl.BlockSpec((None, 4, GC, GC), lambda i: (layer, 0, 0, 0)),
            pl.BlockSpec((None, 1, D), lambda i: (layer, 0, 0)),
        ],
        out_specs=pl.BlockSpec((tm, D), lambda i: (i, 0)),
        out_shape=jax.ShapeDtypeStruct((S, D), F32),
        compiler_params=_cp(),
    )(x, x, g, pw, ps)


def _pool_bwd(x, dy, g, pw, ps, layer):
    S = x.shape[0]
    tm = _row_tile(S, 256)
    hb = tm // HALO
    n_i = S // tm
    n_h = S // HALO

    def body(x_ref, xh_ref, dy_ref, dyn_ref, g_ref, pw_ref, ps_ref, dx_ref, dpw_ref, dps_ref, dg_ref):
        i = pl.program_id(0)

        @pl.when(i == 0)
        def _():
            dpw_ref[...] = jnp.zeros_like(dpw_ref)
            dps_ref[...] = jnp.zeros_like(dps_ref)
            dg_ref[...] = jnp.zeros_like(dg_ref)

        x = x_ref[...]
        gg = g_ref[...]
        ps = ps_ref[...]
        h, xh, r = _rms_fwd(x, gg)
        hh, _, _ = _rms_fwd(xh_ref[...], gg)
        hh = jnp.where(i > 0, hh, 0.0)
        e = jnp.concatenate([hh, h], axis=0)
        row = i * tm + lax.broadcasted_iota(jnp.int32, (tm, 1), 0)
        rown = (i + 1) * tm + lax.broadcasted_iota(jnp.int32, (HALO, 1), 0)
        pm = _pool_pm(e, h, row, tm)
        dy = dy_ref[...]
        dz = dy * ps
        dzn = jnp.where(i < n_i - 1, dyn_ref[...] * ps, 0.0)
        parts = []
        for gi, win in enumerate(WINS):
            cols = slice(gi * GC, (gi + 1) * GC)
            w = pw_ref[gi]
            pmb = pm[gi].astype(MX)
            z = _dot(pmb, w)
            dps_ref[:, cols] += jnp.sum(dy[:, cols] * z, axis=0, keepdims=True)
            dzb = dz[:, cols].astype(MX)
            dpw_ref[gi] += _dot_tn(pmb, dzb)
            dpm = _dot_nt(dzb, w)
            dpmn = _dot_nt(dzn[:, cols].astype(MX), w)
            q = dpm * (1.0 / jnp.minimum(row + 1, win).astype(F32))
            qn = dpmn * (1.0 / jnp.minimum(rown + 1, win).astype(F32))
            s = jnp.concatenate([q, qn], axis=0)
            sh = 1
            while sh < win:
                s = s + pltpu.roll(s, tm + HALO - sh, 0)
                sh *= 2
            parts.append(s[:tm] - dpm)
        dh = jnp.concatenate(parts, axis=1)
        dg_ref[...] += jnp.sum(dh * xh, axis=0, keepdims=True)
        dx_ref[...] = dy + _rms_bwd(dh, xh, r, gg)

    return pl.pallas_call(
        body, name="pool_bwd",
        grid=(n_i,),
        in_specs=[
            pl.BlockSpec((tm, D), lambda i: (i, 0)),
            pl.BlockSpec((HALO, D), lambda i: (jnp.maximum(i * hb - 1, 0), 0)),
            pl.BlockSpec((tm, D), lambda i: (i, 0)),
            pl.BlockSpec((HALO, D), lambda i: (jnp.minimum((i + 1) * hb, n_h - 1), 0)),
            pl.BlockSpec((None, 1, D), lambda i: (layer, 0, 0)),
            pl.BlockSpec((None, 4, GC, GC), lambda i: (layer, 0, 0, 0)),
            pl.BlockSpec((None, 1, D), lambda i: (layer, 0, 0)),
        ],
        out_specs=[
            pl.BlockSpec((tm, D), lambda i: (i, 0)),
            pl.BlockSpec((4, GC, GC), lambda i: (0, 0, 0)),
            pl.BlockSpec((1, D), lambda i: (0, 0)),
            pl.BlockSpec((1, D), lambda i: (0, 0)),
        ],
        out_shape=[
            jax.ShapeDtypeStruct((S, D), F32),
            jax.ShapeDtypeStruct((4, GC, GC), F32),
            jax.ShapeDtypeStruct((1, D), F32),
            jax.ShapeDtypeStruct((1, D), F32),
        ],
        compiler_params=_cp(),
    )(x, x, dy, dy, g, pw, ps)


N_STAGE = 4


def _rows_before(slot, u, prev8):
    tm = u.shape[0]
    m1, m2 = [], []
    for c in range(TN // LANES):
        lanes = slice(c * LANES, (c + 1) * LANES)
        slot[c, 0:SUBLANES, :] = prev8[:, lanes]
        slot[c, SUBLANES:SUBLANES + tm, :] = u[:, lanes]
        m1.append(slot[c, pl.ds(SUBLANES - 1, tm), :])
        m2.append(slot[c, pl.ds(SUBLANES - 2, tm), :])
    return jnp.concatenate(m1, axis=1), jnp.concatenate(m2, axis=1)


def _rows_after(slot, d, next8):
    tm = d.shape[0]
    p1, p2 = [], []
    for c in range(TN // LANES):
        lanes = slice(c * LANES, (c + 1) * LANES)
        slot[c, 0:tm, :] = d[:, lanes]
        slot[c, tm:tm + SUBLANES, :] = next8[:, lanes]
        p1.append(slot[c, pl.ds(1, tm), :])
        p2.append(slot[c, pl.ds(2, tm), :])
    return jnp.concatenate(p1, axis=1), jnp.concatenate(p2, axis=1)


def _conv(slot, u, prev8, cw):
    um1, um2 = _rows_before(slot, u, prev8)
    return cw[0:1] * um2 + cw[1:2] * um1 + cw[2:3] * u


def _ffn_fwd(x, g, wup, wdn, cw, cb, layer):
    S = x.shape[0]
    tm = _row_tile(S, 512)

    def body(x_ref, g_ref, wup_hbm, wdn_hbm, cw_ref, cb_ref, o_ref, u_ref, uc_ref, wup_v, wdn_v, carry, act, stage):
        i = pl.program_id(0)

        @pl.when(i == 0)
        def _():
            pltpu.sync_copy(wup_hbm.at[0], wup_v)
            pltpu.sync_copy(wdn_hbm.at[0], wdn_v)
            carry[...] = jnp.zeros_like(carry)

        x = x_ref[...]
        h, _, _ = _rms_fwd(x, g_ref[...])
        hb = h.astype(MX)
        for j in range(F // TN):
            cg = slice(j * TN, (j + 1) * TN)
            cv = slice(F + j * TN, F + (j + 1) * TN)
            ug = _dot(hb, wup_v[:, cg])
            uv = _dot(hb, wup_v[:, cv])
            u_ref[:, cg] = ug.astype(u_ref.dtype)
            u_ref[:, cv] = uv.astype(u_ref.dtype)
            gt = _conv(stage.at[2 * (j % 2)], ug, carry[:, cg], cw_ref[:, cg])
            vl = _conv(stage.at[2 * (j % 2) + 1], uv, carry[:, cv], cw_ref[:, cv])
            carry[:, cg] = ug[tm - SUBLANES:]
            carry[:, cv] = uv[tm - SUBLANES:]
            gt = gt + cb_ref[:, cg]
            vl = vl + cb_ref[:, cv]
            uc_ref[:, cg] = gt.astype(uc_ref.dtype)
            uc_ref[:, cv] = vl.astype(uc_ref.dtype)
            act[:, cg] = (gt * jax.nn.sigmoid(gt) * vl).astype(act.dtype)
        o_ref[...] = x + _dot(act[...], wdn_v[...])

    return pl.pallas_call(
        body, name="ffn_fwd",
        grid=(S // tm,),
        in_specs=[
            pl.BlockSpec((tm, D), lambda i: (i, 0)),
            pl.BlockSpec((None, 1, D), lambda i: (layer, 0, 0)),
            pl.BlockSpec(memory_space=pl.ANY),
            pl.BlockSpec(memory_space=pl.ANY),
            pl.BlockSpec((None, 3, F2), lambda i: (layer, 0, 0)),
            pl.BlockSpec((None, 1, F2), lambda i: (layer, 0, 0)),
        ],
        out_specs=[
            pl.BlockSpec((tm, D), lambda i: (i, 0)),
            pl.BlockSpec((tm, F2), lambda i: (i, 0)),
            pl.BlockSpec((tm, F2), lambda i: (i, 0)),
        ],
        out_shape=[
            jax.ShapeDtypeStruct((S, D), F32),
            jax.ShapeDtypeStruct((S, F2), MX),
            jax.ShapeDtypeStruct((S, F2), MX),
        ],
        scratch_shapes=[
            pltpu.VMEM((D, F2), MX),
            pltpu.VMEM((F, D), MX),
            pltpu.VMEM((SUBLANES, F2), F32),
            pltpu.VMEM((tm, F), MX),
            pltpu.VMEM((N_STAGE, TN // LANES, tm + SUBLANES, LANES), F32),
        ],
        compiler_params=_cp(vmem=FFN_VMEM_LIMIT),
    )(x, g, wup, wdn, cw, cb)


def _ffn_bwd(x, dy, u, uc, g, wup, wdn, cw, layer):
    S = x.shape[0]
    tm = _row_tile(S, 256)
    n_i = S // tm

    def body(x_ref, dy_ref, u_ref, uc_ref, g_ref, wup_hbm, wdn_hbm, cw_ref,
             dx_ref, du_ref, a_ref, h_ref, dg_ref, dcw_ref, dcb_ref, wup_v, wdn_v, carry, stage):
        i = pl.program_id(0)

        @pl.when(i == 0)
        def _():
            pltpu.sync_copy(wup_hbm.at[0], wup_v)
            pltpu.sync_copy(wdn_hbm.at[0], wdn_v)
            carry[...] = jnp.zeros_like(carry)
            dg_ref[...] = jnp.zeros_like(dg_ref)
            dcw_ref[...] = jnp.zeros_like(dcw_ref)
            dcb_ref[...] = jnp.zeros_like(dcb_ref)

        x = x_ref[...]
        gg = g_ref[...]
        h, xh, r = _rms_fwd(x, gg)
        h_ref[...] = h.T.astype(h_ref.dtype)
        dy = dy_ref[...]
        dyb = dy.astype(MX)
        dh, dus = None, ([], [])
        for j in range(F // TN):
            cg = slice(j * TN, (j + 1) * TN)
            cv = slice(F + j * TN, F + (j + 1) * TN)
            gt = uc_ref[:, cg].astype(F32)
            vl = uc_ref[:, cv].astype(F32)
            sg = jax.nn.sigmoid(gt)
            sil = gt * sg
            a_ref[cg, :] = (sil * vl).T.astype(a_ref.dtype)
            da = _dot_nt(dyb, wdn_v[cg, :])
            dvl = da * sil
            dgt = (da * vl) * (sg + sil * (1.0 - sg))
            for cc, dd in ((cg, dgt), (cv, dvl)):
                dp1, dp2 = _rows_after(stage.at[2 * (j % 2) + (cc is cv)], dd, carry[:, cc])
                carry[:, cc] = dd[0:SUBLANES]
                uu = u_ref[:, cc].astype(F32)
                dcb_ref[:, cc] += jnp.sum(dd, axis=0, keepdims=True)
                dcw_ref[0:1, cc] += jnp.sum(dp2 * uu, axis=0, keepdims=True)
                dcw_ref[1:2, cc] += jnp.sum(dp1 * uu, axis=0, keepdims=True)
                dcw_ref[2:3, cc] += jnp.sum(dd * uu, axis=0, keepdims=True)
                cwc = cw_ref[:, cc]
                duu = (cwc[2:3] * dd + cwc[1:2] * dp1 + cwc[0:1] * dp2).astype(MX)
                du_ref[:, cc] = duu
                dus[cc is cv].append(duu)
            if len(dus[0]) == UP_GROUP or j == F // TN - 1:
                first = j + 1 - len(dus[0])
                for side, base in ((0, 0), (1, F)):
                    cols = slice(base + first * TN, base + (j + 1) * TN)
                    part = _dot_nt(jnp.concatenate(dus[side], axis=1), wup_v[:, cols])
                    dh = part if dh is None else dh + part
                dus = ([], [])
        dg_ref[...] += jnp.sum(dh * xh, axis=0, keepdims=True)
        dx_ref[...] = dy + _rms_bwd(dh, xh, r, gg)

    rev = lambda i: (n_i - 1 - i, 0)
    return pl.pallas_call(
        body, name="ffn_bwd",
        grid=(n_i,),
        in_specs=[
            pl.BlockSpec((tm, D), rev),
            pl.BlockSpec((tm, D), rev),
            pl.BlockSpec((tm, F2), rev),
            pl.BlockSpec((tm, F2), rev),
            pl.BlockSpec((None, 1, D), lambda i: (layer, 0, 0)),
            pl.BlockSpec(memory_space=pl.ANY),
            pl.BlockSpec(memory_space=pl.ANY),
            pl.BlockSpec((None, 3, F2), lambda i: (layer, 0, 0)),
        ],
        out_specs=[
            pl.BlockSpec((tm, D), rev),
            pl.BlockSpec((tm, F2), rev),
            pl.BlockSpec((F, tm), lambda i: (0, n_i - 1 - i)),
            pl.BlockSpec((D, tm), lambda i: (0, n_i - 1 - i)),
            pl.BlockSpec((1, D), lambda i: (0, 0)),
            pl.BlockSpec((3, F2), lambda i: (0, 0)),
            pl.BlockSpec((1, F2), lambda i: (0, 0)),
        ],
        out_shape=[
            jax.ShapeDtypeStruct((S, D), F32),
            jax.ShapeDtypeStruct((S, F2), MX),
            jax.ShapeDtypeStruct((F, S), MX),
            jax.ShapeDtypeStruct((D, S), MX),
            jax.ShapeDtypeStruct((1, D), F32),
            jax.ShapeDtypeStruct((3, F2), F32),
            jax.ShapeDtypeStruct((1, F2), F32),
        ],
        scratch_shapes=[
            pltpu.VMEM((D, F2), MX),
            pltpu.VMEM((F, D), MX),
            pltpu.VMEM((SUBLANES, F2), F32),
            pltpu.VMEM((N_STAGE, TN // LANES, tm + SUBLANES, LANES), F32),
        ],
        compiler_params=_cp(vmem=FFN_VMEM_LIMIT),
    )(x, dy, u, uc, g, wup, wdn, cw)


def _tn_matmul(a, b, out_dtype, name, a_is_transposed=False):
    S, N = b.shape
    M = a.shape[0] if a_is_transposed else a.shape[1]
    bn = N
    while M * bn * 4 > ACC_BYTES and bn % (2 * LANES) == 0:
        bn //= 2
    bk = _row_tile(S, 2048)
    nk = S // bk
    a_spec = pl.BlockSpec((M, bk), lambda j, k: (0, k)) if a_is_transposed else pl.BlockSpec((bk, M), lambda j, k: (k, 0))

    def body(a_ref, b_ref, o_ref, acc):
        k = pl.program_id(1)
        if a_is_transposed:
            p = _dot(a_ref[...].astype(MX), b_ref[...].astype(MX))
        else:
            p = _dot_tn(a_ref[...].astype(MX), b_ref[...].astype(MX))

        @pl.when(k == 0)
        def _():
            acc[...] = p

        @pl.when(k > 0)
        def _():
            acc[...] += p

        @pl.when(k == nk - 1)
        def _():
            o_ref[...] = acc[...].astype(o_ref.dtype)

    return pl.pallas_call(
        body, name=name,
        grid=(N // bn, nk),
        in_specs=[
            a_spec,
            pl.BlockSpec((bk, bn), lambda j, k: (k, j)),
        ],
        out_specs=pl.BlockSpec((M, bn), lambda j, k: (0, j)),
        out_shape=jax.ShapeDtypeStruct((M, N), out_dtype),
        scratch_shapes=[pltpu.VMEM((M, bn), F32)],
        compiler_params=_cp(2),
    )(a, b)


def _rms_linear(x, g, w, b, g_layer, b_layer, name):
    S = x.shape[0]
    N = w.shape[-1]
    tm = _row_tile(S, TM_STREAM)

    def body(x_ref, g_ref, w_ref, b_ref, o_ref):
        h, _, _ = _rms_fwd(x_ref[...], g_ref[...])
        o_ref[...] = (_dot(h.astype(MX), w_ref[...]) + b_ref[...]).astype(o_ref.dtype)

    return pl.pallas_call(
        body, name=name,
        grid=(S // tm,),
        in_specs=[
            pl.BlockSpec((tm, D), lambda i: (i, 0)),
            pl.BlockSpec((None, 1, D), lambda i: (g_layer, 0, 0)),
            pl.BlockSpec((None, D, N), lambda i: (0, 0, 0)),
            pl.BlockSpec((None, 1, N), lambda i: (b_layer, 0, 0)),
        ],
        out_specs=pl.BlockSpec((tm, N), lambda i: (i, 0)),
        out_shape=jax.ShapeDtypeStruct((S, N), MX),
        compiler_params=_cp(),
    )(x, g, w, b)


def _linear_res(o, w, b, xres, layer):
    S = o.shape[0]
    tm = _row_tile(S, TM_STREAM)

    def body(o_ref, w_ref, b_ref, x_ref, y_ref):
        y_ref[...] = x_ref[...] + _dot(o_ref[...], w_ref[...]) + b_ref[...]

    return pl.pallas_call(
        body, name="o_proj",
        grid=(S // tm,),
        in_specs=[
            pl.BlockSpec((tm, D), lambda i: (i, 0)),
            pl.BlockSpec((None, D, D), lambda i: (0, 0, 0)),
            pl.BlockSpec((None, 1, D), lambda i: (layer, 0, 0)),
            pl.BlockSpec((tm, D), lambda i: (i, 0)),
        ],
        out_specs=pl.BlockSpec((tm, D), lambda i: (i, 0)),
        out_shape=jax.ShapeDtypeStruct((S, D), F32),
        compiler_params=_cp(),
    )(o, w, b, xres)


def _linear_nt(dy, w):
    S = dy.shape[0]
    tm = _row_tile(S, TM_STREAM)

    def body(dy_ref, w_ref, o_ref, db_ref):
        @pl.when(pl.program_id(0) == 0)
        def _():
            db_ref[...] = jnp.zeros_like(db_ref)

        dy = dy_ref[...]
        db_ref[...] += jnp.sum(dy, axis=0, keepdims=True)
        o_ref[...] = _dot_nt(dy.astype(MX), w_ref[...]).astype(o_ref.dtype)

    return pl.pallas_call(
        body, name="o_proj_bwd",
        grid=(S // tm,),
        in_specs=[
            pl.BlockSpec((tm, D), lambda i: (i, 0)),
            pl.BlockSpec((None, D, D), lambda i: (0, 0, 0)),
        ],
        out_specs=[
            pl.BlockSpec((tm, D), lambda i: (i, 0)),
            pl.BlockSpec((1, D), lambda i: (0, 0)),
        ],
        out_shape=[
            jax.ShapeDtypeStruct((S, D), MX),
            jax.ShapeDtypeStruct((1, D), F32),
        ],
        compiler_params=_cp(),
    )(dy, w)


def _rms_linear_bwd(x, g, dzs, w, dy, g_layer, name):
    S = x.shape[0]
    N = w.shape[-1]
    tm = _row_tile(S, TM_STREAM)
    nz = len(dzs)

    def body(*refs):
        x_ref, g_ref = refs[0], refs[1]
        dz_refs = refs[2:2 + nz]
        w_ref, dy_ref, dx_ref, dg_ref, db_ref, h_ref, dzb_ref = refs[2 + nz:]

        @pl.when(pl.program_id(0) == 0)
        def _():
            dg_ref[...] = jnp.zeros_like(dg_ref)
            db_ref[...] = jnp.zeros_like(db_ref)

        gg = g_ref[...]
        h, xh, r = _rms_fwd(x_ref[...], gg)
        h_ref[...] = h.astype(h_ref.dtype)
        dz = dz_refs[0][...].astype(F32)
        for zr in dz_refs[1:]:
            dz = dz + zr[...].astype(F32)
        db_ref[...] += jnp.sum(dz, axis=0, keepdims=True)
        dzb = dz.astype(MX)
        dzb_ref[...] = dzb
        dh = _dot_nt(dzb, w_ref[...])
        dg_ref[...] += jnp.sum(dh * xh, axis=0, keepdims=True)
        dx_ref[...] = dy_ref[...] + _rms_bwd(dh, xh, r, gg)

    return pl.pallas_call(
        body, name=name,
        grid=(S // tm,),
        in_specs=[
            pl.BlockSpec((tm, D), lambda i: (i, 0)),
            pl.BlockSpec((None, 1, D), lambda i: (g_layer, 0, 0)),
        ] + [pl.BlockSpec((tm, N), lambda i: (i, 0))] * nz + [
            pl.BlockSpec((None, D, N), lambda i: (0, 0, 0)),
            pl.BlockSpec((tm, D), lambda i: (i, 0)),
        ],
        out_specs=[
            pl.BlockSpec((tm, D), lambda i: (i, 0)),
            pl.BlockSpec((1, D), lambda i: (0, 0)),
            pl.BlockSpec((1, N), lambda i: (0, 0)),
            pl.BlockSpec((tm, D), lambda i: (i, 0)),
            pl.BlockSpec((tm, N), lambda i: (i, 0)),
        ],
        out_shape=[
            jax.ShapeDtypeStruct((S, D), F32),
            jax.ShapeDtypeStruct((1, D), F32),
            jax.ShapeDtypeStruct((1, N), F32),
            jax.ShapeDtypeStruct((S, D), MX),
            jax.ShapeDtypeStruct((S, N), MX),
        ],
        compiler_params=_cp(),
    )(x, g, *dzs, w, dy)


def _loss_head(x, g, tgt):
    S = x.shape[0]
    tm = _row_tile(S, TM_STREAM)

    def body(x_ref, g_ref, t_ref, dx_ref, dg_ref, l_ref):
        @pl.when(pl.program_id(0) == 0)
        def _():
            dg_ref[...] = jnp.zeros_like(dg_ref)
            l_ref[...] = jnp.zeros_like(l_ref)

        gg = g_ref[...]
        y, xh, r = _rms_fwd(x_ref[...], gg)
        err = y - t_ref[...]
        tok = jnp.sum(err * err, axis=-1, keepdims=True) * (1.0 / D)
        l_ref[...] += 0.5 * jnp.sum(tok, axis=0, keepdims=True)
        dyv = err * (1.0 / D)
        dg_ref[...] += jnp.sum(dyv * xh, axis=0, keepdims=True)
        dx_ref[...] = _rms_bwd(dyv, xh, r, gg)

    return pl.pallas_call(
        body, name="loss_head",
        grid=(S // tm,),
        in_specs=[
            pl.BlockSpec((tm, D), lambda i: (i, 0)),
            pl.BlockSpec((1, D), lambda i: (0, 0)),
            pl.BlockSpec((tm, D), lambda i: (i, 0)),
        ],
        out_specs=[
            pl.BlockSpec((tm, D), lambda i: (i, 0)),
            pl.BlockSpec((1, D), lambda i: (0, 0)),
            pl.BlockSpec((1, LANES), lambda i: (0, 0)),
        ],
        out_shape=[
            jax.ShapeDtypeStruct((S, D), F32),
            jax.ShapeDtypeStruct((1, D), F32),
            jax.ShapeDtypeStruct((1, LANES), F32),
        ],
        compiler_params=_cp(),
    )(x, g, tgt)


HPG = NH // 2
QH = BLK // 2
KW = BLK + QH
COLS = HPG * QH


def _attn_setup(kvp_ref, kvc_ref):
    kw = jnp.concatenate([kvp_ref[...], kvc_ref[...]], axis=0).astype(F32)
    kk, vv = kw[:, :LANES], kw[:, LANES:]
    lo = lax.broadcasted_iota(jnp.int32, (1, LANES), 1) < HD
    kr, vr = pltpu.roll(kk, HD, 1), pltpu.roll(vv, HD, 1)
    ks = [jnp.where(lo, kk, kr).astype(MX), jnp.where(lo, kr, kk).astype(MX)]
    vs = [jnp.where(lo, vv, vr).astype(MX), jnp.where(lo, vr, vv).astype(MX)]
    return ks, vs, lo


def _stack_heads(ref, grp, lo, rows):
    parts = []
    for j in range(4 * grp, 4 * grp + 4):
        slab = ref[rows, j * LANES:(j + 1) * LANES]
        zero = jnp.zeros_like(slab)
        parts += [jnp.where(lo, slab, zero), jnp.where(lo, zero, slab)]
    return jnp.concatenate(parts, axis=0)


def _unstack_heads(st, lo):
    nq = st.shape[0] // HPG
    return [jnp.where(lo, st[2 * i * nq:(2 * i + 1) * nq], st[(2 * i + 1) * nq:(2 * i + 2) * nq])
            for i in range(4)]


def _attn_probs(qs, kg, n, sk_ref, layer, grp):
    rows = HPG * BLK
    qi = lax.broadcasted_iota(jnp.int32, (rows, 2 * BLK), 0) & (BLK - 1)
    si = lax.broadcasted_iota(jnp.int32, (rows, 2 * BLK), 1)
    ok = (si > qi) & (si <= qi + BLK) & jnp.logical_or(n > 0, si >= BLK)
    head = lax.broadcasted_iota(jnp.int32, (rows, 1), 0) // BLK
    sink = jnp.zeros((rows, 1), F32)
    for h in range(HPG):
        sink = jnp.where(head == h, sk_ref[layer, HPG * grp + h], sink)
    s = jnp.where(ok, _dot_nt(qs, kg) * SCALE, NEG)
    m = jnp.maximum(jnp.max(s, axis=-1, keepdims=True), sink)
    p = jnp.exp(s - m)
    return p * (1.0 / (jnp.sum(p, axis=-1, keepdims=True) + jnp.exp(sink - m)))


def _attn_mask_t(koff, n):
    si = lax.broadcasted_iota(jnp.int32, (KW, COLS), 0)
    qi = lax.broadcasted_iota(jnp.int32, (KW, COLS), 1) & (QH - 1)
    return (si > qi) & (si <= qi + BLK) & jnp.logical_or(n > 0, si >= BLK - koff)


def _attn_probs_t(qs, kg, ok, sk_ref, layer, grp):
    s = jnp.where(ok, _dot_nt(kg, qs) * SCALE, NEG)
    head = lax.broadcasted_iota(jnp.int32, (1, COLS), 1) // QH
    sink = jnp.zeros((1, COLS), F32)
    for h in range(HPG):
        sink = jnp.where(head == h, sk_ref[layer, HPG * grp + h], sink)
    m = jnp.maximum(jnp.max(s, axis=0, keepdims=True), sink)
    p = jnp.exp(s - m)
    es = jnp.exp(sink - m)
    inv = 1.0 / (jnp.sum(p, axis=0, keepdims=True) + es)
    return p * inv, es * inv, head


def _attn_specs(n_extra_q):
    q_spec = pl.BlockSpec((BLK, D), lambda n: (n, 0))
    return [q_spec] * n_extra_q + [
        pl.BlockSpec((BLK, 4 * HD), lambda n: (jnp.maximum(n - 1, 0), 0)),
        pl.BlockSpec((BLK, 4 * HD), lambda n: (n, 0)),
        pl.BlockSpec(memory_space=pltpu.SMEM),
    ]


def _attn_fwd(q, kv, sinks, layer):
    S = q.shape[0]

    def body(q_ref, kvp_ref, kvc_ref, sk_ref, o_ref):
        n = pl.program_id(0)
        ks, vs, lo = _attn_setup(kvp_ref, kvc_ref)
        for grp in range(2):
            qs = _stack_heads(q_ref, grp, lo, slice(None))
            pr = _attn_probs(qs, ks[grp], n, sk_ref, layer, grp)
            outs = _unstack_heads(_dot(pr.astype(MX), vs[grp]), lo)
            for i in range(4):
                j = 4 * grp + i
                o_ref[:, j * LANES:(j + 1) * LANES] = outs[i].astype(o_ref.dtype)

    return pl.pallas_call(
        body, name="attn_fwd",
        grid=(S // BLK,),
        in_specs=_attn_specs(1),
        out_specs=pl.BlockSpec((BLK, D), lambda n: (n, 0)),
        out_shape=jax.ShapeDtypeStruct((S, D), MX),
        compiler_params=_cp(),
    )(q, kv, kv, sinks)


def _attn_bwd(q, do, kv, sinks, layer):
    S = q.shape[0]

    def body(q_ref, do_ref, kvp_ref, kvc_ref, sk_ref, dq_ref, dkv_ref, dsk_ref):
        n = pl.program_id(0)

        @pl.when(n == 0)
        def _():
            dkv_ref[...] = jnp.zeros_like(dkv_ref)
            dsk_ref[...] = jnp.zeros_like(dsk_ref)

        ks, vs, lo = _attn_setup(kvp_ref, kvc_ref)
        lane = lax.broadcasted_iota(jnp.int32, (1, LANES), 1)
        dsk = jnp.zeros((1, LANES), F32)
        dk = [jnp.zeros((2 * BLK, LANES), F32) for _ in range(2)]
        dv = [jnp.zeros((2 * BLK, LANES), F32) for _ in range(2)]
        for half in range(2):
            rows = slice(half * QH, (half + 1) * QH)
            koff = half * QH
            ok = _attn_mask_t(koff, n)
            above = [jnp.zeros((koff, LANES), F32)] if koff else []
            below = [jnp.zeros((2 * BLK - KW - koff, LANES), F32)] if 2 * BLK - KW - koff else []
            for grp in range(2):
                qs = _stack_heads(q_ref, grp, lo, rows)
                dos = _stack_heads(do_ref, grp, lo, rows)
                kg, vg = ks[grp][koff:koff + KW], vs[grp][koff:koff + KW]
                pr, psink, head = _attn_probs_t(qs, kg, ok, sk_ref, layer, grp)
                dpr = _dot_nt(vg, dos)
                delta = jnp.sum(pr * dpr, axis=0, keepdims=True)
                ds = (pr * (dpr - delta) * SCALE).astype(MX)
                sd = psink * delta
                for h in range(HPG):
                    dsk = dsk + jnp.where(lane == HPG * grp + h,
                                          -jnp.sum(jnp.where(head == h, sd, 0.0), axis=1, keepdims=True), 0.0)
                dqs = _unstack_heads(_dot_tn(kg, ds).T, lo)
                for i in range(4):
                    j = 4 * grp + i
                    dq_ref[rows, j * LANES:(j + 1) * LANES] = dqs[i].astype(dq_ref.dtype)
                dk[grp] = dk[grp] + jnp.concatenate(above + [_dot(ds, qs)] + below, axis=0)
                dv[grp] = dv[grp] + jnp.concatenate(above + [_dot(pr.astype(MX), dos)] + below, axis=0)
        dsk_ref[...] += dsk
        tk = [a + pltpu.roll(a, HD, 1) for a in dk]
        tv = [a + pltpu.roll(a, HD, 1) for a in dv]
        contrib = jnp.concatenate([jnp.where(lo, tk[0], tk[1]), jnp.where(lo, tv[0], tv[1])], axis=1)

        @pl.when(n > 0)
        def _():
            rows = pl.ds(pl.multiple_of((n - 1) * BLK, BLK), 2 * BLK)
            dkv_ref[rows, :] += contrib

        @pl.when(n == 0)
        def _():
            dkv_ref[0:BLK, :] += contrib[BLK:]

    return pl.pallas_call(
        body, name="attn_bwd",
        grid=(S // BLK,),
        in_specs=_attn_specs(2),
        out_specs=[
            pl.BlockSpec((BLK, D), lambda n: (n, 0)),
            pl.BlockSpec((S, 4 * HD), lambda n: (0, 0)),
            pl.BlockSpec((1, LANES), lambda n: (0, 0)),
        ],
        out_shape=[
            jax.ShapeDtypeStruct((S, D), MX),
            jax.ShapeDtypeStruct((S, 4 * HD), F32),
            jax.ShapeDtypeStruct((1, LANES), F32),
        ],
        compiler_params=_cp(),
    )(q, do, kv, kv, sinks)


def _ew_rows(rows, cols, n_bufs=1):
    br = rows
    while br * cols * 4 * n_bufs > EW_BYTES and br % (2 * SUBLANES) == 0:
        br //= 2
    return br


def _adamw(parts, w, m, v):
    L, R, C = w.shape
    br = _ew_rows(R, C)
    npart = len(parts[0])

    def body(*refs):
        p_refs = refs[:L * npart]
        w_ref, m_ref, v_ref, g_ref, d_ref, nm_ref, nv_ref = refs[L * npart:]
        lyr = pl.program_id(0)
        for l in range(L):
            @pl.when(lyr == l)
            def _(l=l):
                g = p_refs[l * npart][...]
                for pr in p_refs[l * npart + 1:(l + 1) * npart]:
                    g = g + pr[...]
                nm = B1 * m_ref[...] + (1.0 - B1) * g
                nv = B2 * v_ref[...] + (1.0 - B2) * (g * g)
                m_hat = nm / (1.0 - B1 ** STEP)
                v_hat = nv / (1.0 - B2 ** STEP)
                g_ref[...] = g
                d_ref[...] = -LR * (m_hat / (jnp.sqrt(v_hat) + AEPS) + WD * w_ref[...])
                nm_ref[...] = nm
                nv_ref[...] = nv

    spec = pl.BlockSpec((None, br, C), lambda a, i: (a, i, 0))
    part_specs = [pl.BlockSpec((br, C), lambda a, i, l=l: (jnp.where(a == l, i, 0), 0))
                  for l in range(L) for _ in range(npart)]
    return pl.pallas_call(
        body, name="adamw",
        grid=(L, R // br),
        in_specs=part_specs + [spec] * 3,
        out_specs=[spec] * 4,
        out_shape=[jax.ShapeDtypeStruct((L, R, C), F32)] * 4,
        compiler_params=_cp(2),
    )(*[a for lp in parts for a in lp], w, m, v)


def _coords():
    return lax.axis_index("x"), lax.axis_index("y"), lax.axis_index("c")


def _other_chips(x, y):
    return [(1 - x, y), (x, 1 - y), (1 - x, 1 - y)]


def _slot(ref, axis, chip, size):
    idx = [slice(None)] * 3
    idx[axis] = pl.ds(pl.multiple_of(chip * size, size), size)
    return ref.at[tuple(idx)]


HBM_SPEC = pl.BlockSpec(memory_space=pltpu.HBM)
SEM_SPEC = pl.BlockSpec(memory_space=pltpu.SEMAPHORE)
ANY_SPEC = pl.BlockSpec(memory_space=pl.ANY)
EFFECT = pltpu.SideEffectType.DATAFLOW_SIDE_EFFECTING


def _slot_specs(shape, axis, br, lead):
    _, b, c = shape
    nrb = b // br
    first = (lambda a: a) if lead is None else (lambda a: lead)
    shard = pl.BlockSpec((None, br, c), lambda a, i, me: (first(a), i, 0))
    if axis == 1:
        slot = pl.BlockSpec((None, br, c), lambda a, i, me: (a, me[0] * nrb + i, 0))
    else:
        slot = pl.BlockSpec((None, br, c), lambda a, i, me: (a, i, me[0]))
    return shard, slot


def _shard_rows(b, c):
    br = b
    while br * c * 4 > 2 * EW_BYTES and br % (4 * SUBLANES) == 0:
        br //= 2
    return br


def _gather_place(shard, axis, me, dtype, lead=None):
    a_dim, b, c = shard.shape
    if lead is not None:
        a_dim = 1
    br = _shard_rows(b, c)
    shp = [a_dim, b, c]
    shp[axis] *= 4
    shard_spec, slot_spec = _slot_specs((a_dim, b, c), axis, br, lead)

    def body(me_ref, s_ref, o_ref):
        o_ref[...] = s_ref[...].astype(o_ref.dtype)

    return pl.pallas_call(
        body, name="gather_place",
        grid_spec=pltpu.PrefetchScalarGridSpec(
            num_scalar_prefetch=1, grid=(a_dim, b // br), in_specs=[shard_spec], out_specs=slot_spec),
        out_shape=jax.ShapeDtypeStruct(tuple(shp), dtype),
        compiler_params=_cp(2),
    )(me, shard)


def _sum_landed(grad, landed, axis, me):
    a_dim, b, c = landed.shape[1:]
    br = _shard_rows(b, c)
    shard_spec, slot_spec = _slot_specs((a_dim, b, c), axis, br, None)

    def body(me_ref, own_ref, r_ref, o_ref):
        o_ref[...] = ((own_ref[...].astype(F32) + r_ref[0].astype(F32)) + r_ref[1].astype(F32)) + r_ref[2].astype(F32)

    return pl.pallas_call(
        body, name="sum_landed",
        grid_spec=pltpu.PrefetchScalarGridSpec(
            num_scalar_prefetch=1, grid=(a_dim, b // br),
            in_specs=[slot_spec, pl.BlockSpec((3, None, br, c), lambda a, i, me: (0, a, i, 0))],
            out_specs=shard_spec),
        out_shape=jax.ShapeDtypeStruct((a_dim, b, c), F32),
        compiler_params=_cp(2),
    )(me, grad, landed)


def _copies(refs, plan, send, recv, to_sibling):
    x, y, c = _coords()
    me = 2 * x + y
    if to_sibling == "everyone":
        me = 4 * x + 2 * y + c
        flips = [(k >> 2, (k >> 1) & 1, k & 1) for k in range(1, 8)]
        targets = [((x ^ fx, y ^ fy, c ^ fc), 4 * (x ^ fx) + 2 * (y ^ fy) + (c ^ fc)) for fx, fy, fc in flips]
    elif to_sibling:
        targets = [((x, y, 1 - c), me)]
    else:
        targets = [((px, py, c), 2 * px + py) for px, py in _other_chips(x, y)]
    out, t = [], 0
    while plan(refs, me, t, 0, me) is not None:
        for k, (device, peer) in enumerate(targets):
            sv, dv = plan(refs, me, t, k, peer)
            n = len(targets) * t + k
            out.append(pltpu.make_async_remote_copy(
                src_ref=sv, dst_ref=dv, send_sem=send.at[n], recv_sem=recv.at[n],
                device_id=device, device_id_type=MESH))
        t += 1
    return out


def _push_start(name, bufs, n_copies, plan, to_sibling=False):
    nb = len(bufs)

    def body(*refs):
        send, recv, token = refs[nb], refs[nb + 1], refs[-1]
        for cp in _copies(refs[:nb], plan, send, recv, to_sibling):
            cp.start()
        token[...] = jnp.zeros_like(token)

    res = pl.pallas_call(
        body, name=name,
        in_specs=[HBM_SPEC] * nb,
        out_specs=[SEM_SPEC, SEM_SPEC] + [HBM_SPEC] * nb + [pl.BlockSpec(memory_space=pltpu.VMEM)],
        out_shape=[pltpu.SemaphoreType.DMA((n_copies,)), pltpu.SemaphoreType.DMA((n_copies,))]
        + [pltpu.HBM(a.shape, a.dtype) for a in bufs] + [jax.ShapeDtypeStruct((SUBLANES, LANES), F32)],
        input_output_aliases={i: 2 + i for i in range(nb)},
        compiler_params=pltpu.CompilerParams(has_side_effects=EFFECT),
    )(*[pltpu.with_memory_space_constraint(a, pltpu.HBM) for a in bufs])
    return res[0], res[1], res[2:2 + nb], res[-1]


def _push_wait(name, send, recv, bufs, plan, after, to_sibling=False):
    nb = len(bufs)

    def body(*refs):
        for cp in _copies(refs[:nb], plan, refs[nb], refs[nb + 1], to_sibling):
            cp.wait_send()
            cp.wait_recv()

    return pl.pallas_call(
        body, name=name,
        in_specs=[HBM_SPEC] * nb + [SEM_SPEC, SEM_SPEC, ANY_SPEC],
        out_specs=[HBM_SPEC] * nb,
        out_shape=[pltpu.HBM(a.shape, a.dtype) for a in bufs],
        input_output_aliases={i: i for i in range(nb)},
        compiler_params=pltpu.CompilerParams(has_side_effects=EFFECT),
    )(*bufs, send, recv, after)


def _gather_plan(axes):
    def plan(refs, me, t, k, peer):
        if t >= len(axes):
            return None
        size = refs[t].shape[axes[t]] // 4
        mine = _slot(refs[t], axes[t], me, size)
        return mine, mine
    return plan


def _half_slot(ref, axis, chip):
    c = lax.axis_index("c")
    if axis == 1:
        half = ref.shape[1] // 8
        return ref.at[:, pl.ds(pl.multiple_of(chip * 2 * half + c * half, 2 * SUBLANES), half), :]
    half = ref.shape[1] // 2
    size = ref.shape[2] // 4
    return ref.at[:, pl.ds(pl.multiple_of(c * half, 2 * SUBLANES), half), pl.ds(pl.multiple_of(chip * size, LANES), size)]


def _gather_half_plan(axes):
    def plan(refs, me, t, k, peer):
        if t >= len(axes):
            return None
        mine = _half_slot(refs[t], axes[t], me)
        return mine, mine
    return plan


def _gather_pass_plan(axes):
    def plan(refs, me, t, k, peer):
        if t >= 3 * len(axes):
            return None
        x, y, _ = _coords()
        px, py = _other_chips(x, y)[t % 3]
        landed = _half_slot(refs[t // 3], axes[t // 3], 2 * px + py)
        return landed, landed
    return plan


def _scatter_plan(axes):
    n = len(axes)

    def plan(refs, me, t, k, peer):
        if t >= n:
            return None
        size = refs[t].shape[axes[t]] // 4
        return _slot(refs[t], axes[t], peer, size), refs[n + t].at[k]
    return plan


def _swap_plan(n):
    def plan(refs, me, t, k, peer):
        if t >= n:
            return None
        return refs[t], refs[n + t]
    return plan


def _everyone_plan():
    def plan(refs, me, t, k, peer):
        if t >= 1:
            return None
        mine = refs[0].at[me]
        return mine, mine
    return plan


def _sum_slots(slots):
    R = slots.shape[1]

    def body(s_ref, o_ref):
        tot = s_ref[0]
        for k in range(1, 8):
            tot = tot + s_ref[k]
        o_ref[...] = tot

    vm = pl.BlockSpec(memory_space=pltpu.VMEM)
    return pl.pallas_call(
        body, name="sum_slots",
        in_specs=[vm],
        out_specs=vm,
        out_shape=jax.ShapeDtypeStruct((R, LANES), F32),
        compiler_params=pltpu.CompilerParams(vmem_limit_bytes=VMEM_LIMIT),
    )(slots)


def _pack(arrs):
    flat = []
    for a in arrs:
        f = a.reshape(-1).astype(F32)
        flat.append(jnp.pad(f, (0, (-f.shape[0]) % LANES)))
    v = jnp.concatenate(flat)
    v = jnp.pad(v, (0, (-v.shape[0]) % (SUBLANES * LANES)))
    return v.reshape(-1, LANES)


def _unpack(v, shapes):
    flat = v.reshape(-1)
    out, off = [], 0
    for shp in shapes:
        n = 1
        for d in shp:
            n *= d
        out.append(flat[off:off + n].reshape(shp))
        off += n + (-n) % LANES
    return out


def _local_step(x, tgt, sp, weights_for, on_grads):
    n1, n2 = sp["norm1_g"], sp["norm2_g"]
    w = dict(weights_for(0, x))
    saved = []
    xs = x
    kv = None
    for l in range(DEPTH):
        x_in = xs
        if l >= N_A:
            w.update(weights_for(1 + 2 * l - N_A, x_in))
        if l == N_A:
            kv = _rms_linear(x_in, sp["kv_norm_g"], w["w_kv"], sp["b_kv"], 0, 0, "kv_proj")
        if l < N_A:
            xa = _pool_fwd(x_in, n1, w["pool_w"], w["pool_scale"], l)
            q = o = None
        else:
            j = l - N_A
            q = _rms_linear(x_in, n1, w["w_q", j], sp["b_q"], l, j, "q_proj")
            o = _attn_fwd(q, kv, sp["sinks"], j)
            xa = _linear_res(o, w["w_o", j], sp["b_o"], x_in, j)
        w.update(weights_for(1 + l if l < N_A else 2 + 2 * l - N_A, xa))
        xs, u, uc = _ffn_fwd(xa, n2, w["ffn_up", l], w["ffn_down", l], w["ffn_conv_w"], sp["ffn_conv_b"], l)
        saved.append((x_in, xa, u, uc, q, o))

    dx, d_final_g, loss = _loss_head(xs, sp["final_g"], tgt)

    g = {k: [None] * DEPTH for k in ("norm1_g", "norm2_g", "ffn_conv_w", "ffn_conv_b")}
    for k in ("pool_scale", "b_q", "sinks", "b_o"):
        g[k] = [None] * N_A
    g["final_g"] = d_final_g
    dkvs = []
    pending = {}
    for l in reversed(range(DEPTH)):
        x_in, xa, u, uc, q, o = saved[l]
        dxa, du, a, hb, g["norm2_g"][l], g["ffn_conv_w"][l], g["ffn_conv_b"][l] = _ffn_bwd(
            xa, dx, u, uc, n2, w["ffn_up", l], w["ffn_down", l], w["ffn_conv_w"], l)
        pending["ffn_up", l] = _tn_matmul(hb, du, MX, "d_ffn_up", a_is_transposed=True)
        if l == 0:
            n1 = n1 + on_grads(DEPTH + 1, pending)
            pending = {}
        pending["ffn_down", l] = _tn_matmul(a, dx, MX, "d_ffn_down", a_is_transposed=True)
        zero = on_grads(DEPTH - 1 - l, pending)
        pending = {}
        n1, n2 = n1 + zero, n2 + zero
        if l < N_A:
            dx, d_pw, g["pool_scale"][l], g["norm1_g"][l] = _pool_bwd(
                x_in, dxa, n1, w["pool_w"], w["pool_scale"], l)
            pending["pool_w", l] = d_pw.astype(MX)
        else:
            j = l - N_A
            d_o, g["b_o"][j] = _linear_nt(dxa, w["w_o", j])
            pending["w_o", j] = _tn_matmul(o, dxa, MX, "d_w_o")
            dq, dkv, g["sinks"][j] = _attn_bwd(q, d_o, kv, sp["sinks"], j)
            dkvs.append(dkv)
            dx, g["norm1_g"][l], g["b_q"][j], hq, dqb = _rms_linear_bwd(
                x_in, n1, [dq], w["w_q", j], dxa, l, "q_proj_bwd")
            pending["w_q", j] = _tn_matmul(hq, dqb, MX, "d_w_q")
        if l == N_A:
            dx, g["kv_norm_g"], g["b_kv"], hk, dkvb = _rms_linear_bwd(
                x_in, sp["kv_norm_g"], dkvs, w["w_kv"], dx, 0, "kv_proj_bwd")
            pending["w_kv", 0] = _tn_matmul(hk, dkvb, MX, "d_w_kv")
    on_grads(DEPTH, pending)
    return loss, dx, g


SMALL = ("norm1_g", "norm2_g", "kv_norm_g", "b_kv", "b_q", "sinks", "b_o", "ffn_conv_b", "final_g")
SMALL_SHARDED = ("pool_scale", "ffn_conv_w")
BIG = ("pool_w", "w_kv", "w_q", "w_o", "ffn_up", "ffn_down")
EARLY_SWAP = 3
ORDER = ("norm1_g", "norm2_g", "pool_w", "pool_scale", "kv_norm_g", "w_kv", "b_kv", "w_q", "b_q", "sinks",
         "w_o", "b_o", "ffn_up", "ffn_conv_w", "ffn_conv_b", "ffn_down", "final_g")


def _as3d(a):
    return a.reshape((-1,) + a.shape[-2:])


def kernel(x, norm1_g, norm2_g, pool_w, pool_scale, kv_norm_g, w_kv, b_kv, w_q, b_q, sinks, w_o, b_o, ffn_up, ffn_conv_w, ffn_conv_b, ffn_down, final_g, loss_target, m_norm1_g, m_norm2_g, m_pool_w, m_pool_scale, m_kv_norm_g, m_w_kv, m_b_kv, m_w_q, m_b_q, m_sinks, m_w_o, m_b_o, m_ffn_up, m_ffn_conv_w, m_ffn_conv_b, m_ffn_down, m_final_g, v_norm1_g, v_norm2_g, v_pool_w, v_pool_scale, v_kv_norm_g, v_w_kv, v_b_kv, v_w_q, v_b_q, v_sinks, v_w_o, v_b_o, v_ffn_up, v_ffn_conv_w, v_ffn_conv_b, v_ffn_down, v_final_g):
    W = dict(norm1_g=norm1_g, norm2_g=norm2_g, pool_w=pool_w, pool_scale=pool_scale, kv_norm_g=kv_norm_g,
             w_kv=w_kv, b_kv=b_kv, w_q=w_q, b_q=b_q, sinks=sinks, w_o=w_o, b_o=b_o, ffn_up=ffn_up,
             ffn_conv_w=ffn_conv_w, ffn_conv_b=ffn_conv_b, ffn_down=ffn_down, final_g=final_g)
    M = dict(norm1_g=m_norm1_g, norm2_g=m_norm2_g, pool_w=m_pool_w, pool_scale=m_pool_scale,
             kv_norm_g=m_kv_norm_g, w_kv=m_w_kv, b_kv=m_b_kv, w_q=m_w_q, b_q=m_b_q, sinks=m_sinks, w_o=m_w_o,
             b_o=m_b_o, ffn_up=m_ffn_up, ffn_conv_w=m_ffn_conv_w, ffn_conv_b=m_ffn_conv_b, ffn_down=m_ffn_down,
             final_g=m_final_g)
    V = dict(norm1_g=v_norm1_g, norm2_g=v_norm2_g, pool_w=v_pool_w, pool_scale=v_pool_scale,
             kv_norm_g=v_kv_norm_g, w_kv=v_w_kv, b_kv=v_b_kv, w_q=v_w_q, b_q=v_b_q, sinks=v_sinks, w_o=v_w_o,
             b_o=v_b_o, ffn_up=v_ffn_up, ffn_conv_w=v_ffn_conv_w, ffn_conv_b=v_ffn_conv_b, ffn_down=v_ffn_down,
             final_g=v_final_g)
    S = x.shape[1]
    chip = 2 * lax.axis_index("x") + lax.axis_index("y")

    gather_axis = dict(pool_w=1, w_kv=1, w_q=1, w_o=1, ffn_up=2, ffn_down=1, pool_scale=2, ffn_conv_w=2)
    me = chip.reshape(1).astype(jnp.int32)
    axis_of = lambda key: gather_axis[key if isinstance(key, str) else key[0]]

    def placed(key, dtype):
        if isinstance(key, str):
            return _gather_place(_as3d(W[key]), axis_of(key), me, dtype)
        return _gather_place(W[key[0]], axis_of(key), me, dtype, lead=key[1])

    stages = [
        ["pool_w", "pool_scale", "ffn_conv_w"],
        [("ffn_up", 0), ("ffn_down", 0)],
        [("ffn_up", 1), ("ffn_down", 1)],
        ["w_kv", ("w_q", 0), ("w_o", 0)],
        [("ffn_up", 2), ("ffn_down", 2)],
        [("w_q", 1), ("w_o", 1)],
        [("ffn_up", 3), ("ffn_down", 3)],
    ]
    TWO_LEVEL = 1
    gathers, zero = [], 0.0
    for si, keys in enumerate(stages):
        axes = [axis_of(k) for k in keys]
        bufs = [placed(k, F32 if k in SMALL_SHARDED else MX) for k in keys]
        plan = _gather_half_plan(axes) if si == TWO_LEVEL else _gather_plan(axes)
        send, recv, bufs, token = _push_start(f"gather_start_{si}", bufs, 3 * len(keys), plan)
        gathers.append((keys, axes, send, recv, bufs))
        zero = zero + token[0, 0]

    def weights_for(stage, after):
        keys, axes, send, recv, bufs = gathers[stage]
        if stage == TWO_LEVEL:
            bufs = _push_wait(f"gather_wait_{stage}", send, recv, bufs, _gather_half_plan(axes), after)
            send, recv, bufs, _ = _push_start("gather_pass_start", bufs, 3 * len(keys), _gather_pass_plan(axes), True)
            bufs = _push_wait("gather_pass_wait", send, recv, bufs, _gather_pass_plan(axes), after, True)
        else:
            bufs = _push_wait(f"gather_wait_{stage}", send, recv, bufs, _gather_plan(axes), after)
        out = dict(zip(keys, bufs))
        if stage == 0:
            out["pool_w"] = out["pool_w"].reshape(N_A, 4, GC, GC)
            out["pool_scale"] = out["pool_scale"].reshape(N_A, 1, D)
        return out

    scatters = []

    def on_grads(stage, grads):
        keys = list(grads)
        axes = [axis_of(k) for k in keys]
        arrs = [_as3d(grads[k]) for k in keys]
        lands = []
        for a, ax in zip(arrs, axes):
            shp = list(a.shape)
            shp[ax] //= 4
            lands.append(lax.empty((3,) + tuple(shp), a.dtype))
        send, recv, bufs, token = _push_start(f"scatter_start_{stage}", arrs + lands, 3 * len(keys), _scatter_plan(axes))
        scatters.append((stage, keys, axes, send, recv, bufs))
        return token[0, 0]

    sp = dict(
        norm1_g=norm1_g.reshape(DEPTH, 1, D) + zero, norm2_g=norm2_g.reshape(DEPTH, 1, D),
        kv_norm_g=kv_norm_g.reshape(1, 1, D), b_kv=b_kv.reshape(1, 1, 4 * HD), b_q=b_q.reshape(N_B, 1, D),
        sinks=sinks, b_o=b_o.reshape(N_B, 1, D), ffn_conv_b=ffn_conv_b.reshape(DEPTH, 1, F2),
        final_g=final_g.reshape(1, D))

    x2d = x.reshape(S, D)
    loss, grad_x, g = _local_step(x2d, loss_target.reshape(S, D), sp, weights_for, on_grads)

    small_full = dict(
        norm1_g=jnp.stack(g["norm1_g"]), norm2_g=jnp.stack(g["norm2_g"]), kv_norm_g=g["kv_norm_g"],
        b_kv=g["b_kv"], b_q=jnp.stack(g["b_q"]), sinks=jnp.stack([s[0, :NH] for s in g["sinks"]]),
        b_o=jnp.stack(g["b_o"]), ffn_conv_b=jnp.stack(g["ffn_conv_b"]), final_g=g["final_g"],
        pool_scale=jnp.stack(g["pool_scale"]), ffn_conv_w=jnp.stack(g["ffn_conv_w"]))
    small_names = SMALL + SMALL_SHARDED
    small_shapes = [tuple(W[k].shape) for k in SMALL] + [(N_A, D), (DEPTH, 3, F2)]
    packed = _pack([small_full[k] for k in small_names] + [loss])
    slots = lax.dynamic_update_slice(lax.empty((8,) + packed.shape, F32), packed[None],
                                     (4 * lax.axis_index("x") + 2 * lax.axis_index("y") + lax.axis_index("c"), 0, 0))
    red_send, red_recv, slots, _ = _push_start("reduce_start", [slots], 7, _everyone_plan(), "everyone")

    pkeys, partial, swaps, after = [], [], [], grad_x

    def swap_start(tag, first):
        mine_now = partial[first:]
        lands = [lax.empty(p.shape, p.dtype) for p in mine_now]
        n = len(mine_now)
        send, recv, bufs, token = _push_start(f"swap_start_{tag}", mine_now + lands, n, _swap_plan(n), True)
        swaps.append((tag, n, send, recv, bufs))
        return token

    for i, (stage, keys, axes, send, recv, bufs) in enumerate(scatters):
        bufs = _push_wait(f"scatter_wait_{stage}", send, recv, bufs, _scatter_plan(axes), after)
        n = len(keys)
        for k, ax, grad, landed in zip(keys, axes, bufs[:n], bufs[n:]):
            pkeys.append(k)
            p_sum = _sum_landed(grad, landed, ax, me)
            partial.append(p_sum.reshape(-1, p_sum.shape[-1]))
        if i == EARLY_SWAP - 1:
            after = swap_start("early", 0)
    after = swap_start("late", sum(n for _, n, *_ in swaps))

    slots = _push_wait("reduce_wait", red_send, red_recv, slots, _everyone_plan(), after, "everyone")[0]
    red = _unpack(_sum_slots(slots), small_shapes + [(1, LANES)])
    red_g = dict(zip(small_names, red[:-1]))
    loss_out = red[-1][0, 0]
    red_g["pool_scale"] = lax.dynamic_slice_in_dim(red_g["pool_scale"], chip * (D // 4), D // 4, axis=1)
    red_g["ffn_conv_w"] = lax.dynamic_slice_in_dim(red_g["ffn_conv_w"], chip * (F2 // 4), F2 // 4, axis=2)
    small_w_shapes = [tuple(W[k].shape) for k in small_names]
    pk = lambda d: _pack([d[k] for k in small_names])[None]
    res = _adamw([[_pack([red_g[k] for k in small_names])]], pk(W), pk(M), pk(V))
    out_g, out_d, out_m, out_v = [dict(zip(small_names, _unpack(r, small_w_shapes))) for r in res]
    after = res[0]

    mine, theirs = [], []
    for tag, n, send, recv, bufs in swaps:
        bufs = _push_wait(f"swap_wait_{tag}", send, recv, bufs, _swap_plan(n), after, True)
        mine += bufs[:n]
        theirs += bufs[n:]
    mine = dict(zip(pkeys, mine))
    theirs = dict(zip(pkeys, theirs))
    for k in BIG:
        n_l = len([pk_ for pk_ in pkeys if pk_[0] == k])
        shp = W[k].shape
        rows, cols = mine[k, 0].shape
        three_d = lambda a: a.reshape(n_l, rows, cols)
        res = _adamw([[mine[k, l], theirs[k, l]] for l in range(n_l)], three_d(W[k]), three_d(M[k]), three_d(V[k]))
        out_g[k], out_d[k], out_m[k], out_v[k] = [r.reshape(shp) for r in res]

    return (loss_out, grad_x.reshape(x.shape), *[out_g[k] for k in ORDER], *[out_d[k] for k in ORDER],
            *[out_m[k] for k in ORDER], *[out_v[k] for k in ORDER])
```

```python
import functools

import jax
import jax.numpy as jnp
from jax import lax
from jax.experimental import pallas as pl
from jax.experimental.pallas import tpu as pltpu

D = 1024
DEPTH = 4
N_A = 2
N_B = 2
WINS = (2, 4, 8, 16)
GC = 256
HD = 64
NH = 16
BLK = 128
F = 2816
F2 = 2 * F
EPS = 1e-5
SCALE = HD ** -0.5
NEG = -1e30
HALO = 16
TN = 256
TM_STREAM = 1024
UP_GROUP = 3
LANES = 128
SUBLANES = 8
VMEM_LIMIT = 56 * 1024 * 1024
FFN_VMEM_LIMIT = 60 * 1024 * 1024
ACC_BYTES = 6 * 1024 * 1024
EW_BYTES = 1536 * 1024

LR, B1, B2, AEPS, WD, STEP = 0.001, 0.9, 0.999, 1e-08, 0.01, 10

MX = jnp.bfloat16
F32 = jnp.float32
MESH = pl.DeviceIdType.MESH


def _cp(n_axes=1, vmem=VMEM_LIMIT):
    return pltpu.CompilerParams(dimension_semantics=("arbitrary",) * n_axes, vmem_limit_bytes=vmem)


def _dot(a, b):
    return jnp.dot(a, b, preferred_element_type=F32)


def _dot_nt(a, b):
    return lax.dot_general(a, b, (((1,), (1,)), ((), ())), preferred_element_type=F32)


def _dot_tn(a, b):
    return lax.dot_general(a, b, (((0,), (0,)), ((), ())), preferred_element_type=F32)


def _rms_fwd(x, g):
    r = lax.rsqrt(jnp.mean(x * x, axis=-1, keepdims=True) + EPS)
    xh = x * r
    return xh * g, xh, r


def _rms_bwd(dh, xh, r, g):
    dxh = dh * g
    return r * (dxh - xh * jnp.mean(dxh * xh, axis=-1, keepdims=True))


def _row_tile(s, want):
    return min(s, want)


def _pool_pm(e, h, row, tm):
    out = []
    for gi, win in enumerate(WINS):
        cols = slice(gi * GC, (gi + 1) * GC)
        s = e[:, cols]
        sh = 1
        while sh < win:
            s = s + pltpu.roll(s, sh, 0)
            sh *= 2
        inv = 1.0 / jnp.minimum(row + 1, win).astype(F32)
        out.append(s[HALO:] * inv - h[:, cols])
    return out


def _pool_fwd(x, g, pw, ps, layer):
    S = x.shape[0]
    tm = _row_tile(S, 512)
    hb = tm // HALO

    def body(x_ref, xh_ref, g_ref, pw_ref, ps_ref, o_ref):
        i = pl.program_id(0)
        x = x_ref[...]
        gg = g_ref[...]
        h, _, _ = _rms_fwd(x, gg)
        hh, _, _ = _rms_fwd(xh_ref[...], gg)
        hh = jnp.where(i > 0, hh, 0.0)
        e = jnp.concatenate([hh, h], axis=0)
        row = i * tm + lax.broadcasted_iota(jnp.int32, (tm, 1), 0)
        pm = _pool_pm(e, h, row, tm)
        for gi in range(len(WINS)):
            cols = slice(gi * GC, (gi + 1) * GC)
            z = _dot(pm[gi].astype(MX), pw_ref[gi])
            o_ref[:, cols] = x[:, cols] + z * ps_ref[:, cols]

    return pl.pallas_call(
        body, name="pool_fwd",
        grid=(S // tm,),
        in_specs=[
            pl.BlockSpec((tm, D), lambda i: (i, 0)),
            pl.BlockSpec((HALO, D), lambda i: (jnp.maximum(i * hb - 1, 0), 0)),
            pl.BlockSpec((None, 1, D), lambda i: (layer, 0, 0)),
            pl.BlockSpec((None, 4, GC, GC), lambda i: (layer, 0, 0, 0)),
            pl.BlockSpec((None, 1, D), lambda i: (layer, 0, 0)),
        ],
        out_specs=pl.BlockSpec((tm, D), lambda i: (i, 0)),
        out_shape=jax.ShapeDtypeStruct((S, D), F32),
        compiler_params=_cp(),
    )(x, x, g, pw, ps)


def _pool_bwd(x, dy, g, pw, ps, layer):
    S = x.shape[0]
    tm = _row_tile(S, 256)
    hb = tm // HALO
    n_i = S // tm
    n_h = S // HALO

    def body(x_ref, xh_ref, dy_ref, dyn_ref, g_ref, pw_ref, ps_ref, dx_ref, dpw_ref, dps_ref, dg_ref):
        i = pl.program_id(0)

        @pl.when(i == 0)
        def _():
            dpw_ref[...] = jnp.zeros_like(dpw_ref)
            dps_ref[...] = jnp.zeros_like(dps_ref)
            dg_ref[...] = jnp.zeros_like(dg_ref)

        x = x_ref[...]
        gg = g_ref[...]
        ps = ps_ref[...]
        h, xh, r = _rms_fwd(x, gg)
        hh, _, _ = _rms_fwd(xh_ref[...], gg)
        hh = jnp.where(i > 0, hh, 0.0)
        e = jnp.concatenate([hh, h], axis=0)
        row = i * tm + lax.broadcasted_iota(jnp.int32, (tm, 1), 0)
        rown = (i + 1) * tm + lax.broadcasted_iota(jnp.int32, (HALO, 1), 0)
        pm = _pool_pm(e, h, row, tm)
        dy = dy_ref[...]
        dz = dy * ps
        dzn = jnp.where(i < n_i - 1, dyn_ref[...] * ps, 0.0)
        parts = []
        for gi, win in enumerate(WINS):
            cols = slice(gi * GC, (gi + 1) * GC)
            w = pw_ref[gi]
            pmb = pm[gi].astype(MX)
            z = _dot(pmb, w)
            dps_ref[:, cols] += jnp.sum(dy[:, cols] * z, axis=0, keepdims=True)
            dzb = dz[:, cols].astype(MX)
            dpw_ref[gi] += _dot_tn(pmb, dzb)
            dpm = _dot_nt(dzb, w)
            dpmn = _dot_nt(dzn[:, cols].astype(MX), w)
            q = dpm * (1.0 / jnp.minimum(row + 1, win).astype(F32))
            qn = dpmn * (1.0 / jnp.minimum(rown + 1, win).astype(F32))
            s = jnp.concatenate([q, qn], axis=0)
            sh = 1
            while sh < win:
                s = s + pltpu.roll(s, tm + HALO - sh, 0)
                sh *= 2
            parts.append(s[:tm] - dpm)
        dh = jnp.concatenate(parts, axis=1)
        dg_ref[...] += jnp.sum(dh * xh, axis=0, keepdims=True)
        dx_ref[...] = dy + _rms_bwd(dh, xh, r, gg)

    return pl.pallas_call(
        body, name="pool_bwd",
        grid=(n_i,),
        in_specs=[
            pl.BlockSpec((tm, D), lambda i: (i, 0)),
            pl.BlockSpec((HALO, D), lambda i: (jnp.maximum(i * hb - 1, 0), 0)),
            pl.BlockSpec((tm, D), lambda i: (i, 0)),
            pl.BlockSpec((HALO, D), lambda i: (jnp.minimum((i + 1) * hb, n_h - 1), 0)),
            pl.BlockSpec((None, 1, D), lambda i: (layer, 0, 0)),
            pl.BlockSpec((None, 4, GC, GC), lambda i: (layer, 0, 0, 0)),
            pl.BlockSpec((None, 1, D), lambda i: (layer, 0, 0)),
        ],
        out_specs=[
            pl.BlockSpec((tm, D), lambda i: (i, 0)),
            pl.BlockSpec((4, GC, GC), lambda i: (0, 0, 0)),
            pl.BlockSpec((1, D), lambda i: (0, 0)),
            pl.BlockSpec((1, D), lambda i: (0, 0)),
        ],
        out_shape=[
            jax.ShapeDtypeStruct((S, D), F32),
            jax.ShapeDtypeStruct((4, GC, GC), F32),
            jax.ShapeDtypeStruct((1, D), F32),
            jax.ShapeDtypeStruct((1, D), F32),
        ],
        compiler_params=_cp(),
    )(x, x, dy, dy, g, pw, ps)


N_STAGE = 4


def _rows_before(slot, u, prev8):
    tm = u.shape[0]
    m1, m2 = [], []
    for c in range(TN // LANES):
        lanes = slice(c * LANES, (c + 1) * LANES)
        slot[c, 0:SUBLANES, :] = prev8[:, lanes]
        slot[c, SUBLANES:SUBLANES + tm, :] = u[:, lanes]
        m1.append(slot[c, pl.ds(SUBLANES - 1, tm), :])
        m2.append(slot[c, pl.ds(SUBLANES - 2, tm), :])
    return jnp.concatenate(m1, axis=1), jnp.concatenate(m2, axis=1)


def _rows_after(slot, d, next8):
    tm = d.shape[0]
    p1, p2 = [], []
    for c in range(TN // LANES):
        lanes = slice(c * LANES, (c + 1) * LANES)
        slot[c, 0:tm, :] = d[:, lanes]
        slot[c, tm:tm + SUBLANES, :] = next8[:, lanes]
        p1.append(slot[c, pl.ds(1, tm), :])
        p2.append(slot[c, pl.ds(2, tm), :])
    return jnp.concatenate(p1, axis=1), jnp.concatenate(p2, axis=1)


def _conv(slot, u, prev8, cw):
    um1, um2 = _rows_before(slot, u, prev8)
    return cw[0:1] * um2 + cw[1:2] * um1 + cw[2:3] * u


def _ffn_fwd(x, g, wup, wdn, cw, cb, layer):
    S = x.shape[0]
    tm = _row_tile(S, 512)

    def body(x_ref, g_ref, wup_hbm, wdn_hbm, cw_ref, cb_ref, o_ref, u_ref, uc_ref, wup_v, wdn_v, carry, act, stage):
        i = pl.program_id(0)

        @pl.when(i == 0)
        def _():
            pltpu.sync_copy(wup_hbm.at[0], wup_v)
            pltpu.sync_copy(wdn_hbm.at[0], wdn_v)
            carry[...] = jnp.zeros_like(carry)

        x = x_ref[...]
        h, _, _ = _rms_fwd(x, g_ref[...])
        hb = h.astype(MX)
        for j in range(F // TN):
            cg = slice(j * TN, (j + 1) * TN)
            cv = slice(F + j * TN, F + (j + 1) * TN)
            ug = _dot(hb, wup_v[:, cg])
            uv = _dot(hb, wup_v[:, cv])
            u_ref[:, cg] = ug.astype(u_ref.dtype)
            u_ref[:, cv] = uv.astype(u_ref.dtype)
            gt = _conv(stage.at[2 * (j % 2)], ug, carry[:, cg], cw_ref[:, cg])
            vl = _conv(stage.at[2 * (j % 2) + 1], uv, carry[:, cv], cw_ref[:, cv])
            carry[:, cg] = ug[tm - SUBLANES:]
            carry[:, cv] = uv[tm - SUBLANES:]
            gt = gt + cb_ref[:, cg]
            vl = vl + cb_ref[:, cv]
            uc_ref[:, cg] = gt.astype(uc_ref.dtype)
            uc_ref[:, cv] = vl.astype(uc_ref.dtype)
            act[:, cg] = (gt * jax.nn.sigmoid(gt) * vl).astype(act.dtype)
        o_ref[...] = x + _dot(act[...], wdn_v[...])

    return pl.pallas_call(
        body, name="ffn_fwd",
        grid=(S // tm,),
        in_specs=[
            pl.BlockSpec((tm, D), lambda i: (i, 0)),
            pl.BlockSpec((None, 1, D), lambda i: (layer, 0, 0)),
            pl.BlockSpec(memory_space=pl.ANY),
            pl.BlockSpec(memory_space=pl.ANY),
            pl.BlockSpec((None, 3, F2), lambda i: (layer, 0, 0)),
            pl.BlockSpec((None, 1, F2), lambda i: (layer, 0, 0)),
        ],
        out_specs=[
            pl.BlockSpec((tm, D), lambda i: (i, 0)),
            pl.BlockSpec((tm, F2), lambda i: (i, 0)),
            pl.BlockSpec((tm, F2), lambda i: (i, 0)),
        ],
        out_shape=[
            jax.ShapeDtypeStruct((S, D), F32),
            jax.ShapeDtypeStruct((S, F2), MX),
            jax.ShapeDtypeStruct((S, F2), MX),
        ],
        scratch_shapes=[
            pltpu.VMEM((D, F2), MX),
            pltpu.VMEM((F, D), MX),
            pltpu.VMEM((SUBLANES, F2), F32),
            pltpu.VMEM((tm, F), MX),
            pltpu.VMEM((N_STAGE, TN // LANES, tm + SUBLANES, LANES), F32),
        ],
        compiler_params=_cp(vmem=FFN_VMEM_LIMIT),
    )(x, g, wup, wdn, cw, cb)


def _ffn_bwd(x, dy, u, uc, g, wup, wdn, cw, layer):
    S = x.shape[0]
    tm = _row_tile(S, 256)
    n_i = S // tm

    def body(x_ref, dy_ref, u_ref, uc_ref, g_ref, wup_hbm, wdn_hbm, cw_ref,
             dx_ref, du_ref, a_ref, h_ref, dg_ref, dcw_ref, dcb_ref, wup_v, wdn_v, carry, stage):
        i = pl.program_id(0)

        @pl.when(i == 0)
        def _():
            pltpu.sync_copy(wup_hbm.at[0], wup_v)
            pltpu.sync_copy(wdn_hbm.at[0], wdn_v)
            carry[...] = jnp.zeros_like(carry)
            dg_ref[...] = jnp.zeros_like(dg_ref)
            dcw_ref[...] = jnp.zeros_like(dcw_ref)
            dcb_ref[...] = jnp.zeros_like(dcb_ref)

        x = x_ref[...]
        gg = g_ref[...]
        h, xh, r = _rms_fwd(x, gg)
        h_ref[...] = h.T.astype(h_ref.dtype)
        dy = dy_ref[...]
        dyb = dy.astype(MX)
        dh, dus = None, ([], [])
        for j in range(F // TN):
            cg = slice(j * TN, (j + 1) * TN)
            cv = slice(F + j * TN, F + (j + 1) * TN)
            gt = uc_ref[:, cg].astype(F32)
            vl = uc_ref[:, cv].astype(F32)
            sg = jax.nn.sigmoid(gt)
            sil = gt * sg
            a_ref[cg, :] = (sil * vl).T.astype(a_ref.dtype)
            da = _dot_nt(dyb, wdn_v[cg, :])
            dvl = da * sil
            dgt = (da * vl) * (sg + sil * (1.0 - sg))
            for cc, dd in ((cg, dgt), (cv, dvl)):
                dp1, dp2 = _rows_after(stage.at[2 * (j % 2) + (cc is cv)], dd, carry[:, cc])
                carry[:, cc] = dd[0:SUBLANES]
                uu = u_ref[:, cc].astype(F32)
                dcb_ref[:, cc] += jnp.sum(dd, axis=0, keepdims=True)
                dcw_ref[0:1, cc] += jnp.sum(dp2 * uu, axis=0, keepdims=True)
                dcw_ref[1:2, cc] += jnp.sum(dp1 * uu, axis=0, keepdims=True)
                dcw_ref[2:3, cc] += jnp.sum(dd * uu, axis=0, keepdims=True)
                cwc = cw_ref[:, cc]
                duu = (cwc[2:3] * dd + cwc[1:2] * dp1 + cwc[0:1] * dp2).astype(MX)
                du_ref[:, cc] = duu
                dus[cc is cv].append(duu)
            if len(dus[0]) == UP_GROUP or j == F // TN - 1:
                first = j + 1 - len(dus[0])
                for side, base in ((0, 0), (1, F)):
                    cols = slice(base + first * TN, base + (j + 1) * TN)
                    part = _dot_nt(jnp.concatenate(dus[side], axis=1), wup_v[:, cols])
                    dh = part if dh is None else dh + part
                dus = ([], [])
        dg_ref[...] += jnp.sum(dh * xh, axis=0, keepdims=True)
        dx_ref[...] = dy + _rms_bwd(dh, xh, r, gg)

    rev = lambda i: (n_i - 1 - i, 0)
    return pl.pallas_call(
        body, name="ffn_bwd",
        grid=(n_i,),
        in_specs=[
            pl.BlockSpec((tm, D), rev),
            pl.BlockSpec((tm, D), rev),
            pl.BlockSpec((tm, F2), rev),
            pl.BlockSpec((tm, F2), rev),
            pl.BlockSpec((None, 1, D), lambda i: (layer, 0, 0)),
            pl.BlockSpec(memory_space=pl.ANY),
            pl.BlockSpec(memory_space=pl.ANY),
            pl.BlockSpec((None, 3, F2), lambda i: (layer, 0, 0)),
        ],
        out_specs=[
            pl.BlockSpec((tm, D), rev),
            pl.BlockSpec((tm, F2), rev),
            pl.BlockSpec((F, tm), lambda i: (0, n_i - 1 - i)),
            pl.BlockSpec((D, tm), lambda i: (0, n_i - 1 - i)),
            pl.BlockSpec((1, D), lambda i: (0, 0)),
            pl.BlockSpec((3, F2), lambda i: (0, 0)),
            pl.BlockSpec((1, F2), lambda i: (0, 0)),
        ],
        out_shape=[
            jax.ShapeDtypeStruct((S, D), F32),
            jax.ShapeDtypeStruct((S, F2), MX),
            jax.ShapeDtypeStruct((F, S), MX),
            jax.ShapeDtypeStruct((D, S), MX),
            jax.ShapeDtypeStruct((1, D), F32),
            jax.ShapeDtypeStruct((3, F2), F32),
            jax.ShapeDtypeStruct((1, F2), F32),
        ],
        scratch_shapes=[
            pltpu.VMEM((D, F2), MX),
            pltpu.VMEM((F, D), MX),
            pltpu.VMEM((SUBLANES, F2), F32),
            pltpu.VMEM((N_STAGE, TN // LANES, tm + SUBLANES, LANES), F32),
        ],
        compiler_params=_cp(vmem=FFN_VMEM_LIMIT),
    )(x, dy, u, uc, g, wup, wdn, cw)


def _tn_matmul(a, b, out_dtype, name, a_is_transposed=False):
    S, N = b.shape
    M = a.shape[0] if a_is_transposed else a.shape[1]
    bn = N
    while M * bn * 4 > ACC_BYTES and bn % (2 * LANES) == 0:
        bn //= 2
    bk = _row_tile(S, 2048)
    nk = S // bk
    a_spec = pl.BlockSpec((M, bk), lambda j, k: (0, k)) if a_is_transposed else pl.BlockSpec((bk, M), lambda j, k: (k, 0))

    def body(a_ref, b_ref, o_ref, acc):
        k = pl.program_id(1)
        if a_is_transposed:
            p = _dot(a_ref[...].astype(MX), b_ref[...].astype(MX))
        else:
            p = _dot_tn(a_ref[...].astype(MX), b_ref[...].astype(MX))

        @pl.when(k == 0)
        def _():
            acc[...] = p

        @pl.when(k > 0)
        def _():
            acc[...] += p

        @pl.when(k == nk - 1)
        def _():
            o_ref[...] = acc[...].astype(o_ref.dtype)

    return pl.pallas_call(
        body, name=name,
        grid=(N // bn, nk),
        in_specs=[
            a_spec,
            pl.BlockSpec((bk, bn), lambda j, k: (k, j)),
        ],
        out_specs=pl.BlockSpec((M, bn), lambda j, k: (0, j)),
        out_shape=jax.ShapeDtypeStruct((M, N), out_dtype),
        scratch_shapes=[pltpu.VMEM((M, bn), F32)],
        compiler_params=_cp(2),
    )(a, b)


def _rms_linear(x, g, w, b, g_layer, b_layer, name):
    S = x.shape[0]
    N = w.shape[-1]
    tm = _row_tile(S, TM_STREAM)

    def body(x_ref, g_ref, w_ref, b_ref, o_ref):
        h, _, _ = _rms_fwd(x_ref[...], g_ref[...])
        o_ref[...] = (_dot(h.astype(MX), w_ref[...]) + b_ref[...]).astype(o_ref.dtype)

    return pl.pallas_call(
        body, name=name,
        grid=(S // tm,),
        in_specs=[
            pl.BlockSpec((tm, D), lambda i: (i, 0)),
            pl.BlockSpec((None, 1, D), lambda i: (g_layer, 0, 0)),
            pl.BlockSpec((None, D, N), lambda i: (0, 0, 0)),
            pl.BlockSpec((None, 1, N), lambda i: (b_layer, 0, 0)),
        ],
        out_specs=pl.BlockSpec((tm, N), lambda i: (i, 0)),
        out_shape=jax.ShapeDtypeStruct((S, N), MX),
        compiler_params=_cp(),
    )(x, g, w, b)


def _linear_res(o, w, b, xres, layer):
    S = o.shape[0]
    tm = _row_tile(S, TM_STREAM)

    def body(o_ref, w_ref, b_ref, x_ref, y_ref):
        y_ref[...] = x_ref[...] + _dot(o_ref[...], w_ref[...]) + b_ref[...]

    return pl.pallas_call(
        body, name="o_proj",
        grid=(S // tm,),
        in_specs=[
            pl.BlockSpec((tm, D), lambda i: (i, 0)),
            pl.BlockSpec((None, D, D), lambda i: (0, 0, 0)),
            pl.BlockSpec((None, 1, D), lambda i: (layer, 0, 0)),
            pl.BlockSpec((tm, D), lambda i: (i, 0)),
        ],
        out_specs=pl.BlockSpec((tm, D), lambda i: (i, 0)),
        out_shape=jax.ShapeDtypeStruct((S, D), F32),
        compiler_params=_cp(),
    )(o, w, b, xres)


def _linear_nt(dy, w):
    S = dy.shape[0]
    tm = _row_tile(S, TM_STREAM)

    def body(dy_ref, w_ref, o_ref, db_ref):
        @pl.when(pl.program_id(0) == 0)
        def _():
            db_ref[...] = jnp.zeros_like(db_ref)

        dy = dy_ref[...]
        db_ref[...] += jnp.sum(dy, axis=0, keepdims=True)
        o_ref[...] = _dot_nt(dy.astype(MX), w_ref[...]).astype(o_ref.dtype)

    return pl.pallas_call(
        body, name="o_proj_bwd",
        grid=(S // tm,),
        in_specs=[
            pl.BlockSpec((tm, D), lambda i: (i, 0)),
            pl.BlockSpec((None, D, D), lambda i: (0, 0, 0)),
        ],
        out_specs=[
            pl.BlockSpec((tm, D), lambda i: (i, 0)),
            pl.BlockSpec((1, D), lambda i: (0, 0)),
        ],
        out_shape=[
            jax.ShapeDtypeStruct((S, D), MX),
            jax.ShapeDtypeStruct((1, D), F32),
        ],
        compiler_params=_cp(),
    )(dy, w)


def _rms_linear_bwd(x, g, dzs, w, dy, g_layer, name):
    S = x.shape[0]
    N = w.shape[-1]
    tm = _row_tile(S, TM_STREAM)
    nz = len(dzs)

    def body(*refs):
        x_ref, g_ref = refs[0], refs[1]
        dz_refs = refs[2:2 + nz]
        w_ref, dy_ref, dx_ref, dg_ref, db_ref, h_ref, dzb_ref = refs[2 + nz:]

        @pl.when(pl.program_id(0) == 0)
        def _():
            dg_ref[...] = jnp.zeros_like(dg_ref)
            db_ref[...] = jnp.zeros_like(db_ref)

        gg = g_ref[...]
        h, xh, r = _rms_fwd(x_ref[...], gg)
        h_ref[...] = h.astype(h_ref.dtype)
        dz = dz_refs[0][...].astype(F32)
        for zr in dz_refs[1:]:
            dz = dz + zr[...].astype(F32)
        db_ref[...] += jnp.sum(dz, axis=0, keepdims=True)
        dzb = dz.astype(MX)
        dzb_ref[...] = dzb
        dh = _dot_nt(dzb, w_ref[...])
        dg_ref[...] += jnp.sum(dh * xh, axis=0, keepdims=True)
        dx_ref[...] = dy_ref[...] + _rms_bwd(dh, xh, r, gg)

    return pl.pallas_call(
        body, name=name,
        grid=(S // tm,),
        in_specs=[
            pl.BlockSpec((tm, D), lambda i: (i, 0)),
            pl.BlockSpec((None, 1, D), lambda i: (g_layer, 0, 0)),
        ] + [pl.BlockSpec((tm, N), lambda i: (i, 0))] * nz + [
            pl.BlockSpec((None, D, N), lambda i: (0, 0, 0)),
            pl.BlockSpec((tm, D), lambda i: (i, 0)),
        ],
        out_specs=[
            pl.BlockSpec((tm, D), lambda i: (i, 0)),
            pl.BlockSpec((1, D), lambda i: (0, 0)),
            pl.BlockSpec((1, N), lambda i: (0, 0)),
            pl.BlockSpec((tm, D), lambda i: (i, 0)),
            pl.BlockSpec((tm, N), lambda i: (i, 0)),
        ],
        out_shape=[
            jax.ShapeDtypeStruct((S, D), F32),
            jax.ShapeDtypeStruct((1, D), F32),
            jax.ShapeDtypeStruct((1, N), F32),
            jax.ShapeDtypeStruct((S, D), MX),
            jax.ShapeDtypeStruct((S, N), MX),
        ],
        compiler_params=_cp(),
    )(x, g, *dzs, w, dy)


def _loss_head(x, g, tgt):
    S = x.shape[0]
    tm = _row_tile(S, TM_STREAM)

    def body(x_ref, g_ref, t_ref, dx_ref, dg_ref, l_ref):
        @pl.when(pl.program_id(0) == 0)
        def _():
            dg_ref[...] = jnp.zeros_like(dg_ref)
            l_ref[...] = jnp.zeros_like(l_ref)

        gg = g_ref[...]
        y, xh, r = _rms_fwd(x_ref[...], gg)
        err = y - t_ref[...]
        tok = jnp.sum(err * err, axis=-1, keepdims=True) * (1.0 / D)
        l_ref[...] += 0.5 * jnp.sum(tok, axis=0, keepdims=True)
        dyv = err * (1.0 / D)
        dg_ref[...] += jnp.sum(dyv * xh, axis=0, keepdims=True)
        dx_ref[...] = _rms_bwd(dyv, xh, r, gg)

    return pl.pallas_call(
        body, name="loss_head",
        grid=(S // tm,),
        in_specs=[
            pl.BlockSpec((tm, D), lambda i: (i, 0)),
            pl.BlockSpec((1, D), lambda i: (0, 0)),
            pl.BlockSpec((tm, D), lambda i: (i, 0)),
        ],
        out_specs=[
            pl.BlockSpec((tm, D), lambda i: (i, 0)),
            pl.BlockSpec((1, D), lambda i: (0, 0)),
            pl.BlockSpec((1, LANES), lambda i: (0, 0)),
        ],
        out_shape=[
            jax.ShapeDtypeStruct((S, D), F32),
            jax.ShapeDtypeStruct((1, D), F32),
            jax.ShapeDtypeStruct((1, LANES), F32),
        ],
        compiler_params=_cp(),
    )(x, g, tgt)


HPG = NH // 2
QH = BLK // 2
KW = BLK + QH
COLS = HPG * QH


def _attn_setup(kvp_ref, kvc_ref):
    kw = jnp.concatenate([kvp_ref[...], kvc_ref[...]], axis=0).astype(F32)
    kk, vv = kw[:, :LANES], kw[:, LANES:]
    lo = lax.broadcasted_iota(jnp.int32, (1, LANES), 1) < HD
    kr, vr = pltpu.roll(kk, HD, 1), pltpu.roll(vv, HD, 1)
    ks = [jnp.where(lo, kk, kr).astype(MX), jnp.where(lo, kr, kk).astype(MX)]
    vs = [jnp.where(lo, vv, vr).astype(MX), jnp.where(lo, vr, vv).astype(MX)]
    return ks, vs, lo


def _stack_heads(ref, grp, lo, rows):
    parts = []
    for j in range(4 * grp, 4 * grp + 4):
        slab = ref[rows, j * LANES:(j + 1) * LANES]
        zero = jnp.zeros_like(slab)
        parts += [jnp.where(lo, slab, zero), jnp.where(lo, zero, slab)]
    return jnp.concatenate(parts, axis=0)


def _unstack_heads(st, lo):
    nq = st.shape[0] // HPG
    return [jnp.where(lo, st[2 * i * nq:(2 * i + 1) * nq], st[(2 * i + 1) * nq:(2 * i + 2) * nq])
            for i in range(4)]


def _attn_probs(qs, kg, n, sk_ref, layer, grp):
    rows = HPG * BLK
    qi = lax.broadcasted_iota(jnp.int32, (rows, 2 * BLK), 0) & (BLK - 1)
    si = lax.broadcasted_iota(jnp.int32, (rows, 2 * BLK), 1)
    ok = (si > qi) & (si <= qi + BLK) & jnp.logical_or(n > 0, si >= BLK)
    head = lax.broadcasted_iota(jnp.int32, (rows, 1), 0) // BLK
    sink = jnp.zeros((rows, 1), F32)
    for h in range(HPG):
        sink = jnp.where(head == h, sk_ref[layer, HPG * grp + h], sink)
    s = jnp.where(ok, _dot_nt(qs, kg) * SCALE, NEG)
    m = jnp.maximum(jnp.max(s, axis=-1, keepdims=True), sink)
    p = jnp.exp(s - m)
    return p * (1.0 / (jnp.sum(p, axis=-1, keepdims=True) + jnp.exp(sink - m)))


def _attn_mask_t(koff, n):
    si = lax.broadcasted_iota(jnp.int32, (KW, COLS), 0)
    qi = lax.broadcasted_iota(jnp.int32, (KW, COLS), 1) & (QH - 1)
    return (si > qi) & (si <= qi + BLK) & jnp.logical_or(n > 0, si >= BLK - koff)


def _attn_probs_t(qs, kg, ok, sk_ref, layer, grp):
    s = jnp.where(ok, _dot_nt(kg, qs) * SCALE, NEG)
    head = lax.broadcasted_iota(jnp.int32, (1, COLS), 1) // QH
    sink = jnp.zeros((1, COLS), F32)
    for h in range(HPG):
        sink = jnp.where(head == h, sk_ref[layer, HPG * grp + h], sink)
    m = jnp.maximum(jnp.max(s, axis=0, keepdims=True), sink)
    p = jnp.exp(s - m)
    es = jnp.exp(sink - m)
    inv = 1.0 / (jnp.sum(p, axis=0, keepdims=True) + es)
    return p * inv, es * inv, head


def _attn_specs(n_extra_q):
    q_spec = pl.BlockSpec((BLK, D), lambda n: (n, 0))
    return [q_spec] * n_extra_q + [
        pl.BlockSpec((BLK, 4 * HD), lambda n: (jnp.maximum(n - 1, 0), 0)),
        pl.BlockSpec((BLK, 4 * HD), lambda n: (n, 0)),
        pl.BlockSpec(memory_space=pltpu.SMEM),
    ]


def _attn_fwd(q, kv, sinks, layer):
    S = q.shape[0]

    def body(q_ref, kvp_ref, kvc_ref, sk_ref, o_ref):
        n = pl.program_id(0)
        ks, vs, lo = _attn_setup(kvp_ref, kvc_ref)
        for grp in range(2):
            qs = _stack_heads(q_ref, grp, lo, slice(None))
            pr = _attn_probs(qs, ks[grp], n, sk_ref, layer, grp)
            outs = _unstack_heads(_dot(pr.astype(MX), vs[grp]), lo)
            for i in range(4):
                j = 4 * grp + i
                o_ref[:, j * LANES:(j + 1) * LANES] = outs[i].astype(o_ref.dtype)

    return pl.pallas_call(
        body, name="attn_fwd",
        grid=(S // BLK,),
        in_specs=_attn_specs(1),
        out_specs=pl.BlockSpec((BLK, D), lambda n: (n, 0)),
        out_shape=jax.ShapeDtypeStruct((S, D), MX),
        compiler_params=_cp(),
    )(q, kv, kv, sinks)


def _attn_bwd(q, do, kv, sinks, layer):
    S = q.shape[0]

    def body(q_ref, do_ref, kvp_ref, kvc_ref, sk_ref, dq_ref, dkv_ref, dsk_ref):
        n = pl.program_id(0)

        @pl.when(n == 0)
        def _():
            dkv_ref[...] = jnp.zeros_like(dkv_ref)
            dsk_ref[...] = jnp.zeros_like(dsk_ref)

        ks, vs, lo = _attn_setup(kvp_ref, kvc_ref)
        lane = lax.broadcasted_iota(jnp.int32, (1, LANES), 1)
        dsk = jnp.zeros((1, LANES), F32)
        dk = [jnp.zeros((2 * BLK, LANES), F32) for _ in range(2)]
        dv = [jnp.zeros((2 * BLK, LANES), F32) for _ in range(2)]
        for half in range(2):
            rows = slice(half * QH, (half + 1) * QH)
            koff = half * QH
            ok = _attn_mask_t(koff, n)
            above = [jnp.zeros((koff, LANES), F32)] if koff else []
            below = [jnp.zeros((2 * BLK - KW - koff, LANES), F32)] if 2 * BLK - KW - koff else []
            for grp in range(2):
                qs = _stack_heads(q_ref, grp, lo, rows)
                dos = _stack_heads(do_ref, grp, lo, rows)
                kg, vg = ks[grp][koff:koff + KW], vs[grp][koff:koff + KW]
                pr, psink, head = _attn_probs_t(qs, kg, ok, sk_ref, layer, grp)
                dpr = _dot_nt(vg, dos)
                delta = jnp.sum(pr * dpr, axis=0, keepdims=True)
                ds = (pr * (dpr - delta) * SCALE).astype(MX)
                sd = psink * delta
                for h in range(HPG):
                    dsk = dsk + jnp.where(lane == HPG * grp + h,
                                          -jnp.sum(jnp.where(head == h, sd, 0.0), axis=1, keepdims=True), 0.0)
                dqs = _unstack_heads(_dot_tn(kg, ds).T, lo)
                for i in range(4):
                    j = 4 * grp + i
                    dq_ref[rows, j * LANES:(j + 1) * LANES] = dqs[i].astype(dq_ref.dtype)
                dk[grp] = dk[grp] + jnp.concatenate(above + [_dot(ds, qs)] + below, axis=0)
                dv[grp] = dv[grp] + jnp.concatenate(above + [_dot(pr.astype(MX), dos)] + below, axis=0)
        dsk_ref[...] += dsk
        tk = [a + pltpu.roll(a, HD, 1) for a in dk]
        tv = [a + pltpu.roll(a, HD, 1) for a in dv]
        contrib = jnp.concatenate([jnp.where(lo, tk[0], tk[1]), jnp.where(lo, tv[0], tv[1])], axis=1)

        @pl.when(n > 0)
        def _():
            rows = pl.ds(pl.multiple_of((n - 1) * BLK, BLK), 2 * BLK)
            dkv_ref[rows, :] += contrib

        @pl.when(n == 0)
        def _():
            dkv_ref[0:BLK, :] += contrib[BLK:]

    return pl.pallas_call(
        body, name="attn_bwd",
        grid=(S // BLK,),
        in_specs=_attn_specs(2),
        out_specs=[
            pl.BlockSpec((BLK, D), lambda n: (n, 0)),
            pl.BlockSpec((S, 4 * HD), lambda n: (0, 0)),
            pl.BlockSpec((1, LANES), lambda n: (0, 0)),
        ],
        out_shape=[
            jax.ShapeDtypeStruct((S, D), MX),
            jax.ShapeDtypeStruct((S, 4 * HD), F32),
            jax.ShapeDtypeStruct((1, LANES), F32),
        ],
        compiler_params=_cp(),
    )(q, do, kv, kv, sinks)


def _ew_rows(rows, cols, n_bufs=1):
    br = rows
    while br * cols * 4 * n_bufs > EW_BYTES and br % (2 * SUBLANES) == 0:
        br //= 2
    return br


def _adam_update(g, w, m, v):
    nm = B1 * m + (1.0 - B1) * g
    nv = B2 * v + (1.0 - B2) * (g * g)
    m_hat = nm / (1.0 - B1 ** STEP)
    v_hat = nv / (1.0 - B2 ** STEP)
    return -LR * (m_hat / (jnp.sqrt(v_hat) + AEPS) + WD * w), nm, nv


def _adamw_small(gs, ws, ms, vs):
    n = len(ws)

    def body(*refs):
        ins, outs = refs[:4 * n], refs[4 * n:]
        for i in range(n):
            g = ins[i][...]
            d, nm, nv = _adam_update(g, ins[n + i][...], ins[2 * n + i][...], ins[3 * n + i][...])
            outs[i][...] = g
            outs[n + i][...] = d
            outs[2 * n + i][...] = nm
            outs[3 * n + i][...] = nv

    vm = pl.BlockSpec(memory_space=pltpu.VMEM)
    res = pl.pallas_call(
        body, name="adamw_small",
        in_specs=[vm] * (4 * n),
        out_specs=[vm] * (4 * n),
        out_shape=[jax.ShapeDtypeStruct(w.shape, F32) for w in ws] * 4,
    )(*gs, *ws, *ms, *vs)
    return res[:n], res[n:2 * n], res[2 * n:3 * n], res[3 * n:]


def _adamw(parts, w, m, v):
    L, R, C = w.shape
    br = _ew_rows(R, C)
    npart = len(parts[0])

    def body(*refs):
        p_refs = refs[:L * npart]
        w_ref, m_ref, v_ref, g_ref, d_ref, nm_ref, nv_ref = refs[L * npart:]
        lyr = pl.program_id(0)
        for l in range(L):
            @pl.when(lyr == l)
            def _(l=l):
                g = p_refs[l * npart][...]
                for pr in p_refs[l * npart + 1:(l + 1) * npart]:
                    g = g + pr[...]
                g_ref[...] = g
                d_ref[...], nm_ref[...], nv_ref[...] = _adam_update(g, w_ref[...], m_ref[...], v_ref[...])

    spec = pl.BlockSpec((None, br, C), lambda a, i: (a, i, 0))
    part_specs = [pl.BlockSpec((br, C), lambda a, i, l=l: (jnp.where(a == l, i, 0), 0))
                  for l in range(L) for _ in range(npart)]
    return pl.pallas_call(
        body, name="adamw",
        grid=(L, R // br),
        in_specs=part_specs + [spec] * 3,
        out_specs=[spec] * 4,
        out_shape=[jax.ShapeDtypeStruct((L, R, C), F32)] * 4,
        compiler_params=_cp(2),
    )(*[a for lp in parts for a in lp], w, m, v)


def _coords():
    return lax.axis_index("x"), lax.axis_index("y"), lax.axis_index("c")


def _other_chips(x, y):
    return [(1 - x, y), (x, 1 - y), (1 - x, 1 - y)]


def _slot(ref, axis, chip, size):
    idx = [slice(None)] * 3
    idx[axis] = pl.ds(pl.multiple_of(chip * size, size), size)
    return ref.at[tuple(idx)]


HBM_SPEC = pl.BlockSpec(memory_space=pltpu.HBM)
SEM_SPEC = pl.BlockSpec(memory_space=pltpu.SEMAPHORE)
ANY_SPEC = pl.BlockSpec(memory_space=pl.ANY)
EFFECT = pltpu.SideEffectType.DATAFLOW_SIDE_EFFECTING


def _slot_specs(shape, axis, br, lead):
    _, b, c = shape
    nrb = b // br
    first = (lambda a: a) if lead is None else (lambda a: lead)
    shard = pl.BlockSpec((None, br, c), lambda a, i, me: (first(a), i, 0))
    if axis == 1:
        slot = pl.BlockSpec((None, br, c), lambda a, i, me: (a, me[0] * nrb + i, 0))
    else:
        slot = pl.BlockSpec((None, br, c), lambda a, i, me: (a, i, me[0]))
    return shard, slot


def _shard_rows(b, c):
    br = b
    while br * c * 4 > 2 * EW_BYTES and br % (4 * SUBLANES) == 0:
        br //= 2
    return br


def _gather_place(shard, axis, me, dtype, lead=None):
    a_dim, b, c = shard.shape
    if lead is not None:
        a_dim = 1
    br = _shard_rows(b, c)
    shp = [a_dim, b, c]
    shp[axis] *= 4
    shard_spec, slot_spec = _slot_specs((a_dim, b, c), axis, br, lead)

    def body(me_ref, s_ref, o_ref):
        o_ref[...] = s_ref[...].astype(o_ref.dtype)

    return pl.pallas_call(
        body, name="gather_place",
        grid_spec=pltpu.PrefetchScalarGridSpec(
            num_scalar_prefetch=1, grid=(a_dim, b // br), in_specs=[shard_spec], out_specs=slot_spec),
        out_shape=jax.ShapeDtypeStruct(tuple(shp), dtype),
        compiler_params=_cp(2),
    )(me, shard)


def _sum_landed(grad, landed, axis, me):
    a_dim, b, c = landed.shape[1:]
    br = _shard_rows(b, c)
    shard_spec, slot_spec = _slot_specs((a_dim, b, c), axis, br, None)

    def body(me_ref, own_ref, r_ref, o_ref):
        o_ref[...] = ((own_ref[...].astype(F32) + r_ref[0].astype(F32)) + r_ref[1].astype(F32)) + r_ref[2].astype(F32)

    return pl.pallas_call(
        body, name="sum_landed",
        grid_spec=pltpu.PrefetchScalarGridSpec(
            num_scalar_prefetch=1, grid=(a_dim, b // br),
            in_specs=[slot_spec, pl.BlockSpec((3, None, br, c), lambda a, i, me: (0, a, i, 0))],
            out_specs=shard_spec),
        out_shape=jax.ShapeDtypeStruct((a_dim, b, c), F32),
        compiler_params=_cp(2),
    )(me, grad, landed)


def _copies(refs, plan, send, recv, to_sibling):
    x, y, c = _coords()
    me = 2 * x + y
    if to_sibling == "everyone":
        me = 4 * x + 2 * y + c
        flips = [(k >> 2, (k >> 1) & 1, k & 1) for k in range(1, 8)]
        targets = [((x ^ fx, y ^ fy, c ^ fc), 4 * (x ^ fx) + 2 * (y ^ fy) + (c ^ fc)) for fx, fy, fc in flips]
    elif to_sibling:
        targets = [((x, y, 1 - c), me)]
    else:
        targets = [((px, py, c), 2 * px + py) for px, py in _other_chips(x, y)]
    out, t = [], 0
    while plan(refs, me, t, 0, me) is not None:
        for k, (device, peer) in enumerate(targets):
            sv, dv = plan(refs, me, t, k, peer)
            n = len(targets) * t + k
            out.append(pltpu.make_async_remote_copy(
                src_ref=sv, dst_ref=dv, send_sem=send.at[n], recv_sem=recv.at[n],
                device_id=device, device_id_type=MESH))
        t += 1
    return out


def _push_start(name, bufs, n_copies, plan, to_sibling=False):
    nb = len(bufs)

    def body(*refs):
        send, recv, token = refs[nb], refs[nb + 1], refs[-1]
        for cp in _copies(refs[:nb], plan, send, recv, to_sibling):
            cp.start()
        token[...] = jnp.zeros_like(token)

    res = pl.pallas_call(
        body, name=name,
        in_specs=[HBM_SPEC] * nb,
        out_specs=[SEM_SPEC, SEM_SPEC] + [HBM_SPEC] * nb + [pl.BlockSpec(memory_space=pltpu.VMEM)],
        out_shape=[pltpu.SemaphoreType.DMA((n_copies,)), pltpu.SemaphoreType.DMA((n_copies,))]
        + [pltpu.HBM(a.shape, a.dtype) for a in bufs] + [jax.ShapeDtypeStruct((SUBLANES, LANES), F32)],
        input_output_aliases={i: 2 + i for i in range(nb)},
        compiler_params=pltpu.CompilerParams(has_side_effects=EFFECT),
    )(*[pltpu.with_memory_space_constraint(a, pltpu.HBM) for a in bufs])
    return res[0], res[1], res[2:2 + nb], res[-1]


def _push_wait(name, send, recv, bufs, plan, after, to_sibling=False):
    nb = len(bufs)

    def body(*refs):
        for cp in _copies(refs[:nb], plan, refs[nb], refs[nb + 1], to_sibling):
            cp.wait_send()
            cp.wait_recv()

    return pl.pallas_call(
        body, name=name,
        in_specs=[HBM_SPEC] * nb + [SEM_SPEC, SEM_SPEC, ANY_SPEC],
        out_specs=[HBM_SPEC] * nb,
        out_shape=[pltpu.HBM(a.shape, a.dtype) for a in bufs],
        input_output_aliases={i: i for i in range(nb)},
        compiler_params=pltpu.CompilerParams(has_side_effects=EFFECT),
    )(*bufs, send, recv, after)


def _gather_plan(axes):
    def plan(refs, me, t, k, peer):
        if t >= len(axes):
            return None
        size = refs[t].shape[axes[t]] // 4
        mine = _slot(refs[t], axes[t], me, size)
        return mine, mine
    return plan


def _half_slot(ref, axis, chip):
    c = lax.axis_index("c")
    if axis == 1:
        half = ref.shape[1] // 8
        return ref.at[:, pl.ds(pl.multiple_of(chip * 2 * half + c * half, 2 * SUBLANES), half), :]
    half = ref.shape[1] // 2
    size = ref.shape[2] // 4
    return ref.at[:, pl.ds(pl.multiple_of(c * half, 2 * SUBLANES), half), pl.ds(pl.multiple_of(chip * size, LANES), size)]


def _gather_half_plan(axes):
    def plan(refs, me, t, k, peer):
        if t >= len(axes):
            return None
        mine = _half_slot(refs[t], axes[t], me)
        return mine, mine
    return plan


def _gather_pass_plan(axes):
    def plan(refs, me, t, k, peer):
        if t >= 3 * len(axes):
            return None
        x, y, _ = _coords()
        px, py = _other_chips(x, y)[t % 3]
        landed = _half_slot(refs[t // 3], axes[t // 3], 2 * px + py)
        return landed, landed
    return plan


def _scatter_plan(axes):
    n = len(axes)

    def plan(refs, me, t, k, peer):
        if t >= n:
            return None
        size = refs[t].shape[axes[t]] // 4
        return _slot(refs[t], axes[t], peer, size), refs[n + t].at[k]
    return plan


def _swap_plan(n):
    def plan(refs, me, t, k, peer):
        if t >= n:
            return None
        return refs[t], refs[n + t]
    return plan


def _everyone_plan():
    def plan(refs, me, t, k, peer):
        if t >= 1:
            return None
        mine = refs[0].at[me]
        return mine, mine
    return plan


def _sum_slots(slots):
    R = slots.shape[1]

    def body(s_ref, o_ref):
        tot = s_ref[0]
        for k in range(1, 8):
            tot = tot + s_ref[k]
        o_ref[...] = tot

    vm = pl.BlockSpec(memory_space=pltpu.VMEM)
    return pl.pallas_call(
        body, name="sum_slots",
        in_specs=[vm],
        out_specs=vm,
        out_shape=jax.ShapeDtypeStruct((R, LANES), F32),
        compiler_params=pltpu.CompilerParams(vmem_limit_bytes=VMEM_LIMIT),
    )(slots)


def _pack(arrs):
    flat = []
    for a in arrs:
        f = a.reshape(-1).astype(F32)
        flat.append(jnp.pad(f, (0, (-f.shape[0]) % LANES)))
    v = jnp.concatenate(flat)
    v = jnp.pad(v, (0, (-v.shape[0]) % (SUBLANES * LANES)))
    return v.reshape(-1, LANES)


def _unpack(v, shapes):
    flat = v.reshape(-1)
    out, off = [], 0
    for shp in shapes:
        n = 1
        for d in shp:
            n *= d
        out.append(flat[off:off + n].reshape(shp))
        off += n + (-n) % LANES
    return out


def _local_step(x, tgt, sp, weights_for, on_grads):
    n1, n2 = sp["norm1_g"], sp["norm2_g"]
    w = dict(weights_for(0, x))
    saved = []
    xs = x
    kv = None
    for l in range(DEPTH):
        x_in = xs
        if l >= N_A:
            w.update(weights_for(1 + 2 * l - N_A, x_in))
        if l == N_A:
            kv = _rms_linear(x_in, sp["kv_norm_g"], w["w_kv"], sp["b_kv"], 0, 0, "kv_proj")
        if l < N_A:
            xa = _pool_fwd(x_in, n1, w["pool_w"], w["pool_scale"], l)
            q = o = None
        else:
            j = l - N_A
            q = _rms_linear(x_in, n1, w["w_q", j], sp["b_q"], l, j, "q_proj")
            o = _attn_fwd(q, kv, sp["sinks"], j)
            xa = _linear_res(o, w["w_o", j], sp["b_o"], x_in, j)
        w.update(weights_for(1 + l if l < N_A else 2 + 2 * l - N_A, xa))
        xs, u, uc = _ffn_fwd(xa, n2, w["ffn_up", l], w["ffn_down", l], w["ffn_conv_w"], sp["ffn_conv_b"], l)
        saved.append((x_in, xa, u, uc, q, o))

    dx, d_final_g, loss = _loss_head(xs, sp["final_g"], tgt)

    g = {k: [None] * DEPTH for k in ("norm1_g", "norm2_g", "ffn_conv_w", "ffn_conv_b")}
    for k in ("pool_scale", "b_q", "sinks", "b_o"):
        g[k] = [None] * N_A
    g["final_g"] = d_final_g
    dkvs = []
    pending = {}
    for l in reversed(range(DEPTH)):
        x_in, xa, u, uc, q, o = saved[l]
        dxa, du, a, hb, g["norm2_g"][l], g["ffn_conv_w"][l], g["ffn_conv_b"][l] = _ffn_bwd(
            xa, dx, u, uc, n2, w["ffn_up", l], w["ffn_down", l], w["ffn_conv_w"], l)
        pending["ffn_up", l] = _tn_matmul(hb, du, MX, "d_ffn_up", a_is_transposed=True)
        if l == 0:
            n1 = n1 + on_grads(DEPTH + 1, pending)
            pending = {}
        pending["ffn_down", l] = _tn_matmul(a, dx, MX, "d_ffn_down", a_is_transposed=True)
        zero = on_grads(DEPTH - 1 - l, pending)
        pending = {}
        n1, n2 = n1 + zero, n2 + zero
        if l < N_A:
            dx, d_pw, g["pool_scale"][l], g["norm1_g"][l] = _pool_bwd(
                x_in, dxa, n1, w["pool_w"], w["pool_scale"], l)
            pending["pool_w", l] = d_pw.astype(MX)
        else:
            j = l - N_A
            d_o, g["b_o"][j] = _linear_nt(dxa, w["w_o", j])
            pending["w_o", j] = _tn_matmul(o, dxa, MX, "d_w_o")
            dq, dkv, g["sinks"][j] = _attn_bwd(q, d_o, kv, sp["sinks"], j)
            dkvs.append(dkv)
            dx, g["norm1_g"][l], g["b_q"][j], hq, dqb = _rms_linear_bwd(
                x_in, n1, [dq], w["w_q", j], dxa, l, "q_proj_bwd")
            pending["w_q", j] = _tn_matmul(hq, dqb, MX, "d_w_q")
        if l == N_A:
            dx, g["kv_norm_g"], g["b_kv"], hk, dkvb = _rms_linear_bwd(
                x_in, sp["kv_norm_g"], dkvs, w["w_kv"], dx, 0, "kv_proj_bwd")
            pending["w_kv", 0] = _tn_matmul(hk, dkvb, MX, "d_w_kv")
    on_grads(DEPTH, pending)
    return loss, dx, g


SMALL = ("norm1_g", "norm2_g", "kv_norm_g", "b_kv", "b_q", "sinks", "b_o", "ffn_conv_b", "final_g")
SMALL_SHARDED = ("pool_scale", "ffn_conv_w")
BIG = ("pool_w", "w_kv", "w_q", "w_o", "ffn_up", "ffn_down")
EARLY_SWAP = 3
ORDER = ("norm1_g", "norm2_g", "pool_w", "pool_scale", "kv_norm_g", "w_kv", "b_kv", "w_q", "b_q", "sinks",
         "w_o", "b_o", "ffn_up", "ffn_conv_w", "ffn_conv_b", "ffn_down", "final_g")


def _as3d(a):
    return a.reshape((-1,) + a.shape[-2:])


def kernel(x, norm1_g, norm2_g, pool_w, pool_scale, kv_norm_g, w_kv, b_kv, w_q, b_q, sinks, w_o, b_o, ffn_up, ffn_conv_w, ffn_conv_b, ffn_down, final_g, loss_target, m_norm1_g, m_norm2_g, m_pool_w, m_pool_scale, m_kv_norm_g, m_w_kv, m_b_kv, m_w_q, m_b_q, m_sinks, m_w_o, m_b_o, m_ffn_up, m_ffn_conv_w, m_ffn_conv_b, m_ffn_down, m_final_g, v_norm1_g, v_norm2_g, v_pool_w, v_pool_scale, v_kv_norm_g, v_w_kv, v_b_kv, v_w_q, v_b_q, v_sinks, v_w_o, v_b_o, v_ffn_up, v_ffn_conv_w, v_ffn_conv_b, v_ffn_down, v_final_g):
    W = dict(norm1_g=norm1_g, norm2_g=norm2_g, pool_w=pool_w, pool_scale=pool_scale, kv_norm_g=kv_norm_g,
             w_kv=w_kv, b_kv=b_kv, w_q=w_q, b_q=b_q, sinks=sinks, w_o=w_o, b_o=b_o, ffn_up=ffn_up,
             ffn_conv_w=ffn_conv_w, ffn_conv_b=ffn_conv_b, ffn_down=ffn_down, final_g=final_g)
    M = dict(norm1_g=m_norm1_g, norm2_g=m_norm2_g, pool_w=m_pool_w, pool_scale=m_pool_scale,
             kv_norm_g=m_kv_norm_g, w_kv=m_w_kv, b_kv=m_b_kv, w_q=m_w_q, b_q=m_b_q, sinks=m_sinks, w_o=m_w_o,
             b_o=m_b_o, ffn_up=m_ffn_up, ffn_conv_w=m_ffn_conv_w, ffn_conv_b=m_ffn_conv_b, ffn_down=m_ffn_down,
             final_g=m_final_g)
    V = dict(norm1_g=v_norm1_g, norm2_g=v_norm2_g, pool_w=v_pool_w, pool_scale=v_pool_scale,
             kv_norm_g=v_kv_norm_g, w_kv=v_w_kv, b_kv=v_b_kv, w_q=v_w_q, b_q=v_b_q, sinks=v_sinks, w_o=v_w_o,
             b_o=v_b_o, ffn_up=v_ffn_up, ffn_conv_w=v_ffn_conv_w, ffn_conv_b=v_ffn_conv_b, ffn_down=v_ffn_down,
             final_g=v_final_g)
    S = x.shape[1]
    chip = 2 * lax.axis_index("x") + lax.axis_index("y")

    gather_axis = dict(pool_w=1, w_kv=1, w_q=1, w_o=1, ffn_up=2, ffn_down=1, pool_scale=2, ffn_conv_w=2)
    me = chip.reshape(1).astype(jnp.int32)
    axis_of = lambda key: gather_axis[key if isinstance(key, str) else key[0]]

    def placed(key, dtype):
        if isinstance(key, str):
            return _gather_place(_as3d(W[key]), axis_of(key), me, dtype)
        return _gather_place(W[key[0]], axis_of(key), me, dtype, lead=key[1])

    stages = [
        ["pool_w", "pool_scale", "ffn_conv_w"],
        [("ffn_up", 0), ("ffn_down", 0)],
        [("ffn_up", 1), ("ffn_down", 1)],
        ["w_kv", ("w_q", 0), ("w_o", 0)],
        [("ffn_up", 2), ("ffn_down", 2)],
        [("w_q", 1), ("w_o", 1)],
        [("ffn_up", 3), ("ffn_down", 3)],
    ]
    TWO_LEVEL = 1
    gathers, zero = [], 0.0
    for si, keys in enumerate(stages):
        axes = [axis_of(k) for k in keys]
        bufs = [placed(k, F32 if k in SMALL_SHARDED else MX) for k in keys]
        plan = _gather_half_plan(axes) if si == TWO_LEVEL else _gather_plan(axes)
        send, recv, bufs, token = _push_start(f"gather_start_{si}", bufs, 3 * len(keys), plan)
        gathers.append((keys, axes, send, recv, bufs))
        zero = zero + token[0, 0]

    def weights_for(stage, after):
        keys, axes, send, recv, bufs = gathers[stage]
        if stage == TWO_LEVEL:
            bufs = _push_wait(f"gather_wait_{stage}", send, recv, bufs, _gather_half_plan(axes), after)
            send, recv, bufs, _ = _push_start("gather_pass_start", bufs, 3 * len(keys), _gather_pass_plan(axes), True)
            bufs = _push_wait("gather_pass_wait", send, recv, bufs, _gather_pass_plan(axes), after, True)
        else:
            bufs = _push_wait(f"gather_wait_{stage}", send, recv, bufs, _gather_plan(axes), after)
        out = dict(zip(keys, bufs))
        if stage == 0:
            out["pool_w"] = out["pool_w"].reshape(N_A, 4, GC, GC)
            out["pool_scale"] = out["pool_scale"].reshape(N_A, 1, D)
        return out

    scatters = []

    def on_grads(stage, grads):
        keys = list(grads)
        axes = [axis_of(k) for k in keys]
        arrs = [_as3d(grads[k]) for k in keys]
        lands = []
        for a, ax in zip(arrs, axes):
            shp = list(a.shape)
            shp[ax] //= 4
            lands.append(lax.empty((3,) + tuple(shp), a.dtype))
        send, recv, bufs, token = _push_start(f"scatter_start_{stage}", arrs + lands, 3 * len(keys), _scatter_plan(axes))
        scatters.append((stage, keys, axes, send, recv, bufs))
        return token[0, 0]

    sp = dict(
        norm1_g=norm1_g.reshape(DEPTH, 1, D) + zero, norm2_g=norm2_g.reshape(DEPTH, 1, D),
        kv_norm_g=kv_norm_g.reshape(1, 1, D), b_kv=b_kv.reshape(1, 1, 4 * HD), b_q=b_q.reshape(N_B, 1, D),
        sinks=sinks, b_o=b_o.reshape(N_B, 1, D), ffn_conv_b=ffn_conv_b.reshape(DEPTH, 1, F2),
        final_g=final_g.reshape(1, D))

    x2d = x.reshape(S, D)
    loss, grad_x, g = _local_step(x2d, loss_target.reshape(S, D), sp, weights_for, on_grads)

    small_full = dict(
        norm1_g=jnp.stack(g["norm1_g"]), norm2_g=jnp.stack(g["norm2_g"]), kv_norm_g=g["kv_norm_g"],
        b_kv=g["b_kv"], b_q=jnp.stack(g["b_q"]), sinks=jnp.stack([s[0, :NH] for s in g["sinks"]]),
        b_o=jnp.stack(g["b_o"]), ffn_conv_b=jnp.stack(g["ffn_conv_b"]), final_g=g["final_g"],
        pool_scale=jnp.stack(g["pool_scale"]), ffn_conv_w=jnp.stack(g["ffn_conv_w"]))
    small_names = SMALL + SMALL_SHARDED
    small_shapes = [tuple(W[k].shape) for k in SMALL] + [(N_A, D), (DEPTH, 3, F2)]
    packed = _pack([small_full[k] for k in small_names] + [loss])
    slots = lax.dynamic_update_slice(lax.empty((8,) + packed.shape, F32), packed[None],
                                     (4 * lax.axis_index("x") + 2 * lax.axis_index("y") + lax.axis_index("c"), 0, 0))
    red_send, red_recv, slots, _ = _push_start("reduce_start", [slots], 7, _everyone_plan(), "everyone")

    pkeys, partial, swaps, after = [], [], [], grad_x

    def swap_start(tag, first):
        mine_now = partial[first:]
        lands = [lax.empty(p.shape, p.dtype) for p in mine_now]
        n = len(mine_now)
        send, recv, bufs, token = _push_start(f"swap_start_{tag}", mine_now + lands, n, _swap_plan(n), True)
        swaps.append((tag, n, send, recv, bufs))
        return token

    for i, (stage, keys, axes, send, recv, bufs) in enumerate(scatters):
        bufs = _push_wait(f"scatter_wait_{stage}", send, recv, bufs, _scatter_plan(axes), after)
        n = len(keys)
        for k, ax, grad, landed in zip(keys, axes, bufs[:n], bufs[n:]):
            pkeys.append(k)
            p_sum = _sum_landed(grad, landed, ax, me)
            partial.append(p_sum.reshape(-1, p_sum.shape[-1]))
        if i == EARLY_SWAP - 1:
            after = swap_start("early", 0)
    after = swap_start("late", sum(n for _, n, *_ in swaps))

    slots = _push_wait("reduce_wait", red_send, red_recv, slots, _everyone_plan(), after, "everyone")[0]
    red = _unpack(_sum_slots(slots), small_shapes + [(1, LANES)])
    red_g = dict(zip(small_names, red[:-1]))
    loss_out = red[-1][0, 0]
    red_g["pool_scale"] = lax.dynamic_slice_in_dim(red_g["pool_scale"], chip * (D // 4), D // 4, axis=1)
    red_g["ffn_conv_w"] = lax.dynamic_slice_in_dim(red_g["ffn_conv_w"], chip * (F2 // 4), F2 // 4, axis=2)
    two_d = lambda a: a.reshape((-1, a.shape[-1]))
    res = _adamw_small([two_d(red_g[k]) for k in small_names], [two_d(W[k]) for k in small_names],
                       [two_d(M[k]) for k in small_names], [two_d(V[k]) for k in small_names])
    out_g, out_d, out_m, out_v = [
        {k: r.reshape(W[k].shape) for k, r in zip(small_names, part)} for part in res]
    after = res[0][0]

    mine, theirs = [], []
    for tag, n, send, recv, bufs in swaps:
        bufs = _push_wait(f"swap_wait_{tag}", send, recv, bufs, _swap_plan(n), after, True)
        mine += bufs[:n]
        theirs += bufs[n:]
    mine = dict(zip(pkeys, mine))
    theirs = dict(zip(pkeys, theirs))
    for k in BIG:
        n_l = len([pk_ for pk_ in pkeys if pk_[0] == k])
        shp = W[k].shape
        rows, cols = mine[k, 0].shape
        three_d = lambda a: a.reshape(n_l, rows, cols)
        res = _adamw([[mine[k, l], theirs[k, l]] for l in range(n_l)], three_d(W[k]), three_d(M[k]), three_d(V[k]))
        out_g[k], out_d[k], out_m[k], out_v[k] = [r.reshape(shp) for r in res]

    return (loss_out, grad_x.reshape(x.shape), *[out_g[k] for k in ORDER], *[out_d[k] for k in ORDER],
            *[out_m[k] for k in ORDER], *[out_v[k] for k in ORDER])
```

```python
import functools

import jax
import jax.numpy as jnp
from jax import lax
from jax.experimental import pallas as pl
from jax.experimental.pallas import tpu as pltpu

D = 1024
DEPTH = 4
N_A = 2
N_B = 2
WINS = (2, 4, 8, 16)
GC = 256
HD = 64
NH = 16
BLK = 128
F = 2816
F2 = 2 * F
EPS = 1e-5
SCALE = HD ** -0.5
NEG = -1e30
HALO = 16
TN = 256
TM_STREAM = 1024
UP_GROUP = 3
LANES = 128
SUBLANES = 8
VMEM_LIMIT = 56 * 1024 * 1024
FFN_VMEM_LIMIT = 60 * 1024 * 1024
ACC_BYTES = 6 * 1024 * 1024
EW_BYTES = 1536 * 1024

LR, B1, B2, AEPS, WD, STEP = 0.001, 0.9, 0.999, 1e-08, 0.01, 10

MX = jnp.bfloat16
F32 = jnp.float32
MESH = pl.DeviceIdType.MESH


def _cp(n_axes=1, vmem=VMEM_LIMIT):
    return pltpu.CompilerParams(dimension_semantics=("arbitrary",) * n_axes, vmem_limit_bytes=vmem)


def _dot(a, b):
    return jnp.dot(a, b, preferred_element_type=F32)


def _dot_nt(a, b):
    return lax.dot_general(a, b, (((1,), (1,)), ((), ())), preferred_element_type=F32)


def _dot_tn(a, b):
    return lax.dot_general(a, b, (((0,), (0,)), ((), ())), preferred_element_type=F32)


def _rms_fwd(x, g):
    r = lax.rsqrt(jnp.mean(x * x, axis=-1, keepdims=True) + EPS)
    xh = x * r
    return xh * g, xh, r


def _rms_bwd(dh, xh, r, g):
    dxh = dh * g
    return r * (dxh - xh * jnp.mean(dxh * xh, axis=-1, keepdims=True))


def _row_tile(s, want):
    return min(s, want)


def _pool_pm(e, h, row, tm):
    out = []
    for gi, win in enumerate(WINS):
        cols = slice(gi * GC, (gi + 1) * GC)
        s = e[:, cols]
        sh = 1
        while sh < win:
            s = s + pltpu.roll(s, sh, 0)
            sh *= 2
        inv = 1.0 / jnp.minimum(row + 1, win).astype(F32)
        out.append(s[HALO:] * inv - h[:, cols])
    return out


def _pool_fwd(x, g, pw, ps, layer):
    S = x.shape[0]
    tm = _row_tile(S, 512)
    hb = tm // HALO

    def body(x_ref, xh_ref, g_ref, pw_ref, ps_ref, o_ref):
        i = pl.program_id(0)
        x = x_ref[...]
        gg = g_ref[...]
        h, _, _ = _rms_fwd(x, gg)
        hh, _, _ = _rms_fwd(xh_ref[...], gg)
        hh = jnp.where(i > 0, hh, 0.0)
        e = jnp.concatenate([hh, h], axis=0)
        row = i * tm + lax.broadcasted_iota(jnp.int32, (tm, 1), 0)
        pm = _pool_pm(e, h, row, tm)
        for gi in range(len(WINS)):
            cols = slice(gi * GC, (gi + 1) * GC)
            z = _dot(pm[gi].astype(MX), pw_ref[gi])
            o_ref[:, cols] = x[:, cols] + z * ps_ref[:, cols]

    return pl.pallas_call(
        body, name="pool_fwd",
        grid=(S // tm,),
        in_specs=[
            pl.BlockSpec((tm, D), lambda i: (i, 0)),
            pl.BlockSpec((HALO, D), lambda i: (jnp.maximum(i * hb - 1, 0), 0)),
            pl.BlockSpec((None, 1, D), lambda i: (layer, 0, 0)),
            pl.BlockSpec((None, 4, GC, GC), lambda i: (layer, 0, 0, 0)),
            pl.BlockSpec((None, 1, D), lambda i: (layer, 0, 0)),
        ],
        out_specs=pl.BlockSpec((tm, D), lambda i: (i, 0)),
        out_shape=jax.ShapeDtypeStruct((S, D), F32),
        compiler_params=_cp(),
    )(x, x, g, pw, ps)


def _pool_bwd(x, dy, g, pw, ps, layer):
    S = x.shape[0]
    tm = _row_tile(S, 256)
    hb = tm // HALO
    n_i = S // tm
    n_h = S // HALO

    def body(x_ref, xh_ref, dy_ref, dyn_ref, g_ref, pw_ref, ps_ref, dx_ref, dpw_ref, dps_ref, dg_ref):
        i = pl.program_id(0)

        @pl.when(i == 0)
        def _():
            dpw_ref[...] = jnp.zeros_like(dpw_ref)
            dps_ref[...] = jnp.zeros_like(dps_ref)
            dg_ref[...] = jnp.zeros_like(dg_ref)

        x = x_ref[...]
        gg = g_ref[...]
        ps = ps_ref[...]
        h, xh, r = _rms_fwd(x, gg)
        hh, _, _ = _rms_fwd(xh_ref[...], gg)
        hh = jnp.where(i > 0, hh, 0.0)
        e = jnp.concatenate([hh, h], axis=0)
        row = i * tm + lax.broadcasted_iota(jnp.int32, (tm, 1), 0)
        rown = (i + 1) * tm + lax.broadcasted_iota(jnp.int32, (HALO, 1), 0)
        pm = _pool_pm(e, h, row, tm)
        dy = dy_ref[...]
        dz = dy * ps
        dzn = jnp.where(i < n_i - 1, dyn_ref[...] * ps, 0.0)
        parts = []
        for gi, win in enumerate(WINS):
            cols = slice(gi * GC, (gi + 1) * GC)
            w = pw_ref[gi]
            pmb = pm[gi].astype(MX)
            z = _dot(pmb, w)
            dps_ref[:, cols] += jnp.sum(dy[:, cols] * z, axis=0, keepdims=True)
            dzb = dz[:, cols].astype(MX)
            dpw_ref[gi] += _dot_tn(pmb, dzb)
            dpm = _dot_nt(dzb, w)
            dpmn = _dot_nt(dzn[:, cols].astype(MX), w)
            q = dpm * (1.0 / jnp.minimum(row + 1, win).astype(F32))
            qn = dpmn * (1.0 / jnp.minimum(rown + 1, win).astype(F32))
            s = jnp.concatenate([q, qn], axis=0)
            sh = 1
            while sh < win:
                s = s + pltpu.roll(s, tm + HALO - sh, 0)
                sh *= 2
            parts.append(s[:tm] - dpm)
        dh = jnp.concatenate(parts, axis=1)
        dg_ref[...] += jnp.sum(dh * xh, axis=0, keepdims=True)
        dx_ref[...] = dy + _rms_bwd(dh, xh, r, gg)

    return pl.pallas_call(
        body, name="pool_bwd",
        grid=(n_i,),
        in_specs=[
            pl.BlockSpec((tm, D), lambda i: (i, 0)),
            pl.BlockSpec((HALO, D), lambda i: (jnp.maximum(i * hb - 1, 0), 0)),
            pl.BlockSpec((tm, D), lambda i: (i, 0)),
            pl.BlockSpec((HALO, D), lambda i: (jnp.minimum((i + 1) * hb, n_h - 1), 0)),
            pl.BlockSpec((None, 1, D), lambda i: (layer, 0, 0)),
            pl.BlockSpec((None, 4, GC, GC), lambda i: (layer, 0, 0, 0)),
            pl.BlockSpec((None, 1, D), lambda i: (layer, 0, 0)),
        ],
        out_specs=[
            pl.BlockSpec((tm, D), lambda i: (i, 0)),
            pl.BlockSpec((4, GC, GC), lambda i: (0, 0, 0)),
            pl.BlockSpec((1, D), lambda i: (0, 0)),
            pl.BlockSpec((1, D), lambda i: (0, 0)),
        ],
        out_shape=[
            jax.ShapeDtypeStruct((S, D), F32),
            jax.ShapeDtypeStruct((4, GC, GC), F32),
            jax.ShapeDtypeStruct((1, D), F32),
            jax.ShapeDtypeStruct((1, D), F32),
        ],
        compiler_params=_cp(),
    )(x, x, dy, dy, g, pw, ps)


N_STAGE = 4


def _rows_before(slot, u, prev8):
    tm = u.shape[0]
    m1, m2 = [], []
    for c in range(TN // LANES):
        lanes = slice(c * LANES, (c + 1) * LANES)
        slot[c, 0:SUBLANES, :] = prev8[:, lanes]
        slot[c, SUBLANES:SUBLANES + tm, :] = u[:, lanes]
        m1.append(slot[c, pl.ds(SUBLANES - 1, tm), :])
        m2.append(slot[c, pl.ds(SUBLANES - 2, tm), :])
    return jnp.concatenate(m1, axis=1), jnp.concatenate(m2, axis=1)


def _rows_after(slot, d, next8):
    tm = d.shape[0]
    p1, p2 = [], []
    for c in range(TN // LANES):
        lanes = slice(c * LANES, (c + 1) * LANES)
        slot[c, 0:tm, :] = d[:, lanes]
        slot[c, tm:tm + SUBLANES, :] = next8[:, lanes]
        p1.append(slot[c, pl.ds(1, tm), :])
        p2.append(slot[c, pl.ds(2, tm), :])
    return jnp.concatenate(p1, axis=1), jnp.concatenate(p2, axis=1)


def _conv(slot, u, prev8, cw):
    um1, um2 = _rows_before(slot, u, prev8)
    return cw[0:1] * um2 + cw[1:2] * um1 + cw[2:3] * u


def _ffn_fwd(x, g, wup, wdn, cw, cb, layer):
    S = x.shape[0]
    tm = _row_tile(S, 512)

    def body(x_ref, g_ref, wup_hbm, wdn_hbm, cw_ref, cb_ref, o_ref, u_ref, uc_ref, wup_v, wdn_v, carry, act, stage):
        i = pl.program_id(0)

        @pl.when(i == 0)
        def _():
            pltpu.sync_copy(wup_hbm.at[0], wup_v)
            pltpu.sync_copy(wdn_hbm.at[0], wdn_v)
            carry[...] = jnp.zeros_like(carry)

        x = x_ref[...]
        h, _, _ = _rms_fwd(x, g_ref[...])
        hb = h.astype(MX)
        for j in range(F // TN):
            cg = slice(j * TN, (j + 1) * TN)
            cv = slice(F + j * TN, F + (j + 1) * TN)
            ug = _dot(hb, wup_v[:, cg])
            uv = _dot(hb, wup_v[:, cv])
            u_ref[:, cg] = ug.astype(u_ref.dtype)
            u_ref[:, cv] = uv.astype(u_ref.dtype)
            gt = _conv(stage.at[2 * (j % 2)], ug, carry[:, cg], cw_ref[:, cg])
            vl = _conv(stage.at[2 * (j % 2) + 1], uv, carry[:, cv], cw_ref[:, cv])
            carry[:, cg] = ug[tm - SUBLANES:]
            carry[:, cv] = uv[tm - SUBLANES:]
            gt = gt + cb_ref[:, cg]
            vl = vl + cb_ref[:, cv]
            uc_ref[:, cg] = gt.astype(uc_ref.dtype)
            uc_ref[:, cv] = vl.astype(uc_ref.dtype)
            act[:, cg] = (gt * jax.nn.sigmoid(gt) * vl).astype(act.dtype)
        o_ref[...] = x + _dot(act[...], wdn_v[...])

    return pl.pallas_call(
        body, name="ffn_fwd",
        grid=(S // tm,),
        in_specs=[
            pl.BlockSpec((tm, D), lambda i: (i, 0)),
            pl.BlockSpec((None, 1, D), lambda i: (layer, 0, 0)),
            pl.BlockSpec(memory_space=pl.ANY),
            pl.BlockSpec(memory_space=pl.ANY),
            pl.BlockSpec((None, 3, F2), lambda i: (layer, 0, 0)),
            pl.BlockSpec((None, 1, F2), lambda i: (layer, 0, 0)),
        ],
        out_specs=[
            pl.BlockSpec((tm, D), lambda i: (i, 0)),
            pl.BlockSpec((tm, F2), lambda i: (i, 0)),
            pl.BlockSpec((tm, F2), lambda i: (i, 0)),
        ],
        out_shape=[
            jax.ShapeDtypeStruct((S, D), F32),
            jax.ShapeDtypeStruct((S, F2), MX),
            jax.ShapeDtypeStruct((S, F2), MX),
        ],
        scratch_shapes=[
            pltpu.VMEM((D, F2), MX),
            pltpu.VMEM((F, D), MX),
            pltpu.VMEM((SUBLANES, F2), F32),
            pltpu.VMEM((tm, F), MX),
            pltpu.VMEM((N_STAGE, TN // LANES, tm + SUBLANES, LANES), F32),
        ],
        compiler_params=_cp(vmem=FFN_VMEM_LIMIT),
    )(x, g, wup, wdn, cw, cb)


def _ffn_bwd(x, dy, u, uc, g, wup, wdn, cw, layer):
    S = x.shape[0]
    tm = _row_tile(S, 256)
    n_i = S // tm

    def body(x_ref, dy_ref, u_ref, uc_ref, g_ref, wup_hbm, wdn_hbm, cw_ref,
             dx_ref, du_ref, a_ref, h_ref, dg_ref, dcw_ref, dcb_ref, wup_v, wdn_v, carry, stage):
        i = pl.program_id(0)

        @pl.when(i == 0)
        def _():
            pltpu.sync_copy(wup_hbm.at[0], wup_v)
            pltpu.sync_copy(wdn_hbm.at[0], wdn_v)
            carry[...] = jnp.zeros_like(carry)
            dg_ref[...] = jnp.zeros_like(dg_ref)
            dcw_ref[...] = jnp.zeros_like(dcw_ref)
            dcb_ref[...] = jnp.zeros_like(dcb_ref)

        x = x_ref[...]
        gg = g_ref[...]
        h, xh, r = _rms_fwd(x, gg)
        h_ref[...] = h.T.astype(h_ref.dtype)
        dy = dy_ref[...]
        dyb = dy.astype(MX)
        dh, dus = None, ([], [])
        for j in range(F // TN):
            cg = slice(j * TN, (j + 1) * TN)
            cv = slice(F + j * TN, F + (j + 1) * TN)
            gt = uc_ref[:, cg].astype(F32)
            vl = uc_ref[:, cv].astype(F32)
            sg = jax.nn.sigmoid(gt)
            sil = gt * sg
            a_ref[cg, :] = (sil * vl).T.astype(a_ref.dtype)
            da = _dot_nt(dyb, wdn_v[cg, :])
            dvl = da * sil
            dgt = (da * vl) * (sg + sil * (1.0 - sg))
            for cc, dd in ((cg, dgt), (cv, dvl)):
                dp1, dp2 = _rows_after(stage.at[2 * (j % 2) + (cc is cv)], dd, carry[:, cc])
                carry[:, cc] = dd[0:SUBLANES]
                uu = u_ref[:, cc].astype(F32)
                dcb_ref[:, cc] += jnp.sum(dd, axis=0, keepdims=True)
                dcw_ref[0:1, cc] += jnp.sum(dp2 * uu, axis=0, keepdims=True)
                dcw_ref[1:2, cc] += jnp.sum(dp1 * uu, axis=0, keepdims=True)
                dcw_ref[2:3, cc] += jnp.sum(dd * uu, axis=0, keepdims=True)
                cwc = cw_ref[:, cc]
                duu = (cwc[2:3] * dd + cwc[1:2] * dp1 + cwc[0:1] * dp2).astype(MX)
                du_ref[:, cc] = duu
                dus[cc is cv].append(duu)
            if len(dus[0]) == UP_GROUP or j == F // TN - 1:
                first = j + 1 - len(dus[0])
                for side, base in ((0, 0), (1, F)):
                    cols = slice(base + first * TN, base + (j + 1) * TN)
                    part = _dot_nt(jnp.concatenate(dus[side], axis=1), wup_v[:, cols])
                    dh = part if dh is None else dh + part
                dus = ([], [])
        dg_ref[...] += jnp.sum(dh * xh, axis=0, keepdims=True)
        dx_ref[...] = dy + _rms_bwd(dh, xh, r, gg)

    rev = lambda i: (n_i - 1 - i, 0)
    return pl.pallas_call(
        body, name="ffn_bwd",
        grid=(n_i,),
        in_specs=[
            pl.BlockSpec((tm, D), rev),
            pl.BlockSpec((tm, D), rev),
            pl.BlockSpec((tm, F2), rev),
            pl.BlockSpec((tm, F2), rev),
            pl.BlockSpec((None, 1, D), lambda i: (layer, 0, 0)),
            pl.BlockSpec(memory_space=pl.ANY),
            pl.BlockSpec(memory_space=pl.ANY),
            pl.BlockSpec((None, 3, F2), lambda i: (layer, 0, 0)),
        ],
        out_specs=[
            pl.BlockSpec((tm, D), rev),
            pl.BlockSpec((tm, F2), rev),
            pl.BlockSpec((F, tm), lambda i: (0, n_i - 1 - i)),
            pl.BlockSpec((D, tm), lambda i: (0, n_i - 1 - i)),
            pl.BlockSpec((1, D), lambda i: (0, 0)),
            pl.BlockSpec((3, F2), lambda i: (0, 0)),
            pl.BlockSpec((1, F2), lambda i: (0, 0)),
        ],
        out_shape=[
            jax.ShapeDtypeStruct((S, D), F32),
            jax.ShapeDtypeStruct((S, F2), MX),
            jax.ShapeDtypeStruct((F, S), MX),
            jax.ShapeDtypeStruct((D, S), MX),
            jax.ShapeDtypeStruct((1, D), F32),
            jax.ShapeDtypeStruct((3, F2), F32),
            jax.ShapeDtypeStruct((1, F2), F32),
        ],
        scratch_shapes=[
            pltpu.VMEM((D, F2), MX),
            pltpu.VMEM((F, D), MX),
            pltpu.VMEM((SUBLANES, F2), F32),
            pltpu.VMEM((N_STAGE, TN // LANES, tm + SUBLANES, LANES), F32),
        ],
        compiler_params=_cp(vmem=FFN_VMEM_LIMIT),
    )(x, dy, u, uc, g, wup, wdn, cw)


def _tn_matmul(a, b, out_dtype, name, a_is_transposed=False):
    S, N = b.shape
    M = a.shape[0] if a_is_transposed else a.shape[1]
    bn = N
    while M * bn * 4 > ACC_BYTES and bn % (2 * LANES) == 0:
        bn //= 2
    bk = _row_tile(S, 2048)
    nk = S // bk
    a_spec = pl.BlockSpec((M, bk), lambda j, k: (0, k)) if a_is_transposed else pl.BlockSpec((bk, M), lambda j, k: (k, 0))

    def body(a_ref, b_ref, o_ref, acc):
        k = pl.program_id(1)

        @pl.when(k == 0)
        def _():
            acc[...] = jnp.zeros_like(acc)

        if a_is_transposed:
            acc[...] += _dot(a_ref[...].astype(MX), b_ref[...].astype(MX))
        else:
            acc[...] += _dot_tn(a_ref[...].astype(MX), b_ref[...].astype(MX))

        @pl.when(k == nk - 1)
        def _():
            o_ref[...] = acc[...].astype(o_ref.dtype)

    return pl.pallas_call(
        body, name=name,
        grid=(N // bn, nk),
        in_specs=[
            a_spec,
            pl.BlockSpec((bk, bn), lambda j, k: (k, j)),
        ],
        out_specs=pl.BlockSpec((M, bn), lambda j, k: (0, j)),
        out_shape=jax.ShapeDtypeStruct((M, N), out_dtype),
        scratch_shapes=[pltpu.VMEM((M, bn), F32)],
        compiler_params=_cp(2),
    )(a, b)


def _rms_linear(x, g, w, b, g_layer, b_layer, name):
    S = x.shape[0]
    N = w.shape[-1]
    tm = _row_tile(S, TM_STREAM)

    def body(x_ref, g_ref, w_ref, b_ref, o_ref):
        h, _, _ = _rms_fwd(x_ref[...], g_ref[...])
        o_ref[...] = (_dot(h.astype(MX), w_ref[...]) + b_ref[...]).astype(o_ref.dtype)

    return pl.pallas_call(
        body, name=name,
        grid=(S // tm,),
        in_specs=[
            pl.BlockSpec((tm, D), lambda i: (i, 0)),
            pl.BlockSpec((None, 1, D), lambda i: (g_layer, 0, 0)),
            pl.BlockSpec((None, D, N), lambda i: (0, 0, 0)),
            pl.BlockSpec((None, 1, N), lambda i: (b_layer, 0, 0)),
        ],
        out_specs=pl.BlockSpec((tm, N), lambda i: (i, 0)),
        out_shape=jax.ShapeDtypeStruct((S, N), MX),
        compiler_params=_cp(),
    )(x, g, w, b)


def _linear_res(o, w, b, xres, layer):
    S = o.shape[0]
    tm = _row_tile(S, TM_STREAM)

    def body(o_ref, w_ref, b_ref, x_ref, y_ref):
        y_ref[...] = x_ref[...] + _dot(o_ref[...], w_ref[...]) + b_ref[...]

    return pl.pallas_call(
        body, name="o_proj",
        grid=(S // tm,),
        in_specs=[
            pl.BlockSpec((tm, D), lambda i: (i, 0)),
            pl.BlockSpec((None, D, D), lambda i: (0, 0, 0)),
            pl.BlockSpec((None, 1, D), lambda i: (layer, 0, 0)),
            pl.BlockSpec((tm, D), lambda i: (i, 0)),
        ],
        out_specs=pl.BlockSpec((tm, D), lambda i: (i, 0)),
        out_shape=jax.ShapeDtypeStruct((S, D), F32),
        compiler_params=_cp(),
    )(o, w, b, xres)


def _linear_nt(dy, w):
    S = dy.shape[0]
    tm = _row_tile(S, TM_STREAM)

    def body(dy_ref, w_ref, o_ref, db_ref):
        @pl.when(pl.program_id(0) == 0)
        def _():
            db_ref[...] = jnp.zeros_like(db_ref)

        dy = dy_ref[...]
        db_ref[...] += jnp.sum(dy, axis=0, keepdims=True)
        o_ref[...] = _dot_nt(dy.astype(MX), w_ref[...]).astype(o_ref.dtype)

    return pl.pallas_call(
        body, name="o_proj_bwd",
        grid=(S // tm,),
        in_specs=[
            pl.BlockSpec((tm, D), lambda i: (i, 0)),
            pl.BlockSpec((None, D, D), lambda i: (0, 0, 0)),
        ],
        out_specs=[
            pl.BlockSpec((tm, D), lambda i: (i, 0)),
            pl.BlockSpec((1, D), lambda i: (0, 0)),
        ],
        out_shape=[
            jax.ShapeDtypeStruct((S, D), MX),
            jax.ShapeDtypeStruct((1, D), F32),
        ],
        compiler_params=_cp(),
    )(dy, w)


def _rms_linear_bwd(x, g, dzs, w, dy, g_layer, name):
    S = x.shape[0]
    N = w.shape[-1]
    tm = _row_tile(S, TM_STREAM)
    nz = len(dzs)

    def body(*refs):
        x_ref, g_ref = refs[0], refs[1]
        dz_refs = refs[2:2 + nz]
        w_ref, dy_ref, dx_ref, dg_ref, db_ref, h_ref, dzb_ref = refs[2 + nz:]

        @pl.when(pl.program_id(0) == 0)
        def _():
            dg_ref[...] = jnp.zeros_like(dg_ref)
            db_ref[...] = jnp.zeros_like(db_ref)

        gg = g_ref[...]
        h, xh, r = _rms_fwd(x_ref[...], gg)
        h_ref[...] = h.astype(h_ref.dtype)
        dz = dz_refs[0][...].astype(F32)
        for zr in dz_refs[1:]:
            dz = dz + zr[...].astype(F32)
        db_ref[...] += jnp.sum(dz, axis=0, keepdims=True)
        dzb = dz.astype(MX)
        dzb_ref[...] = dzb
        dh = _dot_nt(dzb, w_ref[...])
        dg_ref[...] += jnp.sum(dh * xh, axis=0, keepdims=True)
        dx_ref[...] = dy_ref[...] + _rms_bwd(dh, xh, r, gg)

    return pl.pallas_call(
        body, name=name,
        grid=(S // tm,),
        in_specs=[
            pl.BlockSpec((tm, D), lambda i: (i, 0)),
            pl.BlockSpec((None, 1, D), lambda i: (g_layer, 0, 0)),
        ] + [pl.BlockSpec((tm, N), lambda i: (i, 0))] * nz + [
            pl.BlockSpec((None, D, N), lambda i: (0, 0, 0)),
            pl.BlockSpec((tm, D), lambda i: (i, 0)),
        ],
        out_specs=[
            pl.BlockSpec((tm, D), lambda i: (i, 0)),
            pl.BlockSpec((1, D), lambda i: (0, 0)),
            pl.BlockSpec((1, N), lambda i: (0, 0)),
            pl.BlockSpec((tm, D), lambda i: (i, 0)),
            pl.BlockSpec((tm, N), lambda i: (i, 0)),
        ],
        out_shape=[
            jax.ShapeDtypeStruct((S, D), F32),
            jax.ShapeDtypeStruct((1, D), F32),
            jax.ShapeDtypeStruct((1, N), F32),
            jax.ShapeDtypeStruct((S, D), MX),
            jax.ShapeDtypeStruct((S, N), MX),
        ],
        compiler_params=_cp(),
    )(x, g, *dzs, w, dy)


def _loss_head(x, g, tgt):
    S = x.shape[0]
    tm = _row_tile(S, TM_STREAM)

    def body(x_ref, g_ref, t_ref, dx_ref, dg_ref, l_ref):
        @pl.when(pl.program_id(0) == 0)
        def _():
            dg_ref[...] = jnp.zeros_like(dg_ref)
            l_ref[...] = jnp.zeros_like(l_ref)

        gg = g_ref[...]
        y, xh, r = _rms_fwd(x_ref[...], gg)
        err = y - t_ref[...]
        tok = jnp.sum(err * err, axis=-1, keepdims=True) * (1.0 / D)
        l_ref[...] += 0.5 * jnp.sum(tok, axis=0, keepdims=True)
        dyv = err * (1.0 / D)
        dg_ref[...] += jnp.sum(dyv * xh, axis=0, keepdims=True)
        dx_ref[...] = _rms_bwd(dyv, xh, r, gg)

    return pl.pallas_call(
        body, name="loss_head",
        grid=(S // tm,),
        in_specs=[
            pl.BlockSpec((tm, D), lambda i: (i, 0)),
            pl.BlockSpec((1, D), lambda i: (0, 0)),
            pl.BlockSpec((tm, D), lambda i: (i, 0)),
        ],
        out_specs=[
            pl.BlockSpec((tm, D), lambda i: (i, 0)),
            pl.BlockSpec((1, D), lambda i: (0, 0)),
            pl.BlockSpec((1, LANES), lambda i: (0, 0)),
        ],
        out_shape=[
            jax.ShapeDtypeStruct((S, D), F32),
            jax.ShapeDtypeStruct((1, D), F32),
            jax.ShapeDtypeStruct((1, LANES), F32),
        ],
        compiler_params=_cp(),
    )(x, g, tgt)


HPG = NH // 2
QH = BLK // 2
KW = BLK + QH
COLS = HPG * QH


def _attn_setup(kvp_ref, kvc_ref):
    kw = jnp.concatenate([kvp_ref[...], kvc_ref[...]], axis=0).astype(F32)
    kk, vv = kw[:, :LANES], kw[:, LANES:]
    lo = lax.broadcasted_iota(jnp.int32, (1, LANES), 1) < HD
    kr, vr = pltpu.roll(kk, HD, 1), pltpu.roll(vv, HD, 1)
    ks = [jnp.where(lo, kk, kr).astype(MX), jnp.where(lo, kr, kk).astype(MX)]
    vs = [jnp.where(lo, vv, vr).astype(MX), jnp.where(lo, vr, vv).astype(MX)]
    return ks, vs, lo


def _stack_heads(ref, grp, lo, rows):
    parts = []
    for j in range(4 * grp, 4 * grp + 4):
        slab = ref[rows, j * LANES:(j + 1) * LANES]
        zero = jnp.zeros_like(slab)
        parts += [jnp.where(lo, slab, zero), jnp.where(lo, zero, slab)]
    return jnp.concatenate(parts, axis=0)


def _unstack_heads(st, lo):
    nq = st.shape[0] // HPG
    return [jnp.where(lo, st[2 * i * nq:(2 * i + 1) * nq], st[(2 * i + 1) * nq:(2 * i + 2) * nq])
            for i in range(4)]


def _attn_probs(qs, kg, n, sk_ref, layer, grp):
    rows = HPG * BLK
    qi = lax.broadcasted_iota(jnp.int32, (rows, 2 * BLK), 0) & (BLK - 1)
    si = lax.broadcasted_iota(jnp.int32, (rows, 2 * BLK), 1)
    ok = (si > qi) & (si <= qi + BLK) & jnp.logical_or(n > 0, si >= BLK)
    head = lax.broadcasted_iota(jnp.int32, (rows, 1), 0) // BLK
    sink = jnp.zeros((rows, 1), F32)
    for h in range(HPG):
        sink = jnp.where(head == h, sk_ref[layer, HPG * grp + h], sink)
    s = jnp.where(ok, _dot_nt(qs, kg) * SCALE, NEG)
    m = jnp.maximum(jnp.max(s, axis=-1, keepdims=True), sink)
    p = jnp.exp(s - m)
    return p * (1.0 / (jnp.sum(p, axis=-1, keepdims=True) + jnp.exp(sink - m)))


def _attn_mask_t(koff, n):
    si = lax.broadcasted_iota(jnp.int32, (KW, COLS), 0)
    qi = lax.broadcasted_iota(jnp.int32, (KW, COLS), 1) & (QH - 1)
    return (si > qi) & (si <= qi + BLK) & jnp.logical_or(n > 0, si >= BLK - koff)


def _attn_probs_t(qs, kg, ok, sk_ref, layer, grp):
    s = jnp.where(ok, _dot_nt(kg, qs) * SCALE, NEG)
    head = lax.broadcasted_iota(jnp.int32, (1, COLS), 1) // QH
    sink = jnp.zeros((1, COLS), F32)
    for h in range(HPG):
        sink = jnp.where(head == h, sk_ref[layer, HPG * grp + h], sink)
    m = jnp.maximum(jnp.max(s, axis=0, keepdims=True), sink)
    p = jnp.exp(s - m)
    es = jnp.exp(sink - m)
    inv = 1.0 / (jnp.sum(p, axis=0, keepdims=True) + es)
    return p * inv, es * inv, head


def _attn_specs(n_extra_q):
    q_spec = pl.BlockSpec((BLK, D), lambda n: (n, 0))
    return [q_spec] * n_extra_q + [
        pl.BlockSpec((BLK, 4 * HD), lambda n: (jnp.maximum(n - 1, 0), 0)),
        pl.BlockSpec((BLK, 4 * HD), lambda n: (n, 0)),
        pl.BlockSpec(memory_space=pltpu.SMEM),
    ]


def _attn_fwd(q, kv, sinks, layer):
    S = q.shape[0]

    def body(q_ref, kvp_ref, kvc_ref, sk_ref, o_ref):
        n = pl.program_id(0)
        ks, vs, lo = _attn_setup(kvp_ref, kvc_ref)
        for grp in range(2):
            qs = _stack_heads(q_ref, grp, lo, slice(None))
            pr = _attn_probs(qs, ks[grp], n, sk_ref, layer, grp)
            outs = _unstack_heads(_dot(pr.astype(MX), vs[grp]), lo)
            for i in range(4):
                j = 4 * grp + i
                o_ref[:, j * LANES:(j + 1) * LANES] = outs[i].astype(o_ref.dtype)

    return pl.pallas_call(
        body, name="attn_fwd",
        grid=(S // BLK,),
        in_specs=_attn_specs(1),
        out_specs=pl.BlockSpec((BLK, D), lambda n: (n, 0)),
        out_shape=jax.ShapeDtypeStruct((S, D), MX),
        compiler_params=_cp(),
    )(q, kv, kv, sinks)


def _attn_bwd(q, do, kv, sinks, layer):
    S = q.shape[0]

    def body(q_ref, do_ref, kvp_ref, kvc_ref, sk_ref, dq_ref, dkv_ref, dsk_ref):
        n = pl.program_id(0)

        @pl.when(n == 0)
        def _():
            dkv_ref[...] = jnp.zeros_like(dkv_ref)
            dsk_ref[...] = jnp.zeros_like(dsk_ref)

        ks, vs, lo = _attn_setup(kvp_ref, kvc_ref)
        lane = lax.broadcasted_iota(jnp.int32, (1, LANES), 1)
        dsk = jnp.zeros((1, LANES), F32)
        dk = [jnp.zeros((2 * BLK, LANES), F32) for _ in range(2)]
        dv = [jnp.zeros((2 * BLK, LANES), F32) for _ in range(2)]
        for half in range(2):
            rows = slice(half * QH, (half + 1) * QH)
            koff = half * QH
            ok = _attn_mask_t(koff, n)
            above = [jnp.zeros((koff, LANES), F32)] if koff else []
            below = [jnp.zeros((2 * BLK - KW - koff, LANES), F32)] if 2 * BLK - KW - koff else []
            for grp in range(2):
                qs = _stack_heads(q_ref, grp, lo, rows)
                dos = _stack_heads(do_ref, grp, lo, rows)
                kg, vg = ks[grp][koff:koff + KW], vs[grp][koff:koff + KW]
                pr, psink, head = _attn_probs_t(qs, kg, ok, sk_ref, layer, grp)
                dpr = _dot_nt(vg, dos)
                delta = jnp.sum(pr * dpr, axis=0, keepdims=True)
                ds = (pr * (dpr - delta) * SCALE).astype(MX)
                sd = psink * delta
                for h in range(HPG):
                    dsk = dsk + jnp.where(lane == HPG * grp + h,
                                          -jnp.sum(jnp.where(head == h, sd, 0.0), axis=1, keepdims=True), 0.0)
                dqs = _unstack_heads(_dot_tn(kg, ds).T, lo)
                for i in range(4):
                    j = 4 * grp + i
                    dq_ref[rows, j * LANES:(j + 1) * LANES] = dqs[i].astype(dq_ref.dtype)
                dk[grp] = dk[grp] + jnp.concatenate(above + [_dot(ds, qs)] + below, axis=0)
                dv[grp] = dv[grp] + jnp.concatenate(above + [_dot(pr.astype(MX), dos)] + below, axis=0)
        dsk_ref[...] += dsk
        tk = [a + pltpu.roll(a, HD, 1) for a in dk]
        tv = [a + pltpu.roll(a, HD, 1) for a in dv]
        contrib = jnp.concatenate([jnp.where(lo, tk[0], tk[1]), jnp.where(lo, tv[0], tv[1])], axis=1)

        @pl.when(n > 0)
        def _():
            rows = pl.ds(pl.multiple_of((n - 1) * BLK, BLK), 2 * BLK)
            dkv_ref[rows, :] += contrib

        @pl.when(n == 0)
        def _():
            dkv_ref[0:BLK, :] += contrib[BLK:]

    return pl.pallas_call(
        body, name="attn_bwd",
        grid=(S // BLK,),
        in_specs=_attn_specs(2),
        out_specs=[
            pl.BlockSpec((BLK, D), lambda n: (n, 0)),
            pl.BlockSpec((S, 4 * HD), lambda n: (0, 0)),
            pl.BlockSpec((1, LANES), lambda n: (0, 0)),
        ],
        out_shape=[
            jax.ShapeDtypeStruct((S, D), MX),
            jax.ShapeDtypeStruct((S, 4 * HD), F32),
            jax.ShapeDtypeStruct((1, LANES), F32),
        ],
        compiler_params=_cp(),
    )(q, do, kv, kv, sinks)


def _ew_rows(rows, cols, n_bufs=1):
    br = rows
    while br * cols * 4 * n_bufs > EW_BYTES and br % (2 * SUBLANES) == 0:
        br //= 2
    return br


def _adam_update(g, w, m, v):
    nm = B1 * m + (1.0 - B1) * g
    nv = B2 * v + (1.0 - B2) * (g * g)
    m_hat = nm / (1.0 - B1 ** STEP)
    v_hat = nv / (1.0 - B2 ** STEP)
    return -LR * (m_hat / (jnp.sqrt(v_hat) + AEPS) + WD * w), nm, nv


def _adamw_small(gs, ws, ms, vs):
    n = len(ws)

    def body(*refs):
        ins, outs = refs[:4 * n], refs[4 * n:]
        for i in range(n):
            g = ins[i][...]
            d, nm, nv = _adam_update(g, ins[n + i][...], ins[2 * n + i][...], ins[3 * n + i][...])
            outs[i][...] = g
            outs[n + i][...] = d
            outs[2 * n + i][...] = nm
            outs[3 * n + i][...] = nv

    vm = pl.BlockSpec(memory_space=pltpu.VMEM)
    res = pl.pallas_call(
        body, name="adamw_small",
        in_specs=[vm] * (4 * n),
        out_specs=[vm] * (4 * n),
        out_shape=[jax.ShapeDtypeStruct(w.shape, F32) for w in ws] * 4,
    )(*gs, *ws, *ms, *vs)
    return res[:n], res[n:2 * n], res[2 * n:3 * n], res[3 * n:]


def _adamw(parts, w, m, v):
    L, R, C = w.shape
    br = _ew_rows(R, C)
    npart = len(parts[0])

    def body(*refs):
        p_refs = refs[:L * npart]
        w_ref, m_ref, v_ref, g_ref, d_ref, nm_ref, nv_ref = refs[L * npart:]
        lyr = pl.program_id(0)
        for l in range(L):
            @pl.when(lyr == l)
            def _(l=l):
                g = p_refs[l * npart][...]
                for pr in p_refs[l * npart + 1:(l + 1) * npart]:
                    g = g + pr[...]
                g_ref[...] = g
                d_ref[...], nm_ref[...], nv_ref[...] = _adam_update(g, w_ref[...], m_ref[...], v_ref[...])

    spec = pl.BlockSpec((None, br, C), lambda a, i: (a, i, 0))
    part_specs = [pl.BlockSpec((br, C), lambda a, i, l=l: (jnp.where(a == l, i, 0), 0))
                  for l in range(L) for _ in range(npart)]
    return pl.pallas_call(
        body, name="adamw",
        grid=(L, R // br),
        in_specs=part_specs + [spec] * 3,
        out_specs=[spec] * 4,
        out_shape=[jax.ShapeDtypeStruct((L, R, C), F32)] * 4,
        compiler_params=_cp(2),
    )(*[a for lp in parts for a in lp], w, m, v)


def _coords():
    return lax.axis_index("x"), lax.axis_index("y"), lax.axis_index("c")


def _other_chips(x, y):
    return [(1 - x, y), (x, 1 - y), (1 - x, 1 - y)]


def _slot(ref, axis, chip, size):
    idx = [slice(None)] * 3
    idx[axis] = pl.ds(pl.multiple_of(chip * size, size), size)
    return ref.at[tuple(idx)]


HBM_SPEC = pl.BlockSpec(memory_space=pltpu.HBM)
SEM_SPEC = pl.BlockSpec(memory_space=pltpu.SEMAPHORE)
ANY_SPEC = pl.BlockSpec(memory_space=pl.ANY)
EFFECT = pltpu.SideEffectType.DATAFLOW_SIDE_EFFECTING


def _slot_specs(shape, axis, br, lead):
    _, b, c = shape
    nrb = b // br
    first = (lambda a: a) if lead is None else (lambda a: lead)
    shard = pl.BlockSpec((None, br, c), lambda a, i, me: (first(a), i, 0))
    if axis == 1:
        slot = pl.BlockSpec((None, br, c), lambda a, i, me: (a, me[0] * nrb + i, 0))
    else:
        slot = pl.BlockSpec((None, br, c), lambda a, i, me: (a, i, me[0]))
    return shard, slot


def _shard_rows(b, c):
    br = b
    while br * c * 4 > 2 * EW_BYTES and br % (4 * SUBLANES) == 0:
        br //= 2
    return br


def _gather_place(shard, axis, me, dtype, lead=None):
    a_dim, b, c = shard.shape
    if lead is not None:
        a_dim = 1
    br = _shard_rows(b, c)
    shp = [a_dim, b, c]
    shp[axis] *= 4
    shard_spec, slot_spec = _slot_specs((a_dim, b, c), axis, br, lead)

    def body(me_ref, s_ref, o_ref):
        o_ref[...] = s_ref[...].astype(o_ref.dtype)

    return pl.pallas_call(
        body, name="gather_place",
        grid_spec=pltpu.PrefetchScalarGridSpec(
            num_scalar_prefetch=1, grid=(a_dim, b // br), in_specs=[shard_spec], out_specs=slot_spec),
        out_shape=jax.ShapeDtypeStruct(tuple(shp), dtype),
        compiler_params=_cp(2),
    )(me, shard)


def _sum_landed(grad, landed, axis, me):
    a_dim, b, c = landed.shape[1:]
    br = _shard_rows(b, c)
    shard_spec, slot_spec = _slot_specs((a_dim, b, c), axis, br, None)

    def body(me_ref, own_ref, r_ref, o_ref):
        o_ref[...] = ((own_ref[...].astype(F32) + r_ref[0].astype(F32)) + r_ref[1].astype(F32)) + r_ref[2].astype(F32)

    return pl.pallas_call(
        body, name="sum_landed",
        grid_spec=pltpu.PrefetchScalarGridSpec(
            num_scalar_prefetch=1, grid=(a_dim, b // br),
            in_specs=[slot_spec, pl.BlockSpec((3, None, br, c), lambda a, i, me: (0, a, i, 0))],
            out_specs=shard_spec),
        out_shape=jax.ShapeDtypeStruct((a_dim, b, c), F32),
        compiler_params=_cp(2),
    )(me, grad, landed)


def _copies(refs, plan, send, recv, to_sibling):
    x, y, c = _coords()
    me = 2 * x + y
    if to_sibling == "everyone":
        me = 4 * x + 2 * y + c
        flips = [(k >> 2, (k >> 1) & 1, k & 1) for k in range(1, 8)]
        targets = [((x ^ fx, y ^ fy, c ^ fc), 4 * (x ^ fx) + 2 * (y ^ fy) + (c ^ fc)) for fx, fy, fc in flips]
    elif to_sibling:
        targets = [((x, y, 1 - c), me)]
    else:
        targets = [((px, py, c), 2 * px + py) for px, py in _other_chips(x, y)]
    out, t = [], 0
    while plan(refs, me, t, 0, me) is not None:
        for k, (device, peer) in enumerate(targets):
            sv, dv = plan(refs, me, t, k, peer)
            n = len(targets) * t + k
            out.append(pltpu.make_async_remote_copy(
                src_ref=sv, dst_ref=dv, send_sem=send.at[n], recv_sem=recv.at[n],
                device_id=device, device_id_type=MESH))
        t += 1
    return out


def _push_start(name, bufs, n_copies, plan, to_sibling=False):
    nb = len(bufs)

    def body(*refs):
        send, recv, token = refs[nb], refs[nb + 1], refs[-1]
        for cp in _copies(refs[:nb], plan, send, recv, to_sibling):
            cp.start()
        token[...] = jnp.zeros_like(token)

    res = pl.pallas_call(
        body, name=name,
        in_specs=[HBM_SPEC] * nb,
        out_specs=[SEM_SPEC, SEM_SPEC] + [HBM_SPEC] * nb + [pl.BlockSpec(memory_space=pltpu.VMEM)],
        out_shape=[pltpu.SemaphoreType.DMA((n_copies,)), pltpu.SemaphoreType.DMA((n_copies,))]
        + [pltpu.HBM(a.shape, a.dtype) for a in bufs] + [jax.ShapeDtypeStruct((SUBLANES, LANES), F32)],
        input_output_aliases={i: 2 + i for i in range(nb)},
        compiler_params=pltpu.CompilerParams(has_side_effects=EFFECT),
    )(*[pltpu.with_memory_space_constraint(a, pltpu.HBM) for a in bufs])
    return res[0], res[1], res[2:2 + nb], res[-1]


def _push_wait(name, send, recv, bufs, plan, after, to_sibling=False):
    nb = len(bufs)

    def body(*refs):
        for cp in _copies(refs[:nb], plan, refs[nb], refs[nb + 1], to_sibling):
            cp.wait_send()
            cp.wait_recv()

    return pl.pallas_call(
        body, name=name,
        in_specs=[HBM_SPEC] * nb + [SEM_SPEC, SEM_SPEC, ANY_SPEC],
        out_specs=[HBM_SPEC] * nb,
        out_shape=[pltpu.HBM(a.shape, a.dtype) for a in bufs],
        input_output_aliases={i: i for i in range(nb)},
        compiler_params=pltpu.CompilerParams(has_side_effects=EFFECT),
    )(*bufs, send, recv, after)


def _gather_plan(axes):
    def plan(refs, me, t, k, peer):
        if t >= len(axes):
            return None
        size = refs[t].shape[axes[t]] // 4
        mine = _slot(refs[t], axes[t], me, size)
        return mine, mine
    return plan


def _half_slot(ref, axis, chip):
    c = lax.axis_index("c")
    if axis == 1:
        half = ref.shape[1] // 8
        return ref.at[:, pl.ds(pl.multiple_of(chip * 2 * half + c * half, 2 * SUBLANES), half), :]
    half = ref.shape[1] // 2
    size = ref.shape[2] // 4
    return ref.at[:, pl.ds(pl.multiple_of(c * half, 2 * SUBLANES), half), pl.ds(pl.multiple_of(chip * size, LANES), size)]


def _gather_half_plan(axes):
    def plan(refs, me, t, k, peer):
        if t >= len(axes):
            return None
        mine = _half_slot(refs[t], axes[t], me)
        return mine, mine
    return plan


def _gather_pass_plan(axes):
    def plan(refs, me, t, k, peer):
        if t >= 3 * len(axes):
            return None
        x, y, _ = _coords()
        px, py = _other_chips(x, y)[t % 3]
        landed = _half_slot(refs[t // 3], axes[t // 3], 2 * px + py)
        return landed, landed
    return plan


def _scatter_plan(axes):
    n = len(axes)

    def plan(refs, me, t, k, peer):
        if t >= n:
            return None
        size = refs[t].shape[axes[t]] // 4
        return _slot(refs[t], axes[t], peer, size), refs[n + t].at[k]
    return plan


def _swap_plan(n):
    def plan(refs, me, t, k, peer):
        if t >= n:
            return None
        return refs[t], refs[n + t]
    return plan


def _everyone_plan():
    def plan(refs, me, t, k, peer):
        if t >= 1:
            return None
        mine = refs[0].at[me]
        return mine, mine
    return plan


def _sum_slots(slots):
    R = slots.shape[1]

    def body(s_ref, o_ref):
        tot = s_ref[0]
        for k in range(1, 8):
            tot = tot + s_ref[k]
        o_ref[...] = tot

    vm = pl.BlockSpec(memory_space=pltpu.VMEM)
    return pl.pallas_call(
        body, name="sum_slots",
        in_specs=[vm],
        out_specs=vm,
        out_shape=jax.ShapeDtypeStruct((R, LANES), F32),
        compiler_params=pltpu.CompilerParams(vmem_limit_bytes=VMEM_LIMIT),
    )(slots)


def _pack(arrs):
    flat = []
    for a in arrs:
        f = a.reshape(-1).astype(F32)
        flat.append(jnp.pad(f, (0, (-f.shape[0]) % LANES)))
    v = jnp.concatenate(flat)
    v = jnp.pad(v, (0, (-v.shape[0]) % (SUBLANES * LANES)))
    return v.reshape(-1, LANES)


def _unpack(v, shapes):
    flat = v.reshape(-1)
    out, off = [], 0
    for shp in shapes:
        n = 1
        for d in shp:
            n *= d
        out.append(flat[off:off + n].reshape(shp))
        off += n + (-n) % LANES
    return out


def _local_step(x, tgt, sp, weights_for, on_grads):
    n1, n2 = sp["norm1_g"], sp["norm2_g"]
    w = dict(weights_for(0, x))
    saved = []
    xs = x
    kv = None
    for l in range(DEPTH):
        x_in = xs
        if l >= N_A:
            w.update(weights_for(1 + 2 * l - N_A, x_in))
        if l == N_A:
            kv = _rms_linear(x_in, sp["kv_norm_g"], w["w_kv"], sp["b_kv"], 0, 0, "kv_proj")
        if l < N_A:
            xa = _pool_fwd(x_in, n1, w["pool_w"], w["pool_scale"], l)
            q = o = None
        else:
            j = l - N_A
            q = _rms_linear(x_in, n1, w["w_q", j], sp["b_q"], l, j, "q_proj")
            o = _attn_fwd(q, kv, sp["sinks"], j)
            xa = _linear_res(o, w["w_o", j], sp["b_o"], x_in, j)
        w.update(weights_for(1 + l if l < N_A else 2 + 2 * l - N_A, xa))
        xs, u, uc = _ffn_fwd(xa, n2, w["ffn_up", l], w["ffn_down", l], w["ffn_conv_w"], sp["ffn_conv_b"], l)
        saved.append((x_in, xa, u, uc, q, o))

    dx, d_final_g, loss = _loss_head(xs, sp["final_g"], tgt)

    g = {k: [None] * DEPTH for k in ("norm1_g", "norm2_g", "ffn_conv_w", "ffn_conv_b")}
    for k in ("pool_scale", "b_q", "sinks", "b_o"):
        g[k] = [None] * N_A
    g["final_g"] = d_final_g
    dkvs = []
    pending = {}
    for l in reversed(range(DEPTH)):
        x_in, xa, u, uc, q, o = saved[l]
        dxa, du, a, hb, g["norm2_g"][l], g["ffn_conv_w"][l], g["ffn_conv_b"][l] = _ffn_bwd(
            xa, dx, u, uc, n2, w["ffn_up", l], w["ffn_down", l], w["ffn_conv_w"], l)
        pending["ffn_up", l] = _tn_matmul(hb, du, MX, "d_ffn_up", a_is_transposed=True)
        if l == 0:
            n1 = n1 + on_grads(DEPTH + 1, pending)
            pending = {}
        pending["ffn_down", l] = _tn_matmul(a, dx, MX, "d_ffn_down", a_is_transposed=True)
        zero = on_grads(DEPTH - 1 - l, pending)
        pending = {}
        n1, n2 = n1 + zero, n2 + zero
        if l < N_A:
            dx, d_pw, g["pool_scale"][l], g["norm1_g"][l] = _pool_bwd(
                x_in, dxa, n1, w["pool_w"], w["pool_scale"], l)
            pending["pool_w", l] = d_pw.astype(MX)
        else:
            j = l - N_A
            d_o, g["b_o"][j] = _linear_nt(dxa, w["w_o", j])
            pending["w_o", j] = _tn_matmul(o, dxa, MX, "d_w_o")
            dq, dkv, g["sinks"][j] = _attn_bwd(q, d_o, kv, sp["sinks"], j)
            dkvs.append(dkv)
            dx, g["norm1_g"][l], g["b_q"][j], hq, dqb = _rms_linear_bwd(
                x_in, n1, [dq], w["w_q", j], dxa, l, "q_proj_bwd")
            pending["w_q", j] = _tn_matmul(hq, dqb, MX, "d_w_q")
        if l == N_A:
            dx, g["kv_norm_g"], g["b_kv"], hk, dkvb = _rms_linear_bwd(
                x_in, sp["kv_norm_g"], dkvs, w["w_kv"], dx, 0, "kv_proj_bwd")
            pending["w_kv", 0] = _tn_matmul(hk, dkvb, MX, "d_w_kv")
    on_grads(DEPTH, pending)
    return loss, dx, g


SMALL = ("norm1_g", "norm2_g", "kv_norm_g", "b_kv", "b_q", "sinks", "b_o", "ffn_conv_b", "final_g")
SMALL_SHARDED = ("pool_scale", "ffn_conv_w")
BIG = ("pool_w", "w_kv", "w_q", "w_o", "ffn_up", "ffn_down")
EARLY_SWAP = 3
ORDER = ("norm1_g", "norm2_g", "pool_w", "pool_scale", "kv_norm_g", "w_kv", "b_kv", "w_q", "b_q", "sinks",
         "w_o", "b_o", "ffn_up", "ffn_conv_w", "ffn_conv_b", "ffn_down", "final_g")


def _as3d(a):
    return a.reshape((-1,) + a.shape[-2:])


def kernel(x, norm1_g, norm2_g, pool_w, pool_scale, kv_norm_g, w_kv, b_kv, w_q, b_q, sinks, w_o, b_o, ffn_up, ffn_conv_w, ffn_conv_b, ffn_down, final_g, loss_target, m_norm1_g, m_norm2_g, m_pool_w, m_pool_scale, m_kv_norm_g, m_w_kv, m_b_kv, m_w_q, m_b_q, m_sinks, m_w_o, m_b_o, m_ffn_up, m_ffn_conv_w, m_ffn_conv_b, m_ffn_down, m_final_g, v_norm1_g, v_norm2_g, v_pool_w, v_pool_scale, v_kv_norm_g, v_w_kv, v_b_kv, v_w_q, v_b_q, v_sinks, v_w_o, v_b_o, v_ffn_up, v_ffn_conv_w, v_ffn_conv_b, v_ffn_down, v_final_g):
    W = dict(norm1_g=norm1_g, norm2_g=norm2_g, pool_w=pool_w, pool_scale=pool_scale, kv_norm_g=kv_norm_g,
             w_kv=w_kv, b_kv=b_kv, w_q=w_q, b_q=b_q, sinks=sinks, w_o=w_o, b_o=b_o, ffn_up=ffn_up,
             ffn_conv_w=ffn_conv_w, ffn_conv_b=ffn_conv_b, ffn_down=ffn_down, final_g=final_g)
    M = dict(norm1_g=m_norm1_g, norm2_g=m_norm2_g, pool_w=m_pool_w, pool_scale=m_pool_scale,
             kv_norm_g=m_kv_norm_g, w_kv=m_w_kv, b_kv=m_b_kv, w_q=m_w_q, b_q=m_b_q, sinks=m_sinks, w_o=m_w_o,
             b_o=m_b_o, ffn_up=m_ffn_up, ffn_conv_w=m_ffn_conv_w, ffn_conv_b=m_ffn_conv_b, ffn_down=m_ffn_down,
             final_g=m_final_g)
    V = dict(norm1_g=v_norm1_g, norm2_g=v_norm2_g, pool_w=v_pool_w, pool_scale=v_pool_scale,
             kv_norm_g=v_kv_norm_g, w_kv=v_w_kv, b_kv=v_b_kv, w_q=v_w_q, b_q=v_b_q, sinks=v_sinks, w_o=v_w_o,
             b_o=v_b_o, ffn_up=v_ffn_up, ffn_conv_w=v_ffn_conv_w, ffn_conv_b=v_ffn_conv_b, ffn_down=v_ffn_down,
             final_g=v_final_g)
    S = x.shape[1]
    chip = 2 * lax.axis_index("x") + lax.axis_index("y")

    gather_axis = dict(pool_w=1, w_kv=1, w_q=1, w_o=1, ffn_up=2, ffn_down=1, pool_scale=2, ffn_conv_w=2)
    me = chip.reshape(1).astype(jnp.int32)
    axis_of = lambda key: gather_axis[key if isinstance(key, str) else key[0]]

    def placed(key, dtype):
        if isinstance(key, str):
            return _gather_place(_as3d(W[key]), axis_of(key), me, dtype)
        return _gather_place(W[key[0]], axis_of(key), me, dtype, lead=key[1])

    stages = [
        ["pool_w", "pool_scale", "ffn_conv_w"],
        [("ffn_up", 0), ("ffn_down", 0)],
        [("ffn_up", 1), ("ffn_down", 1)],
        ["w_kv", ("w_q", 0), ("w_o", 0)],
        [("ffn_up", 2), ("ffn_down", 2)],
        [("w_q", 1), ("w_o", 1)],
        [("ffn_up", 3), ("ffn_down", 3)],
    ]
    TWO_LEVEL = 1
    gathers, zero = [], 0.0
    for si, keys in enumerate(stages):
        axes = [axis_of(k) for k in keys]
        bufs = [placed(k, F32 if k in SMALL_SHARDED else MX) for k in keys]
        plan = _gather_half_plan(axes) if si == TWO_LEVEL else _gather_plan(axes)
        send, recv, bufs, token = _push_start(f"gather_start_{si}", bufs, 3 * len(keys), plan)
        gathers.append((keys, axes, send, recv, bufs))
        zero = zero + token[0, 0]

    def weights_for(stage, after):
        keys, axes, send, recv, bufs = gathers[stage]
        if stage == TWO_LEVEL:
            bufs = _push_wait(f"gather_wait_{stage}", send, recv, bufs, _gather_half_plan(axes), after)
            send, recv, bufs, _ = _push_start("gather_pass_start", bufs, 3 * len(keys), _gather_pass_plan(axes), True)
            bufs = _push_wait("gather_pass_wait", send, recv, bufs, _gather_pass_plan(axes), after, True)
        else:
            bufs = _push_wait(f"gather_wait_{stage}", send, recv, bufs, _gather_plan(axes), after)
        out = dict(zip(keys, bufs))
        if stage == 0:
            out["pool_w"] = out["pool_w"].reshape(N_A, 4, GC, GC)
            out["pool_scale"] = out["pool_scale"].reshape(N_A, 1, D)
        return out

    scatters = []

    def on_grads(stage, grads):
        keys = list(grads)
        axes = [axis_of(k) for k in keys]
        arrs = [_as3d(grads[k]) for k in keys]
        lands = []
        for a, ax in zip(arrs, axes):
            shp = list(a.shape)
            shp[ax] //= 4
            lands.append(lax.empty((3,) + tuple(shp), a.dtype))
        send, recv, bufs, token = _push_start(f"scatter_start_{stage}", arrs + lands, 3 * len(keys), _scatter_plan(axes))
        scatters.append((stage, keys, axes, send, recv, bufs))
        return token[0, 0]

    sp = dict(
        norm1_g=norm1_g.reshape(DEPTH, 1, D) + zero, norm2_g=norm2_g.reshape(DEPTH, 1, D),
        kv_norm_g=kv_norm_g.reshape(1, 1, D), b_kv=b_kv.reshape(1, 1, 4 * HD), b_q=b_q.reshape(N_B, 1, D),
        sinks=sinks, b_o=b_o.reshape(N_B, 1, D), ffn_conv_b=ffn_conv_b.reshape(DEPTH, 1, F2),
        final_g=final_g.reshape(1, D))

    x2d = x.reshape(S, D)
    loss, grad_x, g = _local_step(x2d, loss_target.reshape(S, D), sp, weights_for, on_grads)

    small_full = dict(
        norm1_g=jnp.stack(g["norm1_g"]), norm2_g=jnp.stack(g["norm2_g"]), kv_norm_g=g["kv_norm_g"],
        b_kv=g["b_kv"], b_q=jnp.stack(g["b_q"]), sinks=jnp.stack([s[0, :NH] for s in g["sinks"]]),
        b_o=jnp.stack(g["b_o"]), ffn_conv_b=jnp.stack(g["ffn_conv_b"]), final_g=g["final_g"],
        pool_scale=jnp.stack(g["pool_scale"]), ffn_conv_w=jnp.stack(g["ffn_conv_w"]))
    small_names = SMALL + SMALL_SHARDED
    small_shapes = [tuple(W[k].shape) for k in SMALL] + [(N_A, D), (DEPTH, 3, F2)]
    packed = _pack([small_full[k] for k in small_names] + [loss])
    slots = lax.dynamic_update_slice(lax.empty((8,) + packed.shape, F32), packed[None],
                                     (4 * lax.axis_index("x") + 2 * lax.axis_index("y") + lax.axis_index("c"), 0, 0))
    red_send, red_recv, slots, _ = _push_start("reduce_start", [slots], 7, _everyone_plan(), "everyone")

    pkeys, partial, swaps, after = [], [], [], grad_x

    def swap_start(tag, first):
        mine_now = partial[first:]
        lands = [lax.empty(p.shape, p.dtype) for p in mine_now]
        n = len(mine_now)
        send, recv, bufs, token = _push_start(f"swap_start_{tag}", mine_now + lands, n, _swap_plan(n), True)
        swaps.append((tag, n, send, recv, bufs))
        return token

    for i, (stage, keys, axes, send, recv, bufs) in enumerate(scatters):
        bufs = _push_wait(f"scatter_wait_{stage}", send, recv, bufs, _scatter_plan(axes), after)
        n = len(keys)
        for k, ax, grad, landed in zip(keys, axes, bufs[:n], bufs[n:]):
            pkeys.append(k)
            p_sum = _sum_landed(grad, landed, ax, me)
            partial.append(p_sum.reshape(-1, p_sum.shape[-1]))
        if i == EARLY_SWAP - 1:
            after = swap_start("early", 0)
    after = swap_start("late", sum(n for _, n, *_ in swaps))

    slots = _push_wait("reduce_wait", red_send, red_recv, slots, _everyone_plan(), after, "everyone")[0]
    red = _unpack(_sum_slots(slots), small_shapes + [(1, LANES)])
    red_g = dict(zip(small_names, red[:-1]))
    loss_out = red[-1][0, 0]
    red_g["pool_scale"] = lax.dynamic_slice_in_dim(red_g["pool_scale"], chip * (D // 4), D // 4, axis=1)
    red_g["ffn_conv_w"] = lax.dynamic_slice_in_dim(red_g["ffn_conv_w"], chip * (F2 // 4), F2 // 4, axis=2)
    two_d = lambda a: a.reshape((-1, a.shape[-1]))
    res = _adamw_small([two_d(red_g[k]) for k in small_names], [two_d(W[k]) for k in small_names],
                       [two_d(M[k]) for k in small_names], [two_d(V[k]) for k in small_names])
    out_g, out_d, out_m, out_v = [
        {k: r.reshape(W[k].shape) for k, r in zip(small_names, part)} for part in res]
    after = res[0][0]

    mine, theirs = [], []
    for tag, n, send, recv, bufs in swaps:
        bufs = _push_wait(f"swap_wait_{tag}", send, recv, bufs, _swap_plan(n), after, True)
        mine += bufs[:n]
        theirs += bufs[n:]
    mine = dict(zip(pkeys, mine))
    theirs = dict(zip(pkeys, theirs))
    for k in BIG:
        n_l = len([pk_ for pk_ in pkeys if pk_[0] == k])
        shp = W[k].shape
        rows, cols = mine[k, 0].shape
        three_d = lambda a: a.reshape(n_l, rows, cols)
        res = _adamw([[mine[k, l], theirs[k, l]] for l in range(n_l)], three_d(W[k]), three_d(M[k]), three_d(V[k]))
        out_g[k], out_d[k], out_m[k], out_v[k] = [r.reshape(shp) for r in res]

    return (loss_out, grad_x.reshape(x.shape), *[out_g[k] for k in ORDER], *[out_d[k] for k in ORDER],
            *[out_m[k] for k in ORDER], *[out_v[k] for k in ORDER])
```

```python
import functools

import jax
import jax.numpy as jnp
from jax import lax
from jax.experimental import pallas as pl
from jax.experimental.pallas import tpu as pltpu

D = 1024
DEPTH = 4
N_A = 2
N_B = 2
WINS = (2, 4, 8, 16)
GC = 256
HD = 64
NH = 16
BLK = 128
F = 2816
F2 = 2 * F
EPS = 1e-5
SCALE = HD ** -0.5
NEG = -1e30
HALO = 16
TN = 256
TM_STREAM = 1024
UP_GROUP = 3
LANES = 128
SUBLANES = 8
VMEM_LIMIT = 56 * 1024 * 1024
FFN_VMEM_LIMIT = 60 * 1024 * 1024
ACC_BYTES = 6 * 1024 * 1024
EW_BYTES = 1536 * 1024

LR, B1, B2, AEPS, WD, STEP = 0.001, 0.9, 0.999, 1e-08, 0.01, 10

MX = jnp.bfloat16
F32 = jnp.float32
MESH = pl.DeviceIdType.MESH


def _cp(n_axes=1, vmem=VMEM_LIMIT):
    return pltpu.CompilerParams(dimension_semantics=("arbitrary",) * n_axes, vmem_limit_bytes=vmem)


def _dot(a, b):
    return jnp.dot(a, b, preferred_element_type=F32)


def _dot_nt(a, b):
    return lax.dot_general(a, b, (((1,), (1,)), ((), ())), preferred_element_type=F32)


def _dot_tn(a, b):
    return lax.dot_general(a, b, (((0,), (0,)), ((), ())), preferred_element_type=F32)


def _rms_fwd(x, g):
    r = lax.rsqrt(jnp.mean(x * x, axis=-1, keepdims=True) + EPS)
    xh = x * r
    return xh * g, xh, r


def _rms_bwd(dh, xh, r, g):
    dxh = dh * g
    return r * (dxh - xh * jnp.mean(dxh * xh, axis=-1, keepdims=True))


def _row_tile(s, want):
    return min(s, want)


def _pool_pm(e, h, row, tm):
    out = []
    for gi, win in enumerate(WINS):
        cols = slice(gi * GC, (gi + 1) * GC)
        s = e[:, cols]
        sh = 1
        while sh < win:
            s = s + pltpu.roll(s, sh, 0)
            sh *= 2
        inv = 1.0 / jnp.minimum(row + 1, win).astype(F32)
        out.append(s[HALO:] * inv - h[:, cols])
    return out


def _pool_fwd(x, g, pw, ps, layer):
    S = x.shape[0]
    tm = _row_tile(S, 512)
    hb = tm // HALO

    def body(x_ref, xh_ref, g_ref, pw_ref, ps_ref, o_ref):
        i = pl.program_id(0)
        x = x_ref[...]
        gg = g_ref[...]
        h, _, _ = _rms_fwd(x, gg)
        hh, _, _ = _rms_fwd(xh_ref[...], gg)
        hh = jnp.where(i > 0, hh, 0.0)
        e = jnp.concatenate([hh, h], axis=0)
        row = i * tm + lax.broadcasted_iota(jnp.int32, (tm, 1), 0)
        pm = _pool_pm(e, h, row, tm)
        for gi in range(len(WINS)):
            cols = slice(gi * GC, (gi + 1) * GC)
            z = _dot(pm[gi].astype(MX), pw_ref[gi])
            o_ref[:, cols] = x[:, cols] + z * ps_ref[:, cols]

    return pl.pallas_call(
        body, name="pool_fwd",
        grid=(S // tm,),
        in_specs=[
            pl.BlockSpec((tm, D), lambda i: (i, 0)),
            pl.BlockSpec((HALO, D), lambda i: (jnp.maximum(i * hb - 1, 0), 0)),
            pl.BlockSpec((None, 1, D), lambda i: (layer, 0, 0)),
            pl.BlockSpec((None, 4, GC, GC), lambda i: (layer, 0, 0, 0)),
            pl.BlockSpec((None, 1, D), lambda i: (layer, 0, 0)),
        ],
        out_specs=pl.BlockSpec((tm, D), lambda i: (i, 0)),
        out_shape=jax.ShapeDtypeStruct((S, D), F32),
        compiler_params=_cp(),
    )(x, x, g, pw, ps)


def _pool_bwd(x, dy, g, pw, ps, layer):
    S = x.shape[0]
    tm = _row_tile(S, 256)
    hb = tm // HALO
    n_i = S // tm
    n_h = S // HALO

    def body(x_ref, xh_ref, dy_ref, dyn_ref, g_ref, pw_ref, ps_ref, dx_ref, dpw_ref, dps_ref, dg_ref):
        i = pl.program_id(0)

        @pl.when(i == 0)
        def _():
            dpw_ref[...] = jnp.zeros_like(dpw_ref)
            dps_ref[...] = jnp.zeros_like(dps_ref)
            dg_ref[...] = jnp.zeros_like(dg_ref)

        x = x_ref[...]
        gg = g_ref[...]
        ps = ps_ref[...]
        h, xh, r = _rms_fwd(x, gg)
        hh, _, _ = _rms_fwd(xh_ref[...], gg)
        hh = jnp.where(i > 0, hh, 0.0)
        e = jnp.concatenate([hh, h], axis=0)
        row = i * tm + lax.broadcasted_iota(jnp.int32, (tm, 1), 0)
        rown = (i + 1) * tm + lax.broadcasted_iota(jnp.int32, (HALO, 1), 0)
        pm = _pool_pm(e, h, row, tm)
        dy = dy_ref[...]
        dz = dy * ps
        dzn = jnp.where(i < n_i - 1, dyn_ref[...] * ps, 0.0)
        parts = []
        for gi, win in enumerate(WINS):
            cols = slice(gi * GC, (gi + 1) * GC)
            w = pw_ref[gi]
            pmb = pm[gi].astype(MX)
            z = _dot(pmb, w)
            dps_ref[:, cols] += jnp.sum(dy[:, cols] * z, axis=0, keepdims=True)
            dzb = dz[:, cols].astype(MX)
            dpw_ref[gi] += _dot_tn(pmb, dzb)
            dpm = _dot_nt(dzb, w)
            dpmn = _dot_nt(dzn[:, cols].astype(MX), w)
            q = dpm * (1.0 / jnp.minimum(row + 1, win).astype(F32))
            qn = dpmn * (1.0 / jnp.minimum(rown + 1, win).astype(F32))
            s = jnp.concatenate([q, qn], axis=0)
            sh = 1
            while sh < win:
                s = s + pltpu.roll(s, tm + HALO - sh, 0)
                sh *= 2
            parts.append(s[:tm] - dpm)
        dh = jnp.concatenate(parts, axis=1)
        dg_ref[...] += jnp.sum(dh * xh, axis=0, keepdims=True)
        dx_ref[...] = dy + _rms_bwd(dh, xh, r, gg)

    return pl.pallas_call(
        body, name="pool_bwd",
        grid=(n_i,),
        in_specs=[
            pl.BlockSpec((tm, D), lambda i: (i, 0)),
            pl.BlockSpec((HALO, D), lambda i: (jnp.maximum(i * hb - 1, 0), 0)),
            pl.BlockSpec((tm, D), lambda i: (i, 0)),
            pl.BlockSpec((HALO, D), lambda i: (jnp.minimum((i + 1) * hb, n_h - 1), 0)),
            pl.BlockSpec((None, 1, D), lambda i: (layer, 0, 0)),
            pl.BlockSpec((None, 4, GC, GC), lambda i: (layer, 0, 0, 0)),
            pl.BlockSpec((None, 1, D), lambda i: (layer, 0, 0)),
        ],
        out_specs=[
            pl.BlockSpec((tm, D), lambda i: (i, 0)),
            pl.BlockSpec((4, GC, GC), lambda i: (0, 0, 0)),
            pl.BlockSpec((1, D), lambda i: (0, 0)),
            pl.BlockSpec((1, D), lambda i: (0, 0)),
        ],
        out_shape=[
            jax.ShapeDtypeStruct((S, D), F32),
            jax.ShapeDtypeStruct((4, GC, GC), F32),
            jax.ShapeDtypeStruct((1, D), F32),
            jax.ShapeDtypeStruct((1, D), F32),
        ],
        compiler_params=_cp(),
    )(x, x, dy, dy, g, pw, ps)


N_STAGE = 4


def _rows_before(slot, u, prev8):
    tm = u.shape[0]
    m1, m2 = [], []
    for c in range(TN // LANES):
        lanes = slice(c * LANES, (c + 1) * LANES)
        slot[c, 0:SUBLANES, :] = prev8[:, lanes]
        slot[c, SUBLANES:SUBLANES + tm, :] = u[:, lanes]
        m1.append(slot[c, pl.ds(SUBLANES - 1, tm), :])
        m2.append(slot[c, pl.ds(SUBLANES - 2, tm), :])
    return jnp.concatenate(m1, axis=1), jnp.concatenate(m2, axis=1)


def _rows_after(slot, d, next8):
    tm = d.shape[0]
    p1, p2 = [], []
    for c in range(TN // LANES):
        lanes = slice(c * LANES, (c + 1) * LANES)
        slot[c, 0:tm, :] = d[:, lanes]
        slot[c, tm:tm + SUBLANES, :] = next8[:, lanes]
        p1.append(slot[c, pl.ds(1, tm), :])
        p2.append(slot[c, pl.ds(2, tm), :])
    return jnp.concatenate(p1, axis=1), jnp.concatenate(p2, axis=1)


def _conv(slot, u, prev8, cw):
    um1, um2 = _rows_before(slot, u, prev8)
    return cw[0:1] * um2 + cw[1:2] * um1 + cw[2:3] * u


def _ffn_fwd(x, g, wup, wdn, cw, cb, layer):
    S = x.shape[0]
    tm = _row_tile(S, 512)

    def body(x_ref, g_ref, wup_hbm, wdn_hbm, cw_ref, cb_ref, o_ref, u_ref, uc_ref, wup_v, wdn_v, carry, act, stage):
        i = pl.program_id(0)

        @pl.when(i == 0)
        def _():
            pltpu.sync_copy(wup_hbm.at[0], wup_v)
            pltpu.sync_copy(wdn_hbm.at[0], wdn_v)
            carry[...] = jnp.zeros_like(carry)

        x = x_ref[...]
        h, _, _ = _rms_fwd(x, g_ref[...])
        hb = h.astype(MX)
        for j in range(F // TN):
            cg = slice(j * TN, (j + 1) * TN)
            cv = slice(F + j * TN, F + (j + 1) * TN)
            ug = _dot(hb, wup_v[:, cg])
            uv = _dot(hb, wup_v[:, cv])
            u_ref[:, cg] = ug.astype(u_ref.dtype)
            u_ref[:, cv] = uv.astype(u_ref.dtype)
            gt = _conv(stage.at[2 * (j % 2)], ug, carry[:, cg], cw_ref[:, cg])
            vl = _conv(stage.at[2 * (j % 2) + 1], uv, carry[:, cv], cw_ref[:, cv])
            carry[:, cg] = ug[tm - SUBLANES:]
            carry[:, cv] = uv[tm - SUBLANES:]
            gt = gt + cb_ref[:, cg]
            vl = vl + cb_ref[:, cv]
            uc_ref[:, cg] = gt.astype(uc_ref.dtype)
            uc_ref[:, cv] = vl.astype(uc_ref.dtype)
            act[:, cg] = (gt * jax.nn.sigmoid(gt) * vl).astype(act.dtype)
        o_ref[...] = x + _dot(act[...], wdn_v[...])

    return pl.pallas_call(
        body, name="ffn_fwd",
        grid=(S // tm,),
        in_specs=[
            pl.BlockSpec((tm, D), lambda i: (i, 0)),
            pl.BlockSpec((None, 1, D), lambda i: (layer, 0, 0)),
            pl.BlockSpec(memory_space=pl.ANY),
            pl.BlockSpec(memory_space=pl.ANY),
            pl.BlockSpec((None, 3, F2), lambda i: (layer, 0, 0)),
            pl.BlockSpec((None, 1, F2), lambda i: (layer, 0, 0)),
        ],
        out_specs=[
            pl.BlockSpec((tm, D), lambda i: (i, 0)),
            pl.BlockSpec((tm, F2), lambda i: (i, 0)),
            pl.BlockSpec((tm, F2), lambda i: (i, 0)),
        ],
        out_shape=[
            jax.ShapeDtypeStruct((S, D), F32),
            jax.ShapeDtypeStruct((S, F2), MX),
            jax.ShapeDtypeStruct((S, F2), MX),
        ],
        scratch_shapes=[
            pltpu.VMEM((D, F2), MX),
            pltpu.VMEM((F, D), MX),
            pltpu.VMEM((SUBLANES, F2), F32),
            pltpu.VMEM((tm, F), MX),
            pltpu.VMEM((N_STAGE, TN // LANES, tm + SUBLANES, LANES), F32),
        ],
        compiler_params=_cp(vmem=FFN_VMEM_LIMIT),
    )(x, g, wup, wdn, cw, cb)


def _ffn_bwd(x, dy, u, uc, g, wup, wdn, cw, layer):
    S = x.shape[0]
    tm = _row_tile(S, 256)
    n_i = S // tm

    def body(x_ref, dy_ref, u_ref, uc_ref, g_ref, wup_hbm, wdn_hbm, cw_ref,
             dx_ref, du_ref, a_ref, h_ref, dg_ref, dcw_ref, dcb_ref, wup_v, wdn_v, carry, stage):
        i = pl.program_id(0)

        @pl.when(i == 0)
        def _():
            pltpu.sync_copy(wup_hbm.at[0], wup_v)
            pltpu.sync_copy(wdn_hbm.at[0], wdn_v)
            carry[...] = jnp.zeros_like(carry)
            dg_ref[...] = jnp.zeros_like(dg_ref)
            dcw_ref[...] = jnp.zeros_like(dcw_ref)
            dcb_ref[...] = jnp.zeros_like(dcb_ref)

        x = x_ref[...]
        gg = g_ref[...]
        h, xh, r = _rms_fwd(x, gg)
        h_ref[...] = h.T.astype(h_ref.dtype)
        dy = dy_ref[...]
        dyb = dy.astype(MX)
        dh, dus = None, ([], [])
        for j in range(F // TN):
            cg = slice(j * TN, (j + 1) * TN)
            cv = slice(F + j * TN, F + (j + 1) * TN)
            gt = uc_ref[:, cg].astype(F32)
            vl = uc_ref[:, cv].astype(F32)
            sg = jax.nn.sigmoid(gt)
            sil = gt * sg
            a_ref[cg, :] = (sil * vl).T.astype(a_ref.dtype)
            da = _dot_nt(dyb, wdn_v[cg, :])
            dvl = da * sil
            dgt = (da * vl) * (sg + sil * (1.0 - sg))
            for cc, dd in ((cg, dgt), (cv, dvl)):
                dp1, dp2 = _rows_after(stage.at[2 * (j % 2) + (cc is cv)], dd, carry[:, cc])
                carry[:, cc] = dd[0:SUBLANES]
                uu = u_ref[:, cc].astype(F32)
                dcb_ref[:, cc] += jnp.sum(dd, axis=0, keepdims=True)
                dcw_ref[0:1, cc] += jnp.sum(dp2 * uu, axis=0, keepdims=True)
                dcw_ref[1:2, cc] += jnp.sum(dp1 * uu, axis=0, keepdims=True)
                dcw_ref[2:3, cc] += jnp.sum(dd * uu, axis=0, keepdims=True)
                cwc = cw_ref[:, cc]
                duu = (cwc[2:3] * dd + cwc[1:2] * dp1 + cwc[0:1] * dp2).astype(MX)
                du_ref[:, cc] = duu
                dus[cc is cv].append(duu)
            if len(dus[0]) == UP_GROUP or j == F // TN - 1:
                first = j + 1 - len(dus[0])
                for side, base in ((0, 0), (1, F)):
                    cols = slice(base + first * TN, base + (j + 1) * TN)
                    part = _dot_nt(jnp.concatenate(dus[side], axis=1), wup_v[:, cols])
                    dh = part if dh is None else dh + part
                dus = ([], [])
        dg_ref[...] += jnp.sum(dh * xh, axis=0, keepdims=True)
        dx_ref[...] = dy + _rms_bwd(dh, xh, r, gg)

    rev = lambda i: (n_i - 1 - i, 0)
    return pl.pallas_call(
        body, name="ffn_bwd",
        grid=(n_i,),
        in_specs=[
            pl.BlockSpec((tm, D), rev),
            pl.BlockSpec((tm, D), rev),
            pl.BlockSpec((tm, F2), rev),
            pl.BlockSpec((tm, F2), rev),
            pl.BlockSpec((None, 1, D), lambda i: (layer, 0, 0)),
            pl.BlockSpec(memory_space=pl.ANY),
            pl.BlockSpec(memory_space=pl.ANY),
            pl.BlockSpec((None, 3, F2), lambda i: (layer, 0, 0)),
        ],
        out_specs=[
            pl.BlockSpec((tm, D), rev),
            pl.BlockSpec((tm, F2), rev),
            pl.BlockSpec((F, tm), lambda i: (0, n_i - 1 - i)),
            pl.BlockSpec((D, tm), lambda i: (0, n_i - 1 - i)),
            pl.BlockSpec((1, D), lambda i: (0, 0)),
            pl.BlockSpec((3, F2), lambda i: (0, 0)),
            pl.BlockSpec((1, F2), lambda i: (0, 0)),
        ],
        out_shape=[
            jax.ShapeDtypeStruct((S, D), F32),
            jax.ShapeDtypeStruct((S, F2), MX),
            jax.ShapeDtypeStruct((F, S), MX),
            jax.ShapeDtypeStruct((D, S), MX),
            jax.ShapeDtypeStruct((1, D), F32),
            jax.ShapeDtypeStruct((3, F2), F32),
            jax.ShapeDtypeStruct((1, F2), F32),
        ],
        scratch_shapes=[
            pltpu.VMEM((D, F2), MX),
            pltpu.VMEM((F, D), MX),
            pltpu.VMEM((SUBLANES, F2), F32),
            pltpu.VMEM((N_STAGE, TN // LANES, tm + SUBLANES, LANES), F32),
        ],
        compiler_params=_cp(vmem=FFN_VMEM_LIMIT),
    )(x, dy, u, uc, g, wup, wdn, cw)


def _tn_matmul(a, b, out_dtype, name, a_is_transposed=False):
    S, N = b.shape
    M = a.shape[0] if a_is_transposed else a.shape[1]
    bn = N
    while M * bn * 4 > ACC_BYTES and bn % (2 * LANES) == 0:
        bn //= 2
    bk = _row_tile(S, 2048)
    nk = S // bk
    a_spec = pl.BlockSpec((M, bk), lambda j, k: (0, k)) if a_is_transposed else pl.BlockSpec((bk, M), lambda j, k: (k, 0))

    def body(a_ref, b_ref, o_ref, acc):
        k = pl.program_id(1)

        @pl.when(k == 0)
        def _():
            acc[...] = jnp.zeros_like(acc)

        if a_is_transposed:
            acc[...] += _dot(a_ref[...].astype(MX), b_ref[...].astype(MX))
        else:
            acc[...] += _dot_tn(a_ref[...].astype(MX), b_ref[...].astype(MX))

        @pl.when(k == nk - 1)
        def _():
            o_ref[...] = acc[...].astype(o_ref.dtype)

    return pl.pallas_call(
        body, name=name,
        grid=(N // bn, nk),
        in_specs=[
            a_spec,
            pl.BlockSpec((bk, bn), lambda j, k: (k, j)),
        ],
        out_specs=pl.BlockSpec((M, bn), lambda j, k: (0, j)),
        out_shape=jax.ShapeDtypeStruct((M, N), out_dtype),
        scratch_shapes=[pltpu.VMEM((M, bn), F32)],
        compiler_params=_cp(2),
    )(a, b)


def _rms_linear(x, g, w, b, g_layer, b_layer, name):
    S = x.shape[0]
    N = w.shape[-1]
    tm = _row_tile(S, TM_STREAM)

    def body(x_ref, g_ref, w_ref, b_ref, o_ref):
        h, _, _ = _rms_fwd(x_ref[...], g_ref[...])
        o_ref[...] = (_dot(h.astype(MX), w_ref[...]) + b_ref[...]).astype(o_ref.dtype)

    return pl.pallas_call(
        body, name=name,
        grid=(S // tm,),
        in_specs=[
            pl.BlockSpec((tm, D), lambda i: (i, 0)),
            pl.BlockSpec((None, 1, D), lambda i: (g_layer, 0, 0)),
            pl.BlockSpec((None, D, N), lambda i: (0, 0, 0)),
            pl.BlockSpec((None, 1, N), lambda i: (b_layer, 0, 0)),
        ],
        out_specs=pl.BlockSpec((tm, N), lambda i: (i, 0)),
        out_shape=jax.ShapeDtypeStruct((S, N), MX),
        compiler_params=_cp(),
    )(x, g, w, b)


def _linear_res(o, w, b, xres, layer):
    S = o.shape[0]
    tm = _row_tile(S, TM_STREAM)

    def body(o_ref, w_ref, b_ref, x_ref, y_ref):
        y_ref[...] = x_ref[...] + _dot(o_ref[...], w_ref[...]) + b_ref[...]

    return pl.pallas_call(
        body, name="o_proj",
        grid=(S // tm,),
        in_specs=[
            pl.BlockSpec((tm, D), lambda i: (i, 0)),
            pl.BlockSpec((None, D, D), lambda i: (0, 0, 0)),
            pl.BlockSpec((None, 1, D), lambda i: (layer, 0, 0)),
            pl.BlockSpec((tm, D), lambda i: (i, 0)),
        ],
        out_specs=pl.BlockSpec((tm, D), lambda i: (i, 0)),
        out_shape=jax.ShapeDtypeStruct((S, D), F32),
        compiler_params=_cp(),
    )(o, w, b, xres)


def _linear_nt(dy, w):
    S = dy.shape[0]
    tm = _row_tile(S, TM_STREAM)

    def body(dy_ref, w_ref, o_ref, db_ref):
        @pl.when(pl.program_id(0) == 0)
        def _():
            db_ref[...] = jnp.zeros_like(db_ref)

        dy = dy_ref[...]
        db_ref[...] += jnp.sum(dy, axis=0, keepdims=True)
        o_ref[...] = _dot_nt(dy.astype(MX), w_ref[...]).astype(o_ref.dtype)

    return pl.pallas_call(
        body, name="o_proj_bwd",
        grid=(S // tm,),
        in_specs=[
            pl.BlockSpec((tm, D), lambda i: (i, 0)),
            pl.BlockSpec((None, D, D), lambda i: (0, 0, 0)),
        ],
        out_specs=[
            pl.BlockSpec((tm, D), lambda i: (i, 0)),
            pl.BlockSpec((1, D), lambda i: (0, 0)),
        ],
        out_shape=[
            jax.ShapeDtypeStruct((S, D), MX),
            jax.ShapeDtypeStruct((1, D), F32),
        ],
        compiler_params=_cp(),
    )(dy, w)


def _rms_linear_bwd(x, g, dzs, w, dy, g_layer, name):
    S = x.shape[0]
    N = w.shape[-1]
    tm = _row_tile(S, TM_STREAM)
    nz = len(dzs)

    def body(*refs):
        x_ref, g_ref = refs[0], refs[1]
        dz_refs = refs[2:2 + nz]
        w_ref, dy_ref, dx_ref, dg_ref, db_ref, h_ref, dzb_ref = refs[2 + nz:]

        @pl.when(pl.program_id(0) == 0)
        def _():
            dg_ref[...] = jnp.zeros_like(dg_ref)
            db_ref[...] = jnp.zeros_like(db_ref)

        gg = g_ref[...]
        h, xh, r = _rms_fwd(x_ref[...], gg)
        h_ref[...] = h.astype(h_ref.dtype)
        dz = dz_refs[0][...].astype(F32)
        for zr in dz_refs[1:]:
            dz = dz + zr[...].astype(F32)
        db_ref[...] += jnp.sum(dz, axis=0, keepdims=True)
        dzb = dz.astype(MX)
        dzb_ref[...] = dzb
        dh = _dot_nt(dzb, w_ref[...])
        dg_ref[...] += jnp.sum(dh * xh, axis=0, keepdims=True)
        dx_ref[...] = dy_ref[...] + _rms_bwd(dh, xh, r, gg)

    return pl.pallas_call(
        body, name=name,
        grid=(S // tm,),
        in_specs=[
            pl.BlockSpec((tm, D), lambda i: (i, 0)),
            pl.BlockSpec((None, 1, D), lambda i: (g_layer, 0, 0)),
        ] + [pl.BlockSpec((tm, N), lambda i: (i, 0))] * nz + [
            pl.BlockSpec((None, D, N), lambda i: (0, 0, 0)),
            pl.BlockSpec((tm, D), lambda i: (i, 0)),
        ],
        out_specs=[
            pl.BlockSpec((tm, D), lambda i: (i, 0)),
            pl.BlockSpec((1, D), lambda i: (0, 0)),
            pl.BlockSpec((1, N), lambda i: (0, 0)),
            pl.BlockSpec((tm, D), lambda i: (i, 0)),
            pl.BlockSpec((tm, N), lambda i: (i, 0)),
        ],
        out_shape=[
            jax.ShapeDtypeStruct((S, D), F32),
            jax.ShapeDtypeStruct((1, D), F32),
            jax.ShapeDtypeStruct((1, N), F32),
            jax.ShapeDtypeStruct((S, D), MX),
            jax.ShapeDtypeStruct((S, N), MX),
        ],
        compiler_params=_cp(),
    )(x, g, *dzs, w, dy)


def _loss_head(x, g, tgt):
    S = x.shape[0]
    tm = _row_tile(S, TM_STREAM)

    def body(x_ref, g_ref, t_ref, dx_ref, dg_ref, l_ref):
        @pl.when(pl.program_id(0) == 0)
        def _():
            dg_ref[...] = jnp.zeros_like(dg_ref)
            l_ref[...] = jnp.zeros_like(l_ref)

        gg = g_ref[...]
        y, xh, r = _rms_fwd(x_ref[...], gg)
        err = y - t_ref[...]
        tok = jnp.sum(err * err, axis=-1, keepdims=True) * (1.0 / D)
        l_ref[...] += 0.5 * jnp.sum(tok, axis=0, keepdims=True)
        dyv = err * (1.0 / D)
        dg_ref[...] += jnp.sum(dyv * xh, axis=0, keepdims=True)
        dx_ref[...] = _rms_bwd(dyv, xh, r, gg)

    return pl.pallas_call(
        body, name="loss_head",
        grid=(S // tm,),
        in_specs=[
            pl.BlockSpec((tm, D), lambda i: (i, 0)),
            pl.BlockSpec((1, D), lambda i: (0, 0)),
            pl.BlockSpec((tm, D), lambda i: (i, 0)),
        ],
        out_specs=[
            pl.BlockSpec((tm, D), lambda i: (i, 0)),
            pl.BlockSpec((1, D), lambda i: (0, 0)),
            pl.BlockSpec((1, LANES), lambda i: (0, 0)),
        ],
        out_shape=[
            jax.ShapeDtypeStruct((S, D), F32),
            jax.ShapeDtypeStruct((1, D), F32),
            jax.ShapeDtypeStruct((1, LANES), F32),
        ],
        compiler_params=_cp(),
    )(x, g, tgt)


HPG = NH // 2
QH = BLK // 2
KW = BLK + QH
COLS = HPG * QH


def _attn_setup(kvp_ref, kvc_ref):
    kw = jnp.concatenate([kvp_ref[...], kvc_ref[...]], axis=0).astype(F32)
    kk, vv = kw[:, :LANES], kw[:, LANES:]
    lo = lax.broadcasted_iota(jnp.int32, (1, LANES), 1) < HD
    kr, vr = pltpu.roll(kk, HD, 1), pltpu.roll(vv, HD, 1)
    ks = [jnp.where(lo, kk, kr).astype(MX), jnp.where(lo, kr, kk).astype(MX)]
    vs = [jnp.where(lo, vv, vr).astype(MX), jnp.where(lo, vr, vv).astype(MX)]
    return ks, vs, lo


def _stack_heads(ref, grp, lo, rows):
    parts = []
    for j in range(4 * grp, 4 * grp + 4):
        slab = ref[rows, j * LANES:(j + 1) * LANES]
        zero = jnp.zeros_like(slab)
        parts += [jnp.where(lo, slab, zero), jnp.where(lo, zero, slab)]
    return jnp.concatenate(parts, axis=0)


def _unstack_heads(st, lo):
    nq = st.shape[0] // HPG
    return [jnp.where(lo, st[2 * i * nq:(2 * i + 1) * nq], st[(2 * i + 1) * nq:(2 * i + 2) * nq])
            for i in range(4)]


def _attn_probs(qs, kg, n, sk_ref, layer, grp):
    rows = HPG * BLK
    qi = lax.broadcasted_iota(jnp.int32, (rows, 2 * BLK), 0) & (BLK - 1)
    si = lax.broadcasted_iota(jnp.int32, (rows, 2 * BLK), 1)
    ok = (si > qi) & (si <= qi + BLK) & jnp.logical_or(n > 0, si >= BLK)
    head = lax.broadcasted_iota(jnp.int32, (rows, 1), 0) // BLK
    sink = jnp.zeros((rows, 1), F32)
    for h in range(HPG):
        sink = jnp.where(head == h, sk_ref[layer, HPG * grp + h], sink)
    s = jnp.where(ok, _dot_nt(qs, kg) * SCALE, NEG)
    m = jnp.maximum(jnp.max(s, axis=-1, keepdims=True), sink)
    p = jnp.exp(s - m)
    return p * (1.0 / (jnp.sum(p, axis=-1, keepdims=True) + jnp.exp(sink - m)))


def _attn_mask_t(koff, n):
    si = lax.broadcasted_iota(jnp.int32, (KW, COLS), 0)
    qi = lax.broadcasted_iota(jnp.int32, (KW, COLS), 1) & (QH - 1)
    return (si > qi) & (si <= qi + BLK) & jnp.logical_or(n > 0, si >= BLK - koff)


def _attn_probs_t(qs, kg, ok, sk_ref, layer, grp):
    s = jnp.where(ok, _dot_nt(kg, qs) * SCALE, NEG)
    head = lax.broadcasted_iota(jnp.int32, (1, COLS), 1) // QH
    sink = jnp.zeros((1, COLS), F32)
    for h in range(HPG):
        sink = jnp.where(head == h, sk_ref[layer, HPG * grp + h], sink)
    m = jnp.maximum(jnp.max(s, axis=0, keepdims=True), sink)
    p = jnp.exp(s - m)
    es = jnp.exp(sink - m)
    inv = 1.0 / (jnp.sum(p, axis=0, keepdims=True) + es)
    return p * inv, es * inv, head


def _attn_specs(n_extra_q):
    q_spec = pl.BlockSpec((BLK, D), lambda n: (n, 0))
    return [q_spec] * n_extra_q + [
        pl.BlockSpec((BLK, 4 * HD), lambda n: (jnp.maximum(n - 1, 0), 0)),
        pl.BlockSpec((BLK, 4 * HD), lambda n: (n, 0)),
        pl.BlockSpec(memory_space=pltpu.SMEM),
    ]


def _attn_fwd(q, kv, sinks, layer):
    S = q.shape[0]

    def body(q_ref, kvp_ref, kvc_ref, sk_ref, o_ref):
        n = pl.program_id(0)
        ks, vs, lo = _attn_setup(kvp_ref, kvc_ref)
        for grp in range(2):
            qs = _stack_heads(q_ref, grp, lo, slice(None))
            pr = _attn_probs(qs, ks[grp], n, sk_ref, layer, grp)
            outs = _unstack_heads(_dot(pr.astype(MX), vs[grp]), lo)
            for i in range(4):
                j = 4 * grp + i
                o_ref[:, j * LANES:(j + 1) * LANES] = outs[i].astype(o_ref.dtype)

    return pl.pallas_call(
        body, name="attn_fwd",
        grid=(S // BLK,),
        in_specs=_attn_specs(1),
        out_specs=pl.BlockSpec((BLK, D), lambda n: (n, 0)),
        out_shape=jax.ShapeDtypeStruct((S, D), MX),
        compiler_params=_cp(),
    )(q, kv, kv, sinks)


def _attn_bwd(q, do, kv, sinks, layer):
    S = q.shape[0]

    def body(q_ref, do_ref, kvp_ref, kvc_ref, sk_ref, dq_ref, dkv_ref, dsk_ref):
        n = pl.program_id(0)

        @pl.when(n == 0)
        def _():
            dkv_ref[...] = jnp.zeros_like(dkv_ref)
            dsk_ref[...] = jnp.zeros_like(dsk_ref)

        ks, vs, lo = _attn_setup(kvp_ref, kvc_ref)
        lane = lax.broadcasted_iota(jnp.int32, (1, LANES), 1)
        dsk = jnp.zeros((1, LANES), F32)
        dk = [jnp.zeros((2 * BLK, LANES), F32) for _ in range(2)]
        dv = [jnp.zeros((2 * BLK, LANES), F32) for _ in range(2)]
        for half in range(2):
            rows = slice(half * QH, (half + 1) * QH)
            koff = half * QH
            ok = _attn_mask_t(koff, n)
            above = [jnp.zeros((koff, LANES), F32)] if koff else []
            below = [jnp.zeros((2 * BLK - KW - koff, LANES), F32)] if 2 * BLK - KW - koff else []
            for grp in range(2):
                qs = _stack_heads(q_ref, grp, lo, rows)
                dos = _stack_heads(do_ref, grp, lo, rows)
                kg, vg = ks[grp][koff:koff + KW], vs[grp][koff:koff + KW]
                pr, psink, head = _attn_probs_t(qs, kg, ok, sk_ref, layer, grp)
                dpr = _dot_nt(vg, dos)
                delta = jnp.sum(pr * dpr, axis=0, keepdims=True)
                ds = (pr * (dpr - delta) * SCALE).astype(MX)
                sd = psink * delta
                for h in range(HPG):
                    dsk = dsk + jnp.where(lane == HPG * grp + h,
                                          -jnp.sum(jnp.where(head == h, sd, 0.0), axis=1, keepdims=True), 0.0)
                dqs = _unstack_heads(_dot_tn(kg, ds).T, lo)
                for i in range(4):
                    j = 4 * grp + i
                    dq_ref[rows, j * LANES:(j + 1) * LANES] = dqs[i].astype(dq_ref.dtype)
                dk[grp] = dk[grp] + jnp.concatenate(above + [_dot(ds, qs)] + below, axis=0)
                dv[grp] = dv[grp] + jnp.concatenate(above + [_dot(pr.astype(MX), dos)] + below, axis=0)
        dsk_ref[...] += dsk
        tk = [a + pltpu.roll(a, HD, 1) for a in dk]
        tv = [a + pltpu.roll(a, HD, 1) for a in dv]
        contrib = jnp.concatenate([jnp.where(lo, tk[0], tk[1]), jnp.where(lo, tv[0], tv[1])], axis=1)

        own_first = jnp.concatenate([contrib[BLK:], jnp.zeros((BLK, 4 * HD), F32)], axis=0)
        rows = pl.ds(pl.multiple_of(jnp.maximum(n - 1, 0) * BLK, BLK), 2 * BLK)
        dkv_ref[rows, :] += jnp.where(n == 0, own_first, contrib)

    return pl.pallas_call(
        body, name="attn_bwd",
        grid=(S // BLK,),
        in_specs=_attn_specs(2),
        out_specs=[
            pl.BlockSpec((BLK, D), lambda n: (n, 0)),
            pl.BlockSpec((S, 4 * HD), lambda n: (0, 0)),
            pl.BlockSpec((1, LANES), lambda n: (0, 0)),
        ],
        out_shape=[
            jax.ShapeDtypeStruct((S, D), MX),
            jax.ShapeDtypeStruct((S, 4 * HD), F32),
            jax.ShapeDtypeStruct((1, LANES), F32),
        ],
        compiler_params=_cp(),
    )(q, do, kv, kv, sinks)


def _ew_rows(rows, cols, n_bufs=1):
    br = rows
    while br * cols * 4 * n_bufs > EW_BYTES and br % (2 * SUBLANES) == 0:
        br //= 2
    return br


def _adam_update(g, w, m, v):
    nm = B1 * m + (1.0 - B1) * g
    nv = B2 * v + (1.0 - B2) * (g * g)
    m_hat = nm / (1.0 - B1 ** STEP)
    v_hat = nv / (1.0 - B2 ** STEP)
    return -LR * (m_hat / (jnp.sqrt(v_hat) + AEPS) + WD * w), nm, nv


def _adamw_small(gs, ws, ms, vs):
    n = len(ws)

    def body(*refs):
        ins, outs = refs[:4 * n], refs[4 * n:]
        for i in range(n):
            g = ins[i][...]
            d, nm, nv = _adam_update(g, ins[n + i][...], ins[2 * n + i][...], ins[3 * n + i][...])
            outs[i][...] = g
            outs[n + i][...] = d
            outs[2 * n + i][...] = nm
            outs[3 * n + i][...] = nv

    vm = pl.BlockSpec(memory_space=pltpu.VMEM)
    res = pl.pallas_call(
        body, name="adamw_small",
        in_specs=[vm] * (4 * n),
        out_specs=[vm] * (4 * n),
        out_shape=[jax.ShapeDtypeStruct(w.shape, F32) for w in ws] * 4,
    )(*gs, *ws, *ms, *vs)
    return res[:n], res[n:2 * n], res[2 * n:3 * n], res[3 * n:]


def _adamw(parts, w, m, v):
    L, R, C = w.shape
    br = _ew_rows(R, C)
    npart = len(parts[0])

    def body(*refs):
        p_refs = refs[:L * npart]
        w_ref, m_ref, v_ref, g_ref, d_ref, nm_ref, nv_ref = refs[L * npart:]
        lyr = pl.program_id(0)
        for l in range(L):
            @pl.when(lyr == l)
            def _(l=l):
                g = p_refs[l * npart][...]
                for pr in p_refs[l * npart + 1:(l + 1) * npart]:
                    g = g + pr[...]
                g_ref[...] = g
                d_ref[...], nm_ref[...], nv_ref[...] = _adam_update(g, w_ref[...], m_ref[...], v_ref[...])

    spec = pl.BlockSpec((None, br, C), lambda a, i: (a, i, 0))
    part_specs = [pl.BlockSpec((br, C), lambda a, i, l=l: (jnp.where(a == l, i, 0), 0))
                  for l in range(L) for _ in range(npart)]
    return pl.pallas_call(
        body, name="adamw",
        grid=(L, R // br),
        in_specs=part_specs + [spec] * 3,
        out_specs=[spec] * 4,
        out_shape=[jax.ShapeDtypeStruct((L, R, C), F32)] * 4,
        compiler_params=_cp(2),
    )(*[a for lp in parts for a in lp], w, m, v)


def _coords():
    return lax.axis_index("x"), lax.axis_index("y"), lax.axis_index("c")


def _other_chips(x, y):
    return [(1 - x, y), (x, 1 - y), (1 - x, 1 - y)]


def _slot(ref, axis, chip, size):
    idx = [slice(None)] * 3
    idx[axis] = pl.ds(pl.multiple_of(chip * size, size), size)
    return ref.at[tuple(idx)]


HBM_SPEC = pl.BlockSpec(memory_space=pltpu.HBM)
SEM_SPEC = pl.BlockSpec(memory_space=pltpu.SEMAPHORE)
ANY_SPEC = pl.BlockSpec(memory_space=pl.ANY)
EFFECT = pltpu.SideEffectType.DATAFLOW_SIDE_EFFECTING


def _slot_specs(shape, axis, br, lead):
    _, b, c = shape
    nrb = b // br
    first = (lambda a: a) if lead is None else (lambda a: lead)
    shard = pl.BlockSpec((None, br, c), lambda a, i, me: (first(a), i, 0))
    if axis == 1:
        slot = pl.BlockSpec((None, br, c), lambda a, i, me: (a, me[0] * nrb + i, 0))
    else:
        slot = pl.BlockSpec((None, br, c), lambda a, i, me: (a, i, me[0]))
    return shard, slot


def _shard_rows(b, c):
    br = b
    while br * c * 4 > 2 * EW_BYTES and br % (4 * SUBLANES) == 0:
        br //= 2
    return br


def _gather_place(shard, axis, me, dtype, lead=None):
    a_dim, b, c = shard.shape
    if lead is not None:
        a_dim = 1
    br = _shard_rows(b, c)
    shp = [a_dim, b, c]
    shp[axis] *= 4
    shard_spec, slot_spec = _slot_specs((a_dim, b, c), axis, br, lead)

    def body(me_ref, s_ref, o_ref):
        o_ref[...] = s_ref[...].astype(o_ref.dtype)

    return pl.pallas_call(
        body, name="gather_place",
        grid_spec=pltpu.PrefetchScalarGridSpec(
            num_scalar_prefetch=1, grid=(a_dim, b // br), in_specs=[shard_spec], out_specs=slot_spec),
        out_shape=jax.ShapeDtypeStruct(tuple(shp), dtype),
        compiler_params=_cp(2),
    )(me, shard)


def _sum_landed(grad, landed, axis, me):
    a_dim, b, c = landed.shape[1:]
    br = _shard_rows(b, c)
    shard_spec, slot_spec = _slot_specs((a_dim, b, c), axis, br, None)

    def body(me_ref, own_ref, r_ref, o_ref):
        o_ref[...] = ((own_ref[...].astype(F32) + r_ref[0].astype(F32)) + r_ref[1].astype(F32)) + r_ref[2].astype(F32)

    return pl.pallas_call(
        body, name="sum_landed",
        grid_spec=pltpu.PrefetchScalarGridSpec(
            num_scalar_prefetch=1, grid=(a_dim, b // br),
            in_specs=[slot_spec, pl.BlockSpec((3, None, br, c), lambda a, i, me: (0, a, i, 0))],
            out_specs=shard_spec),
        out_shape=jax.ShapeDtypeStruct((a_dim, b, c), F32),
        compiler_params=_cp(2),
    )(me, grad, landed)


def _copies(refs, plan, send, recv, to_sibling):
    x, y, c = _coords()
    me = 2 * x + y
    if to_sibling == "everyone":
        me = 4 * x + 2 * y + c
        flips = [(k >> 2, (k >> 1) & 1, k & 1) for k in range(1, 8)]
        targets = [((x ^ fx, y ^ fy, c ^ fc), 4 * (x ^ fx) + 2 * (y ^ fy) + (c ^ fc)) for fx, fy, fc in flips]
    elif to_sibling:
        targets = [((x, y, 1 - c), me)]
    else:
        targets = [((px, py, c), 2 * px + py) for px, py in _other_chips(x, y)]
    out, t = [], 0
    while plan(refs, me, t, 0, me) is not None:
        for k, (device, peer) in enumerate(targets):
            sv, dv = plan(refs, me, t, k, peer)
            n = len(targets) * t + k
            out.append(pltpu.make_async_remote_copy(
                src_ref=sv, dst_ref=dv, send_sem=send.at[n], recv_sem=recv.at[n],
                device_id=device, device_id_type=MESH))
        t += 1
    return out


def _push_start(name, bufs, n_copies, plan, to_sibling=False):
    nb = len(bufs)

    def body(*refs):
        send, recv, token = refs[nb], refs[nb + 1], refs[-1]
        for cp in _copies(refs[:nb], plan, send, recv, to_sibling):
            cp.start()
        token[...] = jnp.zeros_like(token)

    res = pl.pallas_call(
        body, name=name,
        in_specs=[HBM_SPEC] * nb,
        out_specs=[SEM_SPEC, SEM_SPEC] + [HBM_SPEC] * nb + [pl.BlockSpec(memory_space=pltpu.VMEM)],
        out_shape=[pltpu.SemaphoreType.DMA((n_copies,)), pltpu.SemaphoreType.DMA((n_copies,))]
        + [pltpu.HBM(a.shape, a.dtype) for a in bufs] + [jax.ShapeDtypeStruct((SUBLANES, LANES), F32)],
        input_output_aliases={i: 2 + i for i in range(nb)},
        compiler_params=pltpu.CompilerParams(has_side_effects=EFFECT),
    )(*[pltpu.with_memory_space_constraint(a, pltpu.HBM) for a in bufs])
    return res[0], res[1], res[2:2 + nb], res[-1]


def _push_wait(name, send, recv, bufs, plan, after, to_sibling=False):
    nb = len(bufs)

    def body(*refs):
        for cp in _copies(refs[:nb], plan, refs[nb], refs[nb + 1], to_sibling):
            cp.wait_send()
            cp.wait_recv()

    return pl.pallas_call(
        body, name=name,
        in_specs=[HBM_SPEC] * nb + [SEM_SPEC, SEM_SPEC, ANY_SPEC],
        out_specs=[HBM_SPEC] * nb,
        out_shape=[pltpu.HBM(a.shape, a.dtype) for a in bufs],
        input_output_aliases={i: i for i in range(nb)},
        compiler_params=pltpu.CompilerParams(has_side_effects=EFFECT),
    )(*bufs, send, recv, after)


def _gather_plan(axes):
    def plan(refs, me, t, k, peer):
        if t >= len(axes):
            return None
        size = refs[t].shape[axes[t]] // 4
        mine = _slot(refs[t], axes[t], me, size)
        return mine, mine
    return plan


def _half_slot(ref, axis, chip):
    c = lax.axis_index("c")
    if axis == 1:
        half = ref.shape[1] // 8
        return ref.at[:, pl.ds(pl.multiple_of(chip * 2 * half + c * half, 2 * SUBLANES), half), :]
    half = ref.shape[1] // 2
    size = ref.shape[2] // 4
    return ref.at[:, pl.ds(pl.multiple_of(c * half, 2 * SUBLANES), half), pl.ds(pl.multiple_of(chip * size, LANES), size)]


def _gather_half_plan(axes):
    def plan(refs, me, t, k, peer):
        if t >= len(axes):
            return None
        mine = _half_slot(refs[t], axes[t], me)
        return mine, mine
    return plan


def _gather_pass_plan(axes):
    def plan(refs, me, t, k, peer):
        if t >= 3 * len(axes):
            return None
        x, y, _ = _coords()
        px, py = _other_chips(x, y)[t % 3]
        landed = _half_slot(refs[t // 3], axes[t // 3], 2 * px + py)
        return landed, landed
    return plan


def _scatter_plan(axes):
    n = len(axes)

    def plan(refs, me, t, k, peer):
        if t >= n:
            return None
        size = refs[t].shape[axes[t]] // 4
        return _slot(refs[t], axes[t], peer, size), refs[n + t].at[k]
    return plan


def _swap_plan(n):
    def plan(refs, me, t, k, peer):
        if t >= n:
            return None
        return refs[t], refs[n + t]
    return plan


def _everyone_plan():
    def plan(refs, me, t, k, peer):
        if t >= 1:
            return None
        mine = refs[0].at[me]
        return mine, mine
    return plan


def _sum_slots(slots):
    R = slots.shape[1]

    def body(s_ref, o_ref):
        tot = s_ref[0]
        for k in range(1, 8):
            tot = tot + s_ref[k]
        o_ref[...] = tot

    vm = pl.BlockSpec(memory_space=pltpu.VMEM)
    return pl.pallas_call(
        body, name="sum_slots",
        in_specs=[vm],
        out_specs=vm,
        out_shape=jax.ShapeDtypeStruct((R, LANES), F32),
        compiler_params=pltpu.CompilerParams(vmem_limit_bytes=VMEM_LIMIT),
    )(slots)


def _pack(arrs):
    flat = []
    for a in arrs:
        f = a.reshape(-1).astype(F32)
        flat.append(jnp.pad(f, (0, (-f.shape[0]) % LANES)))
    v = jnp.concatenate(flat)
    v = jnp.pad(v, (0, (-v.shape[0]) % (SUBLANES * LANES)))
    return v.reshape(-1, LANES)


def _unpack(v, shapes):
    flat = v.reshape(-1)
    out, off = [], 0
    for shp in shapes:
        n = 1
        for d in shp:
            n *= d
        out.append(flat[off:off + n].reshape(shp))
        off += n + (-n) % LANES
    return out


def _local_step(x, tgt, sp, weights_for, on_grads):
    n1, n2 = sp["norm1_g"], sp["norm2_g"]
    w = dict(weights_for(0, x))
    saved = []
    xs = x
    kv = None
    for l in range(DEPTH):
        x_in = xs
        if l >= N_A:
            w.update(weights_for(1 + 2 * l - N_A, x_in))
        if l == N_A:
            kv = _rms_linear(x_in, sp["kv_norm_g"], w["w_kv"], sp["b_kv"], 0, 0, "kv_proj")
        if l < N_A:
            xa = _pool_fwd(x_in, n1, w["pool_w"], w["pool_scale"], l)
            q = o = None
        else:
            j = l - N_A
            q = _rms_linear(x_in, n1, w["w_q", j], sp["b_q"], l, j, "q_proj")
            o = _attn_fwd(q, kv, sp["sinks"], j)
            xa = _linear_res(o, w["w_o", j], sp["b_o"], x_in, j)
        w.update(weights_for(1 + l if l < N_A else 2 + 2 * l - N_A, xa))
        xs, u, uc = _ffn_fwd(xa, n2, w["ffn_up", l], w["ffn_down", l], w["ffn_conv_w"], sp["ffn_conv_b"], l)
        saved.append((x_in, xa, u, uc, q, o))

    dx, d_final_g, loss = _loss_head(xs, sp["final_g"], tgt)

    g = {k: [None] * DEPTH for k in ("norm1_g", "norm2_g", "ffn_conv_w", "ffn_conv_b")}
    for k in ("pool_scale", "b_q", "sinks", "b_o"):
        g[k] = [None] * N_A
    g["final_g"] = d_final_g
    dkvs = []
    pending = {}
    for l in reversed(range(DEPTH)):
        x_in, xa, u, uc, q, o = saved[l]
        dxa, du, a, hb, g["norm2_g"][l], g["ffn_conv_w"][l], g["ffn_conv_b"][l] = _ffn_bwd(
            xa, dx, u, uc, n2, w["ffn_up", l], w["ffn_down", l], w["ffn_conv_w"], l)
        pending["ffn_up", l] = _tn_matmul(hb, du, MX, "d_ffn_up", a_is_transposed=True)
        if l == 0:
            n1 = n1 + on_grads(DEPTH + 1, pending)
            pending = {}
        pending["ffn_down", l] = _tn_matmul(a, dx, MX, "d_ffn_down", a_is_transposed=True)
        zero = on_grads(DEPTH - 1 - l, pending)
        pending = {}
        n1, n2 = n1 + zero, n2 + zero
        if l < N_A:
            dx, d_pw, g["pool_scale"][l], g["norm1_g"][l] = _pool_bwd(
                x_in, dxa, n1, w["pool_w"], w["pool_scale"], l)
            pending["pool_w", l] = d_pw.astype(MX)
        else:
            j = l - N_A
            d_o, g["b_o"][j] = _linear_nt(dxa, w["w_o", j])
            pending["w_o", j] = _tn_matmul(o, dxa, MX, "d_w_o")
            dq, dkv, g["sinks"][j] = _attn_bwd(q, d_o, kv, sp["sinks"], j)
            dkvs.append(dkv)
            dx, g["norm1_g"][l], g["b_q"][j], hq, dqb = _rms_linear_bwd(
                x_in, n1, [dq], w["w_q", j], dxa, l, "q_proj_bwd")
            pending["w_q", j] = _tn_matmul(hq, dqb, MX, "d_w_q")
        if l == N_A:
            dx, g["kv_norm_g"], g["b_kv"], hk, dkvb = _rms_linear_bwd(
                x_in, sp["kv_norm_g"], dkvs, w["w_kv"], dx, 0, "kv_proj_bwd")
            pending["w_kv", 0] = _tn_matmul(hk, dkvb, MX, "d_w_kv")
    on_grads(DEPTH, pending)
    return loss, dx, g


SMALL = ("norm1_g", "norm2_g", "kv_norm_g", "b_kv", "b_q", "sinks", "b_o", "ffn_conv_b", "final_g")
SMALL_SHARDED = ("pool_scale", "ffn_conv_w")
BIG = ("pool_w", "w_kv", "w_q", "w_o", "ffn_up", "ffn_down")
EARLY_SWAP = 3
ORDER = ("norm1_g", "norm2_g", "pool_w", "pool_scale", "kv_norm_g", "w_kv", "b_kv", "w_q", "b_q", "sinks",
         "w_o", "b_o", "ffn_up", "ffn_conv_w", "ffn_conv_b", "ffn_down", "final_g")


def _as3d(a):
    return a.reshape((-1,) + a.shape[-2:])


def kernel(x, norm1_g, norm2_g, pool_w, pool_scale, kv_norm_g, w_kv, b_kv, w_q, b_q, sinks, w_o, b_o, ffn_up, ffn_conv_w, ffn_conv_b, ffn_down, final_g, loss_target, m_norm1_g, m_norm2_g, m_pool_w, m_pool_scale, m_kv_norm_g, m_w_kv, m_b_kv, m_w_q, m_b_q, m_sinks, m_w_o, m_b_o, m_ffn_up, m_ffn_conv_w, m_ffn_conv_b, m_ffn_down, m_final_g, v_norm1_g, v_norm2_g, v_pool_w, v_pool_scale, v_kv_norm_g, v_w_kv, v_b_kv, v_w_q, v_b_q, v_sinks, v_w_o, v_b_o, v_ffn_up, v_ffn_conv_w, v_ffn_conv_b, v_ffn_down, v_final_g):
    W = dict(norm1_g=norm1_g, norm2_g=norm2_g, pool_w=pool_w, pool_scale=pool_scale, kv_norm_g=kv_norm_g,
             w_kv=w_kv, b_kv=b_kv, w_q=w_q, b_q=b_q, sinks=sinks, w_o=w_o, b_o=b_o, ffn_up=ffn_up,
             ffn_conv_w=ffn_conv_w, ffn_conv_b=ffn_conv_b, ffn_down=ffn_down, final_g=final_g)
    M = dict(norm1_g=m_norm1_g, norm2_g=m_norm2_g, pool_w=m_pool_w, pool_scale=m_pool_scale,
             kv_norm_g=m_kv_norm_g, w_kv=m_w_kv, b_kv=m_b_kv, w_q=m_w_q, b_q=m_b_q, sinks=m_sinks, w_o=m_w_o,
             b_o=m_b_o, ffn_up=m_ffn_up, ffn_conv_w=m_ffn_conv_w, ffn_conv_b=m_ffn_conv_b, ffn_down=m_ffn_down,
             final_g=m_final_g)
    V = dict(norm1_g=v_norm1_g, norm2_g=v_norm2_g, pool_w=v_pool_w, pool_scale=v_pool_scale,
             kv_norm_g=v_kv_norm_g, w_kv=v_w_kv, b_kv=v_b_kv, w_q=v_w_q, b_q=v_b_q, sinks=v_sinks, w_o=v_w_o,
             b_o=v_b_o, ffn_up=v_ffn_up, ffn_conv_w=v_ffn_conv_w, ffn_conv_b=v_ffn_conv_b, ffn_down=v_ffn_down,
             final_g=v_final_g)
    S = x.shape[1]
    chip = 2 * lax.axis_index("x") + lax.axis_index("y")

    gather_axis = dict(pool_w=1, w_kv=1, w_q=1, w_o=1, ffn_up=2, ffn_down=1, pool_scale=2, ffn_conv_w=2)
    me = chip.reshape(1).astype(jnp.int32)
    axis_of = lambda key: gather_axis[key if isinstance(key, str) else key[0]]

    def placed(key, dtype):
        if isinstance(key, str):
            return _gather_place(_as3d(W[key]), axis_of(key), me, dtype)
        return _gather_place(W[key[0]], axis_of(key), me, dtype, lead=key[1])

    stages = [
        ["pool_w", "pool_scale", "ffn_conv_w"],
        [("ffn_up", 0), ("ffn_down", 0)],
        [("ffn_up", 1), ("ffn_down", 1)],
        ["w_kv", ("w_q", 0), ("w_o", 0)],
        [("ffn_up", 2), ("ffn_down", 2)],
        [("w_q", 1), ("w_o", 1)],
        [("ffn_up", 3), ("ffn_down", 3)],
    ]
    TWO_LEVEL = 1
    gathers, zero = [], 0.0
    for si, keys in enumerate(stages):
        axes = [axis_of(k) for k in keys]
        bufs = [placed(k, F32 if k in SMALL_SHARDED else MX) for k in keys]
        plan = _gather_half_plan(axes) if si == TWO_LEVEL else _gather_plan(axes)
        send, recv, bufs, token = _push_start(f"gather_start_{si}", bufs, 3 * len(keys), plan)
        gathers.append((keys, axes, send, recv, bufs))
        zero = zero + token[0, 0]

    def weights_for(stage, after):
        keys, axes, send, recv, bufs = gathers[stage]
        if stage == TWO_LEVEL:
            bufs = _push_wait(f"gather_wait_{stage}", send, recv, bufs, _gather_half_plan(axes), after)
            send, recv, bufs, _ = _push_start("gather_pass_start", bufs, 3 * len(keys), _gather_pass_plan(axes), True)
            bufs = _push_wait("gather_pass_wait", send, recv, bufs, _gather_pass_plan(axes), after, True)
        else:
            bufs = _push_wait(f"gather_wait_{stage}", send, recv, bufs, _gather_plan(axes), after)
        out = dict(zip(keys, bufs))
        if stage == 0:
            out["pool_w"] = out["pool_w"].reshape(N_A, 4, GC, GC)
            out["pool_scale"] = out["pool_scale"].reshape(N_A, 1, D)
        return out

    scatters = []

    def on_grads(stage, grads):
        keys = list(grads)
        axes = [axis_of(k) for k in keys]
        arrs = [_as3d(grads[k]) for k in keys]
        lands = []
        for a, ax in zip(arrs, axes):
            shp = list(a.shape)
            shp[ax] //= 4
            lands.append(lax.empty((3,) + tuple(shp), a.dtype))
        send, recv, bufs, token = _push_start(f"scatter_start_{stage}", arrs + lands, 3 * len(keys), _scatter_plan(axes))
        scatters.append((stage, keys, axes, send, recv, bufs))
        return token[0, 0]

    sp = dict(
        norm1_g=norm1_g.reshape(DEPTH, 1, D) + zero, norm2_g=norm2_g.reshape(DEPTH, 1, D),
        kv_norm_g=kv_norm_g.reshape(1, 1, D), b_kv=b_kv.reshape(1, 1, 4 * HD), b_q=b_q.reshape(N_B, 1, D),
        sinks=sinks, b_o=b_o.reshape(N_B, 1, D), ffn_conv_b=ffn_conv_b.reshape(DEPTH, 1, F2),
        final_g=final_g.reshape(1, D))

    x2d = x.reshape(S, D)
    loss, grad_x, g = _local_step(x2d, loss_target.reshape(S, D), sp, weights_for, on_grads)

    small_full = dict(
        norm1_g=jnp.stack(g["norm1_g"]), norm2_g=jnp.stack(g["norm2_g"]), kv_norm_g=g["kv_norm_g"],
        b_kv=g["b_kv"], b_q=jnp.stack(g["b_q"]), sinks=jnp.stack([s[0, :NH] for s in g["sinks"]]),
        b_o=jnp.stack(g["b_o"]), ffn_conv_b=jnp.stack(g["ffn_conv_b"]), final_g=g["final_g"],
        pool_scale=jnp.stack(g["pool_scale"]), ffn_conv_w=jnp.stack(g["ffn_conv_w"]))
    small_names = SMALL + SMALL_SHARDED
    small_shapes = [tuple(W[k].shape) for k in SMALL] + [(N_A, D), (DEPTH, 3, F2)]
    packed = _pack([small_full[k] for k in small_names] + [loss])
    slots = lax.dynamic_update_slice(lax.empty((8,) + packed.shape, F32), packed[None],
                                     (4 * lax.axis_index("x") + 2 * lax.axis_index("y") + lax.axis_index("c"), 0, 0))
    red_send, red_recv, slots, _ = _push_start("reduce_start", [slots], 7, _everyone_plan(), "everyone")

    pkeys, partial, swaps, after = [], [], [], grad_x

    def swap_start(tag, first):
        mine_now = partial[first:]
        lands = [lax.empty(p.shape, p.dtype) for p in mine_now]
        n = len(mine_now)
        send, recv, bufs, token = _push_start(f"swap_start_{tag}", mine_now + lands, n, _swap_plan(n), True)
        swaps.append((tag, n, send, recv, bufs))
        return token

    for i, (stage, keys, axes, send, recv, bufs) in enumerate(scatters):
        bufs = _push_wait(f"scatter_wait_{stage}", send, recv, bufs, _scatter_plan(axes), after)
        n = len(keys)
        for k, ax, grad, landed in zip(keys, axes, bufs[:n], bufs[n:]):
            pkeys.append(k)
            p_sum = _sum_landed(grad, landed, ax, me)
            partial.append(p_sum.reshape(-1, p_sum.shape[-1]))
        if i == EARLY_SWAP - 1:
            after = swap_start("early", 0)
    after = swap_start("late", sum(n for _, n, *_ in swaps))

    slots = _push_wait("reduce_wait", red_send, red_recv, slots, _everyone_plan(), after, "everyone")[0]
    red = _unpack(_sum_slots(slots), small_shapes + [(1, LANES)])
    red_g = dict(zip(small_names, red[:-1]))
    loss_out = red[-1][0, 0]
    red_g["pool_scale"] = lax.dynamic_slice_in_dim(red_g["pool_scale"], chip * (D // 4), D // 4, axis=1)
    red_g["ffn_conv_w"] = lax.dynamic_slice_in_dim(red_g["ffn_conv_w"], chip * (F2 // 4), F2 // 4, axis=2)
    two_d = lambda a: a.reshape((-1, a.shape[-1]))
    res = _adamw_small([two_d(red_g[k]) for k in small_names], [two_d(W[k]) for k in small_names],
                       [two_d(M[k]) for k in small_names], [two_d(V[k]) for k in small_names])
    out_g, out_d, out_m, out_v = [
        {k: r.reshape(W[k].shape) for k, r in zip(small_names, part)} for part in res]
    after = res[0][0]

    mine, theirs = [], []
    for tag, n, send, recv, bufs in swaps:
        bufs = _push_wait(f"swap_wait_{tag}", send, recv, bufs, _swap_plan(n), after, True)
        mine += bufs[:n]
        theirs += bufs[n:]
    mine = dict(zip(pkeys, mine))
    theirs = dict(zip(pkeys, theirs))
    for k in BIG:
        n_l = len([pk_ for pk_ in pkeys if pk_[0] == k])
        shp = W[k].shape
        rows, cols = mine[k, 0].shape
        three_d = lambda a: a.reshape(n_l, rows, cols)
        res = _adamw([[mine[k, l], theirs[k, l]] for l in range(n_l)], three_d(W[k]), three_d(M[k]), three_d(V[k]))
        out_g[k], out_d[k], out_m[k], out_v[k] = [r.reshape(shp) for r in res]

    return (loss_out, grad_x.reshape(x.shape), *[out_g[k] for k in ORDER], *[out_d[k] for k in ORDER],
            *[out_m[k] for k in ORDER], *[out_v[k] for k in ORDER])
```
